```python
import math, functools
import jax
import jax.numpy as jnp
from jax import lax
import numpy as np

D_MODEL = 1024
BATCH = 1
SEQ = 16384
DEPTH = 1
DEC_BATCH = 32
DEC_SEQ = 8
PAST_LEN = 16384
PAGE_SIZE = 128

HD_A = 64
H_A = (D_MODEL // 2) // HD_A
D_A = H_A * HD_A
DK_M = 128
DV_M = 128
H_M = (D_MODEL // 2) // DV_M
D_M = H_M * DV_M
D_MIX = D_A + D_M
DILATED_PATTERNS = ((128, 1), (512, 4), (2048, 16))
W_MAX = max(w for w, _ in DILATED_PATTERNS)
N_BUCKETS = 32
BUCKET_MAX_DIST = W_MAX
MLSTM_CHUNK = 128
D_FF = 2816
CONV_W = 3
RMS_EPS = 1e-6
IN_SIZES = (D_A, D_A, D_A, H_M * DK_M, H_M * DK_M, H_M * DV_M, D_M, H_M, H_M)
N_IN = sum(IN_SIZES)

kernel_name = 'hymba_dilated_mlstm_convffn_step'


def rmsnorm(x, g):
    xf = x.astype(jnp.float32)
    y = xf * lax.rsqrt(jnp.mean(xf * xf, axis=-1, keepdims=True) + RMS_EPS)
    return (y * g.astype(jnp.float32)).astype(x.dtype)


def t5_bucket(dist):
    max_exact = N_BUCKETS // 2
    d = jnp.maximum(dist, max_exact).astype(jnp.float32)
    large = max_exact + (jnp.log(d / max_exact) / math.log(BUCKET_MAX_DIST / max_exact)
                         * (N_BUCKETS - max_exact)).astype(jnp.int32)
    large = jnp.minimum(large, N_BUCKETS - 1)
    return jnp.where(dist < max_exact, dist, large)


def pattern_bias(rel_bias, win, dil):
    j = jnp.arange(win // dil + 1)
    return rel_bias[t5_bucket(j * dil)].T.astype(jnp.float32)


def dilated_band_attention(q, k, v, bias_hj, win, dil):
    B, S, H, E = q.shape
    span = win // dil
    blk = span
    L = S // dil
    nb = -(-L // blk)
    Lp = nb * blk

    def to_blocks(t):
        t = t.reshape(B, L, dil, H, E).transpose(0, 2, 1, 3, 4)
        t = jnp.pad(t, ((0, 0), (0, 0), (0, Lp - L), (0, 0), (0, 0)))
        return t.reshape(B, dil, nb, blk, H, E)

    def with_prev(t):
        prev = jnp.pad(t[:, :, :-1], ((0, 0), (0, 0), (1, 0), (0, 0), (0, 0), (0, 0)))
        return jnp.concatenate([prev, t], axis=3)

    qb = to_blocks(q)
    kk = with_prev(to_blocks(k))
    vv = with_prev(to_blocks(v))
    qi = jnp.arange(blk)[:, None]
    ki = jnp.arange(2 * blk)[None, :]
    steps = blk + qi - ki
    valid = (((steps >= 0) & (steps <= span))[None]
             & ((jnp.arange(nb) > 0)[:, None, None] | (ki >= blk)[None]))
    bias = bias_hj[:, jnp.clip(steps, 0, span)]
    s = jnp.einsum('brnqhe,brnkhe->brnhqk', qb, kk, preferred_element_type=jnp.float32) * (E ** -0.5)
    s = jnp.where(valid[None, None, :, None], s + bias[None, None, None], -jnp.inf)
    lse = jax.nn.logsumexp(s, axis=-1)
    p = jnp.exp(s - lse[..., None]).astype(vv.dtype)
    o = jnp.einsum('brnhqk,brnkhe->brnqhe', p, vv)
    o = o.reshape(B, dil, Lp, H, E)[:, :, :L].transpose(0, 2, 1, 3, 4).reshape(B, S, H, E)
    lse = lse.transpose(0, 1, 2, 4, 3).reshape(B, dil, Lp, H)[:, :, :L].transpose(0, 2, 1, 3).reshape(B, S, H)
    return o, lse


def dilated_gather_attention(q, k_all, v_all, bias_hj, win, dil):
    B, T, H, E = q.shape
    wb = k_all.shape[1] - T
    nk = win // dil + 1
    idx = wb + jnp.arange(T)[:, None] - jnp.arange(nk)[None, :] * dil
    valid = idx >= 0
    safe = jnp.maximum(idx, 0)
    kg = jnp.take(k_all, safe, axis=1)
    vg = jnp.take(v_all, safe, axis=1)
    s = jnp.einsum('bthe,btjhe->bhtj', q, kg, preferred_element_type=jnp.float32) * (E ** -0.5)
    s = jnp.where(valid[None, None], s + bias_hj[None, :, None, :], -jnp.inf)
    lse = jax.nn.logsumexp(s, axis=-1)
    p = jnp.exp(s - lse[..., None]).astype(vg.dtype)
    o = jnp.einsum('bhtj,btjhe->bthe', p, vg)
    return o, lse.transpose(0, 2, 1)


def combine_dilations(outs, lses):
    w = jax.nn.softmax(jnp.stack(lses), axis=0)
    return jnp.einsum('gbth,gbthe->bthe', w, jnp.stack(outs).astype(jnp.float32))


def attn_prompt(q, k, v, rel_bias):
    outs, lses = [], []
    for win, dil in DILATED_PATTERNS:
        o, l = dilated_band_attention(q, k, v, pattern_bias(rel_bias, win, dil), win, dil)
        outs.append(o)
        lses.append(l)
    wb = min(W_MAX, q.shape[1])
    return combine_dilations(outs, lses), k[:, -wb:], v[:, -wb:]


def attn_sample(q, k, v, rel_bias, cache_k, cache_v):
    wb = cache_k.shape[1]
    k_all = jnp.concatenate([cache_k.astype(k.dtype), k], axis=1)
    v_all = jnp.concatenate([cache_v.astype(v.dtype), v], axis=1)
    outs, lses = [], []
    for win, dil in DILATED_PATTERNS:
        o, l = dilated_gather_attention(q, k_all, v_all, pattern_bias(rel_bias, win, dil), win, dil)
        outs.append(o)
        lses.append(l)
    return combine_dilations(outs, lses), k_all[:, -wb:], v_all[:, -wb:]


def mlstm_chunkwise(q, k, v, ig, lf, C0, n0, m0):
    B, S, H, DK = q.shape
    DV = v.shape[-1]
    c = math.gcd(S, MLSTM_CHUNK)
    nc = S // c

    def chunks(t):
        return jnp.moveaxis(t.reshape((B, nc, c) + t.shape[2:]), 1, 0)

    causal = jnp.tril(jnp.ones((c, c), dtype=bool))

    def step(carry, inp):
        C, n, m = carry
        qc, kc, vc, ic, fc = inp
        b = jnp.cumsum(fc, axis=1).transpose(0, 2, 1)
        it = ic.transpose(0, 2, 1)
        D = jnp.where(causal, b[..., :, None] - b[..., None, :] + it[..., None, :], -jnp.inf)
        inter = b + m[..., None]
        mt = jnp.maximum(inter, jnp.max(D, axis=-1))
        Sm = jnp.einsum('bthk,bshk->bhts', qc, kc) * jnp.exp(D - mt[..., None])
        iw = jnp.exp(inter - mt)
        num = jnp.einsum('bhts,bshv->bhtv', Sm, vc) + iw[..., None] * jnp.einsum('bthk,bhkv->bhtv', qc, C)
        den = jnp.sum(Sm, axis=-1) + iw * jnp.einsum('bthk,bhk->bht', qc, n)
        h = num / jnp.maximum(jnp.abs(den), jnp.exp(-mt))[..., None]
        bL = b[..., -1]
        g = bL[..., None] - b + it
        m_new = jnp.maximum(bL + m, jnp.max(g, axis=-1))
        ws = jnp.exp(g - m_new[..., None])
        wc = jnp.exp(bL + m - m_new)
        C_new = wc[..., None, None] * C + jnp.einsum('bhs,bshk,bshv->bhkv', ws, kc, vc)
        n_new = wc[..., None] * n + jnp.einsum('bhs,bshk->bhk', ws, kc)
        return (C_new, n_new, m_new), h.transpose(0, 2, 1, 3)

    (C, n, m), h = lax.scan(step, (C0, n0, m0), (chunks(q), chunks(k), chunks(v), chunks(ig), chunks(lf)))
    h = jnp.moveaxis(h, 0, 1).reshape(B, S, H, DV)
    return h, C, n, m


def causal_dwconv(u, prev, w, b):
    T = u.shape[1]
    full = jnp.concatenate([prev.astype(u.dtype), u], axis=1)
    y = b + full[:, 0:T] * w[0]
    for tap in range(1, CONV_W):
        y = y + full[:, tap:tap + T] * w[tap]
    return y, full[:, T:]


def trunk_layer(x, attn_fn, C0, n0, m0, conv_prev, rel_bias, g_mix_pre, w_in, b_igate, b_fgate,
                g_mlstm_out, w_out, g_mix_post, g_ffn_pre, w_up, conv_w, conv_b, w_down, g_ffn_post):
    B, T, _ = x.shape
    f32 = jnp.float32
    z = rmsnorm(x, g_mix_pre) @ w_in
    qa, ka, va, qm, km, vm, om, zi, zf = jnp.split(z, np.cumsum(IN_SIZES)[:-1].tolist(), axis=-1)

    def heads(t, h):
        return t.reshape(B, T, h, -1)

    att, k_state, v_state = attn_fn(heads(qa, H_A), heads(ka, H_A), heads(va, H_A), rel_bias)

    ig = zi.astype(f32) + b_igate.astype(f32)
    lf = jax.nn.log_sigmoid(zf.astype(f32) + b_fgate.astype(f32))
    hm, C, n, m = mlstm_chunkwise(heads(qm, H_M).astype(f32), heads(km, H_M).astype(f32) * (DK_M ** -0.5),
                                  heads(vm, H_M).astype(f32), ig, lf,
                                  C0.astype(f32), n0.astype(f32), m0.astype(f32))
    hm = hm * lax.rsqrt(jnp.mean(hm * hm, axis=-1, keepdims=True) + RMS_EPS) * g_mlstm_out.astype(f32).reshape(H_M, DV_M)
    hm = jax.nn.sigmoid(om.astype(f32)) * hm.reshape(B, T, D_M)

    mixed = jnp.concatenate([att.reshape(B, T, D_A).astype(x.dtype), hm.astype(x.dtype)], axis=-1) @ w_out
    x = x + rmsnorm(mixed, g_mix_post)

    u = rmsnorm(x, g_ffn_pre) @ w_up
    uc, conv_state = causal_dwconv(u, conv_prev, conv_w, conv_b)
    gate, val = jnp.split(uc, 2, axis=-1)
    y = (jax.nn.gelu(gate, approximate=True) * val) @ w_down
    x = x + rmsnorm(y, g_ffn_post)
    return x, k_state, v_state, C, n, m, conv_state


def setup_inputs(seed: int = 0) -> dict:
    key = jax.random.key(seed)
    ks = jax.random.split(key, 24)
    f32 = jnp.float32
    wb = min(W_MAX, PAST_LEN)
    nrm = lambda i, shape: jax.random.normal(ks[i], shape, dtype=f32)
    return {
        'x_prompt': nrm(0, (BATCH, SEQ, D_MODEL)),
        'x_sample': nrm(1, (DEC_BATCH, DEC_SEQ, D_MODEL)),
        'cache_attn_k': nrm(2, (DEPTH, DEC_BATCH, wb, H_A, HD_A)),
        'cache_attn_v': nrm(3, (DEPTH, DEC_BATCH, wb, H_A, HD_A)),
        'state_mlstm_C': 0.1 * nrm(4, (DEPTH, DEC_BATCH, H_M, DK_M, DV_M)),
        'state_mlstm_n': 0.1 * nrm(5, (DEPTH, DEC_BATCH, H_M, DK_M)),
        'state_mlstm_m': nrm(6, (DEPTH, DEC_BATCH, H_M)),
        'state_ffn_conv': nrm(7, (DEPTH, DEC_BATCH, CONV_W - 1, 2 * D_FF)),
        'rel_bias': 0.5 * nrm(8, (N_BUCKETS, H_A)),
        'g_mix_pre': 1.0 + 0.05 * nrm(9, (DEPTH, D_MODEL)),
        'w_in': nrm(10, (DEPTH, D_MODEL, N_IN)) * (D_MODEL ** -0.5),
        'b_igate': 0.1 * nrm(11, (DEPTH, H_M)),
        'b_fgate': jnp.linspace(3.0, 6.0, H_M, dtype=f32)[None] + 0.1 * nrm(12, (DEPTH, H_M)),
        'g_mlstm_out': 1.0 + 0.05 * nrm(13, (DEPTH, D_M)),
        'w_out': nrm(14, (DEPTH, D_MIX, D_MODEL)) * (D_MIX ** -0.5),
        'g_mix_post': 1.0 + 0.05 * nrm(15, (DEPTH, D_MODEL)),
        'g_ffn_pre': 1.0 + 0.05 * nrm(16, (DEPTH, D_MODEL)),
        'w_up': nrm(17, (DEPTH, D_MODEL, 2 * D_FF)) * (D_MODEL ** -0.5),
        'conv_w': nrm(18, (DEPTH, CONV_W, 2 * D_FF)) * (CONV_W ** -0.5),
        'conv_b': 0.01 * nrm(19, (DEPTH, 2 * D_FF)),
        'w_down': nrm(20, (DEPTH, D_FF, D_MODEL)) * (D_FF ** -0.5),
        'g_ffn_post': 1.0 + 0.05 * nrm(21, (DEPTH, D_MODEL)),
    }


def reference(x_prompt, x_sample, cache_attn_k, cache_attn_v, state_mlstm_C, state_mlstm_n, state_mlstm_m,
              state_ffn_conv, rel_bias, g_mix_pre, w_in, b_igate, b_fgate, g_mlstm_out, w_out, g_mix_post,
              g_ffn_pre, w_up, conv_w, conv_b, w_down, g_ffn_post):
    f32 = jnp.float32
    yp = x_prompt
    ys = x_sample
    new_p = [[] for _ in range(6)]
    new_s = [[] for _ in range(6)]
    for l in range(DEPTH):
        w_l = (g_mix_pre[l], w_in[l], b_igate[l], b_fgate[l], g_mlstm_out[l], w_out[l], g_mix_post[l],
               g_ffn_pre[l], w_up[l], conv_w[l], conv_b[l], w_down[l], g_ffn_post[l])
        zC = jnp.zeros((BATCH, H_M, DK_M, DV_M), f32)
        zn = jnp.zeros((BATCH, H_M, DK_M), f32)
        zm = jnp.zeros((BATCH, H_M), f32)
        zconv = jnp.zeros((BATCH, CONV_W - 1, 2 * D_FF), x_prompt.dtype)
        yp, *sp = trunk_layer(yp, attn_prompt, zC, zn, zm, zconv, rel_bias, *w_l)
        sample_attn = functools.partial(attn_sample, cache_k=cache_attn_k[l], cache_v=cache_attn_v[l])
        ys, *ss = trunk_layer(ys, sample_attn, state_mlstm_C[l], state_mlstm_n[l], state_mlstm_m[l],
                              state_ffn_conv[l], rel_bias, *w_l)
        for i in range(6):
            new_p[i].append(sp[i])
            new_s[i].append(ss[i])
    k_win_prompt = jnp.stack(new_p[0])
    v_win_prompt = jnp.stack(new_p[1])
    mlstm_C_prompt = jnp.stack(new_p[2])
    mlstm_n_prompt = jnp.stack(new_p[3])
    mlstm_m_prompt = jnp.stack(new_p[4])
    conv_prompt = jnp.stack(new_p[5])
    k_win_sample = jnp.stack(new_s[0])
    v_win_sample = jnp.stack(new_s[1])
    mlstm_C_sample = jnp.stack(new_s[2])
    mlstm_n_sample = jnp.stack(new_s[3])
    mlstm_m_sample = jnp.stack(new_s[4])
    conv_sample = jnp.stack(new_s[5])
    return (yp, ys, k_win_prompt, v_win_prompt, mlstm_C_prompt, mlstm_n_prompt, mlstm_m_prompt, conv_prompt,
            k_win_sample, v_win_sample, mlstm_C_sample, mlstm_n_sample, mlstm_m_sample, conv_sample)
```

```python
import functools
import math

import jax
import jax.numpy as jnp
import numpy as np
from jax import lax
from jax.experimental import pallas as pl
from jax.experimental.pallas import tpu as pltpu

F32 = jnp.float32
BF16 = jnp.bfloat16

D_MODEL = 1024
HD_A = 64
H_A = 8
D_A = H_A * HD_A
DK_M = 128
DV_M = 128
H_M = 4
D_M = H_M * DV_M
PATTERNS = ((128, 1), (512, 4), (2048, 16))
SPAN = 128
N_BUCKETS = 32
BUCKET_MAX_DIST = 2048
MLSTM_CHUNK = 128
D_FF = 2816
CONV_W = 3
RMS_EPS = 1e-6
N_QKV = 3 * D_A
N_MIX = 4 * D_M
LANES = 128
SUPER = 2048
VMEM_LIMIT = 56 * 1024 * 1024
NEG_INF = float("-inf")

for _w, _d in PATTERNS:
    assert _w // _d == SPAN


def _params(sem, vmem=VMEM_LIMIT):
    return pltpu.CompilerParams(dimension_semantics=sem, vmem_limit_bytes=vmem)


def _const_spec(shape):
    nd = len(shape)
    return pl.BlockSpec(shape, lambda *_: (0,) * nd, pipeline_mode=pl.Buffered(1))


def _rms(x, g):
    return x * lax.rsqrt(jnp.mean(x * x, axis=-1, keepdims=True) + RMS_EPS) * g


def _inproj_kernel(x_ref, g_ref, wa_ref, wm_ref, wg_ref, bg_ref, *refs, tile, dils, first_tail):
    nd = len(dils)
    qkv1_ref = refs[0]
    dil_refs = refs[1:1 + nd]
    tail_ref, qm_ref, km_ref, vm_ref, om_ref, gt_ref, zs_ref = refs[1 + nd:]
    i = pl.program_id(0)

    hb = _rms(x_ref[...], g_ref[...]).astype(BF16)
    za = jnp.dot(hb, wa_ref[...], preferred_element_type=F32)
    for c in range(N_QKV // LANES):
        col = za[:, c * LANES:(c + 1) * LANES]
        if c < D_A // LANES:
            col = col * (HD_A ** -0.5)
        zs_ref[c] = col
        qkv1_ref[:, c * LANES:(c + 1) * LANES] = col.astype(BF16)

    for dil, out_ref in zip(dils, dil_refs):
        for r in range(dil):
            for c in range(N_QKV // LANES):
                rows = zs_ref[c, pl.ds(r, tile // dil, stride=dil), :]
                out_ref[r, :, c * LANES:(c + 1) * LANES] = rows.astype(BF16)

    @pl.when(i >= first_tail)
    def _():
        for c in range(N_QKV // LANES):
            tail_ref[:, c * LANES:(c + 1) * LANES] = zs_ref[c]

    zm = jnp.dot(hb, wm_ref[...], preferred_element_type=F32)
    qm_ref[...] = zm[:, 0:D_M].astype(BF16)
    km_ref[...] = (zm[:, D_M:2 * D_M] * (DK_M ** -0.5)).astype(BF16)
    vm_ref[...] = zm[:, 2 * D_M:3 * D_M].astype(BF16)
    om_ref[...] = zm[:, 3 * D_M:4 * D_M]

    zg = jnp.dot(hb, wg_ref[...], preferred_element_type=F32) + bg_ref[...]
    logsig = jnp.minimum(zg, 0.0) - jnp.log1p(jnp.exp(-jnp.abs(zg)))
    lane = lax.broadcasted_iota(jnp.int32, zg.shape, 1)
    gt_ref[...] = jnp.where(lane < H_M, zg, logsig)


def _inproj(x, g, wa, wm, wg, bg, *, tile, dils, tail_rows):
    n = x.shape[0]
    assert n % tile == 0 and tail_rows % tile == 0
    for d in dils:
        assert tile % (16 * d) == 0
    steps = n // tile
    first_tail = (n - tail_rows) // tile
    row = lambda w: pl.BlockSpec((tile, w), lambda i: (i, 0))
    out_shape = [jax.ShapeDtypeStruct((n, N_QKV), BF16)]
    out_specs = [row(N_QKV)]
    for d in dils:
        out_shape.append(jax.ShapeDtypeStruct((d, n // d, N_QKV), BF16))
        out_specs.append(pl.BlockSpec((d, tile // d, N_QKV), lambda i: (0, i, 0)))
    out_shape += [jax.ShapeDtypeStruct((tail_rows, N_QKV), F32),
                  jax.ShapeDtypeStruct((n, D_M), BF16), jax.ShapeDtypeStruct((n, D_M), BF16),
                  jax.ShapeDtypeStruct((n, D_M), BF16), jax.ShapeDtypeStruct((n, D_M), F32),
                  jax.ShapeDtypeStruct((n, LANES), F32)]
    out_specs += [pl.BlockSpec((tile, N_QKV), lambda i: (jnp.maximum(i - first_tail, 0), 0)),
                  row(D_M), row(D_M), row(D_M), row(D_M), row(LANES)]
    return pl.pallas_call(
        functools.partial(_inproj_kernel, tile=tile, dils=dils, first_tail=first_tail),
        grid=(steps,),
        in_specs=[row(D_MODEL), _const_spec((1, D_MODEL)), _const_spec((D_MODEL, N_QKV)),
                  _const_spec((D_MODEL, N_MIX)), _const_spec((D_MODEL, LANES)), _const_spec((1, LANES))],
        out_specs=out_specs,
        out_shape=out_shape,
        scratch_shapes=[pltpu.VMEM((N_QKV // LANES, tile, LANES), F32)],
        compiler_params=_params(("arbitrary",)),
        name="inproj",
    )(x, g, wa, wm, wg, bg)


def _attn_prompt_kernel(*refs):
    np_ = len(PATTERNS)
    in_refs = refs[:5 * np_]
    bias_ref, o_ref = refs[5 * np_], refs[5 * np_ + 1]
    scr = refs[5 * np_ + 2:]
    kbufs, vbufs = scr[0:np_], scr[np_:2 * np_]
    num_s, l_s, m_s = scr[2 * np_:]
    n = pl.program_id(0)

    lane = lax.broadcasted_iota(jnp.int32, (SPAN, LANES), 1)
    even = lane < HD_A
    ones = jnp.ones((2 * SPAN, LANES), BF16)

    for p, (_, dil) in enumerate(PATTERNS):
        q_ref, k_ref, v_ref, kp_ref, vp_ref = in_refs[5 * p:5 * p + 5]
        kbuf, vbuf = kbufs[p], vbufs[p]
        nblk = SUPER // dil // SPAN
        kbuf[:, 0:SPAN, :] = kp_ref[...]
        kbuf[:, SPAN:, :] = k_ref[...]
        vbuf[:, 0:SPAN, :] = vp_ref[...]
        vbuf[:, SPAN:, :] = v_ref[...]

        def unit(j, carry, p=p, dil=dil, nblk=nblk, q_ref=q_ref, kbuf=kbuf, vbuf=vbuf):
            r = j // nblk
            mb = j % nblk
            row0 = pl.multiple_of(mb * SPAN, SPAN)
            qb = q_ref[r, pl.ds(row0, SPAN), :]
            kb = kbuf[r, pl.ds(row0, 2 * SPAN), :]
            vb = vbuf[r, pl.ds(row0, 2 * SPAN), :]
            first = jnp.logical_and(n == 0, mb == 0).astype(jnp.int32)
            zero = jnp.zeros_like(qb)
            qs = jnp.concatenate([jnp.where(even, qb, zero), jnp.where(even, zero, qb)], axis=0)
            s = lax.dot_general(qs, kb, (((1,), (1,)), ((), ())), preferred_element_type=F32)
            s = s + bias_ref[p, 0, first]
            m = jnp.max(s, axis=-1, keepdims=True)
            pr = jnp.exp(s - m).astype(BF16)
            res = jnp.dot(pr, jnp.concatenate([vb, ones], axis=1), preferred_element_type=F32)
            num = jnp.where(even, res[:SPAN, :LANES], res[SPAN:, :LANES])
            den = jnp.where(even, res[:SPAN, LANES:], res[SPAN:, LANES:])
            mb_ = jnp.broadcast_to(m, (2 * SPAN, LANES))
            mm = jnp.where(even, mb_[:SPAN], mb_[SPAN:])
            if dil == 1:
                idx = pl.ds(row0, SPAN)
            else:
                idx = pl.ds(row0 * dil + r, SPAN, stride=dil)
            num_s[p, idx, :] = num
            l_s[p, idx, :] = den
            m_s[p, idx, :] = mm
            return carry

        lax.fori_loop(0, dil * nblk, unit, 0)

    def combine(j, carry):
        rows = pl.ds(pl.multiple_of(j * SPAN, SPAN), SPAN)
        ms = [m_s[p, rows, :] for p in range(np_)]
        m = functools.reduce(jnp.maximum, ms)
        num = jnp.zeros((SPAN, LANES), F32)
        den = jnp.zeros((SPAN, LANES), F32)
        for p in range(np_):
            a = jnp.exp(ms[p] - m)
            num = num + a * num_s[p, rows, :]
            den = den + a * l_s[p, rows, :]
        o_ref[rows, :] = (num / den).astype(o_ref.dtype)
        return carry

    lax.fori_loop(0, SUPER // SPAN, combine, 0)


def _attn_prompt(qkvs, bias):
    s_len = qkvs[0].shape[1]
    assert s_len % SUPER == 0
    nsb = s_len // SUPER
    npair = D_A // LANES
    in_specs, args, scratch = [], [], []
    for (_, dil), a in zip(PATTERNS, qkvs):
        rows = SUPER // dil
        nblk = rows // SPAN
        for part in range(3):
            in_specs.append(pl.BlockSpec((dil, rows, LANES),
                                         lambda n, hp, part=part: (0, n, part * npair + hp)))
            args.append(a)
        for part in (1, 2):
            in_specs.append(pl.BlockSpec((dil, SPAN, LANES),
                                         lambda n, hp, part=part, nblk=nblk:
                                         (0, jnp.maximum(n * nblk - 1, 0), part * npair + hp)))
            args.append(a)
    in_specs.append(pl.BlockSpec((len(PATTERNS), 1, 2, 2 * SPAN, 2 * SPAN), lambda n, hp: (0, hp, 0, 0, 0)))
    args.append(bias)
    for _ in range(2):
        for _, dil in PATTERNS:
            scratch.append(pltpu.VMEM((dil, SPAN + SUPER // dil, LANES), BF16))
    scratch += [pltpu.VMEM((len(PATTERNS), SUPER, LANES), F32)] * 3
    return pl.pallas_call(
        _attn_prompt_kernel,
        grid=(nsb, npair),
        in_specs=in_specs,
        out_specs=pl.BlockSpec((SUPER, LANES), lambda n, hp: (n, hp)),
        out_shape=jax.ShapeDtypeStruct((s_len, D_A), BF16),
        scratch_shapes=scratch,
        compiler_params=_params(("arbitrary", "arbitrary")),
        name="attn_prompt",
    )(*args)


def _attn_sample_kernel(qkv_ref, kc_ref, vc_ref, bc_ref, bn_ref, att_ref, ko_ref, vo_ref, *, t_new, wb):
    qkv = qkv_ref[0]
    qf, kn, vn = qkv[:, 0:D_A], qkv[:, D_A:2 * D_A], qkv[:, 2 * D_A:3 * D_A]
    nrow = H_A * t_new
    row_h = lax.broadcasted_iota(jnp.int32, (nrow, D_A), 0) // t_new
    lane_h = lax.broadcasted_iota(jnp.int32, (nrow, D_A), 1) // HD_A
    own = row_h == lane_h
    qs = jnp.where(own, jnp.concatenate([qf] * H_A, axis=0), 0.0).astype(BF16)
    pad = jnp.zeros((LANES - t_new, D_A), F32)
    knp = jnp.concatenate([kn, pad], axis=0).astype(BF16)
    vnp = jnp.concatenate([vn, pad], axis=0).astype(BF16)
    kc = kc_ref[0].astype(BF16)
    vc = vc_ref[0].astype(BF16)
    nt = (((1,), (1,)), ((), ()))
    s_c = lax.dot_general(qs, kc, nt, preferred_element_type=F32)
    s_n = lax.dot_general(qs, knp, nt, preferred_element_type=F32)

    es_c, es_n = [], []
    m = None
    for p, (win, _) in enumerate(PATTERNS):
        e_c = s_c[:, wb - win:] + bc_ref[p, :, wb - win:]
        e_n = s_n + bn_ref[p]
        es_c.append(e_c)
        es_n.append(e_n)
        mp = jnp.maximum(jnp.max(e_c, axis=-1, keepdims=True), jnp.max(e_n, axis=-1, keepdims=True))
        m = mp if m is None else jnp.maximum(m, mp)
    pn = None
    pieces = []
    wins = [w for w, _ in PATTERNS]
    for p, (win, _) in enumerate(PATTERNS):
        e = jnp.exp(es_n[p] - m)
        pn = e if pn is None else pn + e
    order = sorted(range(len(PATTERNS)), key=lambda p: -wins[p])
    pc = jnp.exp(es_c[order[0]] - m)
    assert wins[order[0]] == wb
    for p in order[1:]:
        e = jnp.exp(es_c[p] - m)
        pc = jnp.concatenate([pc[:, :wb - wins[p]], pc[:, wb - wins[p]:] + e], axis=1)
    den = jnp.sum(pc, axis=-1, keepdims=True) + jnp.sum(pn, axis=-1, keepdims=True)
    num = (jnp.dot(pc.astype(BF16), vc, preferred_element_type=F32)
           + jnp.dot(pn.astype(BF16), vnp, preferred_element_type=F32))
    o = jnp.where(own, num / den, 0.0)
    att = o[0:t_new]
    for h in range(1, H_A):
        att = att + o[h * t_new:(h + 1) * t_new]
    att_ref[0] = att.astype(att_ref.dtype)

    ko_ref[0, 0:wb - t_new, :] = kc_ref[0, t_new:wb, :]
    ko_ref[0, wb - t_new:wb, :] = kn
    vo_ref[0, 0:wb - t_new, :] = vc_ref[0, t_new:wb, :]
    vo_ref[0, wb - t_new:wb, :] = vn


def _attn_sample(qkv, cache_k, cache_v, bias_c, bias_n):
    b, t_new, _ = qkv.shape
    wb = cache_k.shape[1]
    assert wb == max(w for w, _ in PATTERNS) and t_new % 8 == 0 and t_new <= LANES
    nrow = H_A * t_new
    blk = lambda r, w: pl.BlockSpec((1, r, w), lambda i: (i, 0, 0))
    return pl.pallas_call(
        functools.partial(_attn_sample_kernel, t_new=t_new, wb=wb),
        grid=(b,),
        in_specs=[blk(t_new, N_QKV), blk(wb, D_A), blk(wb, D_A),
                  _const_spec((len(PATTERNS), nrow, wb)), _const_spec((len(PATTERNS), nrow, LANES))],
        out_specs=[blk(t_new, D_A), blk(wb, D_A), blk(wb, D_A)],
        out_shape=[jax.ShapeDtypeStruct((b, t_new, D_A), BF16),
                   jax.ShapeDtypeStruct((b, wb, D_A), F32), jax.ShapeDtypeStruct((b, wb, D_A), F32)],
        compiler_params=_params(("arbitrary",)),
        name="attn_sample",
    )(qkv, cache_k, cache_v, bias_c, bias_n)


def _split3(x):
    a = x.astype(BF16)
    r = x - a.astype(F32)
    b = r.astype(BF16)
    c = (r - b.astype(F32)).astype(BF16)
    return a, b, c


def _mlstm_kernel(q_ref, k_ref, v_ref, o_ref, g_ref, gw_ref, cx0_ref, m0_ref,
                  h_ref, cx_out, m_out, cx_s, m_s, *, c):
    ci = pl.program_id(1)

    @pl.when(ci == 0)
    def _():
        cx_s[...] = cx0_ref[0]
        m_s[...] = m0_ref[0]

    gates = g_ref[...]
    lane = lax.broadcasted_iota(jnp.int32, gates.shape, 1)
    lf = jnp.where(jnp.logical_and(lane >= H_M, lane < 2 * H_M), gates, 0.0)
    ri = lax.broadcasted_iota(jnp.int32, (c, c), 0)
    cj = lax.broadcasted_iota(jnp.int32, (c, c), 1)
    causal = ri >= cj
    tri = causal.astype(BF16)
    bsum = None
    for part in _split3(lf):
        t = jnp.dot(tri, part, preferred_element_type=F32)
        bsum = t if bsum is None else bsum + t
    gates_t = gates.T
    bsum_t = bsum.T
    ones = jnp.ones((c, DV_M), BF16)

    for h in range(H_M):
        sl = slice(h * DK_M, (h + 1) * DK_M)
        qh, kh, vh = q_ref[:, sl], k_ref[:, sl], v_ref[:, sl]
        i_row = gates_t[h:h + 1, :]
        b_row = bsum_t[H_M + h:H_M + h + 1, :]
        i_col = gates[:, h:h + 1]
        b_col = bsum[:, H_M + h:H_M + h + 1]
        m_prev = m_s[h, 0:1, 0:1]
        dmat = jnp.where(causal, b_col - b_row + i_row, NEG_INF)
        inter = b_col + m_prev
        mt = jnp.maximum(inter, jnp.max(dmat, axis=-1, keepdims=True))
        s = lax.dot_general(qh, kh, (((1,), (1,)), ((), ())), preferred_element_type=F32)
        sm = (s * jnp.exp(dmat - mt)).astype(BF16)
        iw = jnp.exp(inter - mt)
        vext = jnp.concatenate([vh, ones], axis=1)
        cxh = cx_s[h]
        ne = (jnp.dot(sm, vext, preferred_element_type=F32)
              + iw * jnp.dot(qh, cxh.astype(BF16), preferred_element_type=F32))
        num, den = ne[:, :DV_M], ne[:, DV_M:]
        hq = num / jnp.maximum(jnp.abs(den), jnp.exp(-mt))
        hn = hq * lax.rsqrt(jnp.mean(hq * hq, axis=-1, keepdims=True) + RMS_EPS) * gw_ref[:, sl]
        h_ref[:, sl] = (jax.nn.sigmoid(o_ref[:, sl]) * hn).astype(h_ref.dtype)

        b_last = bsum[c - 1:c, H_M + h:H_M + h + 1]
        g_col = b_last - b_col + i_col
        m_new = jnp.maximum(b_last + m_prev, jnp.max(g_col, axis=0, keepdims=True))
        ws = jnp.exp(g_col - m_new)
        wc = jnp.exp(b_last + m_prev - m_new)
        kw = (kh.astype(F32) * ws).astype(BF16)
        upd = lax.dot_general(kw, vext, (((0,), (0,)), ((), ())), preferred_element_type=F32)
        cx_s[h] = wc * cxh + upd
        m_s[h] = jnp.broadcast_to(m_new, m_s.shape[1:])

    @pl.when(ci == pl.num_programs(1) - 1)
    def _():
        cx_out[0] = cx_s[...]
        m_out[0] = m_s[...]


def _mlstm(qm, km, vm, om, gates, gw, cx0, m0, *, nb, c):
    n = qm.shape[0]
    nc = n // (nb * c)
    assert nb * nc * c == n
    row = lambda w: pl.BlockSpec((c, w), lambda b, i: (b * nc + i, 0))
    st = lambda shape: pl.BlockSpec((1,) + shape, lambda b, i: (b,) + (0,) * len(shape))
    return pl.pallas_call(
        functools.partial(_mlstm_kernel, c=c),
        grid=(nb, nc),
        in_specs=[row(D_M), row(D_M), row(D_M), row(D_M), row(LANES), _const_spec((1, D_M)),
                  st((H_M, DK_M, 2 * DV_M)), st((H_M, 8, LANES))],
        out_specs=[row(D_M), st((H_M, DK_M, 2 * DV_M)), st((H_M, 8, LANES))],
        out_shape=[jax.ShapeDtypeStruct((n, D_M), BF16),
                   jax.ShapeDtypeStruct((nb, H_M, DK_M, 2 * DV_M), F32),
                   jax.ShapeDtypeStruct((nb, H_M, 8, LANES), F32)],
        scratch_shapes=[pltpu.VMEM((H_M, DK_M, 2 * DV_M), F32), pltpu.VMEM((H_M, 8, LANES), F32)],
        compiler_params=_params(("arbitrary", "arbitrary")),
        name="mlstm",
    )(qm, km, vm, om, gates, gw, cx0, m0)


def _outffn_kernel(att_ref, hm_ref, x_ref, prev_ref, wo_ref, gpost_ref, gpre_ref, wup_ref, cw_ref, cb_ref,
                   wdn_ref, gffn_ref, y_ref, st_ref, u_ref, a_ref, *, tile, shift, hdr):
    i = pl.program_id(0)

    @pl.when(i == 0)
    def _():
        u_ref[hdr - 2 * shift:hdr, :] = prev_ref[...]

    @pl.when(i > 0)
    def _():
        u_ref[0:hdr, :] = u_ref[tile:tile + hdr, :]

    mix_in = jnp.concatenate([att_ref[...], hm_ref[...]], axis=1)
    mixed = jnp.dot(mix_in, wo_ref[...], preferred_element_type=F32)
    x1 = x_ref[...] + _rms(mixed, gpost_ref[...])
    h2 = _rms(x1, gpre_ref[...]).astype(BF16)
    u_ref[hdr:hdr + tile, :] = jnp.dot(h2, wup_ref[...], preferred_element_type=F32)

    def conv(c):
        cols = slice(c * LANES, (c + 1) * LANES)
        y = cb_ref[:, cols] + u_ref[hdr - 2 * shift:hdr - 2 * shift + tile, cols] * cw_ref[0:1, cols]
        y = y + u_ref[hdr - shift:hdr - shift + tile, cols] * cw_ref[1:2, cols]
        return y + u_ref[hdr:hdr + tile, cols] * cw_ref[2:3, cols]

    nff = D_FF // LANES
    for c in range(nff):
        gate, val = conv(c), conv(nff + c)
        gelu = 0.5 * gate * (1.0 + jnp.tanh(math.sqrt(2.0 / math.pi) * (gate + 0.044715 * gate * gate * gate)))
        a_ref[:, c * LANES:(c + 1) * LANES] = (gelu * val).astype(BF16)

    y2 = jnp.dot(a_ref[...], wdn_ref[...], preferred_element_type=F32)
    y_ref[...] = x1 + _rms(y2, gffn_ref[...])
    st_ref[...] = u_ref[tile:tile + hdr, :]


def _outffn(att, hm, x, prev, wo, gpost, gpre, wup, cw, cb, wdn, gffn, *, tile, shift):
    n = x.shape[0]
    assert n % tile == 0 and (shift == 1 or n == tile)
    hdr = max(8, 2 * shift)
    assert hdr % 8 == 0 and tile % 8 == 0
    row = lambda w: pl.BlockSpec((tile, w), lambda i: (i, 0))
    return pl.pallas_call(
        functools.partial(_outffn_kernel, tile=tile, shift=shift, hdr=hdr),
        grid=(n // tile,),
        in_specs=[row(D_A), row(D_M), row(D_MODEL), _const_spec((2 * shift, 2 * D_FF)),
                  _const_spec((D_A + D_M, D_MODEL)), _const_spec((1, D_MODEL)), _const_spec((1, D_MODEL)),
                  _const_spec((D_MODEL, 2 * D_FF)), _const_spec((CONV_W, 2 * D_FF)), _const_spec((1, 2 * D_FF)),
                  _const_spec((D_FF, D_MODEL)), _const_spec((1, D_MODEL))],
        out_specs=[row(D_MODEL), pl.BlockSpec((hdr, 2 * D_FF), lambda i: (0, 0))],
        out_shape=[jax.ShapeDtypeStruct((n, D_MODEL), F32), jax.ShapeDtypeStruct((hdr, 2 * D_FF), F32)],
        scratch_shapes=[pltpu.VMEM((hdr + tile, 2 * D_FF), F32), pltpu.VMEM((tile, D_FF), BF16)],
        compiler_params=_params(("arbitrary",)),
        name="outffn",
    )(att, hm, x, prev, wo, gpost, gpre, wup, cw, cb, wdn, gffn)


def _t5_bucket(dist):
    max_exact = N_BUCKETS // 2
    d = jnp.maximum(dist, max_exact).astype(F32)
    large = max_exact + (jnp.log(d / max_exact) / math.log(BUCKET_MAX_DIST / max_exact)
                         * (N_BUCKETS - max_exact)).astype(jnp.int32)
    large = jnp.minimum(large, N_BUCKETS - 1)
    return jnp.where(dist < max_exact, dist, large)


def _pattern_bias(rel_bias, dil):
    j = jnp.arange(SPAN + 1)
    return rel_bias[_t5_bucket(j * dil)].T.astype(F32)


def _prompt_bias(rel_bias):
    qi = jnp.arange(SPAN)[:, None]
    ki = jnp.arange(2 * SPAN)[None, :]
    steps = SPAN + qi - ki
    valid = (steps >= 0) & (steps <= SPAN)
    out = []
    for _, dil in PATTERNS:
        b = _pattern_bias(rel_bias, dil)[:, jnp.clip(steps, 0, SPAN)]
        norm = jnp.where(valid[None], b, NEG_INF)
        first = jnp.where((valid & (ki >= SPAN))[None], b, NEG_INF)
        both = jnp.stack([norm, first], axis=1)
        both = both.reshape(H_A // 2, 2, 2, SPAN, 2 * SPAN).transpose(0, 2, 1, 3, 4)
        out.append(both.reshape(H_A // 2, 2, 2 * SPAN, 2 * SPAN))
    return jnp.stack(out)


def _sample_bias(rel_bias, t_new, wb):
    t = jnp.arange(t_new)[:, None]
    cs, ns = [], []
    for win, dil in PATTERNS:
        bh = _pattern_bias(rel_bias, dil)
        def table(rows):
            dist = wb + t - rows[None, :]
            ok = (dist >= 0) & (dist % dil == 0) & (dist <= win)
            b = bh[:, jnp.clip(dist // dil, 0, SPAN)]
            return jnp.where(ok[None], b, NEG_INF).reshape(H_A * t_new, rows.shape[0])
        cs.append(table(jnp.arange(wb)))
        new = table(wb + jnp.arange(LANES))
        ns.append(jnp.where(jnp.arange(LANES)[None, :] < t_new, new, NEG_INF))
    return jnp.stack(cs), jnp.stack(ns)


def _layer_weights(g_mix_pre, w_in, b_igate, b_fgate, g_mlstm_out, w_out, g_mix_post, g_ffn_pre, w_up,
                   conv_w, conv_b, w_down, g_ffn_post):
    wg = jnp.zeros((D_MODEL, LANES), F32).at[:, :2 * H_M].set(w_in[:, N_QKV + N_MIX:])
    bg = jnp.zeros((1, LANES), F32).at[0, :H_M].set(b_igate.astype(F32)).at[0, H_M:2 * H_M].set(b_fgate.astype(F32))
    row = lambda v: v.astype(F32).reshape(1, -1)
    return dict(
        g_pre=row(g_mix_pre), wa=w_in[:, :N_QKV].astype(BF16), wm=w_in[:, N_QKV:N_QKV + N_MIX].astype(BF16),
        wg=wg.astype(BF16), bg=bg, gw=row(g_mlstm_out), wo=w_out.astype(BF16), g_post=row(g_mix_post),
        g_ffn_pre=row(g_ffn_pre), wup=w_up.astype(BF16), cw=conv_w.astype(F32), cb=row(conv_b),
        wdn=w_down.astype(BF16), g_ffn_post=row(g_ffn_post))


def _state_in(c0, n0, m0):
    nb = c0.shape[0]
    cx = jnp.concatenate([c0.astype(F32), jnp.broadcast_to(n0.astype(F32)[..., None], c0.shape)], axis=-1)
    m = jnp.broadcast_to(m0.astype(F32)[..., None, None], (nb, H_M, 8, LANES))
    return cx, m


def _state_out(cx, m):
    return cx[..., :DV_M], cx[..., DV_M], m[..., 0, 0]


def _prompt_layer(x, w, bias, *, tile_in, tile_ffn):
    s_len = x.shape[0]
    wb = min(max(wd for wd, _ in PATTERNS), s_len)
    dils = tuple(d for _, d in PATTERNS if d > 1)
    outs = _inproj(x, w["g_pre"], w["wa"], w["wm"], w["wg"], w["bg"], tile=tile_in, dils=dils, tail_rows=wb)
    qkv1, qkv4, qkv16, tail, qm, km, vm, om, gates = outs
    att = _attn_prompt([qkv1.reshape(1, s_len, N_QKV), qkv4, qkv16], bias)
    cx0, m0 = _state_in(jnp.zeros((1, H_M, DK_M, DV_M), F32), jnp.zeros((1, H_M, DK_M), F32),
                        jnp.zeros((1, H_M), F32))
    hm, cx, m = _mlstm(qm, km, vm, om, gates, w["gw"], cx0, m0, nb=1, c=math.gcd(s_len, MLSTM_CHUNK))
    prev = jnp.zeros((CONV_W - 1, 2 * D_FF), F32)
    y, st = _outffn(att, hm, x, prev, w["wo"], w["g_post"], w["g_ffn_pre"], w["wup"], w["cw"], w["cb"],
                    w["wdn"], w["g_ffn_post"], tile=tile_ffn, shift=1)
    c_new, n_new, m_new = _state_out(cx, m)
    k_win = tail[:, D_A:2 * D_A].reshape(1, wb, H_A, HD_A)
    v_win = tail[:, 2 * D_A:3 * D_A].reshape(1, wb, H_A, HD_A)
    return y, (k_win, v_win, c_new, n_new, m_new, st[-(CONV_W - 1):][None])


def _sample_layer(x, w, bias_c, bias_n, cache_k, cache_v, c0, n0, m0, conv_prev):
    b, t_new, _ = x.shape
    n = b * t_new
    wb = cache_k.shape[1]
    outs = _inproj(x.reshape(n, D_MODEL), w["g_pre"], w["wa"], w["wm"], w["wg"], w["bg"],
                   tile=n, dils=(), tail_rows=n)
    _, tail, qm, km, vm, om, gates = outs
    att, k_win, v_win = _attn_sample(tail.reshape(b, t_new, N_QKV), cache_k.reshape(b, wb, D_A),
                                     cache_v.reshape(b, wb, D_A), bias_c, bias_n)

    c = MLSTM_CHUNK
    padr = lambda a: jnp.pad(a.reshape(b, t_new, -1), ((0, 0), (0, c - t_new), (0, 0))).reshape(b * c, -1)
    lane = jnp.arange(LANES)
    gpad = jnp.where(lane < H_M, NEG_INF, 0.0).astype(F32)
    gates_p = jnp.concatenate([gates.reshape(b, t_new, LANES),
                               jnp.broadcast_to(gpad, (b, c - t_new, LANES))], axis=1).reshape(b * c, LANES)
    cx0, m0b = _state_in(c0, n0, m0)
    hm_p, cx, m = _mlstm(padr(qm), padr(km), padr(vm), padr(om), gates_p, w["gw"], cx0, m0b, nb=b, c=c)
    hm = hm_p.reshape(b, c, D_M)[:, :t_new]

    tm = lambda a: a.reshape(b, t_new, -1).transpose(1, 0, 2).reshape(n, -1)
    prev = conv_prev.astype(F32).transpose(1, 0, 2).reshape((CONV_W - 1) * b, 2 * D_FF)
    y, st = _outffn(tm(att), tm(hm), tm(x), prev, w["wo"], w["g_post"], w["g_ffn_pre"], w["wup"], w["cw"],
                    w["cb"], w["wdn"], w["g_ffn_post"], tile=n, shift=b)
    y = y.reshape(t_new, b, D_MODEL).transpose(1, 0, 2)
    conv_state = st.reshape(CONV_W - 1, b, 2 * D_FF).transpose(1, 0, 2)
    c_new, n_new, m_new = _state_out(cx, m)
    return y, (k_win.reshape(b, wb, H_A, HD_A), v_win.reshape(b, wb, H_A, HD_A), c_new, n_new, m_new, conv_state)


def kernel(x_prompt, x_sample, cache_attn_k, cache_attn_v, state_mlstm_C, state_mlstm_n, state_mlstm_m,
           state_ffn_conv, rel_bias, g_mix_pre, w_in, b_igate, b_fgate, g_mlstm_out, w_out, g_mix_post,
           g_ffn_pre, w_up, conv_w, conv_b, w_down, g_ffn_post):
    depth = w_in.shape[0]
    batch, s_len, _ = x_prompt.shape
    assert batch == 1
    t_new = x_sample.shape[1]
    wb = cache_attn_k.shape[2]
    bias_p = _prompt_bias(rel_bias)
    bias_c, bias_n = _sample_bias(rel_bias, t_new, wb)
    yp = x_prompt[0]
    ys = x_sample
    new_p, new_s = [], []
    for l in range(depth):
        w = _layer_weights(g_mix_pre[l], w_in[l], b_igate[l], b_fgate[l], g_mlstm_out[l], w_out[l],
                           g_mix_post[l], g_ffn_pre[l], w_up[l], conv_w[l], conv_b[l], w_down[l], g_ffn_post[l])
        yp, sp = _prompt_layer(yp, w, bias_p, tile_in=512, tile_ffn=256)
        ys, ss = _sample_layer(ys, w, bias_c, bias_n, cache_attn_k[l], cache_attn_v[l], state_mlstm_C[l],
                               state_mlstm_n[l], state_mlstm_m[l], state_ffn_conv[l])
        new_p.append(sp)
        new_s.append(ss)
    stack = lambda states, i: jnp.stack([s[i] for s in states])
    return ((yp[None], ys) + tuple(stack(new_p, i) for i in range(6))
            + tuple(stack(new_s, i) for i in range(6)))
```

```python
import functools
import math

import jax
import jax.numpy as jnp
import numpy as np
from jax import lax
from jax.experimental import pallas as pl
from jax.experimental.pallas import tpu as pltpu

F32 = jnp.float32
BF16 = jnp.bfloat16

D_MODEL = 1024
HD_A = 64
H_A = 8
D_A = H_A * HD_A
DK_M = 128
DV_M = 128
H_M = 4
D_M = H_M * DV_M
PATTERNS = ((128, 1), (512, 4), (2048, 16))
SPAN = 128
N_BUCKETS = 32
BUCKET_MAX_DIST = 2048
MLSTM_CHUNK = 128
D_FF = 2816
CONV_W = 3
RMS_EPS = 1e-6
N_QKV = 3 * D_A
N_MIX = 4 * D_M
LANES = 128
SUPER = 2048
VMEM_LIMIT = 56 * 1024 * 1024
NEG_INF = float("-inf")

for _w, _d in PATTERNS:
    assert _w // _d == SPAN


def _params(sem, vmem=VMEM_LIMIT):
    return pltpu.CompilerParams(dimension_semantics=sem, vmem_limit_bytes=vmem)


def _const_spec(shape):
    nd = len(shape)
    return pl.BlockSpec(shape, lambda *_: (0,) * nd, pipeline_mode=pl.Buffered(1))


def _rms(x, g):
    return x * lax.rsqrt(jnp.mean(x * x, axis=-1, keepdims=True) + RMS_EPS) * g


def _inproj_kernel(x_ref, g_ref, wa_ref, wm_ref, wg_ref, bg_ref, *refs, tile, dils, first_tail):
    nd = len(dils)
    qkv1_ref = refs[0]
    dil_refs = refs[1:1 + nd]
    tail_ref, qm_ref, km_ref, vm_ref, om_ref, gt_ref, zs_ref = refs[1 + nd:]
    i = pl.program_id(0)

    hb = _rms(x_ref[...], g_ref[...]).astype(BF16)
    za = jnp.dot(hb, wa_ref[...], preferred_element_type=F32)
    for c in range(N_QKV // LANES):
        col = za[:, c * LANES:(c + 1) * LANES]
        if c < D_A // LANES:
            col = col * (HD_A ** -0.5)
        zs_ref[c] = col
        qkv1_ref[:, c * LANES:(c + 1) * LANES] = col.astype(BF16)

    for dil, out_ref in zip(dils, dil_refs):
        for r in range(dil):
            for c in range(N_QKV // LANES):
                rows = zs_ref[c, pl.ds(r, tile // dil, stride=dil), :]
                out_ref[r, :, c * LANES:(c + 1) * LANES] = rows.astype(BF16)

    @pl.when(i >= first_tail)
    def _():
        for c in range(N_QKV // LANES):
            tail_ref[:, c * LANES:(c + 1) * LANES] = zs_ref[c]

    zm = jnp.dot(hb, wm_ref[...], preferred_element_type=F32)
    qm_ref[...] = zm[:, 0:D_M].astype(BF16)
    km_ref[...] = (zm[:, D_M:2 * D_M] * (DK_M ** -0.5)).astype(BF16)
    vm_ref[...] = zm[:, 2 * D_M:3 * D_M].astype(BF16)
    om_ref[...] = zm[:, 3 * D_M:4 * D_M]

    zg = jnp.dot(hb, wg_ref[...], preferred_element_type=F32) + bg_ref[...]
    logsig = jnp.minimum(zg, 0.0) - jnp.log1p(jnp.exp(-jnp.abs(zg)))
    lane = lax.broadcasted_iota(jnp.int32, zg.shape, 1)
    gt_ref[...] = jnp.where(lane < H_M, zg, logsig)


def _inproj(x, g, wa, wm, wg, bg, *, tile, dils, tail_rows):
    n = x.shape[0]
    assert n % tile == 0 and tail_rows % tile == 0
    for d in dils:
        assert tile % (16 * d) == 0
    steps = n // tile
    first_tail = (n - tail_rows) // tile
    row = lambda w: pl.BlockSpec((tile, w), lambda i: (i, 0))
    out_shape = [jax.ShapeDtypeStruct((n, N_QKV), BF16)]
    out_specs = [row(N_QKV)]
    for d in dils:
        out_shape.append(jax.ShapeDtypeStruct((d, n // d, N_QKV), BF16))
        out_specs.append(pl.BlockSpec((d, tile // d, N_QKV), lambda i: (0, i, 0)))
    out_shape += [jax.ShapeDtypeStruct((tail_rows, N_QKV), F32),
                  jax.ShapeDtypeStruct((n, D_M), BF16), jax.ShapeDtypeStruct((n, D_M), BF16),
                  jax.ShapeDtypeStruct((n, D_M), BF16), jax.ShapeDtypeStruct((n, D_M), F32),
                  jax.ShapeDtypeStruct((n, LANES), F32)]
    out_specs += [pl.BlockSpec((tile, N_QKV), lambda i: (jnp.maximum(i - first_tail, 0), 0)),
                  row(D_M), row(D_M), row(D_M), row(D_M), row(LANES)]
    return pl.pallas_call(
        functools.partial(_inproj_kernel, tile=tile, dils=dils, first_tail=first_tail),
        grid=(steps,),
        in_specs=[row(D_MODEL), _const_spec((1, D_MODEL)), _const_spec((D_MODEL, N_QKV)),
                  _const_spec((D_MODEL, N_MIX)), _const_spec((D_MODEL, LANES)), _const_spec((1, LANES))],
        out_specs=out_specs,
        out_shape=out_shape,
        scratch_shapes=[pltpu.VMEM((N_QKV // LANES, tile, LANES), F32)],
        compiler_params=_params(("arbitrary",)),
        name="inproj",
    )(x, g, wa, wm, wg, bg)


def _attn_prompt_kernel(*refs):
    np_ = len(PATTERNS)
    in_refs = refs[:5 * np_]
    bvec_ref, o_ref = refs[5 * np_], refs[5 * np_ + 1]
    scr = refs[5 * np_ + 2:]
    kbufs, vbufs = scr[0:np_], scr[np_:2 * np_]
    num_s, l_s, m_s, bias_s = scr[2 * np_:]
    n = pl.program_id(0)

    lane = lax.broadcasted_iota(jnp.int32, (SPAN, LANES), 1)
    even = lane < HD_A
    ones = jnp.ones((2 * SPAN, LANES), BF16)
    key_col = lax.broadcasted_iota(jnp.int32, (SPAN, 2 * SPAN), 1)

    for p in range(np_):
        q_ref, k_ref, v_ref, kp_ref, vp_ref = in_refs[5 * p:5 * p + 5]
        kbufs[p][:, 0:SPAN, :] = kp_ref[...]
        kbufs[p][:, SPAN:, :] = k_ref[...]
        vbufs[p][:, 0:SPAN, :] = vp_ref[...]
        vbufs[p][:, SPAN:, :] = v_ref[...]
        for e in range(2):
            vb = jnp.broadcast_to(bvec_ref[p, 0, e:e + 1, :], (SPAN, 2 * SPAN))
            table = pltpu.roll(vb, 0, 1, stride=1, stride_axis=0)
            bias_s[p, 0, e * SPAN:(e + 1) * SPAN, :] = table
            bias_s[p, 1, e * SPAN:(e + 1) * SPAN, :] = jnp.where(key_col >= SPAN, table, NEG_INF)

    def unit(p, j):
        dil = PATTERNS[p][1]
        nblk = SUPER // dil // SPAN
        q_ref = in_refs[5 * p]
        r = j // nblk
        mb = j % nblk
        row0 = pl.multiple_of(mb * SPAN, SPAN)
        qb = q_ref[r, pl.ds(row0, SPAN), :]
        kb = kbufs[p][r, pl.ds(row0, 2 * SPAN), :]
        vb = vbufs[p][r, pl.ds(row0, 2 * SPAN), :]
        first = jnp.logical_and(n == 0, mb == 0).astype(jnp.int32)
        zero = jnp.zeros_like(qb)
        qs = jnp.concatenate([jnp.where(even, qb, zero), jnp.where(even, zero, qb)], axis=0)
        s = lax.dot_general(qs, kb, (((1,), (1,)), ((), ())), preferred_element_type=F32)
        s = s + bias_s[p, first]
        m = jnp.max(s, axis=-1, keepdims=True)
        pr = jnp.exp(s - m).astype(BF16)
        res = jnp.dot(pr, jnp.concatenate([vb, ones], axis=1), preferred_element_type=F32)
        num = jnp.where(even, res[:SPAN, :LANES], res[SPAN:, :LANES])
        den = jnp.where(even, res[:SPAN, LANES:], res[SPAN:, LANES:])
        mb_ = jnp.broadcast_to(m, (2 * SPAN, LANES))
        mm = jnp.where(even, mb_[:SPAN], mb_[SPAN:])
        if dil == 1:
            idx = pl.ds(row0, SPAN)
        else:
            idx = pl.ds(row0 * dil + r, SPAN, stride=dil)
        num_s[p, idx, :] = num
        l_s[p, idx, :] = den
        m_s[p, idx, :] = mm

    def body(j, carry):
        for p in range(np_):
            unit(p, j)
        return carry

    lax.fori_loop(0, SUPER // SPAN, body, 0)

    def combine(j, carry):
        rows = pl.ds(pl.multiple_of(j * SPAN, SPAN), SPAN)
        ms = [m_s[p, rows, :] for p in range(np_)]
        m = functools.reduce(jnp.maximum, ms)
        num = jnp.zeros((SPAN, LANES), F32)
        den = jnp.zeros((SPAN, LANES), F32)
        for p in range(np_):
            a = jnp.exp(ms[p] - m)
            num = num + a * num_s[p, rows, :]
            den = den + a * l_s[p, rows, :]
        o_ref[rows, :] = (num / den).astype(o_ref.dtype)
        return carry

    lax.fori_loop(0, SUPER // SPAN, combine, 0)


def _attn_prompt(qkvs, bvec):
    s_len = qkvs[0].shape[1]
    assert s_len % SUPER == 0
    nsb = s_len // SUPER
    npair = D_A // LANES
    in_specs, args, scratch = [], [], []
    for (_, dil), a in zip(PATTERNS, qkvs):
        rows = SUPER // dil
        nblk = rows // SPAN
        for part in range(3):
            in_specs.append(pl.BlockSpec((dil, rows, LANES),
                                         lambda n, hp, part=part: (0, n, part * npair + hp)))
            args.append(a)
        for part in (1, 2):
            in_specs.append(pl.BlockSpec((dil, SPAN, LANES),
                                         lambda n, hp, part=part, nblk=nblk:
                                         (0, jnp.maximum(n * nblk - 1, 0), part * npair + hp)))
            args.append(a)
    in_specs.append(pl.BlockSpec((len(PATTERNS), 1, 2, 2 * SPAN), lambda n, hp: (0, hp, 0, 0)))
    args.append(bvec)
    for _ in range(2):
        for _, dil in PATTERNS:
            scratch.append(pltpu.VMEM((dil, SPAN + SUPER // dil, LANES), BF16))
    scratch += [pltpu.VMEM((len(PATTERNS), SUPER, LANES), F32)] * 3
    scratch.append(pltpu.VMEM((len(PATTERNS), 2, 2 * SPAN, 2 * SPAN), F32))
    return pl.pallas_call(
        _attn_prompt_kernel,
        grid=(nsb, npair),
        in_specs=in_specs,
        out_specs=pl.BlockSpec((SUPER, LANES), lambda n, hp: (n, hp)),
        out_shape=jax.ShapeDtypeStruct((s_len, D_A), BF16),
        scratch_shapes=scratch,
        compiler_params=_params(("arbitrary", "arbitrary")),
        name="attn_prompt",
    )(*args)


def _attn_sample_kernel(qkv_ref, kc_ref, vc_ref, *refs, t_new, wb):
    np_ = len(PATTERNS)
    bc_refs = refs[:np_]
    bn_ref, att_ref, ko_ref, vo_ref = refs[np_:]
    qkv = qkv_ref[0]
    qf, kn, vn = qkv[:, 0:D_A], qkv[:, D_A:2 * D_A], qkv[:, 2 * D_A:3 * D_A]
    nrow = H_A * t_new
    row_h = lax.broadcasted_iota(jnp.int32, (nrow, D_A), 0) // t_new
    lane_h = lax.broadcasted_iota(jnp.int32, (nrow, D_A), 1) // HD_A
    own = row_h == lane_h
    qs = jnp.where(own, jnp.concatenate([qf] * H_A, axis=0), 0.0).astype(BF16)
    pad = jnp.zeros((LANES - t_new, D_A), F32)
    knt = jnp.concatenate([pad, kn], axis=0).T
    vnt = jnp.concatenate([pad, vn], axis=0).T
    kct = kc_ref[0]
    vct = vc_ref[0]
    s_c = jnp.dot(qs, kct.astype(BF16), preferred_element_type=F32)
    s_n = jnp.dot(qs, knt.astype(BF16), preferred_element_type=F32)

    wins = [w for w, _ in PATTERNS]
    es_c, es_n = [], []
    m = None
    for p, win in enumerate(wins):
        e_c = s_c[:, wb - win:] + bc_refs[p][...]
        e_n = s_n + bn_ref[p]
        es_c.append(e_c)
        es_n.append(e_n)
        mp = jnp.maximum(jnp.max(e_c, axis=-1, keepdims=True), jnp.max(e_n, axis=-1, keepdims=True))
        m = mp if m is None else jnp.maximum(m, mp)
    pn = None
    for p in range(np_):
        e = jnp.exp(es_n[p] - m)
        pn = e if pn is None else pn + e
    order = sorted(range(np_), key=lambda p: -wins[p])
    assert wins[order[0]] == wb
    pc = jnp.exp(es_c[order[0]] - m)
    for p in order[1:]:
        e = jnp.exp(es_c[p] - m)
        pc = jnp.concatenate([pc[:, :wb - wins[p]], pc[:, wb - wins[p]:] + e], axis=1)
    den = jnp.sum(pc, axis=-1, keepdims=True) + jnp.sum(pn, axis=-1, keepdims=True)
    nt = (((1,), (1,)), ((), ()))
    num = (lax.dot_general(pc.astype(BF16), vct.astype(BF16), nt, preferred_element_type=F32)
           + lax.dot_general(pn.astype(BF16), vnt.astype(BF16), nt, preferred_element_type=F32))
    o = jnp.where(own, num / den, 0.0)
    att = o[0:t_new]
    for h in range(1, H_A):
        att = att + o[h * t_new:(h + 1) * t_new]
    att_ref[0] = att.astype(att_ref.dtype)

    is_new = lax.broadcasted_iota(jnp.int32, (D_A, LANES), 1) >= LANES - t_new
    for src, new, dst in ((kct, knt, ko_ref), (vct, vnt, vo_ref)):
        rolled = pltpu.roll(src, wb - t_new, 1)
        dst[0, :, 0:wb - LANES] = rolled[:, 0:wb - LANES]
        dst[0, :, wb - LANES:wb] = jnp.where(is_new, new, rolled[:, wb - LANES:wb])


def _attn_sample(qkv, cache_kt, cache_vt, bias_c, bias_n):
    b, t_new, _ = qkv.shape
    wb = cache_kt.shape[2]
    assert wb == max(w for w, _ in PATTERNS) and t_new % 8 == 0 and t_new <= LANES
    nrow = H_A * t_new
    blk = lambda r, w: pl.BlockSpec((1, r, w), lambda i: (i, 0, 0))
    return pl.pallas_call(
        functools.partial(_attn_sample_kernel, t_new=t_new, wb=wb),
        grid=(b,),
        in_specs=[blk(t_new, N_QKV), blk(D_A, wb), blk(D_A, wb)]
                 + [_const_spec((nrow, w)) for w, _ in PATTERNS]
                 + [_const_spec((len(PATTERNS), nrow, LANES))],
        out_specs=[blk(t_new, D_A), blk(D_A, wb), blk(D_A, wb)],
        out_shape=[jax.ShapeDtypeStruct((b, t_new, D_A), BF16),
                   jax.ShapeDtypeStruct((b, D_A, wb), F32), jax.ShapeDtypeStruct((b, D_A, wb), F32)],
        compiler_params=_params(("arbitrary",)),
        name="attn_sample",
    )(qkv, cache_kt, cache_vt, *bias_c, bias_n)


def _split3(x):
    a = x.astype(BF16)
    r = x - a.astype(F32)
    b = r.astype(BF16)
    c = (r - b.astype(F32)).astype(BF16)
    return a, b, c


def _mlstm_kernel(q_ref, k_ref, v_ref, o_ref, g_ref, gw_ref, cx0_ref, m0_ref,
                  h_ref, cx_out, m_out, cx_s, m_s, *, c):
    ci = pl.program_id(1)

    @pl.when(ci == 0)
    def _():
        cx_s[...] = cx0_ref[0]
        m_s[...] = m0_ref[0]

    gates = g_ref[...]
    lane = lax.broadcasted_iota(jnp.int32, gates.shape, 1)
    lf = jnp.where(jnp.logical_and(lane >= H_M, lane < 2 * H_M), gates, 0.0)
    ri = lax.broadcasted_iota(jnp.int32, (c, c), 0)
    cj = lax.broadcasted_iota(jnp.int32, (c, c), 1)
    causal = ri >= cj
    tri = causal.astype(BF16)
    bsum = None
    for part in _split3(lf):
        t = jnp.dot(tri, part, preferred_element_type=F32)
        bsum = t if bsum is None else bsum + t
    gates_t = gates.T
    bsum_t = bsum.T
    ones = jnp.ones((c, DV_M), BF16)

    for h in range(H_M):
        sl = slice(h * DK_M, (h + 1) * DK_M)
        qh, kh, vh = q_ref[:, sl], k_ref[:, sl], v_ref[:, sl]
        i_row = gates_t[h:h + 1, :]
        b_row = bsum_t[H_M + h:H_M + h + 1, :]
        i_col = gates[:, h:h + 1]
        b_col = bsum[:, H_M + h:H_M + h + 1]
        m_prev = m_s[h, 0:1, 0:1]
        dmat = jnp.where(causal, b_col - b_row + i_row, NEG_INF)
        inter = b_col + m_prev
        mt = jnp.maximum(inter, jnp.max(dmat, axis=-1, keepdims=True))
        s = lax.dot_general(qh, kh, (((1,), (1,)), ((), ())), preferred_element_type=F32)
        sm = (s * jnp.exp(dmat - mt)).astype(BF16)
        iw = jnp.exp(inter - mt)
        vext = jnp.concatenate([vh, ones], axis=1)
        cxh = cx_s[h]
        ne = (jnp.dot(sm, vext, preferred_element_type=F32)
              + iw * jnp.dot(qh, cxh.astype(BF16), preferred_element_type=F32))
        num, den = ne[:, :DV_M], ne[:, DV_M:]
        hq = num / jnp.maximum(jnp.abs(den), jnp.exp(-mt))
        hn = hq * lax.rsqrt(jnp.mean(hq * hq, axis=-1, keepdims=True) + RMS_EPS) * gw_ref[:, sl]
        h_ref[:, sl] = (jax.nn.sigmoid(o_ref[:, sl]) * hn).astype(h_ref.dtype)

        b_last = bsum[c - 1:c, H_M + h:H_M + h + 1]
        g_col = b_last - b_col + i_col
        m_new = jnp.maximum(b_last + m_prev, jnp.max(g_col, axis=0, keepdims=True))
        ws = jnp.exp(g_col - m_new)
        wc = jnp.exp(b_last + m_prev - m_new)
        kw = (kh.astype(F32) * ws).astype(BF16)
        upd = lax.dot_general(kw, vext, (((0,), (0,)), ((), ())), preferred_element_type=F32)
        cx_s[h] = wc * cxh + upd
        m_s[h] = jnp.broadcast_to(m_new, m_s.shape[1:])

    @pl.when(ci == pl.num_programs(1) - 1)
    def _():
        cx_out[0] = cx_s[...]
        m_out[0] = m_s[...]


def _mlstm(qm, km, vm, om, gates, gw, cx0, m0, *, nb, c):
    n = qm.shape[0]
    nc = n // (nb * c)
    assert nb * nc * c == n
    row = lambda w: pl.BlockSpec((c, w), lambda b, i: (b * nc + i, 0))
    st = lambda shape: pl.BlockSpec((1,) + shape, lambda b, i: (b,) + (0,) * len(shape))
    return pl.pallas_call(
        functools.partial(_mlstm_kernel, c=c),
        grid=(nb, nc),
        in_specs=[row(D_M), row(D_M), row(D_M), row(D_M), row(LANES), _const_spec((1, D_M)),
                  st((H_M, DK_M, 2 * DV_M)), st((H_M, 8, LANES))],
        out_specs=[row(D_M), st((H_M, DK_M, 2 * DV_M)), st((H_M, 8, LANES))],
        out_shape=[jax.ShapeDtypeStruct((n, D_M), BF16),
                   jax.ShapeDtypeStruct((nb, H_M, DK_M, 2 * DV_M), F32),
                   jax.ShapeDtypeStruct((nb, H_M, 8, LANES), F32)],
        scratch_shapes=[pltpu.VMEM((H_M, DK_M, 2 * DV_M), F32), pltpu.VMEM((H_M, 8, LANES), F32)],
        compiler_params=_params(("arbitrary", "arbitrary")),
        name="mlstm",
    )(qm, km, vm, om, gates, gw, cx0, m0)


def _outffn_kernel(att_ref, hm_ref, x_ref, prev_ref, wo_ref, gpost_ref, gpre_ref, wup_ref, cw_ref, cb_ref,
                   wdn_ref, gffn_ref, y_ref, st_ref, u_ref, a_ref, *, tile, shift, hdr):
    i = pl.program_id(0)

    @pl.when(i == 0)
    def _():
        u_ref[hdr - 2 * shift:hdr, :] = prev_ref[...]

    @pl.when(i > 0)
    def _():
        u_ref[0:hdr, :] = u_ref[tile:tile + hdr, :]

    mix_in = jnp.concatenate([att_ref[...], hm_ref[...]], axis=1)
    mixed = jnp.dot(mix_in, wo_ref[...], preferred_element_type=F32)
    x1 = x_ref[...] + _rms(mixed, gpost_ref[...])
    h2 = _rms(x1, gpre_ref[...]).astype(BF16)
    u_ref[hdr:hdr + tile, :] = jnp.dot(h2, wup_ref[...], preferred_element_type=F32)

    def conv(c):
        cols = slice(c * LANES, (c + 1) * LANES)
        y = cb_ref[:, cols] + u_ref[hdr - 2 * shift:hdr - 2 * shift + tile, cols] * cw_ref[0:1, cols]
        y = y + u_ref[hdr - shift:hdr - shift + tile, cols] * cw_ref[1:2, cols]
        return y + u_ref[hdr:hdr + tile, cols] * cw_ref[2:3, cols]

    nff = D_FF // LANES
    for c in range(nff):
        gate, val = conv(c), conv(nff + c)
        gelu = 0.5 * gate * (1.0 + jnp.tanh(math.sqrt(2.0 / math.pi) * (gate + 0.044715 * gate * gate * gate)))
        a_ref[:, c * LANES:(c + 1) * LANES] = (gelu * val).astype(BF16)

    y2 = jnp.dot(a_ref[...], wdn_ref[...], preferred_element_type=F32)
    y_ref[...] = x1 + _rms(y2, gffn_ref[...])
    st_ref[...] = u_ref[tile:tile + hdr, :]


def _outffn(att, hm, x, prev, wo, gpost, gpre, wup, cw, cb, wdn, gffn, *, tile, shift):
    n = x.shape[0]
    assert n % tile == 0 and (shift == 1 or n == tile)
    hdr = max(8, 2 * shift)
    assert hdr % 8 == 0 and tile % 8 == 0
    row = lambda w: pl.BlockSpec((tile, w), lambda i: (i, 0))
    return pl.pallas_call(
        functools.partial(_outffn_kernel, tile=tile, shift=shift, hdr=hdr),
        grid=(n // tile,),
        in_specs=[row(D_A), row(D_M), row(D_MODEL), _const_spec((2 * shift, 2 * D_FF)),
                  _const_spec((D_A + D_M, D_MODEL)), _const_spec((1, D_MODEL)), _const_spec((1, D_MODEL)),
                  _const_spec((D_MODEL, 2 * D_FF)), _const_spec((CONV_W, 2 * D_FF)), _const_spec((1, 2 * D_FF)),
                  _const_spec((D_FF, D_MODEL)), _const_spec((1, D_MODEL))],
        out_specs=[row(D_MODEL), pl.BlockSpec((hdr, 2 * D_FF), lambda i: (0, 0))],
        out_shape=[jax.ShapeDtypeStruct((n, D_MODEL), F32), jax.ShapeDtypeStruct((hdr, 2 * D_FF), F32)],
        scratch_shapes=[pltpu.VMEM((hdr + tile, 2 * D_FF), F32), pltpu.VMEM((tile, D_FF), BF16)],
        compiler_params=_params(("arbitrary",)),
        name="outffn",
    )(att, hm, x, prev, wo, gpost, gpre, wup, cw, cb, wdn, gffn)


def _t5_bucket(dist):
    max_exact = N_BUCKETS // 2
    d = jnp.maximum(dist, max_exact).astype(F32)
    large = max_exact + (jnp.log(d / max_exact) / math.log(BUCKET_MAX_DIST / max_exact)
                         * (N_BUCKETS - max_exact)).astype(jnp.int32)
    large = jnp.minimum(large, N_BUCKETS - 1)
    return jnp.where(dist < max_exact, dist, large)


def _pattern_bias(rel_bias, dil):
    j = jnp.arange(SPAN + 1)
    return rel_bias[_t5_bucket(j * dil)].T.astype(F32)


def _prompt_bias(rel_bias):
    out = []
    for _, dil in PATTERNS:
        bh = _pattern_bias(rel_bias, dil)
        v = jnp.concatenate([bh[:, ::-1], jnp.full((H_A, SPAN - 1), NEG_INF, F32)], axis=1)
        out.append(v.reshape(H_A // 2, 2, 2 * SPAN))
    return jnp.stack(out)


def _sample_bias(rel_bias, t_new):
    cs, ns = [], []
    for win, dil in PATTERNS:
        bh = _pattern_bias(rel_bias, dil)
        dist = jnp.arange((SPAN + 1) * dil)
        g = jnp.where(dist % dil == 0, jnp.repeat(bh, dil, axis=1), NEG_INF)[:, :win + 1]
        g = jnp.concatenate([g, jnp.full((H_A, t_new - 1), NEG_INF, F32)], axis=1)
        fill = lambda w: jnp.full((H_A, w), NEG_INF, F32)
        cache = [g[:, t + 1:t + win + 1][:, ::-1] for t in range(t_new)]
        new = [jnp.concatenate([fill(LANES - t_new), g[:, :t + 1][:, ::-1], fill(t_new - 1 - t)], axis=1)
               for t in range(t_new)]
        cs.append(jnp.stack(cache, axis=1).reshape(H_A * t_new, win))
        ns.append(jnp.stack(new, axis=1).reshape(H_A * t_new, LANES))
    return cs, jnp.stack(ns)


def _layer_weights(g_mix_pre, w_in, b_igate, b_fgate, g_mlstm_out, w_out, g_mix_post, g_ffn_pre, w_up,
                   conv_w, conv_b, w_down, g_ffn_post):
    wg = jnp.zeros((D_MODEL, LANES), F32).at[:, :2 * H_M].set(w_in[:, N_QKV + N_MIX:])
    bg = jnp.zeros((1, LANES), F32).at[0, :H_M].set(b_igate.astype(F32)).at[0, H_M:2 * H_M].set(b_fgate.astype(F32))
    row = lambda v: v.astype(F32).reshape(1, -1)
    return dict(
        g_pre=row(g_mix_pre), wa=w_in[:, :N_QKV].astype(BF16), wm=w_in[:, N_QKV:N_QKV + N_MIX].astype(BF16),
        wg=wg.astype(BF16), bg=bg, gw=row(g_mlstm_out), wo=w_out.astype(BF16), g_post=row(g_mix_post),
        g_ffn_pre=row(g_ffn_pre), wup=w_up.astype(BF16), cw=conv_w.astype(F32), cb=row(conv_b),
        wdn=w_down.astype(BF16), g_ffn_post=row(g_ffn_post))


def _state_in(c0, n0, m0):
    nb = c0.shape[0]
    cx = jnp.concatenate([c0.astype(F32), jnp.broadcast_to(n0.astype(F32)[..., None], c0.shape)], axis=-1)
    m = jnp.broadcast_to(m0.astype(F32)[..., None, None], (nb, H_M, 8, LANES))
    return cx, m


def _state_out(cx, m):
    return cx[..., :DV_M], cx[..., DV_M], m[..., 0, 0]


def _prompt_layer(x, w, bias, *, tile_in, tile_ffn):
    s_len = x.shape[0]
    wb = min(max(wd for wd, _ in PATTERNS), s_len)
    dils = tuple(d for _, d in PATTERNS if d > 1)
    outs = _inproj(x, w["g_pre"], w["wa"], w["wm"], w["wg"], w["bg"], tile=tile_in, dils=dils, tail_rows=wb)
    qkv1, qkv4, qkv16, tail, qm, km, vm, om, gates = outs
    att = _attn_prompt([qkv1.reshape(1, s_len, N_QKV), qkv4, qkv16], bias)
    cx0, m0 = _state_in(jnp.zeros((1, H_M, DK_M, DV_M), F32), jnp.zeros((1, H_M, DK_M), F32),
                        jnp.zeros((1, H_M), F32))
    hm, cx, m = _mlstm(qm, km, vm, om, gates, w["gw"], cx0, m0, nb=1, c=math.gcd(s_len, MLSTM_CHUNK))
    prev = jnp.zeros((CONV_W - 1, 2 * D_FF), F32)
    y, st = _outffn(att, hm, x, prev, w["wo"], w["g_post"], w["g_ffn_pre"], w["wup"], w["cw"], w["cb"],
                    w["wdn"], w["g_ffn_post"], tile=tile_ffn, shift=1)
    c_new, n_new, m_new = _state_out(cx, m)
    k_win = tail[:, D_A:2 * D_A].reshape(1, wb, H_A, HD_A)
    v_win = tail[:, 2 * D_A:3 * D_A].reshape(1, wb, H_A, HD_A)
    return y, (k_win, v_win, c_new, n_new, m_new, st[-(CONV_W - 1):][None])


def _sample_layer(x, w, bias_c, bias_n, cache_k, cache_v, c0, n0, m0, conv_prev):
    b, t_new, _ = x.shape
    n = b * t_new
    wb = cache_k.shape[1]
    outs = _inproj(x.reshape(n, D_MODEL), w["g_pre"], w["wa"], w["wm"], w["wg"], w["bg"],
                   tile=n, dils=(), tail_rows=n)
    _, tail, qm, km, vm, om, gates = outs
    to_t = lambda a: jnp.transpose(a.astype(F32), (0, 2, 3, 1)).reshape(b, D_A, wb)
    from_t = lambda a: jnp.transpose(a.reshape(b, H_A, HD_A, wb), (0, 3, 1, 2))
    att, k_win, v_win = _attn_sample(tail.reshape(b, t_new, N_QKV), to_t(cache_k), to_t(cache_v), bias_c, bias_n)

    c = MLSTM_CHUNK
    padr = lambda a: jnp.pad(a.reshape(b, t_new, -1), ((0, 0), (0, c - t_new), (0, 0))).reshape(b * c, -1)
    lane = jnp.arange(LANES)
    gpad = jnp.where(lane < H_M, NEG_INF, 0.0).astype(F32)
    gates_p = jnp.concatenate([gates.reshape(b, t_new, LANES),
                               jnp.broadcast_to(gpad, (b, c - t_new, LANES))], axis=1).reshape(b * c, LANES)
    cx0, m0b = _state_in(c0, n0, m0)
    hm_p, cx, m = _mlstm(padr(qm), padr(km), padr(vm), padr(om), gates_p, w["gw"], cx0, m0b, nb=b, c=c)
    hm = hm_p.reshape(b, c, D_M)[:, :t_new]

    tm = lambda a: a.reshape(b, t_new, -1).transpose(1, 0, 2).reshape(n, -1)
    prev = conv_prev.astype(F32).transpose(1, 0, 2).reshape((CONV_W - 1) * b, 2 * D_FF)
    y, st = _outffn(tm(att), tm(hm), tm(x), prev, w["wo"], w["g_post"], w["g_ffn_pre"], w["wup"], w["cw"],
                    w["cb"], w["wdn"], w["g_ffn_post"], tile=n, shift=b)
    y = y.reshape(t_new, b, D_MODEL).transpose(1, 0, 2)
    conv_state = st.reshape(CONV_W - 1, b, 2 * D_FF).transpose(1, 0, 2)
    c_new, n_new, m_new = _state_out(cx, m)
    return y, (from_t(k_win), from_t(v_win), c_new, n_new, m_new, conv_state)


def kernel(x_prompt, x_sample, cache_attn_k, cache_attn_v, state_mlstm_C, state_mlstm_n, state_mlstm_m,
           state_ffn_conv, rel_bias, g_mix_pre, w_in, b_igate, b_fgate, g_mlstm_out, w_out, g_mix_post,
           g_ffn_pre, w_up, conv_w, conv_b, w_down, g_ffn_post):
    depth = w_in.shape[0]
    batch, s_len, _ = x_prompt.shape
    assert batch == 1
    t_new = x_sample.shape[1]
    wb = cache_attn_k.shape[2]
    bias_p = _prompt_bias(rel_bias)
    bias_c, bias_n = _sample_bias(rel_bias, t_new)
    yp = x_prompt[0]
    ys = x_sample
    new_p, new_s = [], []
    for l in range(depth):
        w = _layer_weights(g_mix_pre[l], w_in[l], b_igate[l], b_fgate[l], g_mlstm_out[l], w_out[l],
                           g_mix_post[l], g_ffn_pre[l], w_up[l], conv_w[l], conv_b[l], w_down[l], g_ffn_post[l])
        yp, sp = _prompt_layer(yp, w, bias_p, tile_in=512, tile_ffn=256)
        ys, ss = _sample_layer(ys, w, bias_c, bias_n, cache_attn_k[l], cache_attn_v[l], state_mlstm_C[l],
                               state_mlstm_n[l], state_mlstm_m[l], state_ffn_conv[l])
        new_p.append(sp)
        new_s.append(ss)
    stack = lambda states, i: jnp.stack([s[i] for s in states])
    return ((yp[None], ys) + tuple(stack(new_p, i) for i in range(6))
            + tuple(stack(new_s, i) for i in range(6)))
```

```python
import functools
import math

import jax
import jax.numpy as jnp
import numpy as np
from jax import lax
from jax.experimental import pallas as pl
from jax.experimental.pallas import tpu as pltpu

F32 = jnp.float32
BF16 = jnp.bfloat16

D_MODEL = 1024
HD_A = 64
H_A = 8
D_A = H_A * HD_A
DK_M = 128
DV_M = 128
H_M = 4
D_M = H_M * DV_M
PATTERNS = ((128, 1), (512, 4), (2048, 16))
SPAN = 128
N_BUCKETS = 32
BUCKET_MAX_DIST = 2048
MLSTM_CHUNK = 128
D_FF = 2816
CONV_W = 3
RMS_EPS = 1e-6
N_QKV = 3 * D_A
N_MIX = 4 * D_M
LANES = 128
SUPER = 2048
VMEM_LIMIT = 56 * 1024 * 1024
NEG_INF = float("-inf")
GELU_C1 = -2.0 * math.sqrt(2.0 / math.pi)
GELU_C3 = GELU_C1 * 0.044715

for _w, _d in PATTERNS:
    assert _w // _d == SPAN


def _params(sem, vmem=VMEM_LIMIT):
    return pltpu.CompilerParams(dimension_semantics=sem, vmem_limit_bytes=vmem)


def _const_spec(shape):
    nd = len(shape)
    return pl.BlockSpec(shape, lambda *_: (0,) * nd, pipeline_mode=pl.Buffered(1))


def _rms(x, g):
    return x * lax.rsqrt(jnp.mean(x * x, axis=-1, keepdims=True) + RMS_EPS) * g


def _inproj_kernel(x_ref, g_ref, wa_ref, wm_ref, wg_ref, bg_ref, *refs, tile, dils, first_tail):
    nd = len(dils)
    qkv1_ref = refs[0]
    dil_refs = refs[1:1 + nd]
    tail_ref, qm_ref, km_ref, vm_ref, om_ref, gt_ref, zs_ref = refs[1 + nd:]
    i = pl.program_id(0)

    hb = _rms(x_ref[...], g_ref[...]).astype(BF16)
    za = jnp.dot(hb, wa_ref[...], preferred_element_type=F32)
    for c in range(N_QKV // LANES):
        col = za[:, c * LANES:(c + 1) * LANES]
        if c < D_A // LANES:
            col = col * (HD_A ** -0.5)
        zs_ref[c] = col
        qkv1_ref[:, c * LANES:(c + 1) * LANES] = col.astype(BF16)

    for dil, out_ref in zip(dils, dil_refs):
        for r in range(dil):
            for c in range(N_QKV // LANES):
                rows = zs_ref[c, pl.ds(r, tile // dil, stride=dil), :]
                out_ref[r, :, c * LANES:(c + 1) * LANES] = rows.astype(BF16)

    zm = jnp.dot(hb, wm_ref[...], preferred_element_type=F32)
    qm_ref[...] = zm[:, 0:D_M].astype(BF16)
    km_ref[...] = (zm[:, D_M:2 * D_M] * (DK_M ** -0.5)).astype(BF16)
    vm_ref[...] = zm[:, 2 * D_M:3 * D_M].astype(BF16)
    om_ref[...] = zm[:, 3 * D_M:4 * D_M]

    zg = jnp.dot(hb, wg_ref[...], preferred_element_type=F32) + bg_ref[...]
    logsig = jnp.minimum(zg, 0.0) - jnp.log1p(jnp.exp(-jnp.abs(zg)))
    lane = lax.broadcasted_iota(jnp.int32, zg.shape, 1)
    gt_ref[...] = jnp.where(lane < H_M, zg, logsig)

    @pl.when(i >= first_tail)
    def _():
        for c in range(N_QKV // LANES):
            tail_ref[:, c * LANES:(c + 1) * LANES] = zs_ref[c]


def _inproj(x, g, wa, wm, wg, bg, *, tile, dils, tail_rows):
    n = x.shape[0]
    assert n % tile == 0 and tail_rows % tile == 0
    for d in dils:
        assert tile % (16 * d) == 0
    steps = n // tile
    first_tail = (n - tail_rows) // tile
    row = lambda w: pl.BlockSpec((tile, w), lambda i: (i, 0))
    out_shape = [jax.ShapeDtypeStruct((n, N_QKV), BF16)]
    out_specs = [row(N_QKV)]
    for d in dils:
        out_shape.append(jax.ShapeDtypeStruct((d, n // d, N_QKV), BF16))
        out_specs.append(pl.BlockSpec((d, tile // d, N_QKV), lambda i: (0, i, 0)))
    out_shape += [jax.ShapeDtypeStruct((tail_rows, N_QKV), F32),
                  jax.ShapeDtypeStruct((n, D_M), BF16), jax.ShapeDtypeStruct((n, D_M), BF16),
                  jax.ShapeDtypeStruct((n, D_M), BF16), jax.ShapeDtypeStruct((n, D_M), F32),
                  jax.ShapeDtypeStruct((n, LANES), F32)]
    out_specs += [pl.BlockSpec((tile, N_QKV), lambda i: (jnp.maximum(i - first_tail, 0), 0)),
                  row(D_M), row(D_M), row(D_M), row(D_M), row(LANES)]
    return pl.pallas_call(
        functools.partial(_inproj_kernel, tile=tile, dils=dils, first_tail=first_tail),
        grid=(steps,),
        in_specs=[row(D_MODEL), _const_spec((1, D_MODEL)), _const_spec((D_MODEL, N_QKV)),
                  _const_spec((D_MODEL, N_MIX)), _const_spec((D_MODEL, LANES)), _const_spec((1, LANES))],
        out_specs=out_specs,
        out_shape=out_shape,
        scratch_shapes=[pltpu.VMEM((N_QKV // LANES, tile, LANES), F32)],
        compiler_params=_params(("arbitrary",)),
        name="inproj",
    )(x, g, wa, wm, wg, bg)


def _attn_prompt_kernel(*refs):
    np_ = len(PATTERNS)
    in_refs = refs[:5 * np_]
    bvec_ref, o_ref = refs[5 * np_], refs[5 * np_ + 1]
    scr = refs[5 * np_ + 2:]
    kbufs, vbufs = scr[0:np_], scr[np_:2 * np_]
    num_s, l_s, m_s, bias_s = scr[2 * np_:]
    n = pl.program_id(1)

    lane = lax.broadcasted_iota(jnp.int32, (SPAN, LANES), 1)
    even = lane < HD_A
    ones = jnp.ones((2 * SPAN, LANES), BF16)

    @pl.when(n == 0)
    def _():
        key_col = lax.broadcasted_iota(jnp.int32, (SPAN, 2 * SPAN), 1)
        for p in range(np_):
            for e in range(2):
                vb = jnp.broadcast_to(bvec_ref[p, 0, e:e + 1, :], (SPAN, 2 * SPAN))
                table = pltpu.roll(vb, 0, 1, stride=1, stride_axis=0)
                bias_s[p, 0, e * SPAN:(e + 1) * SPAN, :] = table
                bias_s[p, 1, e * SPAN:(e + 1) * SPAN, :] = jnp.where(key_col >= SPAN, table, NEG_INF)

    for p in range(np_):
        q_ref, k_ref, v_ref, kp_ref, vp_ref = in_refs[5 * p:5 * p + 5]
        kbufs[p][:, 0:SPAN, :] = kp_ref[...]
        kbufs[p][:, SPAN:, :] = k_ref[...]
        vbufs[p][:, 0:SPAN, :] = vp_ref[...]
        vbufs[p][:, SPAN:, :] = v_ref[...]

    first_n = (n == 0).astype(jnp.int32)

    def unit(p, j):
        dil = PATTERNS[p][1]
        r, mb = divmod(j, SUPER // dil // SPAN)
        row0 = mb * SPAN
        qb = in_refs[5 * p][r, row0:row0 + SPAN, :]
        kb = kbufs[p][r, row0:row0 + 2 * SPAN, :]
        vb = vbufs[p][r, row0:row0 + 2 * SPAN, :]
        bias = bias_s[p, first_n] if mb == 0 else bias_s[p, 0]
        zero = jnp.zeros_like(qb)
        qs = jnp.concatenate([jnp.where(even, qb, zero), jnp.where(even, zero, qb)], axis=0)
        s = lax.dot_general(qs, kb, (((1,), (1,)), ((), ())), preferred_element_type=F32) + bias
        m = jnp.max(s, axis=-1, keepdims=True)
        pr = jnp.exp(s - m).astype(BF16)
        res = jnp.dot(pr, jnp.concatenate([vb, ones], axis=1), preferred_element_type=F32)
        mb_ = jnp.broadcast_to(m, (2 * SPAN, LANES))
        rows = pl.ds(row0, SPAN) if dil == 1 else pl.ds(row0 * dil + r, SPAN, stride=dil)
        num_s[p, rows, :] = jnp.where(even, res[:SPAN, :LANES], res[SPAN:, :LANES])
        l_s[p, rows, :] = jnp.where(even, res[:SPAN, LANES:], res[SPAN:, LANES:])
        m_s[p, rows, :] = jnp.where(even, mb_[:SPAN], mb_[SPAN:])

    for j in range(SUPER // SPAN):
        for p in range(np_):
            unit(p, j)

    def combine(j, carry):
        rows = pl.ds(pl.multiple_of(j * SPAN, SPAN), SPAN)
        ms = [m_s[p, rows, :] for p in range(np_)]
        m = functools.reduce(jnp.maximum, ms)
        num = jnp.zeros((SPAN, LANES), F32)
        den = jnp.zeros((SPAN, LANES), F32)
        for p in range(np_):
            a = jnp.exp(ms[p] - m)
            num = num + a * num_s[p, rows, :]
            den = den + a * l_s[p, rows, :]
        o_ref[rows, :] = (num / den).astype(o_ref.dtype)
        return carry

    lax.fori_loop(0, SUPER // SPAN, combine, 0)


def _attn_prompt(qkvs, bvec):
    s_len = qkvs[0].shape[1]
    assert s_len % SUPER == 0
    nsb = s_len // SUPER
    npair = D_A // LANES
    in_specs, args, scratch = [], [], []
    for (_, dil), a in zip(PATTERNS, qkvs):
        rows = SUPER // dil
        nblk = rows // SPAN
        for part in range(3):
            in_specs.append(pl.BlockSpec((dil, rows, LANES),
                                         lambda hp, n, part=part: (0, n, part * npair + hp)))
            args.append(a)
        for part in (1, 2):
            in_specs.append(pl.BlockSpec((dil, SPAN, LANES),
                                         lambda hp, n, part=part, nblk=nblk:
                                         (0, jnp.maximum(n * nblk - 1, 0), part * npair + hp)))
            args.append(a)
    in_specs.append(pl.BlockSpec((len(PATTERNS), 1, 2, 2 * SPAN), lambda hp, n: (0, hp, 0, 0)))
    args.append(bvec)
    for _ in range(2):
        for _, dil in PATTERNS:
            scratch.append(pltpu.VMEM((dil, SPAN + SUPER // dil, LANES), BF16))
    scratch += [pltpu.VMEM((len(PATTERNS), SUPER, LANES), F32)] * 3
    scratch.append(pltpu.VMEM((len(PATTERNS), 2, 2 * SPAN, 2 * SPAN), F32))
    return pl.pallas_call(
        _attn_prompt_kernel,
        grid=(npair, nsb),
        in_specs=in_specs,
        out_specs=pl.BlockSpec((SUPER, LANES), lambda hp, n: (n, hp)),
        out_shape=jax.ShapeDtypeStruct((s_len, D_A), BF16),
        scratch_shapes=scratch,
        compiler_params=_params(("arbitrary", "arbitrary")),
        name="attn_prompt",
    )(*args)


def _attn_sample_kernel(qkv_ref, kc_ref, vc_ref, *refs, t_new, wb):
    np_ = len(PATTERNS)
    bc_refs = refs[:np_]
    bn_ref, att_ref, ko_ref, vo_ref = refs[np_:]
    qkv = qkv_ref[0]
    qf, kn, vn = qkv[:, 0:D_A], qkv[:, D_A:2 * D_A], qkv[:, 2 * D_A:3 * D_A]
    nrow = H_A * t_new
    row_h = lax.broadcasted_iota(jnp.int32, (nrow, D_A), 0) // t_new
    lane_h = lax.broadcasted_iota(jnp.int32, (nrow, D_A), 1) // HD_A
    own = row_h == lane_h
    qs = jnp.where(own, jnp.concatenate([qf] * H_A, axis=0), 0.0).astype(BF16)
    pad = jnp.zeros((LANES - t_new, D_A), F32)
    knt = jnp.concatenate([pad, kn], axis=0).T
    vnt = jnp.concatenate([pad, vn], axis=0).T
    kct = kc_ref[0]
    vct = vc_ref[0]
    s_c = jnp.dot(qs, kct.astype(BF16), preferred_element_type=F32)
    s_n = jnp.dot(qs, knt.astype(BF16), preferred_element_type=F32)

    wins = [w for w, _ in PATTERNS]
    es_c, es_n = [], []
    m = None
    for p, win in enumerate(wins):
        e_c = s_c[:, wb - win:] + bc_refs[p][...]
        e_n = s_n + bn_ref[p]
        es_c.append(e_c)
        es_n.append(e_n)
        mp = jnp.maximum(jnp.max(e_c, axis=-1, keepdims=True), jnp.max(e_n, axis=-1, keepdims=True))
        m = mp if m is None else jnp.maximum(m, mp)
    pn = None
    for p in range(np_):
        e = jnp.exp(es_n[p] - m)
        pn = e if pn is None else pn + e
    order = sorted(range(np_), key=lambda p: -wins[p])
    assert wins[order[0]] == wb
    pc = jnp.exp(es_c[order[0]] - m)
    for p in order[1:]:
        e = jnp.exp(es_c[p] - m)
        pc = jnp.concatenate([pc[:, :wb - wins[p]], pc[:, wb - wins[p]:] + e], axis=1)
    den = jnp.sum(pc, axis=-1, keepdims=True) + jnp.sum(pn, axis=-1, keepdims=True)
    nt = (((1,), (1,)), ((), ()))
    num = (lax.dot_general(pc.astype(BF16), vct.astype(BF16), nt, preferred_element_type=F32)
           + lax.dot_general(pn.astype(BF16), vnt.astype(BF16), nt, preferred_element_type=F32))
    o = jnp.where(own, num / den, 0.0)
    att = o[0:t_new]
    for h in range(1, H_A):
        att = att + o[h * t_new:(h + 1) * t_new]
    att_ref[0] = att.astype(att_ref.dtype)

    is_new = lax.broadcasted_iota(jnp.int32, (D_A, LANES), 1) >= LANES - t_new
    for src, new, dst in ((kct, knt, ko_ref), (vct, vnt, vo_ref)):
        rolled = pltpu.roll(src, wb - t_new, 1)
        dst[0, :, 0:wb - LANES] = rolled[:, 0:wb - LANES]
        dst[0, :, wb - LANES:wb] = jnp.where(is_new, new, rolled[:, wb - LANES:wb])


def _attn_sample(qkv, cache_kt, cache_vt, bias_c, bias_n):
    b, t_new, _ = qkv.shape
    wb = cache_kt.shape[2]
    assert wb == max(w for w, _ in PATTERNS) and t_new % 8 == 0 and t_new <= LANES
    nrow = H_A * t_new
    blk = lambda r, w: pl.BlockSpec((1, r, w), lambda i: (i, 0, 0))
    return pl.pallas_call(
        functools.partial(_attn_sample_kernel, t_new=t_new, wb=wb),
        grid=(b,),
        in_specs=[blk(t_new, N_QKV), blk(D_A, wb), blk(D_A, wb)]
                 + [_const_spec((nrow, w)) for w, _ in PATTERNS]
                 + [_const_spec((len(PATTERNS), nrow, LANES))],
        out_specs=[blk(t_new, D_A), blk(D_A, wb), blk(D_A, wb)],
        out_shape=[jax.ShapeDtypeStruct((b, t_new, D_A), BF16),
                   jax.ShapeDtypeStruct((b, D_A, wb), F32), jax.ShapeDtypeStruct((b, D_A, wb), F32)],
        compiler_params=_params(("arbitrary",)),
        name="attn_sample",
    )(qkv, cache_kt, cache_vt, *bias_c, bias_n)


def _split3(x):
    a = x.astype(BF16)
    r = x - a.astype(F32)
    b = r.astype(BF16)
    c = (r - b.astype(F32)).astype(BF16)
    return a, b, c


def _mlstm_kernel(q_ref, k_ref, v_ref, o_ref, g_ref, gw_ref, cx0_ref, m0_ref,
                  h_ref, cx_out, m_out, cx_s, m_s, *, c):
    ci = pl.program_id(1)

    @pl.when(ci == 0)
    def _():
        cx_s[...] = cx0_ref[0]
        m_s[...] = m0_ref[0]

    gates = g_ref[...]
    lane = lax.broadcasted_iota(jnp.int32, gates.shape, 1)
    lf = jnp.where(jnp.logical_and(lane >= H_M, lane < 2 * H_M), gates, 0.0)
    ri = lax.broadcasted_iota(jnp.int32, (c, c), 0)
    cj = lax.broadcasted_iota(jnp.int32, (c, c), 1)
    causal = ri >= cj
    tri = causal.astype(BF16)
    bsum = None
    for part in _split3(lf):
        t = jnp.dot(tri, part, preferred_element_type=F32)
        bsum = t if bsum is None else bsum + t
    gates_t = gates.T
    bsum_t = bsum.T
    ones = jnp.ones((c, DV_M), BF16)

    for h in range(H_M):
        sl = slice(h * DK_M, (h + 1) * DK_M)
        qh, kh, vh = q_ref[:, sl], k_ref[:, sl], v_ref[:, sl]
        i_row = gates_t[h:h + 1, :]
        b_row = bsum_t[H_M + h:H_M + h + 1, :]
        i_col = gates[:, h:h + 1]
        b_col = bsum[:, H_M + h:H_M + h + 1]
        m_prev = m_s[h, 0:1, 0:1]
        dmat = jnp.where(causal, b_col - b_row + i_row, NEG_INF)
        inter = b_col + m_prev
        mt = jnp.maximum(inter, jnp.max(dmat, axis=-1, keepdims=True))
        s = lax.dot_general(qh, kh, (((1,), (1,)), ((), ())), preferred_element_type=F32)
        sm = (s * jnp.exp(dmat - mt)).astype(BF16)
        iw = jnp.exp(inter - mt)
        vext = jnp.concatenate([vh, ones], axis=1)
        cxh = cx_s[h]
        ne = (jnp.dot(sm, vext, preferred_element_type=F32)
              + iw * jnp.dot(qh, cxh.astype(BF16), preferred_element_type=F32))
        num, den = ne[:, :DV_M], ne[:, DV_M:]
        hq = num / jnp.maximum(jnp.abs(den), jnp.exp(-mt))
        hn = hq * lax.rsqrt(jnp.mean(hq * hq, axis=-1, keepdims=True) + RMS_EPS) * gw_ref[:, sl]
        h_ref[:, sl] = (jax.nn.sigmoid(o_ref[:, sl]) * hn).astype(h_ref.dtype)

        b_last = bsum[c - 1:c, H_M + h:H_M + h + 1]
        g_col = b_last - b_col + i_col
        m_new = jnp.maximum(b_last + m_prev, jnp.max(g_col, axis=0, keepdims=True))
        ws = jnp.exp(g_col - m_new)
        wc = jnp.exp(b_last + m_prev - m_new)
        kw = (kh.astype(F32) * ws).astype(BF16)
        upd = lax.dot_general(kw, vext, (((0,), (0,)), ((), ())), preferred_element_type=F32)
        cx_s[h] = wc * cxh + upd
        m_s[h] = jnp.broadcast_to(m_new, m_s.shape[1:])

    @pl.when(ci == pl.num_programs(1) - 1)
    def _():
        cx_out[0] = cx_s[...]
        m_out[0] = m_s[...]


def _mlstm(qm, km, vm, om, gates, gw, cx0, m0, *, nb, c):
    n = qm.shape[0]
    nc = n // (nb * c)
    assert nb * nc * c == n
    row = lambda w: pl.BlockSpec((c, w), lambda b, i: (b * nc + i, 0))
    st = lambda shape: pl.BlockSpec((1,) + shape, lambda b, i: (b,) + (0,) * len(shape))
    return pl.pallas_call(
        functools.partial(_mlstm_kernel, c=c),
        grid=(nb, nc),
        in_specs=[row(D_M), row(D_M), row(D_M), row(D_M), row(LANES), _const_spec((1, D_M)),
                  st((H_M, DK_M, 2 * DV_M)), st((H_M, 8, LANES))],
        out_specs=[row(D_M), st((H_M, DK_M, 2 * DV_M)), st((H_M, 8, LANES))],
        out_shape=[jax.ShapeDtypeStruct((n, D_M), BF16),
                   jax.ShapeDtypeStruct((nb, H_M, DK_M, 2 * DV_M), F32),
                   jax.ShapeDtypeStruct((nb, H_M, 8, LANES), F32)],
        scratch_shapes=[pltpu.VMEM((H_M, DK_M, 2 * DV_M), F32), pltpu.VMEM((H_M, 8, LANES), F32)],
        compiler_params=_params(("arbitrary", "arbitrary")),
        name="mlstm",
    )(qm, km, vm, om, gates, gw, cx0, m0)


def _outffn_kernel(att_ref, hm_ref, x_ref, prev_ref, wo_ref, gpost_ref, gpre_ref, wup_ref, cw_ref, cb_ref,
                   wdn_ref, gffn_ref, y_ref, st_ref, u_ref, a_ref, *, tile, shift, hdr):
    i = pl.program_id(0)

    @pl.when(i == 0)
    def _():
        u_ref[hdr - 2 * shift:hdr, :] = prev_ref[...]

    @pl.when(i > 0)
    def _():
        u_ref[0:hdr, :] = u_ref[tile:tile + hdr, :]

    mix_in = jnp.concatenate([att_ref[...], hm_ref[...]], axis=1)
    mixed = jnp.dot(mix_in, wo_ref[...], preferred_element_type=F32)
    x1 = x_ref[...] + _rms(mixed, gpost_ref[...])
    h2 = _rms(x1, gpre_ref[...]).astype(BF16)
    u_ref[hdr:hdr + tile, :] = jnp.dot(h2, wup_ref[...], preferred_element_type=F32)

    def conv(c):
        cols = slice(c * LANES, (c + 1) * LANES)
        y = cb_ref[:, cols] + u_ref[hdr - 2 * shift:hdr - 2 * shift + tile, cols] * cw_ref[0:1, cols]
        y = y + u_ref[hdr - shift:hdr - shift + tile, cols] * cw_ref[1:2, cols]
        return y + u_ref[hdr:hdr + tile, cols] * cw_ref[2:3, cols]

    nff = D_FF // LANES
    for c in range(nff):
        gate, val = conv(c), conv(nff + c)
        e = jnp.exp(gate * (GELU_C1 + GELU_C3 * (gate * gate)))
        a_ref[:, c * LANES:(c + 1) * LANES] = (gate * val / (1.0 + e)).astype(BF16)

    y2 = jnp.dot(a_ref[...], wdn_ref[...], preferred_element_type=F32)
    y_ref[...] = x1 + _rms(y2, gffn_ref[...])
    st_ref[...] = u_ref[tile:tile + hdr, :]


def _outffn(att, hm, x, prev, wo, gpost, gpre, wup, cw, cb, wdn, gffn, *, tile, shift):
    n = x.shape[0]
    assert n % tile == 0 and (shift == 1 or n == tile)
    hdr = max(8, 2 * shift)
    assert hdr % 8 == 0 and tile % 8 == 0
    row = lambda w: pl.BlockSpec((tile, w), lambda i: (i, 0))
    return pl.pallas_call(
        functools.partial(_outffn_kernel, tile=tile, shift=shift, hdr=hdr),
        grid=(n // tile,),
        in_specs=[row(D_A), row(D_M), row(D_MODEL), _const_spec((2 * shift, 2 * D_FF)),
                  _const_spec((D_A + D_M, D_MODEL)), _const_spec((1, D_MODEL)), _const_spec((1, D_MODEL)),
                  _const_spec((D_MODEL, 2 * D_FF)), _const_spec((CONV_W, 2 * D_FF)), _const_spec((1, 2 * D_FF)),
                  _const_spec((D_FF, D_MODEL)), _const_spec((1, D_MODEL))],
        out_specs=[row(D_MODEL), pl.BlockSpec((hdr, 2 * D_FF), lambda i: (0, 0))],
        out_shape=[jax.ShapeDtypeStruct((n, D_MODEL), F32), jax.ShapeDtypeStruct((hdr, 2 * D_FF), F32)],
        scratch_shapes=[pltpu.VMEM((hdr + tile, 2 * D_FF), F32), pltpu.VMEM((tile, D_FF), BF16)],
        compiler_params=_params(("arbitrary",)),
        name="outffn",
    )(att, hm, x, prev, wo, gpost, gpre, wup, cw, cb, wdn, gffn)


def _t5_bucket(dist):
    max_exact = N_BUCKETS // 2
    d = jnp.maximum(dist, max_exact).astype(F32)
    large = max_exact + (jnp.log(d / max_exact) / math.log(BUCKET_MAX_DIST / max_exact)
                         * (N_BUCKETS - max_exact)).astype(jnp.int32)
    large = jnp.minimum(large, N_BUCKETS - 1)
    return jnp.where(dist < max_exact, dist, large)


def _pattern_bias(rel_bias, dil):
    j = jnp.arange(SPAN + 1)
    return rel_bias[_t5_bucket(j * dil)].T.astype(F32)


def _prompt_bias(rel_bias):
    out = []
    for _, dil in PATTERNS:
        bh = _pattern_bias(rel_bias, dil)
        v = jnp.concatenate([bh[:, ::-1], jnp.full((H_A, SPAN - 1), NEG_INF, F32)], axis=1)
        out.append(v.reshape(H_A // 2, 2, 2 * SPAN))
    return jnp.stack(out)


def _sample_bias(rel_bias, t_new):
    cs, ns = [], []
    for win, dil in PATTERNS:
        bh = _pattern_bias(rel_bias, dil)
        dist = jnp.arange((SPAN + 1) * dil)
        g = jnp.where(dist % dil == 0, jnp.repeat(bh, dil, axis=1), NEG_INF)[:, :win + 1]
        g = jnp.concatenate([g, jnp.full((H_A, t_new - 1), NEG_INF, F32)], axis=1)
        fill = lambda w: jnp.full((H_A, w), NEG_INF, F32)
        cache = [g[:, t + 1:t + win + 1][:, ::-1] for t in range(t_new)]
        new = [jnp.concatenate([fill(LANES - t_new), g[:, :t + 1][:, ::-1], fill(t_new - 1 - t)], axis=1)
               for t in range(t_new)]
        cs.append(jnp.stack(cache, axis=1).reshape(H_A * t_new, win))
        ns.append(jnp.stack(new, axis=1).reshape(H_A * t_new, LANES))
    return cs, jnp.stack(ns)


def _layer_weights(g_mix_pre, w_in, b_igate, b_fgate, g_mlstm_out, w_out, g_mix_post, g_ffn_pre, w_up,
                   conv_w, conv_b, w_down, g_ffn_post):
    wg = jnp.zeros((D_MODEL, LANES), F32).at[:, :2 * H_M].set(w_in[:, N_QKV + N_MIX:])
    bg = jnp.zeros((1, LANES), F32).at[0, :H_M].set(b_igate.astype(F32)).at[0, H_M:2 * H_M].set(b_fgate.astype(F32))
    row = lambda v: v.astype(F32).reshape(1, -1)
    return dict(
        g_pre=row(g_mix_pre), wa=w_in[:, :N_QKV].astype(BF16), wm=w_in[:, N_QKV:N_QKV + N_MIX].astype(BF16),
        wg=wg.astype(BF16), bg=bg, gw=row(g_mlstm_out), wo=w_out.astype(BF16), g_post=row(g_mix_post),
        g_ffn_pre=row(g_ffn_pre), wup=w_up.astype(BF16), cw=conv_w.astype(F32), cb=row(conv_b),
        wdn=w_down.astype(BF16), g_ffn_post=row(g_ffn_post))


def _state_in(c0, n0, m0):
    nb = c0.shape[0]
    cx = jnp.concatenate([c0.astype(F32), jnp.broadcast_to(n0.astype(F32)[..., None], c0.shape)], axis=-1)
    m = jnp.broadcast_to(m0.astype(F32)[..., None, None], (nb, H_M, 8, LANES))
    return cx, m


def _state_out(cx, m):
    return cx[..., :DV_M], cx[..., DV_M], m[..., 0, 0]


def _prompt_layer(x, w, bias, *, tile_in, tile_ffn):
    s_len = x.shape[0]
    wb = min(max(wd for wd, _ in PATTERNS), s_len)
    dils = tuple(d for _, d in PATTERNS if d > 1)
    outs = _inproj(x, w["g_pre"], w["wa"], w["wm"], w["wg"], w["bg"], tile=tile_in, dils=dils, tail_rows=wb)
    qkv1, qkv4, qkv16, tail, qm, km, vm, om, gates = outs
    att = _attn_prompt([qkv1.reshape(1, s_len, N_QKV), qkv4, qkv16], bias)
    cx0, m0 = _state_in(jnp.zeros((1, H_M, DK_M, DV_M), F32), jnp.zeros((1, H_M, DK_M), F32),
                        jnp.zeros((1, H_M), F32))
    hm, cx, m = _mlstm(qm, km, vm, om, gates, w["gw"], cx0, m0, nb=1, c=math.gcd(s_len, MLSTM_CHUNK))
    prev = jnp.zeros((CONV_W - 1, 2 * D_FF), F32)
    y, st = _outffn(att, hm, x, prev, w["wo"], w["g_post"], w["g_ffn_pre"], w["wup"], w["cw"], w["cb"],
                    w["wdn"], w["g_ffn_post"], tile=tile_ffn, shift=1)
    c_new, n_new, m_new = _state_out(cx, m)
    k_win = tail[:, D_A:2 * D_A].reshape(1, wb, H_A, HD_A)
    v_win = tail[:, 2 * D_A:3 * D_A].reshape(1, wb, H_A, HD_A)
    return y, (k_win, v_win, c_new, n_new, m_new, st[-(CONV_W - 1):][None])


def _sample_layer(x, w, bias_c, bias_n, cache_k, cache_v, c0, n0, m0, conv_prev):
    b, t_new, _ = x.shape
    n = b * t_new
    wb = cache_k.shape[1]
    outs = _inproj(x.reshape(n, D_MODEL), w["g_pre"], w["wa"], w["wm"], w["wg"], w["bg"],
                   tile=n, dils=(), tail_rows=n)
    _, tail, qm, km, vm, om, gates = outs
    to_t = lambda a: jnp.transpose(a.astype(F32), (0, 2, 3, 1)).reshape(b, D_A, wb)
    from_t = lambda a: jnp.transpose(a.reshape(b, H_A, HD_A, wb), (0, 3, 1, 2))
    att, k_win, v_win = _attn_sample(tail.reshape(b, t_new, N_QKV), to_t(cache_k), to_t(cache_v), bias_c, bias_n)

    c = MLSTM_CHUNK
    padr = lambda a: jnp.pad(a.reshape(b, t_new, -1), ((0, 0), (0, c - t_new), (0, 0))).reshape(b * c, -1)
    lane = jnp.arange(LANES)
    gpad = jnp.where(lane < H_M, NEG_INF, 0.0).astype(F32)
    gates_p = jnp.concatenate([gates.reshape(b, t_new, LANES),
                               jnp.broadcast_to(gpad, (b, c - t_new, LANES))], axis=1).reshape(b * c, LANES)
    cx0, m0b = _state_in(c0, n0, m0)
    hm_p, cx, m = _mlstm(padr(qm), padr(km), padr(vm), padr(om), gates_p, w["gw"], cx0, m0b, nb=b, c=c)
    hm = hm_p.reshape(b, c, D_M)[:, :t_new]

    tm = lambda a: a.reshape(b, t_new, -1).transpose(1, 0, 2).reshape(n, -1)
    prev = conv_prev.astype(F32).transpose(1, 0, 2).reshape((CONV_W - 1) * b, 2 * D_FF)
    y, st = _outffn(tm(att), tm(hm), tm(x), prev, w["wo"], w["g_post"], w["g_ffn_pre"], w["wup"], w["cw"],
                    w["cb"], w["wdn"], w["g_ffn_post"], tile=n, shift=b)
    y = y.reshape(t_new, b, D_MODEL).transpose(1, 0, 2)
    conv_state = st.reshape(CONV_W - 1, b, 2 * D_FF).transpose(1, 0, 2)
    c_new, n_new, m_new = _state_out(cx, m)
    return y, (from_t(k_win), from_t(v_win), c_new, n_new, m_new, conv_state)


def kernel(x_prompt, x_sample, cache_attn_k, cache_attn_v, state_mlstm_C, state_mlstm_n, state_mlstm_m,
           state_ffn_conv, rel_bias, g_mix_pre, w_in, b_igate, b_fgate, g_mlstm_out, w_out, g_mix_post,
           g_ffn_pre, w_up, conv_w, conv_b, w_down, g_ffn_post):
    depth = w_in.shape[0]
    batch, s_len, _ = x_prompt.shape
    assert batch == 1
    t_new = x_sample.shape[1]
    wb = cache_attn_k.shape[2]
    bias_p = _prompt_bias(rel_bias)
    bias_c, bias_n = _sample_bias(rel_bias, t_new)
    yp = x_prompt[0]
    ys = x_sample
    new_p, new_s = [], []
    for l in range(depth):
        w = _layer_weights(g_mix_pre[l], w_in[l], b_igate[l], b_fgate[l], g_mlstm_out[l], w_out[l],
                           g_mix_post[l], g_ffn_pre[l], w_up[l], conv_w[l], conv_b[l], w_down[l], g_ffn_post[l])
        yp, sp = _prompt_layer(yp, w, bias_p, tile_in=256, tile_ffn=512)
        ys, ss = _sample_layer(ys, w, bias_c, bias_n, cache_attn_k[l], cache_attn_v[l], state_mlstm_C[l],
                               state_mlstm_n[l], state_mlstm_m[l], state_ffn_conv[l])
        new_p.append(sp)
        new_s.append(ss)
    stack = lambda states, i: jnp.stack([s[i] for s in states])
    return ((yp[None], ys) + tuple(stack(new_p, i) for i in range(6))
            + tuple(stack(new_s, i) for i in range(6)))
```

```python
import functools
import math

import jax
import jax.numpy as jnp
import numpy as np
from jax import lax
from jax.experimental import pallas as pl
from jax.experimental.pallas import tpu as pltpu

F32 = jnp.float32
BF16 = jnp.bfloat16

D_MODEL = 1024
HD_A = 64
H_A = 8
D_A = H_A * HD_A
DK_M = 128
DV_M = 128
H_M = 4
D_M = H_M * DV_M
PATTERNS = ((128, 1), (512, 4), (2048, 16))
SPAN = 128
N_BUCKETS = 32
BUCKET_MAX_DIST = 2048
MLSTM_STEP = 256
D_FF = 2816
CONV_W = 3
RMS_EPS = 1e-6
N_QKV = 3 * D_A
N_MIX = 4 * D_M
LANES = 128
GATE_ROWS = 16
SUPER = 2048
VMEM_LIMIT = 56 * 1024 * 1024
NEG_INF = float("-inf")
GELU_C1 = -2.0 * math.sqrt(2.0 / math.pi)
GELU_C3 = GELU_C1 * 0.044715

for _w, _d in PATTERNS:
    assert _w // _d == SPAN


def _params(sem, vmem=VMEM_LIMIT):
    return pltpu.CompilerParams(dimension_semantics=sem, vmem_limit_bytes=vmem)


def _const_spec(shape):
    nd = len(shape)
    return pl.BlockSpec(shape, lambda *_: (0,) * nd, pipeline_mode=pl.Buffered(1))


def _rms(x, g):
    return x * lax.rsqrt(jnp.mean(x * x, axis=-1, keepdims=True) + RMS_EPS) * g


def _inproj_kernel(x_ref, g_ref, wa_ref, wmt_ref, wk_ref, wg_ref, bg_ref, *refs, tile, dils, first_tail):
    nd = len(dils)
    qkv1_ref = refs[0]
    dil_refs = refs[1:1 + nd]
    tail_ref, qmt_ref, km_ref, vmt_ref, omt_ref, gt_ref, gtt_ref, zs_ref = refs[1 + nd:]
    i = pl.program_id(0)

    hb = _rms(x_ref[...], g_ref[...]).astype(BF16)
    za = jnp.dot(hb, wa_ref[...], preferred_element_type=F32)
    for c in range(N_QKV // LANES):
        col = za[:, c * LANES:(c + 1) * LANES]
        if c < D_A // LANES:
            col = col * (HD_A ** -0.5)
        zs_ref[c] = col
        qkv1_ref[:, c * LANES:(c + 1) * LANES] = col.astype(BF16)

    for dil, out_ref in zip(dils, dil_refs):
        for r in range(dil):
            for c in range(N_QKV // LANES):
                rows = zs_ref[c, pl.ds(r, tile // dil, stride=dil), :]
                out_ref[r, :, c * LANES:(c + 1) * LANES] = rows.astype(BF16)

    zt = lax.dot_general(wmt_ref[...], hb, (((1,), (1,)), ((), ())), preferred_element_type=F32)
    qmt_ref[...] = zt[0:D_M].astype(BF16)
    vmt_ref[...] = zt[D_M:2 * D_M].astype(BF16)
    omt_ref[...] = zt[2 * D_M:3 * D_M]
    zk = jnp.dot(hb, wk_ref[...], preferred_element_type=F32)
    km_ref[...] = (zk * (DK_M ** -0.5)).astype(BF16)

    zg = jnp.dot(hb, wg_ref[...], preferred_element_type=F32) + bg_ref[...]
    logsig = jnp.minimum(zg, 0.0) - jnp.log1p(jnp.exp(-jnp.abs(zg)))
    lane = lax.broadcasted_iota(jnp.int32, zg.shape, 1)
    gates = jnp.where(lane < H_M, zg, logsig)
    gt_ref[...] = gates
    gtt_ref[...] = gates.T[0:GATE_ROWS]

    @pl.when(i >= first_tail)
    def _():
        for c in range(N_QKV // LANES):
            tail_ref[:, c * LANES:(c + 1) * LANES] = zs_ref[c]


def _inproj(x, g, wa, wmt, wk, wg, bg, *, tile, dils, tail_rows):
    n = x.shape[0]
    assert n % tile == 0 and tail_rows % tile == 0 and (tile % LANES == 0 or tile == n)
    for d in dils:
        assert tile % (16 * d) == 0
    steps = n // tile
    first_tail = (n - tail_rows) // tile
    row = lambda w: pl.BlockSpec((tile, w), lambda i: (i, 0))
    col = lambda h: pl.BlockSpec((h, tile), lambda i: (0, i))
    out_shape = [jax.ShapeDtypeStruct((n, N_QKV), BF16)]
    out_specs = [row(N_QKV)]
    for d in dils:
        out_shape.append(jax.ShapeDtypeStruct((d, n // d, N_QKV), BF16))
        out_specs.append(pl.BlockSpec((d, tile // d, N_QKV), lambda i: (0, i, 0)))
    out_shape += [jax.ShapeDtypeStruct((tail_rows, N_QKV), F32),
                  jax.ShapeDtypeStruct((D_M, n), BF16), jax.ShapeDtypeStruct((n, D_M), BF16),
                  jax.ShapeDtypeStruct((D_M, n), BF16), jax.ShapeDtypeStruct((D_M, n), F32),
                  jax.ShapeDtypeStruct((n, LANES), F32), jax.ShapeDtypeStruct((GATE_ROWS, n), F32)]
    out_specs += [pl.BlockSpec((tile, N_QKV), lambda i: (jnp.maximum(i - first_tail, 0), 0)),
                  col(D_M), row(D_M), col(D_M), col(D_M), row(LANES), col(GATE_ROWS)]
    return pl.pallas_call(
        functools.partial(_inproj_kernel, tile=tile, dils=dils, first_tail=first_tail),
        grid=(steps,),
        in_specs=[row(D_MODEL), _const_spec((1, D_MODEL)), _const_spec((D_MODEL, N_QKV)),
                  _const_spec((3 * D_M, D_MODEL)), _const_spec((D_MODEL, D_M)),
                  _const_spec((D_MODEL, LANES)), _const_spec((1, LANES))],
        out_specs=out_specs,
        out_shape=out_shape,
        scratch_shapes=[pltpu.VMEM((N_QKV // LANES, tile, LANES), F32)],
        compiler_params=_params(("arbitrary",)),
        name="inproj",
    )(x, g, wa, wmt, wk, wg, bg)


def _attn_prompt_kernel(*refs):
    np_ = len(PATTERNS)
    in_refs = refs[:5 * np_]
    bvec_ref, o_ref = refs[5 * np_], refs[5 * np_ + 1]
    scr = refs[5 * np_ + 2:]
    kbufs, vbufs = scr[0:np_], scr[np_:2 * np_]
    num_s, l_s, m_s, bias_s = scr[2 * np_:]
    n = pl.program_id(1)

    lane = lax.broadcasted_iota(jnp.int32, (SPAN, LANES), 1)
    even = lane < HD_A
    ones = jnp.ones((2 * SPAN, LANES), BF16)

    @pl.when(n == 0)
    def _():
        key_col = lax.broadcasted_iota(jnp.int32, (SPAN, 2 * SPAN), 1)
        for p in range(np_):
            for e in range(2):
                vb = jnp.broadcast_to(bvec_ref[p, 0, e:e + 1, :], (SPAN, 2 * SPAN))
                table = pltpu.roll(vb, 0, 1, stride=1, stride_axis=0)
                bias_s[p, 0, e * SPAN:(e + 1) * SPAN, :] = table
                bias_s[p, 1, e * SPAN:(e + 1) * SPAN, :] = jnp.where(key_col >= SPAN, table, NEG_INF)

    for p in range(np_):
        q_ref, k_ref, v_ref, kp_ref, vp_ref = in_refs[5 * p:5 * p + 5]
        kbufs[p][:, 0:SPAN, :] = kp_ref[...]
        kbufs[p][:, SPAN:, :] = k_ref[...]
        vbufs[p][:, 0:SPAN, :] = vp_ref[...]
        vbufs[p][:, SPAN:, :] = v_ref[...]

    first_n = (n == 0).astype(jnp.int32)

    def unit(p, j):
        dil = PATTERNS[p][1]
        r, mb = divmod(j, SUPER // dil // SPAN)
        row0 = mb * SPAN
        qb = in_refs[5 * p][r, row0:row0 + SPAN, :]
        kb = kbufs[p][r, row0:row0 + 2 * SPAN, :]
        vb = vbufs[p][r, row0:row0 + 2 * SPAN, :]
        bias = bias_s[p, first_n] if mb == 0 else bias_s[p, 0]
        zero = jnp.zeros_like(qb)
        qs = jnp.concatenate([jnp.where(even, qb, zero), jnp.where(even, zero, qb)], axis=0)
        s = lax.dot_general(qs, kb, (((1,), (1,)), ((), ())), preferred_element_type=F32) + bias
        m = jnp.max(s, axis=-1, keepdims=True)
        pr = jnp.exp(s - m).astype(BF16)
        res = jnp.dot(pr, jnp.concatenate([vb, ones], axis=1), preferred_element_type=F32)
        mb_ = jnp.broadcast_to(m, (2 * SPAN, LANES))
        rows = pl.ds(row0, SPAN) if dil == 1 else pl.ds(row0 * dil + r, SPAN, stride=dil)
        num_s[p, rows, :] = jnp.where(even, res[:SPAN, :LANES], res[SPAN:, :LANES])
        l_s[p, rows, :] = jnp.where(even, res[:SPAN, LANES:], res[SPAN:, LANES:])
        m_s[p, rows, :] = jnp.where(even, mb_[:SPAN], mb_[SPAN:])

    for j in range(SUPER // SPAN):
        for p in range(np_):
            unit(p, j)

    def combine(j, carry):
        rows = pl.ds(pl.multiple_of(j * SPAN, SPAN), SPAN)
        ms = [m_s[p, rows, :] for p in range(np_)]
        m = functools.reduce(jnp.maximum, ms)
        num = jnp.zeros((SPAN, LANES), F32)
        den = jnp.zeros((SPAN, LANES), F32)
        for p in range(np_):
            a = jnp.exp(ms[p] - m)
            num = num + a * num_s[p, rows, :]
            den = den + a * l_s[p, rows, :]
        o_ref[rows, :] = (num / den).astype(o_ref.dtype)
        return carry

    lax.fori_loop(0, SUPER // SPAN, combine, 0)


def _attn_prompt(qkvs, bvec):
    s_len = qkvs[0].shape[1]
    assert s_len % SUPER == 0
    nsb = s_len // SUPER
    npair = D_A // LANES
    in_specs, args, scratch = [], [], []
    for (_, dil), a in zip(PATTERNS, qkvs):
        rows = SUPER // dil
        nblk = rows // SPAN
        for part in range(3):
            in_specs.append(pl.BlockSpec((dil, rows, LANES),
                                         lambda hp, n, part=part: (0, n, part * npair + hp)))
            args.append(a)
        for part in (1, 2):
            in_specs.append(pl.BlockSpec((dil, SPAN, LANES),
                                         lambda hp, n, part=part, nblk=nblk:
                                         (0, jnp.maximum(n * nblk - 1, 0), part * npair + hp)))
            args.append(a)
    in_specs.append(pl.BlockSpec((len(PATTERNS), 1, 2, 2 * SPAN), lambda hp, n: (0, hp, 0, 0)))
    args.append(bvec)
    for _ in range(2):
        for _, dil in PATTERNS:
            scratch.append(pltpu.VMEM((dil, SPAN + SUPER // dil, LANES), BF16))
    scratch += [pltpu.VMEM((len(PATTERNS), SUPER, LANES), F32)] * 3
    scratch.append(pltpu.VMEM((len(PATTERNS), 2, 2 * SPAN, 2 * SPAN), F32))
    return pl.pallas_call(
        _attn_prompt_kernel,
        grid=(npair, nsb),
        in_specs=in_specs,
        out_specs=pl.BlockSpec((SUPER, LANES), lambda hp, n: (n, hp)),
        out_shape=jax.ShapeDtypeStruct((s_len, D_A), BF16),
        scratch_shapes=scratch,
        compiler_params=_params(("arbitrary", "arbitrary")),
        name="attn_prompt",
    )(*args)


def _attn_sample_kernel(qkv_ref, kc_ref, vc_ref, *refs, t_new, wb):
    np_ = len(PATTERNS)
    bc_refs = refs[:np_]
    bn_ref, att_ref, ko_ref, vo_ref = refs[np_:]
    qkv = qkv_ref[0]
    qf, kn, vn = qkv[:, 0:D_A], qkv[:, D_A:2 * D_A], qkv[:, 2 * D_A:3 * D_A]
    nrow = H_A * t_new
    row_h = lax.broadcasted_iota(jnp.int32, (nrow, D_A), 0) // t_new
    lane_h = lax.broadcasted_iota(jnp.int32, (nrow, D_A), 1) // HD_A
    own = row_h == lane_h
    qs = jnp.where(own, jnp.concatenate([qf] * H_A, axis=0), 0.0).astype(BF16)
    pad = jnp.zeros((LANES - t_new, D_A), F32)
    knt = jnp.concatenate([pad, kn], axis=0).T
    vnt = jnp.concatenate([pad, vn], axis=0).T
    kct = kc_ref[0]
    vct = vc_ref[0]
    s_c = jnp.dot(qs, kct.astype(BF16), preferred_element_type=F32)
    s_n = jnp.dot(qs, knt.astype(BF16), preferred_element_type=F32)

    wins = [w for w, _ in PATTERNS]
    es_c, es_n = [], []
    m = None
    for p, win in enumerate(wins):
        e_c = s_c[:, wb - win:] + bc_refs[p][...]
        e_n = s_n + bn_ref[p]
        es_c.append(e_c)
        es_n.append(e_n)
        mp = jnp.maximum(jnp.max(e_c, axis=-1, keepdims=True), jnp.max(e_n, axis=-1, keepdims=True))
        m = mp if m is None else jnp.maximum(m, mp)
    pn = None
    for p in range(np_):
        e = jnp.exp(es_n[p] - m)
        pn = e if pn is None else pn + e
    order = sorted(range(np_), key=lambda p: -wins[p])
    assert wins[order[0]] == wb
    pc = jnp.exp(es_c[order[0]] - m)
    for p in order[1:]:
        e = jnp.exp(es_c[p] - m)
        pc = jnp.concatenate([pc[:, :wb - wins[p]], pc[:, wb - wins[p]:] + e], axis=1)
    den = jnp.sum(pc, axis=-1, keepdims=True) + jnp.sum(pn, axis=-1, keepdims=True)
    nt = (((1,), (1,)), ((), ()))
    num = (lax.dot_general(pc.astype(BF16), vct.astype(BF16), nt, preferred_element_type=F32)
           + lax.dot_general(pn.astype(BF16), vnt.astype(BF16), nt, preferred_element_type=F32))
    o = jnp.where(own, num / den, 0.0)
    att = o[0:t_new]
    for h in range(1, H_A):
        att = att + o[h * t_new:(h + 1) * t_new]
    att_ref[0] = att.astype(att_ref.dtype)

    is_new = lax.broadcasted_iota(jnp.int32, (D_A, LANES), 1) >= LANES - t_new
    for src, new, dst in ((kct, knt, ko_ref), (vct, vnt, vo_ref)):
        rolled = pltpu.roll(src, wb - t_new, 1)
        dst[0, :, 0:wb - LANES] = rolled[:, 0:wb - LANES]
        dst[0, :, wb - LANES:wb] = jnp.where(is_new, new, rolled[:, wb - LANES:wb])


def _attn_sample(qkv, cache_kt, cache_vt, bias_c, bias_n):
    b, t_new, _ = qkv.shape
    wb = cache_kt.shape[2]
    assert wb == max(w for w, _ in PATTERNS) and t_new % 8 == 0 and t_new <= LANES
    nrow = H_A * t_new
    blk = lambda r, w: pl.BlockSpec((1, r, w), lambda i: (i, 0, 0))
    return pl.pallas_call(
        functools.partial(_attn_sample_kernel, t_new=t_new, wb=wb),
        grid=(b,),
        in_specs=[blk(t_new, N_QKV), blk(D_A, wb), blk(D_A, wb)]
                 + [_const_spec((nrow, w)) for w, _ in PATTERNS]
                 + [_const_spec((len(PATTERNS), nrow, LANES))],
        out_specs=[blk(t_new, D_A), blk(D_A, wb), blk(D_A, wb)],
        out_shape=[jax.ShapeDtypeStruct((b, t_new, D_A), BF16),
                   jax.ShapeDtypeStruct((b, D_A, wb), F32), jax.ShapeDtypeStruct((b, D_A, wb), F32)],
        compiler_params=_params(("arbitrary",)),
        name="attn_sample",
    )(qkv, cache_kt, cache_vt, *bias_c, bias_n)


def _split3(x):
    a = x.astype(BF16)
    r = x - a.astype(F32)
    b = r.astype(BF16)
    c = (r - b.astype(F32)).astype(BF16)
    return a, b, c


def _mlstm_kernel(qt_ref, k_ref, vt_ref, ot_ref, g_ref, gtt_ref, gw_ref, c0_ref, n0_ref, m0_ref,
                  h_ref, c_out, n_out, m_out, cx_s, m_s, *, c):
    ci = pl.program_id(1)

    @pl.when(ci == 0)
    def _():
        for h in range(H_M):
            cx_s[h, 0:DV_M, :] = c0_ref[0, h].T
            cx_s[h, DV_M:, :] = jnp.broadcast_to(n0_ref[0, h], (DV_M, DK_M))
        m_s[...] = m0_ref[0]

    gates = g_ref[...]
    gates_t = gtt_ref[...]
    lane = lax.broadcasted_iota(jnp.int32, gates.shape, 1)
    lf = jnp.where(jnp.logical_and(lane >= H_M, lane < 2 * H_M), gates, 0.0)
    grow = lax.broadcasted_iota(jnp.int32, gates_t.shape, 0)
    lf_t = jnp.where(jnp.logical_and(grow >= H_M, grow < 2 * H_M), gates_t, 0.0)
    ri = lax.broadcasted_iota(jnp.int32, (c, c), 0)
    cj = lax.broadcasted_iota(jnp.int32, (c, c), 1)
    upper = ri <= cj
    tril = (ri >= cj).astype(BF16)
    triu = upper.astype(BF16)
    bsum = brow = None
    for part, part_t in zip(_split3(lf), _split3(lf_t)):
        t1 = jnp.dot(tril, part, preferred_element_type=F32)
        t2 = jnp.dot(part_t, triu, preferred_element_type=F32)
        bsum = t1 if bsum is None else bsum + t1
        brow = t2 if brow is None else brow + t2
    ones = jnp.ones((DV_M, c), BF16)

    for h in range(H_M):
        sl = slice(h * DK_M, (h + 1) * DK_M)
        qt, kh, vt = qt_ref[sl, :], k_ref[:, sl], vt_ref[sl, :]
        b_row = brow[H_M + h:H_M + h + 1, :]
        i_row = gates_t[h:h + 1, :]
        a_col = gates[:, h:h + 1] - bsum[:, H_M + h:H_M + h + 1]
        m_prev = m_s[h, 0:1, 0:1]
        dmat = jnp.where(upper, b_row + a_col, NEG_INF)
        inter = b_row + m_prev
        mt = jnp.maximum(inter, jnp.max(dmat, axis=0, keepdims=True))
        st = jnp.dot(kh, qt, preferred_element_type=F32)
        smt = (st * jnp.exp(dmat - mt)).astype(BF16)
        iw = jnp.exp(inter - mt)
        vext = jnp.concatenate([vt, ones], axis=0)
        cxh = cx_s[h]
        ne = (jnp.dot(vext, smt, preferred_element_type=F32)
              + iw * jnp.dot(cxh.astype(BF16), qt, preferred_element_type=F32))
        num, den = ne[:DV_M], ne[DV_M:]
        hq = num / jnp.maximum(jnp.abs(den), jnp.exp(-mt))
        hn = hq * lax.rsqrt(jnp.mean(hq * hq, axis=0, keepdims=True) + RMS_EPS) * gw_ref[sl, :]
        h_ref[sl, :] = (jax.nn.sigmoid(ot_ref[sl, :]) * hn).astype(h_ref.dtype)

        b_last = b_row[:, c - 1:c]
        g_row = b_last - b_row + i_row
        m_new = jnp.maximum(b_last + m_prev, jnp.max(g_row, axis=1, keepdims=True))
        ws = jnp.exp(g_row - m_new)
        wc = jnp.exp(b_last + m_prev - m_new)
        vw = (vext.astype(F32) * ws).astype(BF16)
        cx_s[h] = wc * cxh + jnp.dot(vw, kh, preferred_element_type=F32)
        m_s[h] = jnp.broadcast_to(m_new, m_s.shape[1:])

    @pl.when(ci == pl.num_programs(1) - 1)
    def _():
        for h in range(H_M):
            c_out[0, h] = cx_s[h, 0:DV_M, :].T
            n_out[0, h] = cx_s[h, DV_M:DV_M + 8, :]
        m_out[0] = m_s[...]


def _mlstm(qmt, km, vmt, omt, gates, gates_t, gw, c0, n0, m0, *, nb, c):
    n = km.shape[0]
    nc = n // (nb * c)
    assert nb * nc * c == n and c % LANES == 0
    row = lambda w: pl.BlockSpec((c, w), lambda b, i: (b * nc + i, 0))
    col = lambda h: pl.BlockSpec((h, c), lambda b, i: (0, b * nc + i))
    st = lambda shape: pl.BlockSpec((1,) + shape, lambda b, i: (b,) + (0,) * len(shape))
    return pl.pallas_call(
        functools.partial(_mlstm_kernel, c=c),
        grid=(nb, nc),
        in_specs=[col(D_M), row(D_M), col(D_M), col(D_M), row(LANES), col(GATE_ROWS), _const_spec((D_M, c)),
                  st((H_M, DK_M, DV_M)), st((H_M, 1, DK_M)), st((H_M, 8, LANES))],
        out_specs=[col(D_M), st((H_M, DK_M, DV_M)), st((H_M, 8, DK_M)), st((H_M, 8, LANES))],
        out_shape=[jax.ShapeDtypeStruct((D_M, n), BF16),
                   jax.ShapeDtypeStruct((nb, H_M, DK_M, DV_M), F32),
                   jax.ShapeDtypeStruct((nb, H_M, 8, DK_M), F32),
                   jax.ShapeDtypeStruct((nb, H_M, 8, LANES), F32)],
        scratch_shapes=[pltpu.VMEM((H_M, 2 * DV_M, DK_M), F32), pltpu.VMEM((H_M, 8, LANES), F32)],
        compiler_params=_params(("arbitrary", "arbitrary")),
        name="mlstm",
    )(qmt, km, vmt, omt, gates, gates_t, gw, c0, n0, m0)


def _outffn_kernel(att_ref, hmt_ref, x_ref, prev_ref, wo_ref, gpost_ref, gpre_ref, wup_ref, cw_ref, cb_ref,
                   wdn_ref, gffn_ref, y_ref, st_ref, u_ref, a_ref, *, tile, shift, hdr):
    i = pl.program_id(0)

    @pl.when(i == 0)
    def _():
        u_ref[hdr - 2 * shift:hdr, :] = prev_ref[...]

    @pl.when(i > 0)
    def _():
        u_ref[0:hdr, :] = u_ref[tile:tile + hdr, :]

    mixed = (jnp.dot(att_ref[...], wo_ref[0:D_A, :], preferred_element_type=F32)
             + lax.dot_general(hmt_ref[...], wo_ref[D_A:, :], (((0,), (0,)), ((), ())),
                               preferred_element_type=F32))
    x1 = x_ref[...] + _rms(mixed, gpost_ref[...])
    h2 = _rms(x1, gpre_ref[...]).astype(BF16)
    u_ref[hdr:hdr + tile, :] = jnp.dot(h2, wup_ref[...], preferred_element_type=F32)

    def conv(c):
        cols = slice(c * LANES, (c + 1) * LANES)
        y = cb_ref[:, cols] + u_ref[hdr - 2 * shift:hdr - 2 * shift + tile, cols] * cw_ref[0:1, cols]
        y = y + u_ref[hdr - shift:hdr - shift + tile, cols] * cw_ref[1:2, cols]
        return y + u_ref[hdr:hdr + tile, cols] * cw_ref[2:3, cols]

    nff = D_FF // LANES
    for c in range(nff):
        gate, val = conv(c), conv(nff + c)
        e = jnp.exp(gate * (GELU_C1 + GELU_C3 * (gate * gate)))
        a_ref[:, c * LANES:(c + 1) * LANES] = (gate * val / (1.0 + e)).astype(BF16)

    y2 = jnp.dot(a_ref[...], wdn_ref[...], preferred_element_type=F32)
    y_ref[...] = x1 + _rms(y2, gffn_ref[...])
    st_ref[...] = u_ref[tile:tile + hdr, :]


def _outffn(att, hmt, x, prev, wo, gpost, gpre, wup, cw, cb, wdn, gffn, *, tile, shift):
    n = x.shape[0]
    assert n % tile == 0 and (shift == 1 or n == tile)
    hdr = max(8, 2 * shift)
    assert hdr % 8 == 0 and (tile % LANES == 0 or tile == n)
    row = lambda w: pl.BlockSpec((tile, w), lambda i: (i, 0))
    return pl.pallas_call(
        functools.partial(_outffn_kernel, tile=tile, shift=shift, hdr=hdr),
        grid=(n // tile,),
        in_specs=[row(D_A), pl.BlockSpec((D_M, tile), lambda i: (0, i)), row(D_MODEL),
                  _const_spec((2 * shift, 2 * D_FF)),
                  _const_spec((D_A + D_M, D_MODEL)), _const_spec((1, D_MODEL)), _const_spec((1, D_MODEL)),
                  _const_spec((D_MODEL, 2 * D_FF)), _const_spec((CONV_W, 2 * D_FF)), _const_spec((1, 2 * D_FF)),
                  _const_spec((D_FF, D_MODEL)), _const_spec((1, D_MODEL))],
        out_specs=[row(D_MODEL), pl.BlockSpec((hdr, 2 * D_FF), lambda i: (0, 0))],
        out_shape=[jax.ShapeDtypeStruct((n, D_MODEL), F32), jax.ShapeDtypeStruct((hdr, 2 * D_FF), F32)],
        scratch_shapes=[pltpu.VMEM((hdr + tile, 2 * D_FF), F32), pltpu.VMEM((tile, D_FF), BF16)],
        compiler_params=_params(("arbitrary",)),
        name="outffn",
    )(att, hmt, x, prev, wo, gpost, gpre, wup, cw, cb, wdn, gffn)


def _t5_bucket(dist):
    max_exact = N_BUCKETS // 2
    d = jnp.maximum(dist, max_exact).astype(F32)
    large = max_exact + (jnp.log(d / max_exact) / math.log(BUCKET_MAX_DIST / max_exact)
                         * (N_BUCKETS - max_exact)).astype(jnp.int32)
    large = jnp.minimum(large, N_BUCKETS - 1)
    return jnp.where(dist < max_exact, dist, large)


def _pattern_bias(rel_bias, dil):
    j = jnp.arange(SPAN + 1)
    return rel_bias[_t5_bucket(j * dil)].T.astype(F32)


def _prompt_bias(rel_bias):
    out = []
    for _, dil in PATTERNS:
        bh = _pattern_bias(rel_bias, dil)
        v = jnp.concatenate([bh[:, ::-1], jnp.full((H_A, SPAN - 1), NEG_INF, F32)], axis=1)
        out.append(v.reshape(H_A // 2, 2, 2 * SPAN))
    return jnp.stack(out)


def _sample_bias(rel_bias, t_new):
    cs, ns = [], []
    for win, dil in PATTERNS:
        bh = _pattern_bias(rel_bias, dil)
        dist = jnp.arange((SPAN + 1) * dil)
        g = jnp.where(dist % dil == 0, jnp.repeat(bh, dil, axis=1), NEG_INF)[:, :win + 1]
        g = jnp.concatenate([g, jnp.full((H_A, t_new - 1), NEG_INF, F32)], axis=1)
        fill = lambda w: jnp.full((H_A, w), NEG_INF, F32)
        cache = [g[:, t + 1:t + win + 1][:, ::-1] for t in range(t_new)]
        new = [jnp.concatenate([fill(LANES - t_new), g[:, :t + 1][:, ::-1], fill(t_new - 1 - t)], axis=1)
               for t in range(t_new)]
        cs.append(jnp.stack(cache, axis=1).reshape(H_A * t_new, win))
        ns.append(jnp.stack(new, axis=1).reshape(H_A * t_new, LANES))
    return cs, jnp.stack(ns)


def _layer_weights(g_mix_pre, w_in, b_igate, b_fgate, g_mlstm_out, w_out, g_mix_post, g_ffn_pre, w_up,
                   conv_w, conv_b, w_down, g_ffn_post):
    wg = jnp.zeros((D_MODEL, LANES), F32).at[:, :2 * H_M].set(w_in[:, N_QKV + N_MIX:])
    bg = jnp.zeros((1, LANES), F32).at[0, :H_M].set(b_igate.astype(F32)).at[0, H_M:2 * H_M].set(b_fgate.astype(F32))
    row = lambda v: v.astype(F32).reshape(1, -1)
    mix = lambda j: w_in[:, N_QKV + j * D_M:N_QKV + (j + 1) * D_M]
    wmt = jnp.concatenate([mix(0), mix(2), mix(3)], axis=1).T.astype(BF16)
    gw = jnp.broadcast_to(g_mlstm_out.astype(F32)[:, None], (D_M, MLSTM_STEP))
    return dict(
        g_pre=row(g_mix_pre), wa=w_in[:, :N_QKV].astype(BF16), wmt=wmt, wk=mix(1).astype(BF16),
        wg=wg.astype(BF16), bg=bg, gw=gw, wo=w_out.astype(BF16), g_post=row(g_mix_post),
        g_ffn_pre=row(g_ffn_pre), wup=w_up.astype(BF16), cw=conv_w.astype(F32), cb=row(conv_b),
        wdn=w_down.astype(BF16), g_ffn_post=row(g_ffn_post))


def _state_in(c0, n0, m0):
    nb = c0.shape[0]
    return (c0.astype(F32), n0.astype(F32).reshape(nb, H_M, 1, DK_M),
            jnp.broadcast_to(m0.astype(F32)[..., None, None], (nb, H_M, 8, LANES)))


def _state_out(c, n, m):
    return c, n[:, :, 0, :], m[:, :, 0, 0]


def _prompt_layer(x, w, bias, *, tile_in, tile_ffn):
    s_len = x.shape[0]
    wb = min(max(wd for wd, _ in PATTERNS), s_len)
    dils = tuple(d for _, d in PATTERNS if d > 1)
    outs = _inproj(x, w["g_pre"], w["wa"], w["wmt"], w["wk"], w["wg"], w["bg"],
                   tile=tile_in, dils=dils, tail_rows=wb)
    qkv1, qkv4, qkv16, tail, qmt, km, vmt, omt, gates, gates_t = outs
    att = _attn_prompt([qkv1.reshape(1, s_len, N_QKV), qkv4, qkv16], bias)
    state0 = _state_in(jnp.zeros((1, H_M, DK_M, DV_M), F32), jnp.zeros((1, H_M, DK_M), F32),
                       jnp.zeros((1, H_M), F32))
    assert s_len % MLSTM_STEP == 0
    hmt, c_o, n_o, m_o = _mlstm(qmt, km, vmt, omt, gates, gates_t, w["gw"], *state0, nb=1, c=MLSTM_STEP)
    prev = jnp.zeros((CONV_W - 1, 2 * D_FF), F32)
    y, st = _outffn(att, hmt, x, prev, w["wo"], w["g_post"], w["g_ffn_pre"], w["wup"], w["cw"], w["cb"],
                    w["wdn"], w["g_ffn_post"], tile=tile_ffn, shift=1)
    c_new, n_new, m_new = _state_out(c_o, n_o, m_o)
    k_win = tail[:, D_A:2 * D_A].reshape(1, wb, H_A, HD_A)
    v_win = tail[:, 2 * D_A:3 * D_A].reshape(1, wb, H_A, HD_A)
    return y, (k_win, v_win, c_new, n_new, m_new, st[-(CONV_W - 1):][None])


def _sample_layer(x, w, bias_c, bias_n, cache_k, cache_v, c0, n0, m0, conv_prev):
    b, t_new, _ = x.shape
    n = b * t_new
    wb = cache_k.shape[1]
    outs = _inproj(x.reshape(n, D_MODEL), w["g_pre"], w["wa"], w["wmt"], w["wk"], w["wg"], w["bg"],
                   tile=n, dils=(), tail_rows=n)
    _, tail, qmt, km, vmt, omt, gates, gates_t = outs
    to_t = lambda a: jnp.transpose(a.astype(F32), (0, 2, 3, 1)).reshape(b, D_A, wb)
    from_t = lambda a: jnp.transpose(a.reshape(b, H_A, HD_A, wb), (0, 3, 1, 2))
    att, k_win, v_win = _attn_sample(tail.reshape(b, t_new, N_QKV), to_t(cache_k), to_t(cache_v), bias_c, bias_n)

    c = LANES
    assert t_new <= c
    padr = lambda a: jnp.pad(a.reshape(b, t_new, -1), ((0, 0), (0, c - t_new), (0, 0))).reshape(b * c, -1)
    padc = lambda a: jnp.pad(a.reshape(-1, b, t_new), ((0, 0), (0, 0), (0, c - t_new))).reshape(-1, b * c)
    gpad = jnp.where(jnp.arange(LANES) < H_M, NEG_INF, 0.0).astype(F32)
    gates_p = jnp.concatenate([gates.reshape(b, t_new, LANES),
                               jnp.broadcast_to(gpad, (b, c - t_new, LANES))], axis=1).reshape(b * c, LANES)
    gates_tp = jnp.concatenate([gates_t.reshape(GATE_ROWS, b, t_new),
                                jnp.broadcast_to(gpad[:GATE_ROWS, None, None], (GATE_ROWS, b, c - t_new))],
                               axis=2).reshape(GATE_ROWS, b * c)
    hmt_p, c_o, n_o, m_o = _mlstm(padc(qmt), padr(km), padc(vmt), padc(omt), gates_p, gates_tp, w["gw"][:, :c],
                                  *_state_in(c0, n0, m0), nb=b, c=c)

    tm = lambda a: a.reshape(b, t_new, -1).transpose(1, 0, 2).reshape(n, -1)
    hmt = hmt_p.reshape(D_M, b, c)[:, :, :t_new].transpose(0, 2, 1).reshape(D_M, n)
    prev = conv_prev.astype(F32).transpose(1, 0, 2).reshape((CONV_W - 1) * b, 2 * D_FF)
    y, st = _outffn(tm(att), hmt, tm(x), prev, w["wo"], w["g_post"], w["g_ffn_pre"], w["wup"], w["cw"],
                    w["cb"], w["wdn"], w["g_ffn_post"], tile=n, shift=b)
    y = y.reshape(t_new, b, D_MODEL).transpose(1, 0, 2)
    conv_state = st.reshape(CONV_W - 1, b, 2 * D_FF).transpose(1, 0, 2)
    c_new, n_new, m_new = _state_out(c_o, n_o, m_o)
    return y, (from_t(k_win), from_t(v_win), c_new, n_new, m_new, conv_state)


def kernel(x_prompt, x_sample, cache_attn_k, cache_attn_v, state_mlstm_C, state_mlstm_n, state_mlstm_m,
           state_ffn_conv, rel_bias, g_mix_pre, w_in, b_igate, b_fgate, g_mlstm_out, w_out, g_mix_post,
           g_ffn_pre, w_up, conv_w, conv_b, w_down, g_ffn_post):
    depth = w_in.shape[0]
    batch, s_len, _ = x_prompt.shape
    assert batch == 1
    t_new = x_sample.shape[1]
    wb = cache_attn_k.shape[2]
    bias_p = _prompt_bias(rel_bias)
    bias_c, bias_n = _sample_bias(rel_bias, t_new)
    yp = x_prompt[0]
    ys = x_sample
    new_p, new_s = [], []
    for l in range(depth):
        w = _layer_weights(g_mix_pre[l], w_in[l], b_igate[l], b_fgate[l], g_mlstm_out[l], w_out[l],
                           g_mix_post[l], g_ffn_pre[l], w_up[l], conv_w[l], conv_b[l], w_down[l], g_ffn_post[l])
        yp, sp = _prompt_layer(yp, w, bias_p, tile_in=256, tile_ffn=512)
        ys, ss = _sample_layer(ys, w, bias_c, bias_n, cache_attn_k[l], cache_attn_v[l], state_mlstm_C[l],
                               state_mlstm_n[l], state_mlstm_m[l], state_ffn_conv[l])
        new_p.append(sp)
        new_s.append(ss)
    stack = lambda states, i: jnp.stack([s[i] for s in states])
    return ((yp[None], ys) + tuple(stack(new_p, i) for i in range(6))
            + tuple(stack(new_s, i) for i in range(6)))
```

```python
import functools
import math

import jax
import jax.numpy as jnp
import numpy as np
from jax import lax
from jax.experimental import pallas as pl
from jax.experimental.pallas import tpu as pltpu

F32 = jnp.float32
BF16 = jnp.bfloat16

D_MODEL = 1024
HD_A = 64
H_A = 8
D_A = H_A * HD_A
DK_M = 128
DV_M = 128
H_M = 4
D_M = H_M * DV_M
PATTERNS = ((128, 1), (512, 4), (2048, 16))
SPAN = 128
N_BUCKETS = 32
BUCKET_MAX_DIST = 2048
MLSTM_STEP = 256
D_FF = 2816
CONV_W = 3
RMS_EPS = 1e-6
N_QKV = 3 * D_A
N_MIX = 4 * D_M
LANES = 128
GATE_ROWS = 16
SUPER = 2048
VMEM_LIMIT = 56 * 1024 * 1024
NEG_INF = float("-inf")
GELU_C1 = -2.0 * math.sqrt(2.0 / math.pi)
GELU_C3 = GELU_C1 * 0.044715

for _w, _d in PATTERNS:
    assert _w // _d == SPAN


def _params(sem, vmem=VMEM_LIMIT):
    return pltpu.CompilerParams(dimension_semantics=sem, vmem_limit_bytes=vmem)


def _const_spec(shape):
    nd = len(shape)
    return pl.BlockSpec(shape, lambda *_: (0,) * nd, pipeline_mode=pl.Buffered(1))


def _rms(x, g):
    return x * lax.rsqrt(jnp.mean(x * x, axis=-1, keepdims=True) + RMS_EPS) * g


def _inproj_kernel(x_ref, g_ref, wa_ref, wmt_ref, wk_ref, wg_ref, bg_ref, *refs, tile, dils):
    nd = len(dils)
    perm_ref = refs[0] if nd else None
    refs = refs[1:] if nd else refs
    qkv1_ref = refs[0]
    dil_refs = refs[1:1 + nd]
    tail_ref, qmt_ref, km_ref, vmt_ref, omt_ref, gt_ref, gtt_ref = refs[1 + nd:]

    hb = _rms(x_ref[...], g_ref[...]).astype(BF16)
    za = jnp.dot(hb, wa_ref[...], preferred_element_type=F32)
    za = jnp.concatenate([za[:, :D_A] * (HD_A ** -0.5), za[:, D_A:]], axis=1)
    zb = za.astype(BF16)
    qkv1_ref[...] = zb
    tail_ref[...] = za

    if nd:
        zp = jnp.dot(perm_ref[...], zb, preferred_element_type=F32).astype(BF16)
        base = 0
        for dil, out_ref in zip(dils, dil_refs):
            for r in range(dil):
                out_ref[r] = zp[base + r * (tile // dil):base + (r + 1) * (tile // dil)]
            base += tile

    zt = lax.dot_general(wmt_ref[...], hb, (((1,), (1,)), ((), ())), preferred_element_type=F32)
    qmt_ref[...] = zt[0:D_M].astype(BF16)
    vmt_ref[...] = zt[D_M:2 * D_M].astype(BF16)
    omt_ref[...] = zt[2 * D_M:3 * D_M]
    zk = jnp.dot(hb, wk_ref[...], preferred_element_type=F32)
    km_ref[...] = (zk * (DK_M ** -0.5)).astype(BF16)

    zg = jnp.dot(hb, wg_ref[...], preferred_element_type=F32) + bg_ref[...]
    logsig = jnp.minimum(zg, 0.0) - jnp.log1p(jnp.exp(-jnp.abs(zg)))
    glane = lax.broadcasted_iota(jnp.int32, zg.shape, 1)
    gates = jnp.where(glane < H_M, zg, logsig)
    gt_ref[...] = gates
    gtt_ref[...] = gates.T[0:GATE_ROWS]


def _row_permutation(tile, dils):
    p = np.zeros((len(dils) * tile, tile), np.float32)
    for j, dil in enumerate(dils):
        for r in range(dil):
            for m in range(tile // dil):
                p[j * tile + r * (tile // dil) + m, m * dil + r] = 1.0
    return jnp.asarray(p, BF16)


def _inproj(x, g, wa, wmt, wk, wg, bg, *, tile, dils, tail_rows):
    n = x.shape[0]
    assert n % tile == 0 and tail_rows % tile == 0 and (tile % LANES == 0 or tile == n)
    for d in dils:
        assert tile % (16 * d) == 0
    steps = n // tile
    first_tail = (n - tail_rows) // tile
    row = lambda w: pl.BlockSpec((tile, w), lambda i: (i, 0))
    col = lambda h: pl.BlockSpec((h, tile), lambda i: (0, i))
    out_shape = [jax.ShapeDtypeStruct((n, N_QKV), BF16)]
    out_specs = [row(N_QKV)]
    for d in dils:
        out_shape.append(jax.ShapeDtypeStruct((d, n // d, N_QKV), BF16))
        out_specs.append(pl.BlockSpec((d, tile // d, N_QKV), lambda i: (0, i, 0)))
    out_shape += [jax.ShapeDtypeStruct((tail_rows, N_QKV), F32),
                  jax.ShapeDtypeStruct((D_M, n), BF16), jax.ShapeDtypeStruct((n, D_M), BF16),
                  jax.ShapeDtypeStruct((D_M, n), BF16), jax.ShapeDtypeStruct((D_M, n), F32),
                  jax.ShapeDtypeStruct((n, LANES), F32), jax.ShapeDtypeStruct((GATE_ROWS, n), F32)]
    out_specs += [pl.BlockSpec((tile, N_QKV), lambda i: (jnp.maximum(i - first_tail, 0), 0)),
                  col(D_M), row(D_M), col(D_M), col(D_M), row(LANES), col(GATE_ROWS)]
    in_specs = [row(D_MODEL), _const_spec((1, D_MODEL)), _const_spec((D_MODEL, N_QKV)),
                _const_spec((3 * D_M, D_MODEL)), _const_spec((D_MODEL, D_M)),
                _const_spec((D_MODEL, LANES)), _const_spec((1, LANES))]
    args = [x, g, wa, wmt, wk, wg, bg]
    if dils:
        in_specs.append(_const_spec((len(dils) * tile, tile)))
        args.append(_row_permutation(tile, dils))
    return pl.pallas_call(
        functools.partial(_inproj_kernel, tile=tile, dils=dils),
        grid=(steps,),
        in_specs=in_specs,
        out_specs=out_specs,
        out_shape=out_shape,
        compiler_params=_params(("arbitrary",)),
        name="inproj",
    )(*args)


def _attn_prompt_kernel(*refs):
    np_ = len(PATTERNS)
    in_refs = refs[:5 * np_]
    bvec_ref, o_ref = refs[5 * np_], refs[5 * np_ + 1]
    scr = refs[5 * np_ + 2:]
    kbufs, vbufs = scr[0:np_], scr[np_:2 * np_]
    num_s, l_s, m_s, bias_s = scr[2 * np_:]
    n = pl.program_id(1)

    lane = lax.broadcasted_iota(jnp.int32, (SPAN, LANES), 1)
    even = lane < HD_A
    ones = jnp.ones((2 * SPAN, LANES), BF16)

    @pl.when(n == 0)
    def _():
        key_col = lax.broadcasted_iota(jnp.int32, (SPAN, 2 * SPAN), 1)
        for p in range(np_):
            for e in range(2):
                vb = jnp.broadcast_to(bvec_ref[p, 0, e:e + 1, :], (SPAN, 2 * SPAN))
                table = pltpu.roll(vb, 0, 1, stride=1, stride_axis=0)
                bias_s[p, 0, e * SPAN:(e + 1) * SPAN, :] = table
                bias_s[p, 1, e * SPAN:(e + 1) * SPAN, :] = jnp.where(key_col >= SPAN, table, NEG_INF)

    for p in range(np_):
        q_ref, k_ref, v_ref, kp_ref, vp_ref = in_refs[5 * p:5 * p + 5]
        kbufs[p][:, 0:SPAN, :] = kp_ref[...]
        kbufs[p][:, SPAN:, :] = k_ref[...]
        vbufs[p][:, 0:SPAN, :] = vp_ref[...]
        vbufs[p][:, SPAN:, :] = v_ref[...]

    first_n = (n == 0).astype(jnp.int32)

    def unit(p, j):
        dil = PATTERNS[p][1]
        r, mb = divmod(j, SUPER // dil // SPAN)
        row0 = mb * SPAN
        qb = in_refs[5 * p][r, row0:row0 + SPAN, :]
        kb = kbufs[p][r, row0:row0 + 2 * SPAN, :]
        vb = vbufs[p][r, row0:row0 + 2 * SPAN, :]
        bias = bias_s[p, first_n] if mb == 0 else bias_s[p, 0]
        zero = jnp.zeros_like(qb)
        qs = jnp.concatenate([jnp.where(even, qb, zero), jnp.where(even, zero, qb)], axis=0)
        s = lax.dot_general(qs, kb, (((1,), (1,)), ((), ())), preferred_element_type=F32) + bias
        m = jnp.max(s, axis=-1, keepdims=True)
        pr = jnp.exp(s - m).astype(BF16)
        res = jnp.dot(pr, jnp.concatenate([vb, ones], axis=1), preferred_element_type=F32)
        mb_ = jnp.broadcast_to(m, (2 * SPAN, LANES))
        rows = pl.ds(row0, SPAN) if dil == 1 else pl.ds(row0 * dil + r, SPAN, stride=dil)
        num_s[p, rows, :] = jnp.where(even, res[:SPAN, :LANES], res[SPAN:, :LANES])
        l_s[p, rows, :] = jnp.where(even, res[:SPAN, LANES:], res[SPAN:, LANES:])
        m_s[p, rows, :] = jnp.where(even, mb_[:SPAN], mb_[SPAN:])

    for j in range(SUPER // SPAN):
        for p in range(np_):
            unit(p, j)

    def combine(j, carry):
        rows = pl.ds(pl.multiple_of(j * SPAN, SPAN), SPAN)
        ms = [m_s[p, rows, :] for p in range(np_)]
        m = functools.reduce(jnp.maximum, ms)
        num = jnp.zeros((SPAN, LANES), F32)
        den = jnp.zeros((SPAN, LANES), F32)
        for p in range(np_):
            a = jnp.exp(ms[p] - m)
            num = num + a * num_s[p, rows, :]
            den = den + a * l_s[p, rows, :]
        o_ref[rows, :] = (num / den).astype(o_ref.dtype)
        return carry

    lax.fori_loop(0, SUPER // SPAN, combine, 0)


def _attn_prompt(qkvs, bvec):
    s_len = qkvs[0].shape[1]
    assert s_len % SUPER == 0
    nsb = s_len // SUPER
    npair = D_A // LANES
    in_specs, args, scratch = [], [], []
    for (_, dil), a in zip(PATTERNS, qkvs):
        rows = SUPER // dil
        nblk = rows // SPAN
        for part in range(3):
            in_specs.append(pl.BlockSpec((dil, rows, LANES),
                                         lambda hp, n, part=part: (0, n, part * npair + hp)))
            args.append(a)
        for part in (1, 2):
            in_specs.append(pl.BlockSpec((dil, SPAN, LANES),
                                         lambda hp, n, part=part, nblk=nblk:
                                         (0, jnp.maximum(n * nblk - 1, 0), part * npair + hp)))
            args.append(a)
    in_specs.append(pl.BlockSpec((len(PATTERNS), 1, 2, 2 * SPAN), lambda hp, n: (0, hp, 0, 0)))
    args.append(bvec)
    for _ in range(2):
        for _, dil in PATTERNS:
            scratch.append(pltpu.VMEM((dil, SPAN + SUPER // dil, LANES), BF16))
    scratch += [pltpu.VMEM((len(PATTERNS), SUPER, LANES), F32)] * 3
    scratch.append(pltpu.VMEM((len(PATTERNS), 2, 2 * SPAN, 2 * SPAN), F32))
    return pl.pallas_call(
        _attn_prompt_kernel,
        grid=(npair, nsb),
        in_specs=in_specs,
        out_specs=pl.BlockSpec((SUPER, LANES), lambda hp, n: (n, hp)),
        out_shape=jax.ShapeDtypeStruct((s_len, D_A), BF16),
        scratch_shapes=scratch,
        compiler_params=_params(("arbitrary", "arbitrary")),
        name="attn_prompt",
    )(*args)


def _attn_sample_kernel(qkv_ref, kc_ref, vc_ref, *refs, t_new, wb):
    np_ = len(PATTERNS)
    u_refs = refs[:np_]
    w_ref, att_ref, ko_ref, vo_ref = refs[np_:np_ + 4]
    bc_refs = refs[np_ + 4:2 * np_ + 4]
    bn_ref = refs[2 * np_ + 4]

    @pl.when(pl.program_id(0) == 0)
    def _():
        new_lane = lax.broadcasted_iota(jnp.int32, (t_new, LANES), 1) >= LANES - t_new
        for p, (win, _) in enumerate(PATTERNS):
            for h in range(H_A):
                rows = slice(h * t_new, (h + 1) * t_new)
                u = jnp.broadcast_to(u_refs[p][h:h + 1, :], (t_new, win + LANES))
                bc_refs[p][rows, :] = pltpu.roll(u, 0, 1, stride=1, stride_axis=0)[:, :win]
                w = jnp.broadcast_to(w_ref[p, h:h + 1, :], (t_new, LANES))
                bn_ref[p, rows, :] = jnp.where(new_lane, pltpu.roll(w, 0, 1, stride=1, stride_axis=0), NEG_INF)

    qkv = qkv_ref[0]
    qf, kn, vn = qkv[:, 0:D_A], qkv[:, D_A:2 * D_A], qkv[:, 2 * D_A:3 * D_A]
    nrow = H_A * t_new
    row_h = lax.broadcasted_iota(jnp.int32, (nrow, D_A), 0) // t_new
    lane_h = lax.broadcasted_iota(jnp.int32, (nrow, D_A), 1) // HD_A
    own = row_h == lane_h
    qs = jnp.where(own, jnp.concatenate([qf] * H_A, axis=0), 0.0).astype(BF16)
    pad = jnp.zeros((LANES - t_new, D_A), F32)
    knt = jnp.concatenate([pad, kn], axis=0).T
    vnt = jnp.concatenate([pad, vn], axis=0).T
    kct = kc_ref[0]
    vct = vc_ref[0]
    s_c = jnp.dot(qs, kct.astype(BF16), preferred_element_type=F32)
    s_n = jnp.dot(qs, knt.astype(BF16), preferred_element_type=F32)

    wins = [w for w, _ in PATTERNS]
    es_c, es_n = [], []
    m = None
    for p, win in enumerate(wins):
        e_c = s_c[:, wb - win:] + bc_refs[p][...]
        e_n = s_n + bn_ref[p]
        es_c.append(e_c)
        es_n.append(e_n)
        mp = jnp.maximum(jnp.max(e_c, axis=-1, keepdims=True), jnp.max(e_n, axis=-1, keepdims=True))
        m = mp if m is None else jnp.maximum(m, mp)
    pn = None
    for p in range(np_):
        e = jnp.exp(es_n[p] - m)
        pn = e if pn is None else pn + e
    order = sorted(range(np_), key=lambda p: -wins[p])
    assert wins[order[0]] == wb
    pc = jnp.exp(es_c[order[0]] - m)
    for p in order[1:]:
        e = jnp.exp(es_c[p] - m)
        pc = jnp.concatenate([pc[:, :wb - wins[p]], pc[:, wb - wins[p]:] + e], axis=1)
    den = jnp.sum(pc, axis=-1, keepdims=True) + jnp.sum(pn, axis=-1, keepdims=True)
    nt = (((1,), (1,)), ((), ()))
    num = (lax.dot_general(pc.astype(BF16), vct.astype(BF16), nt, preferred_element_type=F32)
           + lax.dot_general(pn.astype(BF16), vnt.astype(BF16), nt, preferred_element_type=F32))
    o = jnp.where(own, num / den, 0.0)
    att = o[0:t_new]
    for h in range(1, H_A):
        att = att + o[h * t_new:(h + 1) * t_new]
    att_ref[0] = att.astype(att_ref.dtype)

    is_new = lax.broadcasted_iota(jnp.int32, (D_A, LANES), 1) >= LANES - t_new
    for src, new, dst in ((kct, knt, ko_ref), (vct, vnt, vo_ref)):
        rolled = pltpu.roll(src, wb - t_new, 1)
        dst[0, :, 0:wb - LANES] = rolled[:, 0:wb - LANES]
        dst[0, :, wb - LANES:wb] = jnp.where(is_new, new, rolled[:, wb - LANES:wb])


def _attn_sample(qkv, cache_kt, cache_vt, bias_u, bias_w):
    b, t_new, _ = qkv.shape
    wb = cache_kt.shape[2]
    assert wb == max(w for w, _ in PATTERNS) and t_new % 8 == 0 and t_new <= LANES
    nrow = H_A * t_new
    blk = lambda r, w: pl.BlockSpec((1, r, w), lambda i: (i, 0, 0))
    return pl.pallas_call(
        functools.partial(_attn_sample_kernel, t_new=t_new, wb=wb),
        grid=(b,),
        in_specs=[blk(t_new, N_QKV), blk(D_A, wb), blk(D_A, wb)]
                 + [_const_spec((H_A, w + LANES)) for w, _ in PATTERNS]
                 + [_const_spec((len(PATTERNS), H_A, LANES))],
        out_specs=[blk(t_new, D_A), blk(D_A, wb), blk(D_A, wb)],
        out_shape=[jax.ShapeDtypeStruct((b, t_new, D_A), BF16),
                   jax.ShapeDtypeStruct((b, D_A, wb), F32), jax.ShapeDtypeStruct((b, D_A, wb), F32)],
        scratch_shapes=[pltpu.VMEM((nrow, w), F32) for w, _ in PATTERNS]
                       + [pltpu.VMEM((len(PATTERNS), nrow, LANES), F32)],
        compiler_params=_params(("arbitrary",)),
        name="attn_sample",
    )(qkv, cache_kt, cache_vt, *bias_u, bias_w)


def _split3(x):
    a = x.astype(BF16)
    r = x - a.astype(F32)
    b = r.astype(BF16)
    c = (r - b.astype(F32)).astype(BF16)
    return a, b, c


def _mlstm_kernel(qt_ref, k_ref, vt_ref, ot_ref, g_ref, gtt_ref, gw_ref, c0_ref, n0_ref, m0_ref,
                  h_ref, c_out, n_out, m_out, cx_s, m_s, *, c):
    ci = pl.program_id(1)

    @pl.when(ci == 0)
    def _():
        for h in range(H_M):
            cx_s[h, 0:DV_M, :] = c0_ref[0, h].T
            cx_s[h, DV_M:, :] = jnp.broadcast_to(n0_ref[0, h], (DV_M, DK_M))
        m_s[...] = m0_ref[0]

    gates = g_ref[...]
    gates_t = gtt_ref[...]
    lane = lax.broadcasted_iota(jnp.int32, gates.shape, 1)
    lf = jnp.where(jnp.logical_and(lane >= H_M, lane < 2 * H_M), gates, 0.0)
    grow = lax.broadcasted_iota(jnp.int32, gates_t.shape, 0)
    lf_t = jnp.where(jnp.logical_and(grow >= H_M, grow < 2 * H_M), gates_t, 0.0)
    ri = lax.broadcasted_iota(jnp.int32, (c, c), 0)
    cj = lax.broadcasted_iota(jnp.int32, (c, c), 1)
    upper = ri <= cj
    tril = (ri >= cj).astype(BF16)
    triu = upper.astype(BF16)
    bsum = brow = None
    for part, part_t in zip(_split3(lf), _split3(lf_t)):
        t1 = jnp.dot(tril, part, preferred_element_type=F32)
        t2 = jnp.dot(part_t, triu, preferred_element_type=F32)
        bsum = t1 if bsum is None else bsum + t1
        brow = t2 if brow is None else brow + t2
    ones = jnp.ones((DV_M, c), BF16)

    for h in range(H_M):
        sl = slice(h * DK_M, (h + 1) * DK_M)
        qt, kh, vt = qt_ref[sl, :], k_ref[:, sl], vt_ref[sl, :]
        b_row = brow[H_M + h:H_M + h + 1, :]
        i_row = gates_t[h:h + 1, :]
        a_col = gates[:, h:h + 1] - bsum[:, H_M + h:H_M + h + 1]
        m_prev = m_s[h, 0:1, 0:1]
        dmat = jnp.where(upper, b_row + a_col, NEG_INF)
        inter = b_row + m_prev
        mt = jnp.maximum(inter, jnp.max(dmat, axis=0, keepdims=True))
        st = jnp.dot(kh, qt, preferred_element_type=F32)
        smt = (st * jnp.exp(dmat - mt)).astype(BF16)
        iw = jnp.exp(inter - mt)
        vext = jnp.concatenate([vt, ones], axis=0)
        cxh = cx_s[h]
        ne = (jnp.dot(vext, smt, preferred_element_type=F32)
              + iw * jnp.dot(cxh.astype(BF16), qt, preferred_element_type=F32))
        num, den = ne[:DV_M], ne[DV_M:]
        hq = num / jnp.maximum(jnp.abs(den), jnp.exp(-mt))
        hn = hq * lax.rsqrt(jnp.mean(hq * hq, axis=0, keepdims=True) + RMS_EPS) * gw_ref[sl, :]
        h_ref[sl, :] = (jax.nn.sigmoid(ot_ref[sl, :]) * hn).astype(h_ref.dtype)

        b_last = b_row[:, c - 1:c]
        g_row = b_last - b_row + i_row
        m_new = jnp.maximum(b_last + m_prev, jnp.max(g_row, axis=1, keepdims=True))
        ws = jnp.exp(g_row - m_new)
        wc = jnp.exp(b_last + m_prev - m_new)
        vw = (vext.astype(F32) * ws).astype(BF16)
        cx_s[h] = wc * cxh + jnp.dot(vw, kh, preferred_element_type=F32)
        m_s[h] = jnp.broadcast_to(m_new, m_s.shape[1:])

    @pl.when(ci == pl.num_programs(1) - 1)
    def _():
        for h in range(H_M):
            c_out[0, h] = cx_s[h, 0:DV_M, :].T
            n_out[0, h] = cx_s[h, DV_M:DV_M + 8, :]
        m_out[0] = m_s[...]


def _mlstm(qmt, km, vmt, omt, gates, gates_t, gw, c0, n0, m0, *, nb, c):
    n = km.shape[0]
    nc = n // (nb * c)
    assert nb * nc * c == n and c % LANES == 0
    row = lambda w: pl.BlockSpec((c, w), lambda b, i: (b * nc + i, 0))
    col = lambda h: pl.BlockSpec((h, c), lambda b, i: (0, b * nc + i))
    st = lambda shape: pl.BlockSpec((1,) + shape, lambda b, i: (b,) + (0,) * len(shape))
    return pl.pallas_call(
        functools.partial(_mlstm_kernel, c=c),
        grid=(nb, nc),
        in_specs=[col(D_M), row(D_M), col(D_M), col(D_M), row(LANES), col(GATE_ROWS), _const_spec((D_M, c)),
                  st((H_M, DK_M, DV_M)), st((H_M, 1, DK_M)), st((H_M, 8, LANES))],
        out_specs=[col(D_M), st((H_M, DK_M, DV_M)), st((H_M, 8, DK_M)), st((H_M, 8, LANES))],
        out_shape=[jax.ShapeDtypeStruct((D_M, n), BF16),
                   jax.ShapeDtypeStruct((nb, H_M, DK_M, DV_M), F32),
                   jax.ShapeDtypeStruct((nb, H_M, 8, DK_M), F32),
                   jax.ShapeDtypeStruct((nb, H_M, 8, LANES), F32)],
        scratch_shapes=[pltpu.VMEM((H_M, 2 * DV_M, DK_M), F32), pltpu.VMEM((H_M, 8, LANES), F32)],
        compiler_params=_params(("arbitrary", "arbitrary")),
        name="mlstm",
    )(qmt, km, vmt, omt, gates, gates_t, gw, c0, n0, m0)


def _outffn_kernel(att_ref, hmt_ref, x_ref, prev_ref, wo_ref, gpost_ref, gpre_ref, wup_ref, cw_ref, cb_ref,
                   wdn_ref, gffn_ref, y_ref, st_ref, u_ref, a_ref, *, tile, shift, hdr):
    i = pl.program_id(0)

    @pl.when(i == 0)
    def _():
        u_ref[hdr - 2 * shift:hdr, :] = prev_ref[...]

    @pl.when(i > 0)
    def _():
        u_ref[0:hdr, :] = u_ref[tile:tile + hdr, :]

    mixed = (jnp.dot(att_ref[...], wo_ref[0:D_A, :], preferred_element_type=F32)
             + lax.dot_general(hmt_ref[...], wo_ref[D_A:, :], (((0,), (0,)), ((), ())),
                               preferred_element_type=F32))
    x1 = x_ref[...] + _rms(mixed, gpost_ref[...])
    h2 = _rms(x1, gpre_ref[...]).astype(BF16)
    u_ref[hdr:hdr + tile, :] = jnp.dot(h2, wup_ref[...], preferred_element_type=F32)

    def conv(c):
        cols = slice(c * LANES, (c + 1) * LANES)
        y = cb_ref[:, cols] + u_ref[hdr - 2 * shift:hdr - 2 * shift + tile, cols] * cw_ref[0:1, cols]
        y = y + u_ref[hdr - shift:hdr - shift + tile, cols] * cw_ref[1:2, cols]
        return y + u_ref[hdr:hdr + tile, cols] * cw_ref[2:3, cols]

    nff = D_FF // LANES
    for c in range(nff):
        gate, val = conv(c), conv(nff + c)
        e = jnp.exp(gate * (GELU_C1 + GELU_C3 * (gate * gate)))
        a_ref[:, c * LANES:(c + 1) * LANES] = (gate * val / (1.0 + e)).astype(BF16)

    y2 = jnp.dot(a_ref[...], wdn_ref[...], preferred_element_type=F32)
    y_ref[...] = x1 + _rms(y2, gffn_ref[...])
    st_ref[...] = u_ref[tile:tile + hdr, :]


def _outffn(att, hmt, x, prev, wo, gpost, gpre, wup, cw, cb, wdn, gffn, *, tile, shift):
    n = x.shape[0]
    assert n % tile == 0 and (shift == 1 or n == tile)
    hdr = max(8, 2 * shift)
    assert hdr % 8 == 0 and (tile % LANES == 0 or tile == n)
    row = lambda w: pl.BlockSpec((tile, w), lambda i: (i, 0))
    return pl.pallas_call(
        functools.partial(_outffn_kernel, tile=tile, shift=shift, hdr=hdr),
        grid=(n // tile,),
        in_specs=[row(D_A), pl.BlockSpec((D_M, tile), lambda i: (0, i)), row(D_MODEL),
                  _const_spec((2 * shift, 2 * D_FF)),
                  _const_spec((D_A + D_M, D_MODEL)), _const_spec((1, D_MODEL)), _const_spec((1, D_MODEL)),
                  _const_spec((D_MODEL, 2 * D_FF)), _const_spec((CONV_W, 2 * D_FF)), _const_spec((1, 2 * D_FF)),
                  _const_spec((D_FF, D_MODEL)), _const_spec((1, D_MODEL))],
        out_specs=[row(D_MODEL), pl.BlockSpec((hdr, 2 * D_FF), lambda i: (0, 0))],
        out_shape=[jax.ShapeDtypeStruct((n, D_MODEL), F32), jax.ShapeDtypeStruct((hdr, 2 * D_FF), F32)],
        scratch_shapes=[pltpu.VMEM((hdr + tile, 2 * D_FF), F32), pltpu.VMEM((tile, D_FF), BF16)],
        compiler_params=_params(("arbitrary",)),
        name="outffn",
    )(att, hmt, x, prev, wo, gpost, gpre, wup, cw, cb, wdn, gffn)


def _t5_bucket(dist):
    max_exact = N_BUCKETS // 2
    d = jnp.maximum(dist, max_exact).astype(F32)
    large = max_exact + (jnp.log(d / max_exact) / math.log(BUCKET_MAX_DIST / max_exact)
                         * (N_BUCKETS - max_exact)).astype(jnp.int32)
    large = jnp.minimum(large, N_BUCKETS - 1)
    return jnp.where(dist < max_exact, dist, large)


def _pattern_bias(rel_bias, dil):
    j = jnp.arange(SPAN + 1)
    return rel_bias[_t5_bucket(j * dil)].T.astype(F32)


def _prompt_bias(rel_bias):
    out = []
    for _, dil in PATTERNS:
        bh = _pattern_bias(rel_bias, dil)
        v = jnp.concatenate([bh[:, ::-1], jnp.full((H_A, SPAN - 1), NEG_INF, F32)], axis=1)
        out.append(v.reshape(H_A // 2, 2, 2 * SPAN))
    return jnp.stack(out)


def _sample_bias(rel_bias, t_new):
    us, ws = [], []
    fill = lambda n: jnp.full((H_A, n), NEG_INF, F32)
    for win, dil in PATTERNS:
        bh = _pattern_bias(rel_bias, dil)
        dist = jnp.arange((SPAN + 1) * dil)
        g = jnp.where(dist % dil == 0, jnp.repeat(bh, dil, axis=1), NEG_INF)[:, :win + 1]
        us.append(jnp.concatenate([g[:, ::-1], fill(LANES - 1)], axis=1))
        ws.append(jnp.concatenate([g[:, :LANES - t_new + 1][:, ::-1], fill(t_new - 1)], axis=1))
    return us, jnp.stack(ws)


def _layer_weights(g_mix_pre, w_in, b_igate, b_fgate, g_mlstm_out, w_out, g_mix_post, g_ffn_pre, w_up,
                   conv_w, conv_b, w_down, g_ffn_post):
    wg = jnp.zeros((D_MODEL, LANES), F32).at[:, :2 * H_M].set(w_in[:, N_QKV + N_MIX:])
    bg = jnp.zeros((1, LANES), F32).at[0, :H_M].set(b_igate.astype(F32)).at[0, H_M:2 * H_M].set(b_fgate.astype(F32))
    row = lambda v: v.astype(F32).reshape(1, -1)
    mix = lambda j: w_in[:, N_QKV + j * D_M:N_QKV + (j + 1) * D_M]
    wmt = jnp.concatenate([mix(0), mix(2), mix(3)], axis=1).T.astype(BF16)
    gw = jnp.broadcast_to(g_mlstm_out.astype(F32)[:, None], (D_M, MLSTM_STEP))
    return dict(
        g_pre=row(g_mix_pre), wa=w_in[:, :N_QKV].astype(BF16), wmt=wmt, wk=mix(1).astype(BF16),
        wg=wg.astype(BF16), bg=bg, gw=gw, wo=w_out.astype(BF16), g_post=row(g_mix_post),
        g_ffn_pre=row(g_ffn_pre), wup=w_up.astype(BF16), cw=conv_w.astype(F32), cb=row(conv_b),
        wdn=w_down.astype(BF16), g_ffn_post=row(g_ffn_post))


def _state_in(c0, n0, m0):
    nb = c0.shape[0]
    return (c0.astype(F32), n0.astype(F32).reshape(nb, H_M, 1, DK_M),
            jnp.broadcast_to(m0.astype(F32)[..., None, None], (nb, H_M, 8, LANES)))


def _state_out(c, n, m):
    return c, n[:, :, 0, :], m[:, :, 0, 0]


def _prompt_layer(x, w, bias, *, tile_in, tile_ffn):
    s_len = x.shape[0]
    wb = min(max(wd for wd, _ in PATTERNS), s_len)
    dils = tuple(d for _, d in PATTERNS if d > 1)
    outs = _inproj(x, w["g_pre"], w["wa"], w["wmt"], w["wk"], w["wg"], w["bg"],
                   tile=tile_in, dils=dils, tail_rows=wb)
    qkv1, qkv4, qkv16, tail, qmt, km, vmt, omt, gates, gates_t = outs
    att = _attn_prompt([qkv1.reshape(1, s_len, N_QKV), qkv4, qkv16], bias)
    state0 = _state_in(jnp.zeros((1, H_M, DK_M, DV_M), F32), jnp.zeros((1, H_M, DK_M), F32),
                       jnp.zeros((1, H_M), F32))
    assert s_len % MLSTM_STEP == 0
    hmt, c_o, n_o, m_o = _mlstm(qmt, km, vmt, omt, gates, gates_t, w["gw"], *state0, nb=1, c=MLSTM_STEP)
    prev = jnp.zeros((CONV_W - 1, 2 * D_FF), F32)
    y, st = _outffn(att, hmt, x, prev, w["wo"], w["g_post"], w["g_ffn_pre"], w["wup"], w["cw"], w["cb"],
                    w["wdn"], w["g_ffn_post"], tile=tile_ffn, shift=1)
    c_new, n_new, m_new = _state_out(c_o, n_o, m_o)
    k_win = tail[:, D_A:2 * D_A].reshape(1, wb, H_A, HD_A)
    v_win = tail[:, 2 * D_A:3 * D_A].reshape(1, wb, H_A, HD_A)
    return y, (k_win, v_win, c_new, n_new, m_new, st[-(CONV_W - 1):][None])


def _sample_layer(x, w, bias_c, bias_n, cache_k, cache_v, c0, n0, m0, conv_prev):
    b, t_new, _ = x.shape
    n = b * t_new
    wb = cache_k.shape[1]
    outs = _inproj(x.reshape(n, D_MODEL), w["g_pre"], w["wa"], w["wmt"], w["wk"], w["wg"], w["bg"],
                   tile=n, dils=(), tail_rows=n)
    _, tail, qmt, km, vmt, omt, gates, gates_t = outs
    to_t = lambda a: jnp.transpose(a.astype(F32), (0, 2, 3, 1)).reshape(b, D_A, wb)
    from_t = lambda a: jnp.transpose(a.reshape(b, H_A, HD_A, wb), (0, 3, 1, 2))
    att, k_win, v_win = _attn_sample(tail.reshape(b, t_new, N_QKV), to_t(cache_k), to_t(cache_v), bias_c, bias_n)

    c = LANES
    assert t_new <= c
    padr = lambda a: jnp.pad(a.reshape(b, t_new, -1), ((0, 0), (0, c - t_new), (0, 0))).reshape(b * c, -1)
    padc = lambda a: jnp.pad(a.reshape(-1, b, t_new), ((0, 0), (0, 0), (0, c - t_new))).reshape(-1, b * c)
    gpad = jnp.where(jnp.arange(LANES) < H_M, NEG_INF, 0.0).astype(F32)
    gates_p = jnp.concatenate([gates.reshape(b, t_new, LANES),
                               jnp.broadcast_to(gpad, (b, c - t_new, LANES))], axis=1).reshape(b * c, LANES)
    gates_tp = jnp.concatenate([gates_t.reshape(GATE_ROWS, b, t_new),
                                jnp.broadcast_to(gpad[:GATE_ROWS, None, None], (GATE_ROWS, b, c - t_new))],
                               axis=2).reshape(GATE_ROWS, b * c)
    hmt_p, c_o, n_o, m_o = _mlstm(padc(qmt), padr(km), padc(vmt), padc(omt), gates_p, gates_tp, w["gw"][:, :c],
                                  *_state_in(c0, n0, m0), nb=b, c=c)

    tm = lambda a: a.reshape(b, t_new, -1).transpose(1, 0, 2).reshape(n, -1)
    hmt = hmt_p.reshape(D_M, b, c)[:, :, :t_new].transpose(0, 2, 1).reshape(D_M, n)
    prev = conv_prev.astype(F32).transpose(1, 0, 2).reshape((CONV_W - 1) * b, 2 * D_FF)
    y, st = _outffn(tm(att), hmt, tm(x), prev, w["wo"], w["g_post"], w["g_ffn_pre"], w["wup"], w["cw"],
                    w["cb"], w["wdn"], w["g_ffn_post"], tile=n, shift=b)
    y = y.reshape(t_new, b, D_MODEL).transpose(1, 0, 2)
    conv_state = st.reshape(CONV_W - 1, b, 2 * D_FF).transpose(1, 0, 2)
    c_new, n_new, m_new = _state_out(c_o, n_o, m_o)
    return y, (from_t(k_win), from_t(v_win), c_new, n_new, m_new, conv_state)


def kernel(x_prompt, x_sample, cache_attn_k, cache_attn_v, state_mlstm_C, state_mlstm_n, state_mlstm_m,
           state_ffn_conv, rel_bias, g_mix_pre, w_in, b_igate, b_fgate, g_mlstm_out, w_out, g_mix_post,
           g_ffn_pre, w_up, conv_w, conv_b, w_down, g_ffn_post):
    depth = w_in.shape[0]
    batch, s_len, _ = x_prompt.shape
    assert batch == 1
    t_new = x_sample.shape[1]
    wb = cache_attn_k.shape[2]
    bias_p = _prompt_bias(rel_bias)
    bias_c, bias_n = _sample_bias(rel_bias, t_new)
    yp = x_prompt[0]
    ys = x_sample
    new_p, new_s = [], []
    for l in range(depth):
        w = _layer_weights(g_mix_pre[l], w_in[l], b_igate[l], b_fgate[l], g_mlstm_out[l], w_out[l],
                           g_mix_post[l], g_ffn_pre[l], w_up[l], conv_w[l], conv_b[l], w_down[l], g_ffn_post[l])
        yp, sp = _prompt_layer(yp, w, bias_p, tile_in=256, tile_ffn=512)
        ys, ss = _sample_layer(ys, w, bias_c, bias_n, cache_attn_k[l], cache_attn_v[l], state_mlstm_C[l],
                               state_mlstm_n[l], state_mlstm_m[l], state_ffn_conv[l])
        new_p.append(sp)
        new_s.append(ss)
    stack = lambda states, i: jnp.stack([s[i] for s in states])
    return ((yp[None], ys) + tuple(stack(new_p, i) for i in range(6))
            + tuple(stack(new_s, i) for i in range(6)))
```

```python
import functools
import math

import jax
import jax.numpy as jnp
import numpy as np
from jax import lax
from jax.experimental import pallas as pl
from jax.experimental.pallas import tpu as pltpu

F32 = jnp.float32
BF16 = jnp.bfloat16

D_MODEL = 1024
HD_A = 64
H_A = 8
D_A = H_A * HD_A
DK_M = 128
DV_M = 128
H_M = 4
D_M = H_M * DV_M
PATTERNS = ((128, 1), (512, 4), (2048, 16))
SPAN = 128
N_BUCKETS = 32
BUCKET_MAX_DIST = 2048
MLSTM_STEP = 256
D_FF = 2816
CONV_W = 3
RMS_EPS = 1e-6
N_QKV = 3 * D_A
N_MIX = 4 * D_M
LANES = 128
GATE_ROWS = 16
SUPER = 2048
VMEM_LIMIT = 56 * 1024 * 1024
NEG_INF = float("-inf")
GELU_C1 = -2.0 * math.sqrt(2.0 / math.pi)
GELU_C3 = GELU_C1 * 0.044715

for _w, _d in PATTERNS:
    assert _w // _d == SPAN


def _params(sem, vmem=VMEM_LIMIT):
    return pltpu.CompilerParams(dimension_semantics=sem, vmem_limit_bytes=vmem)


def _const_spec(shape):
    nd = len(shape)
    return pl.BlockSpec(shape, lambda *_: (0,) * nd, pipeline_mode=pl.Buffered(1))


def _rms(x, g):
    return x * lax.rsqrt(jnp.mean(x * x, axis=-1, keepdims=True) + RMS_EPS) * g


def _inproj_kernel(x_ref, g_ref, wa_ref, wmt_ref, wk_ref, wg_ref, bg_ref, *refs, tile, dils):
    nd = len(dils)
    perm_ref = refs[0] if nd else None
    refs = refs[1:] if nd else refs
    qkv1_ref = refs[0]
    dil_refs = refs[1:1 + nd]
    tail_ref, qmt_ref, km_ref, vmt_ref, omt_ref, gt_ref, gtt_ref = refs[1 + nd:]

    hb = _rms(x_ref[...], g_ref[...]).astype(BF16)
    za = jnp.dot(hb, wa_ref[...], preferred_element_type=F32)
    za = jnp.concatenate([za[:, :D_A] * (HD_A ** -0.5), za[:, D_A:]], axis=1)
    zb = za.astype(BF16)
    qkv1_ref[...] = zb
    tail_ref[...] = za

    if nd:
        zp = jnp.dot(perm_ref[...], zb, preferred_element_type=F32).astype(BF16)
        base = 0
        for dil, out_ref in zip(dils, dil_refs):
            for r in range(dil):
                out_ref[r] = zp[base + r * (tile // dil):base + (r + 1) * (tile // dil)]
            base += tile

    zt = lax.dot_general(wmt_ref[...], hb, (((1,), (1,)), ((), ())), preferred_element_type=F32)
    qmt_ref[...] = zt[0:D_M].astype(BF16)
    vmt_ref[...] = zt[D_M:2 * D_M].astype(BF16)
    omt_ref[...] = zt[2 * D_M:3 * D_M]
    zk = jnp.dot(hb, wk_ref[...], preferred_element_type=F32)
    km_ref[...] = (zk * (DK_M ** -0.5)).astype(BF16)

    zg = jnp.dot(hb, wg_ref[...], preferred_element_type=F32) + bg_ref[...]
    logsig = jnp.minimum(zg, 0.0) - jnp.log1p(jnp.exp(-jnp.abs(zg)))
    glane = lax.broadcasted_iota(jnp.int32, zg.shape, 1)
    gates = jnp.where(glane < H_M, zg, logsig)
    gt_ref[...] = gates
    gtt_ref[...] = gates.T[0:GATE_ROWS]


def _row_permutation(tile, dils):
    p = np.zeros((len(dils) * tile, tile), np.float32)
    for j, dil in enumerate(dils):
        for r in range(dil):
            for m in range(tile // dil):
                p[j * tile + r * (tile // dil) + m, m * dil + r] = 1.0
    return jnp.asarray(p, BF16)


def _inproj(x, g, wa, wmt, wk, wg, bg, *, tile, dils, tail_rows):
    n = x.shape[0]
    assert n % tile == 0 and tail_rows % tile == 0 and (tile % LANES == 0 or tile == n)
    for d in dils:
        assert tile % (16 * d) == 0
    steps = n // tile
    first_tail = (n - tail_rows) // tile
    row = lambda w: pl.BlockSpec((tile, w), lambda i: (i, 0))
    col = lambda h: pl.BlockSpec((h, tile), lambda i: (0, i))
    out_shape = [jax.ShapeDtypeStruct((n, N_QKV), BF16)]
    out_specs = [row(N_QKV)]
    for d in dils:
        out_shape.append(jax.ShapeDtypeStruct((d, n // d, N_QKV), BF16))
        out_specs.append(pl.BlockSpec((d, tile // d, N_QKV), lambda i: (0, i, 0)))
    out_shape += [jax.ShapeDtypeStruct((tail_rows, N_QKV), F32),
                  jax.ShapeDtypeStruct((D_M, n), BF16), jax.ShapeDtypeStruct((n, D_M), BF16),
                  jax.ShapeDtypeStruct((D_M, n), BF16), jax.ShapeDtypeStruct((D_M, n), F32),
                  jax.ShapeDtypeStruct((n, LANES), F32), jax.ShapeDtypeStruct((GATE_ROWS, n), F32)]
    out_specs += [pl.BlockSpec((tile, N_QKV), lambda i: (jnp.maximum(i - first_tail, 0), 0)),
                  col(D_M), row(D_M), col(D_M), col(D_M), row(LANES), col(GATE_ROWS)]
    in_specs = [row(D_MODEL), _const_spec((1, D_MODEL)), _const_spec((D_MODEL, N_QKV)),
                _const_spec((3 * D_M, D_MODEL)), _const_spec((D_MODEL, D_M)),
                _const_spec((D_MODEL, LANES)), _const_spec((1, LANES))]
    args = [x, g, wa, wmt, wk, wg, bg]
    if dils:
        in_specs.append(_const_spec((len(dils) * tile, tile)))
        args.append(_row_permutation(tile, dils))
    return pl.pallas_call(
        functools.partial(_inproj_kernel, tile=tile, dils=dils),
        grid=(steps,),
        in_specs=in_specs,
        out_specs=out_specs,
        out_shape=out_shape,
        compiler_params=_params(("arbitrary",)),
        name="inproj",
    )(*args)


def _attn_prompt_kernel(*refs):
    np_ = len(PATTERNS)
    in_refs = refs[:5 * np_]
    bvec_ref, o_ref = refs[5 * np_], refs[5 * np_ + 1]
    scr = refs[5 * np_ + 2:]
    kbufs, vbufs = scr[0:np_], scr[np_:2 * np_]
    num_s, l_s, m_s, bias_s = scr[2 * np_:]
    n = pl.program_id(1)

    lane = lax.broadcasted_iota(jnp.int32, (SPAN, LANES), 1)
    even = lane < HD_A
    ones = jnp.ones((2 * SPAN, LANES), BF16)

    @pl.when(n == 0)
    def _():
        key_col = lax.broadcasted_iota(jnp.int32, (SPAN, 2 * SPAN), 1)
        for p in range(np_):
            for e in range(2):
                vb = jnp.broadcast_to(bvec_ref[p, 0, e:e + 1, :], (SPAN, 2 * SPAN))
                table = pltpu.roll(vb, 0, 1, stride=1, stride_axis=0)
                bias_s[p, 0, e * SPAN:(e + 1) * SPAN, :] = table
                bias_s[p, 1, e * SPAN:(e + 1) * SPAN, :] = jnp.where(key_col >= SPAN, table, NEG_INF)

    for p in range(np_):
        q_ref, k_ref, v_ref, kp_ref, vp_ref = in_refs[5 * p:5 * p + 5]
        kbufs[p][:, 0:SPAN, :] = kp_ref[...]
        kbufs[p][:, SPAN:, :] = k_ref[...]
        vbufs[p][:, 0:SPAN, :] = vp_ref[...]
        vbufs[p][:, SPAN:, :] = v_ref[...]

    first_n = (n == 0).astype(jnp.int32)

    def unit(p, j):
        dil = PATTERNS[p][1]
        r, mb = divmod(j, SUPER // dil // SPAN)
        row0 = mb * SPAN
        qb = in_refs[5 * p][r, row0:row0 + SPAN, :]
        kb = kbufs[p][r, row0:row0 + 2 * SPAN, :]
        vb = vbufs[p][r, row0:row0 + 2 * SPAN, :]
        bias = bias_s[p, first_n] if mb == 0 else bias_s[p, 0]
        zero = jnp.zeros_like(qb)
        qs = jnp.concatenate([jnp.where(even, qb, zero), jnp.where(even, zero, qb)], axis=0)
        s = lax.dot_general(qs, kb, (((1,), (1,)), ((), ())), preferred_element_type=F32) + bias
        m = jnp.max(s, axis=-1, keepdims=True)
        pr = jnp.exp(s - m).astype(BF16)
        res = jnp.dot(pr, jnp.concatenate([vb, ones], axis=1), preferred_element_type=F32)
        mb_ = jnp.broadcast_to(m, (2 * SPAN, LANES))
        rows = pl.ds(row0, SPAN) if dil == 1 else pl.ds(row0 * dil + r, SPAN, stride=dil)
        num_s[p, rows, :] = jnp.where(even, res[:SPAN, :LANES], res[SPAN:, :LANES])
        l_s[p, rows, :] = jnp.where(even, res[:SPAN, LANES:], res[SPAN:, LANES:])
        m_s[p, rows, :] = jnp.where(even, mb_[:SPAN], mb_[SPAN:])

    for j in range(SUPER // SPAN):
        for p in range(np_):
            unit(p, j)

    def combine(j, carry):
        rows = pl.ds(pl.multiple_of(j * SPAN, SPAN), SPAN)
        ms = [m_s[p, rows, :] for p in range(np_)]
        m = functools.reduce(jnp.maximum, ms)
        num = jnp.zeros((SPAN, LANES), F32)
        den = jnp.zeros((SPAN, LANES), F32)
        for p in range(np_):
            a = jnp.exp(ms[p] - m)
            num = num + a * num_s[p, rows, :]
            den = den + a * l_s[p, rows, :]
        o_ref[rows, :] = (num / den).astype(o_ref.dtype)
        return carry

    lax.fori_loop(0, SUPER // SPAN, combine, 0)


def _attn_prompt(qkvs, bvec):
    s_len = qkvs[0].shape[1]
    assert s_len % SUPER == 0
    nsb = s_len // SUPER
    npair = D_A // LANES
    in_specs, args, scratch = [], [], []
    for (_, dil), a in zip(PATTERNS, qkvs):
        rows = SUPER // dil
        nblk = rows // SPAN
        for part in range(3):
            in_specs.append(pl.BlockSpec((dil, rows, LANES),
                                         lambda hp, n, part=part: (0, n, part * npair + hp)))
            args.append(a)
        for part in (1, 2):
            in_specs.append(pl.BlockSpec((dil, SPAN, LANES),
                                         lambda hp, n, part=part, nblk=nblk:
                                         (0, jnp.maximum(n * nblk - 1, 0), part * npair + hp)))
            args.append(a)
    in_specs.append(pl.BlockSpec((len(PATTERNS), 1, 2, 2 * SPAN), lambda hp, n: (0, hp, 0, 0)))
    args.append(bvec)
    for _ in range(2):
        for _, dil in PATTERNS:
            scratch.append(pltpu.VMEM((dil, SPAN + SUPER // dil, LANES), BF16))
    scratch += [pltpu.VMEM((len(PATTERNS), SUPER, LANES), F32)] * 3
    scratch.append(pltpu.VMEM((len(PATTERNS), 2, 2 * SPAN, 2 * SPAN), F32))
    return pl.pallas_call(
        _attn_prompt_kernel,
        grid=(npair, nsb),
        in_specs=in_specs,
        out_specs=pl.BlockSpec((SUPER, LANES), lambda hp, n: (n, hp)),
        out_shape=jax.ShapeDtypeStruct((s_len, D_A), BF16),
        scratch_shapes=scratch,
        compiler_params=_params(("arbitrary", "arbitrary")),
        name="attn_prompt",
    )(*args)


def _sample_bias_tables(u_refs, w_ref, bc_refs, bn_ref, t_new):
    new_lane = lax.broadcasted_iota(jnp.int32, (t_new, LANES), 1) >= LANES - t_new
    for p, (win, _) in enumerate(PATTERNS):
        for h in range(H_A):
            rows = slice(h * t_new, (h + 1) * t_new)
            u = jnp.broadcast_to(u_refs[p][h:h + 1, :], (t_new, win + LANES))
            bc_refs[p][rows, :] = pltpu.roll(u, 0, 1, stride=1, stride_axis=0)[:, :win]
            w = jnp.broadcast_to(w_ref[p, h:h + 1, :], (t_new, LANES))
            bn_ref[p, rows, :] = jnp.where(new_lane, pltpu.roll(w, 0, 1, stride=1, stride_axis=0), NEG_INF)


def _attn_sample_step(qkv_ref, kc_ref, vc_ref, bc_refs, bn_ref, att_ref, ko_ref, vo_ref, *, t_new, wb):
    np_ = len(PATTERNS)
    qkv = qkv_ref[0]
    qf, kn, vn = qkv[:, 0:D_A], qkv[:, D_A:2 * D_A], qkv[:, 2 * D_A:3 * D_A]
    nrow = H_A * t_new
    row_h = lax.broadcasted_iota(jnp.int32, (nrow, D_A), 0) // t_new
    lane_h = lax.broadcasted_iota(jnp.int32, (nrow, D_A), 1) // HD_A
    own = row_h == lane_h
    qs = jnp.where(own, jnp.concatenate([qf] * H_A, axis=0), 0.0).astype(BF16)
    pad = jnp.zeros((LANES - t_new, D_A), F32)
    knt = jnp.concatenate([pad, kn], axis=0).T
    vnt = jnp.concatenate([pad, vn], axis=0).T
    kct = kc_ref[0]
    vct = vc_ref[0]
    s_c = jnp.dot(qs, kct.astype(BF16), preferred_element_type=F32)
    s_n = jnp.dot(qs, knt.astype(BF16), preferred_element_type=F32)

    wins = [w for w, _ in PATTERNS]
    es_c, es_n = [], []
    m = None
    for p, win in enumerate(wins):
        e_c = s_c[:, wb - win:] + bc_refs[p][...]
        e_n = s_n + bn_ref[p]
        es_c.append(e_c)
        es_n.append(e_n)
        mp = jnp.maximum(jnp.max(e_c, axis=-1, keepdims=True), jnp.max(e_n, axis=-1, keepdims=True))
        m = mp if m is None else jnp.maximum(m, mp)
    pn = None
    for p in range(np_):
        e = jnp.exp(es_n[p] - m)
        pn = e if pn is None else pn + e
    order = sorted(range(np_), key=lambda p: -wins[p])
    assert wins[order[0]] == wb
    pc = jnp.exp(es_c[order[0]] - m)
    for p in order[1:]:
        e = jnp.exp(es_c[p] - m)
        pc = jnp.concatenate([pc[:, :wb - wins[p]], pc[:, wb - wins[p]:] + e], axis=1)
    den = jnp.sum(pc, axis=-1, keepdims=True) + jnp.sum(pn, axis=-1, keepdims=True)
    nt = (((1,), (1,)), ((), ()))
    num = (lax.dot_general(pc.astype(BF16), vct.astype(BF16), nt, preferred_element_type=F32)
           + lax.dot_general(pn.astype(BF16), vnt.astype(BF16), nt, preferred_element_type=F32))
    o = jnp.where(own, num / den, 0.0)
    att = o[0:t_new]
    for h in range(1, H_A):
        att = att + o[h * t_new:(h + 1) * t_new]
    att_ref[0] = att.astype(att_ref.dtype)

    is_new = lax.broadcasted_iota(jnp.int32, (D_A, LANES), 1) >= LANES - t_new
    for src, new, dst in ((kct, knt, ko_ref), (vct, vnt, vo_ref)):
        rolled = pltpu.roll(src, wb - t_new, 1)
        dst[0, :, 0:wb - LANES] = rolled[:, 0:wb - LANES]
        dst[0, :, wb - LANES:wb] = jnp.where(is_new, new, rolled[:, wb - LANES:wb])


def _mixers_kernel(*refs, t_new, wb, c, chunks):
    np_ = len(PATTERNS)
    (qt_ref, k_ref, vt_ref, ot_ref, g_ref, gtt_ref, gw_ref, c0_ref, n0_ref, m0_ref,
     qkv_ref, kc_ref, vc_ref) = refs[:13]
    u_refs = refs[13:13 + np_]
    w_ref = refs[13 + np_]
    h_ref, c_out, n_out, m_out, att_ref, ko_ref, vo_ref = refs[14 + np_:21 + np_]
    cx_s, m_s = refs[21 + np_:23 + np_]
    bc_refs = refs[23 + np_:23 + 2 * np_]
    bn_ref = refs[23 + 2 * np_]
    i = pl.program_id(0)

    @pl.when(i == 0)
    def _():
        _mlstm_load_state(c0_ref, n0_ref, m0_ref, cx_s, m_s)
        _sample_bias_tables(u_refs, w_ref, bc_refs, bn_ref, t_new)

    _attn_sample_step(qkv_ref, kc_ref, vc_ref, bc_refs, bn_ref, att_ref, ko_ref, vo_ref, t_new=t_new, wb=wb)
    for j in range(chunks):
        _mlstm_chunk(qt_ref, k_ref, vt_ref, ot_ref, g_ref, gtt_ref, gw_ref, h_ref, cx_s, m_s, c=c, off=j * c)

    @pl.when(i == pl.num_programs(0) - 1)
    def _():
        _mlstm_store_state(c_out, n_out, m_out, cx_s, m_s)


def _mixers(qmt, km, vmt, omt, gates, gates_t, gw, c0, n0, m0, qkv, cache_kt, cache_vt, bias_u, bias_w, *, c):
    n = km.shape[0]
    b, t_new, _ = qkv.shape
    wb = cache_kt.shape[2]
    assert wb == max(w for w, _ in PATTERNS) and t_new % 8 == 0 and t_new <= LANES
    assert n % (b * c) == 0
    chunks = n // (b * c)
    cs = chunks * c
    nrow = H_A * t_new
    row = lambda w: pl.BlockSpec((cs, w), lambda i: (i, 0))
    col = lambda h: pl.BlockSpec((h, cs), lambda i: (0, i))
    blk = lambda r, w: pl.BlockSpec((1, r, w), lambda i: (i, 0, 0))
    st = lambda shape: pl.BlockSpec((1,) + shape, lambda i: (0,) * (1 + len(shape)))
    return pl.pallas_call(
        functools.partial(_mixers_kernel, t_new=t_new, wb=wb, c=c, chunks=chunks),
        grid=(b,),
        in_specs=[col(D_M), row(D_M), col(D_M), col(D_M), row(LANES), col(GATE_ROWS), _const_spec((D_M, c)),
                  st((H_M, DK_M, DV_M)), st((H_M, 1, DK_M)), st((H_M, 8, LANES)),
                  blk(t_new, N_QKV), blk(D_A, wb), blk(D_A, wb)]
                 + [_const_spec((H_A, w + LANES)) for w, _ in PATTERNS]
                 + [_const_spec((len(PATTERNS), H_A, LANES))],
        out_specs=[col(D_M), st((H_M, DK_M, DV_M)), st((H_M, 8, DK_M)), st((H_M, 8, LANES)),
                   blk(t_new, D_A), blk(D_A, wb), blk(D_A, wb)],
        out_shape=[jax.ShapeDtypeStruct((D_M, n), BF16),
                   jax.ShapeDtypeStruct((1, H_M, DK_M, DV_M), F32),
                   jax.ShapeDtypeStruct((1, H_M, 8, DK_M), F32),
                   jax.ShapeDtypeStruct((1, H_M, 8, LANES), F32),
                   jax.ShapeDtypeStruct((b, t_new, D_A), BF16),
                   jax.ShapeDtypeStruct((b, D_A, wb), F32), jax.ShapeDtypeStruct((b, D_A, wb), F32)],
        scratch_shapes=[pltpu.VMEM((H_M, 2 * DV_M, DK_M), F32), pltpu.VMEM((H_M, 8, LANES), F32)]
                       + [pltpu.VMEM((nrow, w), F32) for w, _ in PATTERNS]
                       + [pltpu.VMEM((len(PATTERNS), nrow, LANES), F32)],
        compiler_params=_params(("arbitrary",)),
        name="mixers",
    )(qmt, km, vmt, omt, gates, gates_t, gw, c0, n0, m0, qkv, cache_kt, cache_vt, *bias_u, bias_w)


def _split3(x):
    a = x.astype(BF16)
    r = x - a.astype(F32)
    b = r.astype(BF16)
    c = (r - b.astype(F32)).astype(BF16)
    return a, b, c


def _mlstm_load_state(c0_ref, n0_ref, m0_ref, cx_s, m_s):
    for h in range(H_M):
        cx_s[h, 0:DV_M, :] = c0_ref[0, h].T
        cx_s[h, DV_M:, :] = jnp.broadcast_to(n0_ref[0, h], (DV_M, DK_M))
    m_s[...] = m0_ref[0]


def _mlstm_store_state(c_out, n_out, m_out, cx_s, m_s):
    for h in range(H_M):
        c_out[0, h] = cx_s[h, 0:DV_M, :].T
        n_out[0, h] = cx_s[h, DV_M:DV_M + 8, :]
    m_out[0] = m_s[...]


def _mlstm_chunk(qt_ref, k_ref, vt_ref, ot_ref, g_ref, gtt_ref, gw_ref, h_ref, cx_s, m_s, *, c, off):
    tok = slice(off, off + c)
    gates = g_ref[tok, :]
    gates_t = gtt_ref[:, tok]
    lane = lax.broadcasted_iota(jnp.int32, gates.shape, 1)
    lf = jnp.where(jnp.logical_and(lane >= H_M, lane < 2 * H_M), gates, 0.0)
    grow = lax.broadcasted_iota(jnp.int32, gates_t.shape, 0)
    lf_t = jnp.where(jnp.logical_and(grow >= H_M, grow < 2 * H_M), gates_t, 0.0)
    ri = lax.broadcasted_iota(jnp.int32, (c, c), 0)
    cj = lax.broadcasted_iota(jnp.int32, (c, c), 1)
    upper = ri <= cj
    tril = (ri >= cj).astype(BF16)
    triu = upper.astype(BF16)
    bsum = brow = None
    for part, part_t in zip(_split3(lf), _split3(lf_t)):
        t1 = jnp.dot(tril, part, preferred_element_type=F32)
        t2 = jnp.dot(part_t, triu, preferred_element_type=F32)
        bsum = t1 if bsum is None else bsum + t1
        brow = t2 if brow is None else brow + t2
    ones = jnp.ones((DV_M, c), BF16)

    for h in range(H_M):
        sl = slice(h * DK_M, (h + 1) * DK_M)
        qt, kh, vt = qt_ref[sl, tok], k_ref[tok, sl], vt_ref[sl, tok]
        b_row = brow[H_M + h:H_M + h + 1, :]
        i_row = gates_t[h:h + 1, :]
        a_col = gates[:, h:h + 1] - bsum[:, H_M + h:H_M + h + 1]
        m_prev = m_s[h, 0:1, 0:1]
        dmat = jnp.where(upper, b_row + a_col, NEG_INF)
        inter = b_row + m_prev
        mt = jnp.maximum(inter, jnp.max(dmat, axis=0, keepdims=True))
        st = jnp.dot(kh, qt, preferred_element_type=F32)
        smt = (st * jnp.exp(dmat - mt)).astype(BF16)
        iw = jnp.exp(inter - mt)
        vext = jnp.concatenate([vt, ones], axis=0)
        cxh = cx_s[h]
        ne = (jnp.dot(vext, smt, preferred_element_type=F32)
              + iw * jnp.dot(cxh.astype(BF16), qt, preferred_element_type=F32))
        num, den = ne[:DV_M], ne[DV_M:]
        hq = num / jnp.maximum(jnp.abs(den), jnp.exp(-mt))
        hn = hq * lax.rsqrt(jnp.mean(hq * hq, axis=0, keepdims=True) + RMS_EPS) * gw_ref[sl, :]
        h_ref[sl, tok] = (jax.nn.sigmoid(ot_ref[sl, tok]) * hn).astype(h_ref.dtype)

        b_last = b_row[:, c - 1:c]
        g_row = b_last - b_row + i_row
        m_new = jnp.maximum(b_last + m_prev, jnp.max(g_row, axis=1, keepdims=True))
        ws = jnp.exp(g_row - m_new)
        wc = jnp.exp(b_last + m_prev - m_new)
        vw = (vext.astype(F32) * ws).astype(BF16)
        cx_s[h] = wc * cxh + jnp.dot(vw, kh, preferred_element_type=F32)
        m_s[h] = jnp.broadcast_to(m_new, m_s.shape[1:])


def _mlstm_kernel(qt_ref, k_ref, vt_ref, ot_ref, g_ref, gtt_ref, gw_ref, c0_ref, n0_ref, m0_ref,
                  h_ref, c_out, n_out, m_out, cx_s, m_s, *, c):
    ci = pl.program_id(1)

    @pl.when(ci == 0)
    def _():
        _mlstm_load_state(c0_ref, n0_ref, m0_ref, cx_s, m_s)

    _mlstm_chunk(qt_ref, k_ref, vt_ref, ot_ref, g_ref, gtt_ref, gw_ref, h_ref, cx_s, m_s, c=c, off=0)

    @pl.when(ci == pl.num_programs(1) - 1)
    def _():
        _mlstm_store_state(c_out, n_out, m_out, cx_s, m_s)


def _mlstm(qmt, km, vmt, omt, gates, gates_t, gw, c0, n0, m0, *, nb, c):
    n = km.shape[0]
    nc = n // (nb * c)
    assert nb * nc * c == n and c % LANES == 0
    row = lambda w: pl.BlockSpec((c, w), lambda b, i: (b * nc + i, 0))
    col = lambda h: pl.BlockSpec((h, c), lambda b, i: (0, b * nc + i))
    st = lambda shape: pl.BlockSpec((1,) + shape, lambda b, i: (b,) + (0,) * len(shape))
    return pl.pallas_call(
        functools.partial(_mlstm_kernel, c=c),
        grid=(nb, nc),
        in_specs=[col(D_M), row(D_M), col(D_M), col(D_M), row(LANES), col(GATE_ROWS), _const_spec((D_M, c)),
                  st((H_M, DK_M, DV_M)), st((H_M, 1, DK_M)), st((H_M, 8, LANES))],
        out_specs=[col(D_M), st((H_M, DK_M, DV_M)), st((H_M, 8, DK_M)), st((H_M, 8, LANES))],
        out_shape=[jax.ShapeDtypeStruct((D_M, n), BF16),
                   jax.ShapeDtypeStruct((nb, H_M, DK_M, DV_M), F32),
                   jax.ShapeDtypeStruct((nb, H_M, 8, DK_M), F32),
                   jax.ShapeDtypeStruct((nb, H_M, 8, LANES), F32)],
        scratch_shapes=[pltpu.VMEM((H_M, 2 * DV_M, DK_M), F32), pltpu.VMEM((H_M, 8, LANES), F32)],
        compiler_params=_params(("arbitrary", "arbitrary")),
        name="mlstm",
    )(qmt, km, vmt, omt, gates, gates_t, gw, c0, n0, m0)


def _outffn_kernel(att_ref, hmt_ref, x_ref, prev_ref, wo_ref, gpost_ref, gpre_ref, wup_ref, cw_ref, cb_ref,
                   wdn_ref, gffn_ref, y_ref, st_ref, u_ref, a_ref, *, tile, shift, hdr):
    i = pl.program_id(0)

    @pl.when(i == 0)
    def _():
        u_ref[hdr - 2 * shift:hdr, :] = prev_ref[...]

    @pl.when(i > 0)
    def _():
        u_ref[0:hdr, :] = u_ref[tile:tile + hdr, :]

    mixed = (jnp.dot(att_ref[...], wo_ref[0:D_A, :], preferred_element_type=F32)
             + lax.dot_general(hmt_ref[...], wo_ref[D_A:, :], (((0,), (0,)), ((), ())),
                               preferred_element_type=F32))
    x1 = x_ref[...] + _rms(mixed, gpost_ref[...])
    h2 = _rms(x1, gpre_ref[...]).astype(BF16)
    u_ref[hdr:hdr + tile, :] = jnp.dot(h2, wup_ref[...], preferred_element_type=F32)

    def conv(c):
        cols = slice(c * LANES, (c + 1) * LANES)
        y = cb_ref[:, cols] + u_ref[hdr - 2 * shift:hdr - 2 * shift + tile, cols] * cw_ref[0:1, cols]
        y = y + u_ref[hdr - shift:hdr - shift + tile, cols] * cw_ref[1:2, cols]
        return y + u_ref[hdr:hdr + tile, cols] * cw_ref[2:3, cols]

    nff = D_FF // LANES
    for c in range(nff):
        gate, val = conv(c), conv(nff + c)
        e = jnp.exp(gate * (GELU_C1 + GELU_C3 * (gate * gate)))
        a_ref[:, c * LANES:(c + 1) * LANES] = (gate * val / (1.0 + e)).astype(BF16)

    y2 = jnp.dot(a_ref[...], wdn_ref[...], preferred_element_type=F32)
    y_ref[...] = x1 + _rms(y2, gffn_ref[...])
    st_ref[...] = u_ref[tile:tile + hdr, :]


def _outffn(att, hmt, x, prev, wo, gpost, gpre, wup, cw, cb, wdn, gffn, *, tile, shift):
    n = x.shape[0]
    assert n % tile == 0 and (shift == 1 or n == tile)
    hdr = max(8, 2 * shift)
    assert hdr % 8 == 0 and (tile % LANES == 0 or tile == n)
    row = lambda w: pl.BlockSpec((tile, w), lambda i: (i, 0))
    return pl.pallas_call(
        functools.partial(_outffn_kernel, tile=tile, shift=shift, hdr=hdr),
        grid=(n // tile,),
        in_specs=[row(D_A), pl.BlockSpec((D_M, tile), lambda i: (0, i)), row(D_MODEL),
                  _const_spec((2 * shift, 2 * D_FF)),
                  _const_spec((D_A + D_M, D_MODEL)), _const_spec((1, D_MODEL)), _const_spec((1, D_MODEL)),
                  _const_spec((D_MODEL, 2 * D_FF)), _const_spec((CONV_W, 2 * D_FF)), _const_spec((1, 2 * D_FF)),
                  _const_spec((D_FF, D_MODEL)), _const_spec((1, D_MODEL))],
        out_specs=[row(D_MODEL), pl.BlockSpec((hdr, 2 * D_FF), lambda i: (0, 0))],
        out_shape=[jax.ShapeDtypeStruct((n, D_MODEL), F32), jax.ShapeDtypeStruct((hdr, 2 * D_FF), F32)],
        scratch_shapes=[pltpu.VMEM((hdr + tile, 2 * D_FF), F32), pltpu.VMEM((tile, D_FF), BF16)],
        compiler_params=_params(("arbitrary",)),
        name="outffn",
    )(att, hmt, x, prev, wo, gpost, gpre, wup, cw, cb, wdn, gffn)


def _t5_bucket(dist):
    max_exact = N_BUCKETS // 2
    d = jnp.maximum(dist, max_exact).astype(F32)
    large = max_exact + (jnp.log(d / max_exact) / math.log(BUCKET_MAX_DIST / max_exact)
                         * (N_BUCKETS - max_exact)).astype(jnp.int32)
    large = jnp.minimum(large, N_BUCKETS - 1)
    return jnp.where(dist < max_exact, dist, large)


def _pattern_bias(rel_bias, dil):
    j = jnp.arange(SPAN + 1)
    return rel_bias[_t5_bucket(j * dil)].T.astype(F32)


def _prompt_bias(rel_bias):
    out = []
    for _, dil in PATTERNS:
        bh = _pattern_bias(rel_bias, dil)
        v = jnp.concatenate([bh[:, ::-1], jnp.full((H_A, SPAN - 1), NEG_INF, F32)], axis=1)
        out.append(v.reshape(H_A // 2, 2, 2 * SPAN))
    return jnp.stack(out)


def _sample_bias(rel_bias, t_new):
    us, ws = [], []
    fill = lambda n: jnp.full((H_A, n), NEG_INF, F32)
    for win, dil in PATTERNS:
        bh = _pattern_bias(rel_bias, dil)
        dist = jnp.arange((SPAN + 1) * dil)
        g = jnp.where(dist % dil == 0, jnp.repeat(bh, dil, axis=1), NEG_INF)[:, :win + 1]
        us.append(jnp.concatenate([g[:, ::-1], fill(LANES - 1)], axis=1))
        ws.append(jnp.concatenate([g[:, :LANES - t_new + 1][:, ::-1], fill(t_new - 1)], axis=1))
    return us, jnp.stack(ws)


def _layer_weights(g_mix_pre, w_in, b_igate, b_fgate, g_mlstm_out, w_out, g_mix_post, g_ffn_pre, w_up,
                   conv_w, conv_b, w_down, g_ffn_post):
    wg = jnp.zeros((D_MODEL, LANES), F32).at[:, :2 * H_M].set(w_in[:, N_QKV + N_MIX:])
    bg = jnp.zeros((1, LANES), F32).at[0, :H_M].set(b_igate.astype(F32)).at[0, H_M:2 * H_M].set(b_fgate.astype(F32))
    row = lambda v: v.astype(F32).reshape(1, -1)
    mix = lambda j: w_in[:, N_QKV + j * D_M:N_QKV + (j + 1) * D_M]
    wmt = jnp.concatenate([mix(0), mix(2), mix(3)], axis=1).T.astype(BF16)
    gw = jnp.broadcast_to(g_mlstm_out.astype(F32)[:, None], (D_M, MLSTM_STEP))
    return dict(
        g_pre=row(g_mix_pre), wa=w_in[:, :N_QKV].astype(BF16), wmt=wmt, wk=mix(1).astype(BF16),
        wg=wg.astype(BF16), bg=bg, gw=gw, wo=w_out.astype(BF16), g_post=row(g_mix_post),
        g_ffn_pre=row(g_ffn_pre), wup=w_up.astype(BF16), cw=conv_w.astype(F32), cb=row(conv_b),
        wdn=w_down.astype(BF16), g_ffn_post=row(g_ffn_post))


def _state_in(c0, n0, m0):
    nb = c0.shape[0]
    return (c0.astype(F32), n0.astype(F32).reshape(nb, H_M, 1, DK_M),
            jnp.broadcast_to(m0.astype(F32)[..., None, None], (nb, H_M, 8, LANES)))


def _state_out(c, n, m):
    return c, n[:, :, 0, :], m[:, :, 0, 0]


def _layer(xp, xs, w, bias_p, bias_u, bias_w, cache_k, cache_v, c0, n0, m0, conv_prev, *, tile_in, tile_ffn):
    s_len = xp.shape[0]
    b, t_new, _ = xs.shape
    n = b * t_new
    wb = cache_k.shape[1]
    proj = (w["g_pre"], w["wa"], w["wmt"], w["wk"], w["wg"], w["bg"])

    wbp = min(max(wd for wd, _ in PATTERNS), s_len)
    dils = tuple(d for _, d in PATTERNS if d > 1)
    qkv1, qkv4, qkv16, tail_p, qmt_p, km_p, vmt_p, omt_p, gates_p, gates_tp = _inproj(
        xp, *proj, tile=tile_in, dils=dils, tail_rows=wbp)
    att_p = _attn_prompt([qkv1.reshape(1, s_len, N_QKV), qkv4, qkv16], bias_p)

    _, tail, qmt, km, vmt, omt, gates, gates_t = _inproj(xs.reshape(n, D_MODEL), *proj, tile=n, dils=(), tail_rows=n)
    to_t = lambda a: jnp.transpose(a.astype(F32), (0, 2, 3, 1)).reshape(b, D_A, wb)
    from_t = lambda a: jnp.transpose(a.reshape(b, H_A, HD_A, wb), (0, 3, 1, 2))
    state0 = _state_in(jnp.zeros((1, H_M, DK_M, DV_M), F32), jnp.zeros((1, H_M, DK_M), F32),
                       jnp.zeros((1, H_M), F32))
    hmt_p, c_p, n_p, m_p, att, k_win, v_win = _mixers(
        qmt_p, km_p, vmt_p, omt_p, gates_p, gates_tp, w["gw"], *state0,
        tail.reshape(b, t_new, N_QKV), to_t(cache_k), to_t(cache_v), bias_u, bias_w, c=MLSTM_STEP)

    prev0 = jnp.zeros((CONV_W - 1, 2 * D_FF), F32)
    yp, st_p = _outffn(att_p, hmt_p, xp, prev0, w["wo"], w["g_post"], w["g_ffn_pre"], w["wup"], w["cw"], w["cb"],
                       w["wdn"], w["g_ffn_post"], tile=tile_ffn, shift=1)
    k_win_p = tail_p[:, D_A:2 * D_A].reshape(1, wbp, H_A, HD_A)
    v_win_p = tail_p[:, 2 * D_A:3 * D_A].reshape(1, wbp, H_A, HD_A)
    state_p = (k_win_p, v_win_p) + _state_out(c_p, n_p, m_p) + (st_p[-(CONV_W - 1):][None],)

    c = LANES
    assert t_new <= c
    padr = lambda a: jnp.pad(a.reshape(b, t_new, -1), ((0, 0), (0, c - t_new), (0, 0))).reshape(b * c, -1)
    padc = lambda a: jnp.pad(a.reshape(-1, b, t_new), ((0, 0), (0, 0), (0, c - t_new))).reshape(-1, b * c)
    gpad = jnp.where(jnp.arange(LANES) < H_M, NEG_INF, 0.0).astype(F32)
    gates_pad = jnp.concatenate([gates.reshape(b, t_new, LANES),
                                 jnp.broadcast_to(gpad, (b, c - t_new, LANES))], axis=1).reshape(b * c, LANES)
    gates_tpad = jnp.concatenate([gates_t.reshape(GATE_ROWS, b, t_new),
                                  jnp.broadcast_to(gpad[:GATE_ROWS, None, None], (GATE_ROWS, b, c - t_new))],
                                 axis=2).reshape(GATE_ROWS, b * c)
    hmt_s, c_o, n_o, m_o = _mlstm(padc(qmt), padr(km), padc(vmt), padc(omt), gates_pad, gates_tpad, w["gw"][:, :c],
                                  *_state_in(c0, n0, m0), nb=b, c=c)

    tm = lambda a: a.reshape(b, t_new, -1).transpose(1, 0, 2).reshape(n, -1)
    hmt = hmt_s.reshape(D_M, b, c)[:, :, :t_new].transpose(0, 2, 1).reshape(D_M, n)
    prev = conv_prev.astype(F32).transpose(1, 0, 2).reshape((CONV_W - 1) * b, 2 * D_FF)
    ys, st = _outffn(tm(att), hmt, tm(xs), prev, w["wo"], w["g_post"], w["g_ffn_pre"], w["wup"], w["cw"],
                     w["cb"], w["wdn"], w["g_ffn_post"], tile=n, shift=b)
    ys = ys.reshape(t_new, b, D_MODEL).transpose(1, 0, 2)
    conv_state = st.reshape(CONV_W - 1, b, 2 * D_FF).transpose(1, 0, 2)
    state_s = (from_t(k_win), from_t(v_win)) + _state_out(c_o, n_o, m_o) + (conv_state,)
    return yp, ys, state_p, state_s


def kernel(x_prompt, x_sample, cache_attn_k, cache_attn_v, state_mlstm_C, state_mlstm_n, state_mlstm_m,
           state_ffn_conv, rel_bias, g_mix_pre, w_in, b_igate, b_fgate, g_mlstm_out, w_out, g_mix_post,
           g_ffn_pre, w_up, conv_w, conv_b, w_down, g_ffn_post):
    depth = w_in.shape[0]
    batch, s_len, _ = x_prompt.shape
    assert batch == 1
    t_new = x_sample.shape[1]
    bias_p = _prompt_bias(rel_bias)
    bias_u, bias_w = _sample_bias(rel_bias, t_new)
    yp = x_prompt[0]
    ys = x_sample
    new_p, new_s = [], []
    for l in range(depth):
        w = _layer_weights(g_mix_pre[l], w_in[l], b_igate[l], b_fgate[l], g_mlstm_out[l], w_out[l],
                           g_mix_post[l], g_ffn_pre[l], w_up[l], conv_w[l], conv_b[l], w_down[l], g_ffn_post[l])
        yp, ys, sp, ss = _layer(yp, ys, w, bias_p, bias_u, bias_w, cache_attn_k[l], cache_attn_v[l],
                                state_mlstm_C[l], state_mlstm_n[l], state_mlstm_m[l], state_ffn_conv[l],
                                tile_in=256, tile_ffn=512)
        new_p.append(sp)
        new_s.append(ss)
    stack = lambda states, i: jnp.stack([s[i] for s in states])
    return ((yp[None], ys) + tuple(stack(new_p, i) for i in range(6))
            + tuple(stack(new_s, i) for i in range(6)))
```

```python
import functools
import math

import jax
import jax.numpy as jnp
import numpy as np
from jax import lax
from jax.experimental import pallas as pl
from jax.experimental.pallas import tpu as pltpu

F32 = jnp.float32
BF16 = jnp.bfloat16

D_MODEL = 1024
HD_A = 64
H_A = 8
D_A = H_A * HD_A
DK_M = 128
DV_M = 128
H_M = 4
D_M = H_M * DV_M
PATTERNS = ((128, 1), (512, 4), (2048, 16))
SPAN = 128
N_BUCKETS = 32
BUCKET_MAX_DIST = 2048
MLSTM_STEP = 256
D_FF = 2816
CONV_W = 3
RMS_EPS = 1e-6
N_QKV = 3 * D_A
N_MIX = 4 * D_M
LANES = 128
GATE_ROWS = 16
SUPER = 2048
VMEM_LIMIT = 56 * 1024 * 1024
NEG_INF = float("-inf")
GELU_C1 = -2.0 * math.sqrt(2.0 / math.pi)
GELU_C3 = GELU_C1 * 0.044715

for _w, _d in PATTERNS:
    assert _w // _d == SPAN


def _params(sem, vmem=VMEM_LIMIT):
    return pltpu.CompilerParams(dimension_semantics=sem, vmem_limit_bytes=vmem)


def _const_spec(shape):
    nd = len(shape)
    return pl.BlockSpec(shape, lambda *_: (0,) * nd, pipeline_mode=pl.Buffered(1))


def _rms(x, g):
    return x * lax.rsqrt(jnp.mean(x * x, axis=-1, keepdims=True) + RMS_EPS) * g


def _inproj_kernel(x_ref, g_ref, wa_ref, wmt_ref, wk_ref, wg_ref, bg_ref, *refs, tile, dils, first_tail):
    nd = len(dils)
    perm_ref, wkvt_ref = (refs[0], refs[1]) if nd else (None, None)
    refs = refs[2:] if nd else refs
    qkv1_ref = refs[0]
    dil_refs = refs[1:1 + nd]
    tail_ref, qmt_ref, km_ref, vmt_ref, omt_ref, gt_ref, gtt_ref = refs[1 + nd:]

    hb = _rms(x_ref[...], g_ref[...]).astype(BF16)
    za = jnp.dot(hb, wa_ref[...], preferred_element_type=F32)
    za = jnp.concatenate([za[:, :D_A] * (HD_A ** -0.5), za[:, D_A:]], axis=1)
    zb = za.astype(BF16)
    qkv1_ref[...] = zb
    if not nd:
        tail_ref[...] = za

    if nd:
        zp = jnp.dot(perm_ref[...], zb, preferred_element_type=F32).astype(BF16)
        base = 0
        for dil, out_ref in zip(dils, dil_refs):
            for r in range(dil):
                out_ref[r] = zp[base + r * (tile // dil):base + (r + 1) * (tile // dil)]
            base += tile

    zt = lax.dot_general(wmt_ref[...], hb, (((1,), (1,)), ((), ())), preferred_element_type=F32)
    qmt_ref[...] = zt[0:D_M].astype(BF16)
    vmt_ref[...] = zt[D_M:2 * D_M].astype(BF16)
    omt_ref[...] = zt[2 * D_M:3 * D_M]
    zk = jnp.dot(hb, wk_ref[...], preferred_element_type=F32)
    km_ref[...] = (zk * (DK_M ** -0.5)).astype(BF16)

    zg = jnp.dot(hb, wg_ref[...], preferred_element_type=F32) + bg_ref[...]
    logsig = jnp.minimum(zg, 0.0) - jnp.log1p(jnp.exp(-jnp.abs(zg)))
    glane = lax.broadcasted_iota(jnp.int32, zg.shape, 1)
    gates = jnp.where(glane < H_M, zg, logsig)
    gt_ref[...] = gates
    gtt_ref[...] = gates.T[0:GATE_ROWS]

    if nd:
        @pl.when(pl.program_id(0) >= first_tail)
        def _():
            tail_ref[...] = lax.dot_general(wkvt_ref[...], hb, (((1,), (1,)), ((), ())),
                                            preferred_element_type=F32)


def _row_permutation(tile, dils):
    p = np.zeros((len(dils) * tile, tile), np.float32)
    for j, dil in enumerate(dils):
        for r in range(dil):
            for m in range(tile // dil):
                p[j * tile + r * (tile // dil) + m, m * dil + r] = 1.0
    return jnp.asarray(p, BF16)


def _inproj(x, g, wa, wmt, wk, wg, bg, wkvt=None, *, tile, dils, tail_rows):
    n = x.shape[0]
    assert n % tile == 0 and tail_rows % tile == 0 and (tile % LANES == 0 or tile == n)
    for d in dils:
        assert tile % (16 * d) == 0
    steps = n // tile
    first_tail = (n - tail_rows) // tile
    row = lambda w: pl.BlockSpec((tile, w), lambda i: (i, 0))
    col = lambda h: pl.BlockSpec((h, tile), lambda i: (0, i))
    out_shape = [jax.ShapeDtypeStruct((n, N_QKV), BF16)]
    out_specs = [row(N_QKV)]
    for d in dils:
        out_shape.append(jax.ShapeDtypeStruct((d, n // d, N_QKV), BF16))
        out_specs.append(pl.BlockSpec((d, tile // d, N_QKV), lambda i: (0, i, 0)))
    if dils:
        out_shape.append(jax.ShapeDtypeStruct((2 * D_A, tail_rows), F32))
        out_specs.append(pl.BlockSpec((2 * D_A, tile), lambda i: (0, jnp.maximum(i - first_tail, 0))))
    else:
        out_shape.append(jax.ShapeDtypeStruct((tail_rows, N_QKV), F32))
        out_specs.append(pl.BlockSpec((tile, N_QKV), lambda i: (jnp.maximum(i - first_tail, 0), 0)))
    out_shape += [jax.ShapeDtypeStruct((D_M, n), BF16), jax.ShapeDtypeStruct((n, D_M), BF16),
                  jax.ShapeDtypeStruct((D_M, n), BF16), jax.ShapeDtypeStruct((D_M, n), F32),
                  jax.ShapeDtypeStruct((n, LANES), F32), jax.ShapeDtypeStruct((GATE_ROWS, n), F32)]
    out_specs += [col(D_M), row(D_M), col(D_M), col(D_M), row(LANES), col(GATE_ROWS)]
    in_specs = [row(D_MODEL), _const_spec((1, D_MODEL)), _const_spec((D_MODEL, N_QKV)),
                _const_spec((3 * D_M, D_MODEL)), _const_spec((D_MODEL, D_M)),
                _const_spec((D_MODEL, LANES)), _const_spec((1, LANES))]
    args = [x, g, wa, wmt, wk, wg, bg]
    if dils:
        in_specs += [_const_spec((len(dils) * tile, tile)), _const_spec((2 * D_A, D_MODEL))]
        args += [_row_permutation(tile, dils), wkvt]
    return pl.pallas_call(
        functools.partial(_inproj_kernel, tile=tile, dils=dils, first_tail=first_tail),
        grid=(steps,),
        in_specs=in_specs,
        out_specs=out_specs,
        out_shape=out_shape,
        compiler_params=_params(("arbitrary",)),
        name="inproj",
    )(*args)


def _attn_prompt_kernel(*refs):
    np_ = len(PATTERNS)
    in_refs = refs[:5 * np_]
    bvec_ref, o_ref = refs[5 * np_], refs[5 * np_ + 1]
    scr = refs[5 * np_ + 2:]
    kbufs, vbufs = scr[0:np_], scr[np_:2 * np_]
    num_s, l_s, m_s, bias_s = scr[2 * np_:]
    n = pl.program_id(1)

    lane = lax.broadcasted_iota(jnp.int32, (SPAN, LANES), 1)
    even = lane < HD_A
    ones = jnp.ones((2 * SPAN, LANES), BF16)

    @pl.when(n == 0)
    def _():
        key_col = lax.broadcasted_iota(jnp.int32, (SPAN, 2 * SPAN), 1)
        for p in range(np_):
            for e in range(2):
                vb = jnp.broadcast_to(bvec_ref[p, 0, e:e + 1, :], (SPAN, 2 * SPAN))
                table = pltpu.roll(vb, 0, 1, stride=1, stride_axis=0)
                bias_s[p, 0, e * SPAN:(e + 1) * SPAN, :] = table
                bias_s[p, 1, e * SPAN:(e + 1) * SPAN, :] = jnp.where(key_col >= SPAN, table, NEG_INF)

    for p in range(np_):
        q_ref, k_ref, v_ref, kp_ref, vp_ref = in_refs[5 * p:5 * p + 5]
        kbufs[p][:, 0:SPAN, :] = kp_ref[...]
        kbufs[p][:, SPAN:, :] = k_ref[...]
        vbufs[p][:, 0:SPAN, :] = vp_ref[...]
        vbufs[p][:, SPAN:, :] = v_ref[...]

    first_n = (n == 0).astype(jnp.int32)

    def unit(p, j):
        dil = PATTERNS[p][1]
        r, mb = divmod(j, SUPER // dil // SPAN)
        row0 = mb * SPAN
        qb = in_refs[5 * p][r, row0:row0 + SPAN, :]
        kb = kbufs[p][r, row0:row0 + 2 * SPAN, :]
        vb = vbufs[p][r, row0:row0 + 2 * SPAN, :]
        bias = bias_s[p, first_n] if mb == 0 else bias_s[p, 0]
        zero = jnp.zeros_like(qb)
        qs = jnp.concatenate([jnp.where(even, qb, zero), jnp.where(even, zero, qb)], axis=0)
        s = lax.dot_general(qs, kb, (((1,), (1,)), ((), ())), preferred_element_type=F32) + bias
        m = jnp.max(s, axis=-1, keepdims=True)
        pr = jnp.exp(s - m).astype(BF16)
        res = jnp.dot(pr, jnp.concatenate([vb, ones], axis=1), preferred_element_type=F32)
        mb_ = jnp.broadcast_to(m, (2 * SPAN, LANES))
        rows = pl.ds(row0, SPAN) if dil == 1 else pl.ds(row0 * dil + r, SPAN, stride=dil)
        num_s[p, rows, :] = jnp.where(even, res[:SPAN, :LANES], res[SPAN:, :LANES])
        l_s[p, rows, :] = jnp.where(even, res[:SPAN, LANES:], res[SPAN:, LANES:])
        m_s[p, rows, :] = jnp.where(even, mb_[:SPAN], mb_[SPAN:])

    for j in range(SUPER // SPAN):
        for p in range(np_):
            unit(p, j)

    def combine(j, carry):
        rows = pl.ds(pl.multiple_of(j * SPAN, SPAN), SPAN)
        ms = [m_s[p, rows, :] for p in range(np_)]
        m = functools.reduce(jnp.maximum, ms)
        num = jnp.zeros((SPAN, LANES), F32)
        den = jnp.zeros((SPAN, LANES), F32)
        for p in range(np_):
            a = jnp.exp(ms[p] - m)
            num = num + a * num_s[p, rows, :]
            den = den + a * l_s[p, rows, :]
        o_ref[rows, :] = (num / den).astype(o_ref.dtype)
        return carry

    lax.fori_loop(0, SUPER // SPAN, combine, 0)


def _attn_prompt(qkvs, bvec):
    s_len = qkvs[0].shape[1]
    assert s_len % SUPER == 0
    nsb = s_len // SUPER
    npair = D_A // LANES
    in_specs, args, scratch = [], [], []
    for (_, dil), a in zip(PATTERNS, qkvs):
        rows = SUPER // dil
        nblk = rows // SPAN
        for part in range(3):
            in_specs.append(pl.BlockSpec((dil, rows, LANES),
                                         lambda hp, n, part=part: (0, n, part * npair + hp)))
            args.append(a)
        for part in (1, 2):
            in_specs.append(pl.BlockSpec((dil, SPAN, LANES),
                                         lambda hp, n, part=part, nblk=nblk:
                                         (0, jnp.maximum(n * nblk - 1, 0), part * npair + hp)))
            args.append(a)
    in_specs.append(pl.BlockSpec((len(PATTERNS), 1, 2, 2 * SPAN), lambda hp, n: (0, hp, 0, 0)))
    args.append(bvec)
    for _ in range(2):
        for _, dil in PATTERNS:
            scratch.append(pltpu.VMEM((dil, SPAN + SUPER // dil, LANES), BF16))
    scratch += [pltpu.VMEM((len(PATTERNS), SUPER, LANES), F32)] * 3
    scratch.append(pltpu.VMEM((len(PATTERNS), 2, 2 * SPAN, 2 * SPAN), F32))
    return pl.pallas_call(
        _attn_prompt_kernel,
        grid=(npair, nsb),
        in_specs=in_specs,
        out_specs=pl.BlockSpec((SUPER, LANES), lambda hp, n: (n, hp)),
        out_shape=jax.ShapeDtypeStruct((s_len, D_A), BF16),
        scratch_shapes=scratch,
        compiler_params=_params(("arbitrary", "arbitrary")),
        name="attn_prompt",
    )(*args)


def _sample_bias_tables(u_refs, w_ref, bc_refs, bn_ref, t_new):
    new_lane = lax.broadcasted_iota(jnp.int32, (t_new, LANES), 1) >= LANES - t_new
    for p, (win, _) in enumerate(PATTERNS):
        for h in range(H_A):
            rows = slice(h * t_new, (h + 1) * t_new)
            u = jnp.broadcast_to(u_refs[p][h:h + 1, :], (t_new, win + LANES))
            bc_refs[p][rows, :] = pltpu.roll(u, 0, 1, stride=1, stride_axis=0)[:, :win]
            w = jnp.broadcast_to(w_ref[p, h:h + 1, :], (t_new, LANES))
            bn_ref[p, rows, :] = jnp.where(new_lane, pltpu.roll(w, 0, 1, stride=1, stride_axis=0), NEG_INF)


def _attn_sample_step(qkv_ref, kc_ref, vc_ref, bc_refs, bn_ref, att_ref, ko_ref, vo_ref, *, t_new, wb):
    np_ = len(PATTERNS)
    qkv = qkv_ref[0]
    qf, kn, vn = qkv[:, 0:D_A], qkv[:, D_A:2 * D_A], qkv[:, 2 * D_A:3 * D_A]
    nrow = H_A * t_new
    row_h = lax.broadcasted_iota(jnp.int32, (nrow, D_A), 0) // t_new
    lane_h = lax.broadcasted_iota(jnp.int32, (nrow, D_A), 1) // HD_A
    own = row_h == lane_h
    qs = jnp.where(own, jnp.concatenate([qf] * H_A, axis=0), 0.0).astype(BF16)
    pad = jnp.zeros((LANES - t_new, D_A), F32)
    knt = jnp.concatenate([pad, kn], axis=0).T
    vnt = jnp.concatenate([pad, vn], axis=0).T
    kct = kc_ref[0]
    vct = vc_ref[0]
    s_c = jnp.dot(qs, kct.astype(BF16), preferred_element_type=F32)
    s_n = jnp.dot(qs, knt.astype(BF16), preferred_element_type=F32)

    wins = [w for w, _ in PATTERNS]
    es_c, es_n = [], []
    m = None
    for p, win in enumerate(wins):
        e_c = s_c[:, wb - win:] + bc_refs[p][...]
        e_n = s_n + bn_ref[p]
        es_c.append(e_c)
        es_n.append(e_n)
        mp = jnp.maximum(jnp.max(e_c, axis=-1, keepdims=True), jnp.max(e_n, axis=-1, keepdims=True))
        m = mp if m is None else jnp.maximum(m, mp)
    pn = None
    for p in range(np_):
        e = jnp.exp(es_n[p] - m)
        pn = e if pn is None else pn + e
    order = sorted(range(np_), key=lambda p: -wins[p])
    assert wins[order[0]] == wb
    pc = jnp.exp(es_c[order[0]] - m)
    for p in order[1:]:
        e = jnp.exp(es_c[p] - m)
        pc = jnp.concatenate([pc[:, :wb - wins[p]], pc[:, wb - wins[p]:] + e], axis=1)
    den = jnp.sum(pc, axis=-1, keepdims=True) + jnp.sum(pn, axis=-1, keepdims=True)
    nt = (((1,), (1,)), ((), ()))
    num = (lax.dot_general(pc.astype(BF16), vct.astype(BF16), nt, preferred_element_type=F32)
           + lax.dot_general(pn.astype(BF16), vnt.astype(BF16), nt, preferred_element_type=F32))
    o = jnp.where(own, num / den, 0.0)
    att = o[0:t_new]
    for h in range(1, H_A):
        att = att + o[h * t_new:(h + 1) * t_new]
    att_ref[0] = att.astype(att_ref.dtype)

    is_new = lax.broadcasted_iota(jnp.int32, (D_A, LANES), 1) >= LANES - t_new
    for src, new, dst in ((kct, knt, ko_ref), (vct, vnt, vo_ref)):
        rolled = pltpu.roll(src, wb - t_new, 1)
        dst[0, :, 0:wb - LANES] = rolled[:, 0:wb - LANES]
        dst[0, :, wb - LANES:wb] = jnp.where(is_new, new, rolled[:, wb - LANES:wb])


def _mixers_kernel(*refs, t_new, wb, c, chunks, cs):
    np_ = len(PATTERNS)
    nm = 10
    p_in, s_in = refs[0:nm], refs[nm:2 * nm]
    qkv_ref, kc_ref, vc_ref = refs[2 * nm:2 * nm + 3]
    u_refs = refs[2 * nm + 3:2 * nm + 3 + np_]
    w_ref = refs[2 * nm + 3 + np_]
    outs = refs[2 * nm + 4 + np_:]
    (ph_ref, pc_out, pn_out, pm_out), (sh_ref, sc_out, sn_out, sm_out) = outs[0:4], outs[4:8]
    att_ref, ko_ref, vo_ref = outs[8:11]
    pcx_s, pm_s, scx_s, sm_s = outs[11:15]
    bc_refs, bn_ref = outs[15:15 + np_], outs[15 + np_]
    i = pl.program_id(0)

    @pl.when(i == 0)
    def _():
        _mlstm_load_state(*p_in[7:10], pcx_s, pm_s)
        _sample_bias_tables(u_refs, w_ref, bc_refs, bn_ref, t_new)

    _attn_sample_step(qkv_ref, kc_ref, vc_ref, bc_refs, bn_ref, att_ref, ko_ref, vo_ref, t_new=t_new, wb=wb)
    _mlstm_load_state(*s_in[7:10], scx_s, sm_s)
    _mlstm_chunk(*s_in[0:7], sh_ref, scx_s, sm_s, c=cs, off=0)
    _mlstm_store_state(sc_out, sn_out, sm_out, scx_s, sm_s)
    for j in range(chunks):
        _mlstm_chunk(*p_in[0:7], ph_ref, pcx_s, pm_s, c=c, off=j * c)

    @pl.when(i == pl.num_programs(0) - 1)
    def _():
        _mlstm_store_state(pc_out, pn_out, pm_out, pcx_s, pm_s)


def _mixers(prompt_m, sample_m, qkv, cache_kt, cache_vt, bias_u, bias_w, *, c, cs):
    n = prompt_m[1].shape[0]
    b, t_new, _ = qkv.shape
    wb = cache_kt.shape[2]
    assert wb == max(w for w, _ in PATTERNS) and t_new % 8 == 0 and t_new <= LANES
    assert n % (b * c) == 0 and sample_m[1].shape[0] == b * cs
    chunks = n // (b * c)
    nrow = H_A * t_new
    blk = lambda r, w: pl.BlockSpec((1, r, w), lambda i: (i, 0, 0))
    fixed = lambda shape: pl.BlockSpec((1,) + shape, lambda i: (0,) * (1 + len(shape)))
    per_b = lambda shape: pl.BlockSpec((1,) + shape, lambda i: (i,) + (0,) * len(shape))

    def group(tokens, chunk, st):
        row = lambda w: pl.BlockSpec((tokens, w), lambda i: (i, 0))
        col = lambda h: pl.BlockSpec((h, tokens), lambda i: (0, i))
        ins = [col(D_M), row(D_M), col(D_M), col(D_M), row(LANES), col(GATE_ROWS), _const_spec((D_M, chunk)),
               st((H_M, DK_M, DV_M)), st((H_M, 1, DK_M)), st((H_M, 8, LANES))]
        outs = [col(D_M), st((H_M, DK_M, DV_M)), st((H_M, 8, DK_M)), st((H_M, 8, LANES))]
        return ins, outs

    def group_shapes(tokens_total, nb):
        return [jax.ShapeDtypeStruct((D_M, tokens_total), BF16), jax.ShapeDtypeStruct((nb, H_M, DK_M, DV_M), F32),
                jax.ShapeDtypeStruct((nb, H_M, 8, DK_M), F32), jax.ShapeDtypeStruct((nb, H_M, 8, LANES), F32)]

    p_ins, p_outs = group(chunks * c, c, fixed)
    s_ins, s_outs = group(cs, cs, per_b)
    state = [pltpu.VMEM((H_M, 2 * DV_M, DK_M), F32), pltpu.VMEM((H_M, 8, LANES), F32)]
    res = pl.pallas_call(
        functools.partial(_mixers_kernel, t_new=t_new, wb=wb, c=c, chunks=chunks, cs=cs),
        grid=(b,),
        in_specs=p_ins + s_ins + [blk(t_new, N_QKV), blk(D_A, wb), blk(D_A, wb)]
                 + [_const_spec((H_A, w + LANES)) for w, _ in PATTERNS]
                 + [_const_spec((len(PATTERNS), H_A, LANES))],
        out_specs=p_outs + s_outs + [blk(t_new, D_A), blk(D_A, wb), blk(D_A, wb)],
        out_shape=group_shapes(n, 1) + group_shapes(b * cs, b)
                  + [jax.ShapeDtypeStruct((b, t_new, D_A), BF16),
                     jax.ShapeDtypeStruct((b, D_A, wb), F32), jax.ShapeDtypeStruct((b, D_A, wb), F32)],
        scratch_shapes=state + state + [pltpu.VMEM((nrow, w), F32) for w, _ in PATTERNS]
                       + [pltpu.VMEM((len(PATTERNS), nrow, LANES), F32)],
        compiler_params=_params(("arbitrary",)),
        name="mixers",
    )(*prompt_m, *sample_m, qkv, cache_kt, cache_vt, *bias_u, bias_w)
    return res[0:4], res[4:8], res[8], res[9], res[10]


def _split3(x):
    a = x.astype(BF16)
    r = x - a.astype(F32)
    b = r.astype(BF16)
    c = (r - b.astype(F32)).astype(BF16)
    return a, b, c


def _mlstm_load_state(c0_ref, n0_ref, m0_ref, cx_s, m_s):
    for h in range(H_M):
        cx_s[h, 0:DV_M, :] = c0_ref[0, h].T
        cx_s[h, DV_M:, :] = jnp.broadcast_to(n0_ref[0, h], (DV_M, DK_M))
    m_s[...] = m0_ref[0]


def _mlstm_store_state(c_out, n_out, m_out, cx_s, m_s):
    for h in range(H_M):
        c_out[0, h] = cx_s[h, 0:DV_M, :].T
        n_out[0, h] = cx_s[h, DV_M:DV_M + 8, :]
    m_out[0] = m_s[...]


def _mlstm_chunk(qt_ref, k_ref, vt_ref, ot_ref, g_ref, gtt_ref, gw_ref, h_ref, cx_s, m_s, *, c, off):
    tok = slice(off, off + c)
    gates = g_ref[tok, :]
    gates_t = gtt_ref[:, tok]
    lane = lax.broadcasted_iota(jnp.int32, gates.shape, 1)
    lf = jnp.where(jnp.logical_and(lane >= H_M, lane < 2 * H_M), gates, 0.0)
    grow = lax.broadcasted_iota(jnp.int32, gates_t.shape, 0)
    lf_t = jnp.where(jnp.logical_and(grow >= H_M, grow < 2 * H_M), gates_t, 0.0)
    ri = lax.broadcasted_iota(jnp.int32, (c, c), 0)
    cj = lax.broadcasted_iota(jnp.int32, (c, c), 1)
    upper = ri <= cj
    tril = (ri >= cj).astype(BF16)
    triu = upper.astype(BF16)
    bsum = brow = None
    for part, part_t in zip(_split3(lf), _split3(lf_t)):
        t1 = jnp.dot(tril, part, preferred_element_type=F32)
        t2 = jnp.dot(part_t, triu, preferred_element_type=F32)
        bsum = t1 if bsum is None else bsum + t1
        brow = t2 if brow is None else brow + t2
    ones = jnp.ones((DV_M, c), BF16)

    for h in range(H_M):
        sl = slice(h * DK_M, (h + 1) * DK_M)
        qt, kh, vt = qt_ref[sl, tok], k_ref[tok, sl], vt_ref[sl, tok]
        b_row = brow[H_M + h:H_M + h + 1, :]
        i_row = gates_t[h:h + 1, :]
        a_col = gates[:, h:h + 1] - bsum[:, H_M + h:H_M + h + 1]
        m_prev = m_s[h, 0:1, 0:1]
        dmat = jnp.where(upper, b_row + a_col, NEG_INF)
        inter = b_row + m_prev
        mt = jnp.maximum(inter, jnp.max(dmat, axis=0, keepdims=True))
        st = jnp.dot(kh, qt, preferred_element_type=F32)
        smt = (st * jnp.exp(dmat - mt)).astype(BF16)
        iw = jnp.exp(inter - mt)
        vext = jnp.concatenate([vt, ones], axis=0)
        cxh = cx_s[h]
        ne = (jnp.dot(vext, smt, preferred_element_type=F32)
              + iw * jnp.dot(cxh.astype(BF16), qt, preferred_element_type=F32))
        num, den = ne[:DV_M], ne[DV_M:]
        hq = num / jnp.maximum(jnp.abs(den), jnp.exp(-mt))
        hn = hq * lax.rsqrt(jnp.mean(hq * hq, axis=0, keepdims=True) + RMS_EPS) * gw_ref[sl, :]
        h_ref[sl, tok] = (jax.nn.sigmoid(ot_ref[sl, tok]) * hn).astype(h_ref.dtype)

        b_last = b_row[:, c - 1:c]
        g_row = b_last - b_row + i_row
        m_new = jnp.maximum(b_last + m_prev, jnp.max(g_row, axis=1, keepdims=True))
        ws = jnp.exp(g_row - m_new)
        wc = jnp.exp(b_last + m_prev - m_new)
        vw = (vext.astype(F32) * ws).astype(BF16)
        cx_s[h] = wc * cxh + jnp.dot(vw, kh, preferred_element_type=F32)
        m_s[h] = jnp.broadcast_to(m_new, m_s.shape[1:])


def _outffn_kernel(att_ref, hmt_ref, x_ref, prev_ref, wo_ref, gpost_ref, gpre_ref, wup_ref, cw_ref, cb_ref,
                   wdn_ref, gffn_ref, y_ref, st_ref, u_ref, a_ref, *, tile, shift, hdr):
    i = pl.program_id(0)

    @pl.when(i == 0)
    def _():
        u_ref[hdr - 2 * shift:hdr, :] = prev_ref[...]

    @pl.when(i > 0)
    def _():
        u_ref[0:hdr, :] = u_ref[tile:tile + hdr, :]

    mixed = (jnp.dot(att_ref[...], wo_ref[0:D_A, :], preferred_element_type=F32)
             + lax.dot_general(hmt_ref[...], wo_ref[D_A:, :], (((0,), (0,)), ((), ())),
                               preferred_element_type=F32))
    x1 = x_ref[...] + _rms(mixed, gpost_ref[...])
    h2 = _rms(x1, gpre_ref[...]).astype(BF16)
    u_ref[hdr:hdr + tile, :] = jnp.dot(h2, wup_ref[...], preferred_element_type=F32)

    def conv(c):
        cols = slice(c * LANES, (c + 1) * LANES)
        y = cb_ref[:, cols] + u_ref[hdr - 2 * shift:hdr - 2 * shift + tile, cols] * cw_ref[0:1, cols]
        y = y + u_ref[hdr - shift:hdr - shift + tile, cols] * cw_ref[1:2, cols]
        return y + u_ref[hdr:hdr + tile, cols] * cw_ref[2:3, cols]

    nff = D_FF // LANES
    for c in range(nff):
        gate, val = conv(c), conv(nff + c)
        e = jnp.exp(gate * (GELU_C1 + GELU_C3 * (gate * gate)))
        a_ref[:, c * LANES:(c + 1) * LANES] = (gate * val / (1.0 + e)).astype(BF16)

    y2 = jnp.dot(a_ref[...], wdn_ref[...], preferred_element_type=F32)
    y_ref[...] = x1 + _rms(y2, gffn_ref[...])
    st_ref[...] = u_ref[tile:tile + hdr, :]


def _outffn(att, hmt, x, prev, wo, gpost, gpre, wup, cw, cb, wdn, gffn, *, tile, shift):
    n = x.shape[0]
    assert n % tile == 0 and (shift == 1 or n == tile)
    hdr = max(8, 2 * shift)
    assert hdr % 8 == 0 and (tile % LANES == 0 or tile == n)
    row = lambda w: pl.BlockSpec((tile, w), lambda i: (i, 0))
    return pl.pallas_call(
        functools.partial(_outffn_kernel, tile=tile, shift=shift, hdr=hdr),
        grid=(n // tile,),
        in_specs=[row(D_A), pl.BlockSpec((D_M, tile), lambda i: (0, i)), row(D_MODEL),
                  _const_spec((2 * shift, 2 * D_FF)),
                  _const_spec((D_A + D_M, D_MODEL)), _const_spec((1, D_MODEL)), _const_spec((1, D_MODEL)),
                  _const_spec((D_MODEL, 2 * D_FF)), _const_spec((CONV_W, 2 * D_FF)), _const_spec((1, 2 * D_FF)),
                  _const_spec((D_FF, D_MODEL)), _const_spec((1, D_MODEL))],
        out_specs=[row(D_MODEL), pl.BlockSpec((hdr, 2 * D_FF), lambda i: (0, 0))],
        out_shape=[jax.ShapeDtypeStruct((n, D_MODEL), F32), jax.ShapeDtypeStruct((hdr, 2 * D_FF), F32)],
        scratch_shapes=[pltpu.VMEM((hdr + tile, 2 * D_FF), F32), pltpu.VMEM((tile, D_FF), BF16)],
        compiler_params=_params(("arbitrary",)),
        name="outffn",
    )(att, hmt, x, prev, wo, gpost, gpre, wup, cw, cb, wdn, gffn)


def _t5_bucket(dist):
    max_exact = N_BUCKETS // 2
    d = jnp.maximum(dist, max_exact).astype(F32)
    large = max_exact + (jnp.log(d / max_exact) / math.log(BUCKET_MAX_DIST / max_exact)
                         * (N_BUCKETS - max_exact)).astype(jnp.int32)
    large = jnp.minimum(large, N_BUCKETS - 1)
    return jnp.where(dist < max_exact, dist, large)


def _pattern_bias(rel_bias, dil):
    j = jnp.arange(SPAN + 1)
    return rel_bias[_t5_bucket(j * dil)].T.astype(F32)


def _prompt_bias(rel_bias):
    out = []
    for _, dil in PATTERNS:
        bh = _pattern_bias(rel_bias, dil)
        v = jnp.concatenate([bh[:, ::-1], jnp.full((H_A, SPAN - 1), NEG_INF, F32)], axis=1)
        out.append(v.reshape(H_A // 2, 2, 2 * SPAN))
    return jnp.stack(out)


def _sample_bias(rel_bias, t_new):
    us, ws = [], []
    fill = lambda n: jnp.full((H_A, n), NEG_INF, F32)
    for win, dil in PATTERNS:
        bh = _pattern_bias(rel_bias, dil)
        dist = jnp.arange((SPAN + 1) * dil)
        g = jnp.where(dist % dil == 0, jnp.repeat(bh, dil, axis=1), NEG_INF)[:, :win + 1]
        us.append(jnp.concatenate([g[:, ::-1], fill(LANES - 1)], axis=1))
        ws.append(jnp.concatenate([g[:, :LANES - t_new + 1][:, ::-1], fill(t_new - 1)], axis=1))
    return us, jnp.stack(ws)


def _layer_weights(g_mix_pre, w_in, b_igate, b_fgate, g_mlstm_out, w_out, g_mix_post, g_ffn_pre, w_up,
                   conv_w, conv_b, w_down, g_ffn_post):
    wg = jnp.zeros((D_MODEL, LANES), F32).at[:, :2 * H_M].set(w_in[:, N_QKV + N_MIX:])
    bg = jnp.zeros((1, LANES), F32).at[0, :H_M].set(b_igate.astype(F32)).at[0, H_M:2 * H_M].set(b_fgate.astype(F32))
    row = lambda v: v.astype(F32).reshape(1, -1)
    mix = lambda j: w_in[:, N_QKV + j * D_M:N_QKV + (j + 1) * D_M]
    wmt = jnp.concatenate([mix(0), mix(2), mix(3)], axis=1).T.astype(BF16)
    gw = jnp.broadcast_to(g_mlstm_out.astype(F32)[:, None], (D_M, MLSTM_STEP))
    return dict(
        g_pre=row(g_mix_pre), wa=w_in[:, :N_QKV].astype(BF16), wkvt=w_in[:, D_A:N_QKV].T.astype(BF16),
        wmt=wmt, wk=mix(1).astype(BF16),
        wg=wg.astype(BF16), bg=bg, gw=gw, wo=w_out.astype(BF16), g_post=row(g_mix_post),
        g_ffn_pre=row(g_ffn_pre), wup=w_up.astype(BF16), cw=conv_w.astype(F32), cb=row(conv_b),
        wdn=w_down.astype(BF16), g_ffn_post=row(g_ffn_post))


def _state_in(c0, n0, m0):
    nb = c0.shape[0]
    return (c0.astype(F32), n0.astype(F32).reshape(nb, H_M, 1, DK_M),
            jnp.broadcast_to(m0.astype(F32)[..., None, None], (nb, H_M, 8, LANES)))


def _state_out(c, n, m):
    return c, n[:, :, 0, :], m[:, :, 0, 0]


def _layer(xp, xs, w, bias_p, bias_u, bias_w, cache_k, cache_v, c0, n0, m0, conv_prev, *, tile_in, tile_ffn):
    s_len = xp.shape[0]
    b, t_new, _ = xs.shape
    n = b * t_new
    wb = cache_k.shape[1]
    proj = (w["g_pre"], w["wa"], w["wmt"], w["wk"], w["wg"], w["bg"])

    wbp = min(max(wd for wd, _ in PATTERNS), s_len)
    dils = tuple(d for _, d in PATTERNS if d > 1)
    qkv1, qkv4, qkv16, tail_p, qmt_p, km_p, vmt_p, omt_p, gates_p, gates_tp = _inproj(
        xp, *proj, w["wkvt"], tile=tile_in, dils=dils, tail_rows=wbp)
    att_p = _attn_prompt([qkv1.reshape(1, s_len, N_QKV), qkv4, qkv16], bias_p)

    _, tail, qmt, km, vmt, omt, gates, gates_t = _inproj(xs.reshape(n, D_MODEL), *proj, tile=n, dils=(), tail_rows=n)
    c = LANES
    assert t_new <= c
    padr = lambda a: jnp.pad(a.reshape(b, t_new, -1), ((0, 0), (0, c - t_new), (0, 0))).reshape(b * c, -1)
    padc = lambda a: jnp.pad(a.reshape(-1, b, t_new), ((0, 0), (0, 0), (0, c - t_new))).reshape(-1, b * c)
    gpad = jnp.where(jnp.arange(LANES) < H_M, NEG_INF, 0.0).astype(F32)
    gates_pad = jnp.concatenate([gates.reshape(b, t_new, LANES),
                                 jnp.broadcast_to(gpad, (b, c - t_new, LANES))], axis=1).reshape(b * c, LANES)
    gates_tpad = jnp.concatenate([gates_t.reshape(GATE_ROWS, b, t_new),
                                  jnp.broadcast_to(gpad[:GATE_ROWS, None, None], (GATE_ROWS, b, c - t_new))],
                                 axis=2).reshape(GATE_ROWS, b * c)

    to_t = lambda a: jnp.transpose(a.astype(F32), (0, 2, 3, 1)).reshape(b, D_A, wb)
    from_t = lambda a: jnp.transpose(a.reshape(b, H_A, HD_A, wb), (0, 3, 1, 2))
    state0 = _state_in(jnp.zeros((1, H_M, DK_M, DV_M), F32), jnp.zeros((1, H_M, DK_M), F32),
                       jnp.zeros((1, H_M), F32))
    (hmt_p, c_p, n_p, m_p), (hmt_s, c_o, n_o, m_o), att, k_win, v_win = _mixers(
        (qmt_p, km_p, vmt_p, omt_p, gates_p, gates_tp, w["gw"]) + state0,
        (padc(qmt), padr(km), padc(vmt), padc(omt), gates_pad, gates_tpad, w["gw"][:, :c]) + _state_in(c0, n0, m0),
        tail.reshape(b, t_new, N_QKV), to_t(cache_k), to_t(cache_v), bias_u, bias_w, c=MLSTM_STEP, cs=c)

    prev0 = jnp.zeros((CONV_W - 1, 2 * D_FF), F32)
    yp, st_p = _outffn(att_p, hmt_p, xp, prev0, w["wo"], w["g_post"], w["g_ffn_pre"], w["wup"], w["cw"], w["cb"],
                       w["wdn"], w["g_ffn_post"], tile=tile_ffn, shift=1)
    window = lambda a: jnp.transpose(a.reshape(H_A, HD_A, wbp), (2, 0, 1))[None]
    k_win_p, v_win_p = window(tail_p[0:D_A]), window(tail_p[D_A:2 * D_A])
    state_p = (k_win_p, v_win_p) + _state_out(c_p, n_p, m_p) + (st_p[-(CONV_W - 1):][None],)

    tm = lambda a: a.reshape(b, t_new, -1).transpose(1, 0, 2).reshape(n, -1)
    hmt = hmt_s.reshape(D_M, b, c)[:, :, :t_new].transpose(0, 2, 1).reshape(D_M, n)
    prev = conv_prev.astype(F32).transpose(1, 0, 2).reshape((CONV_W - 1) * b, 2 * D_FF)
    ys, st = _outffn(tm(att), hmt, tm(xs), prev, w["wo"], w["g_post"], w["g_ffn_pre"], w["wup"], w["cw"],
                     w["cb"], w["wdn"], w["g_ffn_post"], tile=n, shift=b)
    ys = ys.reshape(t_new, b, D_MODEL).transpose(1, 0, 2)
    conv_state = st.reshape(CONV_W - 1, b, 2 * D_FF).transpose(1, 0, 2)
    state_s = (from_t(k_win), from_t(v_win)) + _state_out(c_o, n_o, m_o) + (conv_state,)
    return yp, ys, state_p, state_s


def kernel(x_prompt, x_sample, cache_attn_k, cache_attn_v, state_mlstm_C, state_mlstm_n, state_mlstm_m,
           state_ffn_conv, rel_bias, g_mix_pre, w_in, b_igate, b_fgate, g_mlstm_out, w_out, g_mix_post,
           g_ffn_pre, w_up, conv_w, conv_b, w_down, g_ffn_post):
    depth = w_in.shape[0]
    batch, s_len, _ = x_prompt.shape
    assert batch == 1
    t_new = x_sample.shape[1]
    bias_p = _prompt_bias(rel_bias)
    bias_u, bias_w = _sample_bias(rel_bias, t_new)
    yp = x_prompt[0]
    ys = x_sample
    new_p, new_s = [], []
    for l in range(depth):
        w = _layer_weights(g_mix_pre[l], w_in[l], b_igate[l], b_fgate[l], g_mlstm_out[l], w_out[l],
                           g_mix_post[l], g_ffn_pre[l], w_up[l], conv_w[l], conv_b[l], w_down[l], g_ffn_post[l])
        yp, ys, sp, ss = _layer(yp, ys, w, bias_p, bias_u, bias_w, cache_attn_k[l], cache_attn_v[l],
                                state_mlstm_C[l], state_mlstm_n[l], state_mlstm_m[l], state_ffn_conv[l],
                                tile_in=256, tile_ffn=512)
        new_p.append(sp)
        new_s.append(ss)
    stack = lambda states, i: jnp.stack([s[i] for s in states])
    return ((yp[None], ys) + tuple(stack(new_p, i) for i in range(6))
            + tuple(stack(new_s, i) for i in range(6)))
```

```python
import functools
import math

import jax
import jax.numpy as jnp
import numpy as np
from jax import lax
from jax.experimental import pallas as pl
from jax.experimental.pallas import tpu as pltpu

F32 = jnp.float32
BF16 = jnp.bfloat16

D_MODEL = 1024
HD_A = 64
H_A = 8
D_A = H_A * HD_A
DK_M = 128
DV_M = 128
H_M = 4
D_M = H_M * DV_M
PATTERNS = ((128, 1), (512, 4), (2048, 16))
SPAN = 128
N_BUCKETS = 32
BUCKET_MAX_DIST = 2048
MLSTM_STEP = 256
D_FF = 2816
CONV_W = 3
RMS_EPS = 1e-6
N_QKV = 3 * D_A
N_MIX = 4 * D_M
LANES = 128
GATE_ROWS = 16
SUPER = 2048
VMEM_LIMIT = 56 * 1024 * 1024
NEG_INF = float("-inf")
GELU_C1 = -2.0 * math.sqrt(2.0 / math.pi)
GELU_C3 = GELU_C1 * 0.044715

for _w, _d in PATTERNS:
    assert _w // _d == SPAN


def _params(sem, vmem=VMEM_LIMIT):
    return pltpu.CompilerParams(dimension_semantics=sem, vmem_limit_bytes=vmem)


def _const_spec(shape):
    nd = len(shape)
    return pl.BlockSpec(shape, lambda *_: (0,) * nd, pipeline_mode=pl.Buffered(1))


def _rms(x, g):
    return x * lax.rsqrt(jnp.mean(x * x, axis=-1, keepdims=True) + RMS_EPS) * g


def _inproj_kernel(x_ref, g_ref, wa_ref, wmt_ref, wk_ref, wg_ref, bg_ref, *refs, tile, dils, first_tail):
    nd = len(dils)
    perm_ref, wkvt_ref = (refs[0], refs[1]) if nd else (None, None)
    refs = refs[2:] if nd else refs
    qkv1_ref = refs[0]
    dil_refs = refs[1:1 + nd]
    tail_ref, qmt_ref, km_ref, vmt_ref, omt_ref, gt_ref, gtt_ref = refs[1 + nd:]

    hb = _rms(x_ref[...], g_ref[...]).astype(BF16)
    za = jnp.dot(hb, wa_ref[...], preferred_element_type=F32)
    za = jnp.concatenate([za[:, :D_A] * (HD_A ** -0.5), za[:, D_A:]], axis=1)
    zb = za.astype(BF16)
    qkv1_ref[...] = zb
    if not nd:
        tail_ref[...] = za

    if nd:
        zp = jnp.dot(perm_ref[...], zb, preferred_element_type=F32).astype(BF16)
        base = 0
        for dil, out_ref in zip(dils, dil_refs):
            for r in range(dil):
                out_ref[r] = zp[base + r * (tile // dil):base + (r + 1) * (tile // dil)]
            base += tile

    zt = lax.dot_general(wmt_ref[...], hb, (((1,), (1,)), ((), ())), preferred_element_type=F32)
    qmt_ref[...] = zt[0:D_M].astype(BF16)
    vmt_ref[...] = zt[D_M:2 * D_M].astype(BF16)
    omt_ref[...] = zt[2 * D_M:3 * D_M]
    zk = jnp.dot(hb, wk_ref[...], preferred_element_type=F32)
    km_ref[...] = (zk * (DK_M ** -0.5)).astype(BF16)

    zg = jnp.dot(hb, wg_ref[...], preferred_element_type=F32) + bg_ref[...]
    logsig = jnp.minimum(zg, 0.0) - jnp.log1p(jnp.exp(-jnp.abs(zg)))
    glane = lax.broadcasted_iota(jnp.int32, zg.shape, 1)
    gates = jnp.where(glane < H_M, zg, logsig)
    gt_ref[...] = gates
    gtt_ref[...] = gates.T[0:GATE_ROWS]

    if nd:
        @pl.when(pl.program_id(0) >= first_tail)
        def _():
            tail_ref[...] = lax.dot_general(wkvt_ref[...], hb, (((1,), (1,)), ((), ())),
                                            preferred_element_type=F32)


def _row_permutation(tile, dils):
    p = np.zeros((len(dils) * tile, tile), np.float32)
    for j, dil in enumerate(dils):
        for r in range(dil):
            for m in range(tile // dil):
                p[j * tile + r * (tile // dil) + m, m * dil + r] = 1.0
    return jnp.asarray(p, BF16)


def _inproj(x, g, wa, wmt, wk, wg, bg, wkvt=None, *, tile, dils, tail_rows):
    n = x.shape[0]
    assert n % tile == 0 and tail_rows % tile == 0 and (tile % LANES == 0 or tile == n)
    for d in dils:
        assert tile % (16 * d) == 0
    steps = n // tile
    first_tail = (n - tail_rows) // tile
    row = lambda w: pl.BlockSpec((tile, w), lambda i: (i, 0))
    col = lambda h: pl.BlockSpec((h, tile), lambda i: (0, i))
    out_shape = [jax.ShapeDtypeStruct((n, N_QKV), BF16)]
    out_specs = [row(N_QKV)]
    for d in dils:
        out_shape.append(jax.ShapeDtypeStruct((d, n // d, N_QKV), BF16))
        out_specs.append(pl.BlockSpec((d, tile // d, N_QKV), lambda i: (0, i, 0)))
    if dils:
        out_shape.append(jax.ShapeDtypeStruct((2 * D_A, tail_rows), F32))
        out_specs.append(pl.BlockSpec((2 * D_A, tile), lambda i: (0, jnp.maximum(i - first_tail, 0))))
    else:
        out_shape.append(jax.ShapeDtypeStruct((tail_rows, N_QKV), F32))
        out_specs.append(pl.BlockSpec((tile, N_QKV), lambda i: (jnp.maximum(i - first_tail, 0), 0)))
    out_shape += [jax.ShapeDtypeStruct((D_M, n), BF16), jax.ShapeDtypeStruct((n, D_M), BF16),
                  jax.ShapeDtypeStruct((D_M, n), BF16), jax.ShapeDtypeStruct((D_M, n), F32),
                  jax.ShapeDtypeStruct((n, LANES), F32), jax.ShapeDtypeStruct((GATE_ROWS, n), F32)]
    out_specs += [col(D_M), row(D_M), col(D_M), col(D_M), row(LANES), col(GATE_ROWS)]
    in_specs = [row(D_MODEL), _const_spec((1, D_MODEL)), _const_spec((D_MODEL, N_QKV)),
                _const_spec((3 * D_M, D_MODEL)), _const_spec((D_MODEL, D_M)),
                _const_spec((D_MODEL, LANES)), _const_spec((1, LANES))]
    args = [x, g, wa, wmt, wk, wg, bg]
    if dils:
        in_specs += [_const_spec((len(dils) * tile, tile)), _const_spec((2 * D_A, D_MODEL))]
        args += [_row_permutation(tile, dils), wkvt]
    return pl.pallas_call(
        functools.partial(_inproj_kernel, tile=tile, dils=dils, first_tail=first_tail),
        grid=(steps,),
        in_specs=in_specs,
        out_specs=out_specs,
        out_shape=out_shape,
        compiler_params=_params(("arbitrary",)),
        name="inproj",
    )(*args)


def _attn_prompt_kernel(*refs):
    np_ = len(PATTERNS)
    in_refs = refs[:5 * np_]
    bvec_ref, o_ref = refs[5 * np_], refs[5 * np_ + 1]
    scr = refs[5 * np_ + 2:]
    kbufs, vbufs = scr[0:np_], scr[np_:2 * np_]
    num_s, l_s, m_s, bias_s = scr[2 * np_:]
    n = pl.program_id(1)

    lane = lax.broadcasted_iota(jnp.int32, (SPAN, LANES), 1)
    even = lane < HD_A
    ones = jnp.ones((2 * SPAN, LANES), BF16)

    @pl.when(n == 0)
    def _():
        key_col = lax.broadcasted_iota(jnp.int32, (SPAN, 2 * SPAN), 1)
        for p in range(np_):
            for e in range(2):
                vb = jnp.broadcast_to(bvec_ref[p, 0, e:e + 1, :], (SPAN, 2 * SPAN))
                table = pltpu.roll(vb, 0, 1, stride=1, stride_axis=0)
                bias_s[p, 0, e * SPAN:(e + 1) * SPAN, :] = table
                bias_s[p, 1, e * SPAN:(e + 1) * SPAN, :] = jnp.where(key_col >= SPAN, table, NEG_INF)

    for p in range(np_):
        q_ref, k_ref, v_ref, kp_ref, vp_ref = in_refs[5 * p:5 * p + 5]
        kbufs[p][:, 0:SPAN, :] = kp_ref[...]
        kbufs[p][:, SPAN:, :] = k_ref[...]
        vbufs[p][:, 0:SPAN, :] = vp_ref[...]
        vbufs[p][:, SPAN:, :] = v_ref[...]

    first_n = (n == 0).astype(jnp.int32)

    def unit(p, j):
        dil = PATTERNS[p][1]
        r, mb = divmod(j, SUPER // dil // SPAN)
        row0 = mb * SPAN
        qb = in_refs[5 * p][r, row0:row0 + SPAN, :]
        kb = kbufs[p][r, row0:row0 + 2 * SPAN, :]
        vb = vbufs[p][r, row0:row0 + 2 * SPAN, :]
        bias = bias_s[p, first_n] if mb == 0 else bias_s[p, 0]
        zero = jnp.zeros_like(qb)
        qs = jnp.concatenate([jnp.where(even, qb, zero), jnp.where(even, zero, qb)], axis=0)
        s = lax.dot_general(qs, kb, (((1,), (1,)), ((), ())), preferred_element_type=F32) + bias
        m = jnp.max(s, axis=-1, keepdims=True)
        pr = jnp.exp(s - m).astype(BF16)
        res = jnp.dot(pr, jnp.concatenate([vb, ones], axis=1), preferred_element_type=F32)
        mb_ = jnp.broadcast_to(m, (2 * SPAN, LANES))
        rows = pl.ds(row0, SPAN) if dil == 1 else pl.ds(row0 * dil + r, SPAN, stride=dil)
        num_s[p, rows, :] = jnp.where(even, res[:SPAN, :LANES], res[SPAN:, :LANES])
        l_s[p, rows, :] = jnp.where(even, res[:SPAN, LANES:], res[SPAN:, LANES:])
        m_s[p, rows, :] = jnp.where(even, mb_[:SPAN], mb_[SPAN:])

    def combine(jn):
        rows = slice(jn * SPAN, (jn + 1) * SPAN)
        ms = [m_s[p, rows, :] for p in range(np_)]
        m = functools.reduce(jnp.maximum, ms)
        num = jnp.zeros((SPAN, LANES), F32)
        den = jnp.zeros((SPAN, LANES), F32)
        for p in range(np_):
            a = jnp.exp(ms[p] - m)
            num = num + a * num_s[p, rows, :]
            den = den + a * l_s[p, rows, :]
        o_ref[rows, :] = (num / den).astype(o_ref.dtype)

    done = set()
    for jn in range(SUPER // SPAN):
        for p in sorted(range(np_), key=lambda p: -PATTERNS[p][1]):
            dil = PATTERNS[p][1]
            nblk = SUPER // dil // SPAN
            for r in range(dil):
                j = r * nblk + jn // dil
                if (p, j) not in done:
                    done.add((p, j))
                    unit(p, j)
        combine(jn)


def _attn_prompt(qkvs, bvec):
    s_len = qkvs[0].shape[1]
    assert s_len % SUPER == 0
    nsb = s_len // SUPER
    npair = D_A // LANES
    in_specs, args, scratch = [], [], []
    for (_, dil), a in zip(PATTERNS, qkvs):
        rows = SUPER // dil
        nblk = rows // SPAN
        for part in range(3):
            in_specs.append(pl.BlockSpec((dil, rows, LANES),
                                         lambda hp, n, part=part: (0, n, part * npair + hp)))
            args.append(a)
        for part in (1, 2):
            in_specs.append(pl.BlockSpec((dil, SPAN, LANES),
                                         lambda hp, n, part=part, nblk=nblk:
                                         (0, jnp.maximum(n * nblk - 1, 0), part * npair + hp)))
            args.append(a)
    in_specs.append(pl.BlockSpec((len(PATTERNS), 1, 2, 2 * SPAN), lambda hp, n: (0, hp, 0, 0)))
    args.append(bvec)
    for _ in range(2):
        for _, dil in PATTERNS:
            scratch.append(pltpu.VMEM((dil, SPAN + SUPER // dil, LANES), BF16))
    scratch += [pltpu.VMEM((len(PATTERNS), SUPER, LANES), F32)] * 3
    scratch.append(pltpu.VMEM((len(PATTERNS), 2, 2 * SPAN, 2 * SPAN), F32))
    return pl.pallas_call(
        _attn_prompt_kernel,
        grid=(npair, nsb),
        in_specs=in_specs,
        out_specs=pl.BlockSpec((SUPER, LANES), lambda hp, n: (n, hp)),
        out_shape=jax.ShapeDtypeStruct((s_len, D_A), BF16),
        scratch_shapes=scratch,
        compiler_params=_params(("arbitrary", "arbitrary")),
        name="attn_prompt",
    )(*args)


def _sample_bias_tables(u_refs, w_ref, bc_refs, bn_ref, t_new):
    new_lane = lax.broadcasted_iota(jnp.int32, (t_new, LANES), 1) >= LANES - t_new
    for p, (win, _) in enumerate(PATTERNS):
        for h in range(H_A):
            rows = slice(h * t_new, (h + 1) * t_new)
            u = jnp.broadcast_to(u_refs[p][h:h + 1, :], (t_new, win + LANES))
            bc_refs[p][rows, :] = pltpu.roll(u, 0, 1, stride=1, stride_axis=0)[:, :win]
            w = jnp.broadcast_to(w_ref[p, h:h + 1, :], (t_new, LANES))
            bn_ref[p, rows, :] = jnp.where(new_lane, pltpu.roll(w, 0, 1, stride=1, stride_axis=0), NEG_INF)


def _attn_sample_step(qkv_ref, kc_ref, vc_ref, bc_refs, bn_ref, att_ref, ko_ref, vo_ref, *, t_new, wb):
    np_ = len(PATTERNS)
    qkv = qkv_ref[0]
    qf, kn, vn = qkv[:, 0:D_A], qkv[:, D_A:2 * D_A], qkv[:, 2 * D_A:3 * D_A]
    nrow = H_A * t_new
    row_h = lax.broadcasted_iota(jnp.int32, (nrow, D_A), 0) // t_new
    lane_h = lax.broadcasted_iota(jnp.int32, (nrow, D_A), 1) // HD_A
    own = row_h == lane_h
    qs = jnp.where(own, jnp.concatenate([qf] * H_A, axis=0), 0.0).astype(BF16)
    pad = jnp.zeros((LANES - t_new, D_A), F32)
    knt = jnp.concatenate([pad, kn], axis=0).T
    vnt = jnp.concatenate([pad, vn], axis=0).T
    kct = kc_ref[0]
    vct = vc_ref[0]
    s_c = jnp.dot(qs, kct.astype(BF16), preferred_element_type=F32)
    s_n = jnp.dot(qs, knt.astype(BF16), preferred_element_type=F32)

    wins = [w for w, _ in PATTERNS]
    es_c, es_n = [], []
    m = None
    for p, win in enumerate(wins):
        e_c = s_c[:, wb - win:] + bc_refs[p][...]
        e_n = s_n + bn_ref[p]
        es_c.append(e_c)
        es_n.append(e_n)
        mp = jnp.maximum(jnp.max(e_c, axis=-1, keepdims=True), jnp.max(e_n, axis=-1, keepdims=True))
        m = mp if m is None else jnp.maximum(m, mp)
    pn = None
    for p in range(np_):
        e = jnp.exp(es_n[p] - m)
        pn = e if pn is None else pn + e
    order = sorted(range(np_), key=lambda p: -wins[p])
    assert wins[order[0]] == wb
    pc = jnp.exp(es_c[order[0]] - m)
    for p in order[1:]:
        e = jnp.exp(es_c[p] - m)
        pc = jnp.concatenate([pc[:, :wb - wins[p]], pc[:, wb - wins[p]:] + e], axis=1)
    den = jnp.sum(pc, axis=-1, keepdims=True) + jnp.sum(pn, axis=-1, keepdims=True)
    nt = (((1,), (1,)), ((), ()))
    num = (lax.dot_general(pc.astype(BF16), vct.astype(BF16), nt, preferred_element_type=F32)
           + lax.dot_general(pn.astype(BF16), vnt.astype(BF16), nt, preferred_element_type=F32))
    o = jnp.where(own, num / den, 0.0)
    att = o[0:t_new]
    for h in range(1, H_A):
        att = att + o[h * t_new:(h + 1) * t_new]
    att_ref[0] = att.astype(att_ref.dtype)

    is_new = lax.broadcasted_iota(jnp.int32, (D_A, LANES), 1) >= LANES - t_new
    for src, new, dst in ((kct, knt, ko_ref), (vct, vnt, vo_ref)):
        rolled = pltpu.roll(src, wb - t_new, 1)
        dst[0, :, 0:wb - LANES] = rolled[:, 0:wb - LANES]
        dst[0, :, wb - LANES:wb] = jnp.where(is_new, new, rolled[:, wb - LANES:wb])


def _mixers_kernel(*refs, t_new, wb, c, chunks, cs):
    np_ = len(PATTERNS)
    nm = 10
    p_in, s_in = refs[0:nm], refs[nm:2 * nm]
    qkv_ref, kc_ref, vc_ref = refs[2 * nm:2 * nm + 3]
    u_refs = refs[2 * nm + 3:2 * nm + 3 + np_]
    w_ref = refs[2 * nm + 3 + np_]
    outs = refs[2 * nm + 4 + np_:]
    (ph_ref, pc_out, pn_out, pm_out), (sh_ref, sc_out, sn_out, sm_out) = outs[0:4], outs[4:8]
    att_ref, ko_ref, vo_ref = outs[8:11]
    pcx_s, pm_s, scx_s, sm_s = outs[11:15]
    bc_refs, bn_ref = outs[15:15 + np_], outs[15 + np_]
    i = pl.program_id(0)

    @pl.when(i == 0)
    def _():
        _mlstm_load_state(*p_in[7:10], pcx_s, pm_s)
        _sample_bias_tables(u_refs, w_ref, bc_refs, bn_ref, t_new)

    _attn_sample_step(qkv_ref, kc_ref, vc_ref, bc_refs, bn_ref, att_ref, ko_ref, vo_ref, t_new=t_new, wb=wb)
    _mlstm_load_state(*s_in[7:10], scx_s, sm_s)
    _mlstm_chunk(*s_in[0:7], sh_ref, scx_s, sm_s, c=cs, off=0)
    _mlstm_store_state(sc_out, sn_out, sm_out, scx_s, sm_s)
    for j in range(chunks):
        _mlstm_chunk(*p_in[0:7], ph_ref, pcx_s, pm_s, c=c, off=j * c)

    @pl.when(i == pl.num_programs(0) - 1)
    def _():
        _mlstm_store_state(pc_out, pn_out, pm_out, pcx_s, pm_s)


def _mixers(prompt_m, sample_m, qkv, cache_kt, cache_vt, bias_u, bias_w, *, c, cs):
    n = prompt_m[1].shape[0]
    b, t_new, _ = qkv.shape
    wb = cache_kt.shape[2]
    assert wb == max(w for w, _ in PATTERNS) and t_new % 8 == 0 and t_new <= LANES
    assert n % (b * c) == 0 and sample_m[1].shape[0] == b * cs
    chunks = n // (b * c)
    nrow = H_A * t_new
    blk = lambda r, w: pl.BlockSpec((1, r, w), lambda i: (i, 0, 0))
    fixed = lambda shape: pl.BlockSpec((1,) + shape, lambda i: (0,) * (1 + len(shape)))
    per_b = lambda shape: pl.BlockSpec((1,) + shape, lambda i: (i,) + (0,) * len(shape))

    def group(tokens, chunk, st):
        row = lambda w: pl.BlockSpec((tokens, w), lambda i: (i, 0))
        col = lambda h: pl.BlockSpec((h, tokens), lambda i: (0, i))
        ins = [col(D_M), row(D_M), col(D_M), col(D_M), row(LANES), col(GATE_ROWS), _const_spec((D_M, chunk)),
               st((H_M, DK_M, DV_M)), st((H_M, 1, DK_M)), st((H_M, 8, LANES))]
        outs = [col(D_M), st((H_M, DK_M, DV_M)), st((H_M, 8, DK_M)), st((H_M, 8, LANES))]
        return ins, outs

    def group_shapes(tokens_total, nb):
        return [jax.ShapeDtypeStruct((D_M, tokens_total), BF16), jax.ShapeDtypeStruct((nb, H_M, DK_M, DV_M), F32),
                jax.ShapeDtypeStruct((nb, H_M, 8, DK_M), F32), jax.ShapeDtypeStruct((nb, H_M, 8, LANES), F32)]

    p_ins, p_outs = group(chunks * c, c, fixed)
    s_ins, s_outs = group(cs, cs, per_b)
    state = [pltpu.VMEM((H_M, 2 * DV_M, DK_M), F32), pltpu.VMEM((H_M, 8, LANES), F32)]
    res = pl.pallas_call(
        functools.partial(_mixers_kernel, t_new=t_new, wb=wb, c=c, chunks=chunks, cs=cs),
        grid=(b,),
        in_specs=p_ins + s_ins + [blk(t_new, N_QKV), blk(D_A, wb), blk(D_A, wb)]
                 + [_const_spec((H_A, w + LANES)) for w, _ in PATTERNS]
                 + [_const_spec((len(PATTERNS), H_A, LANES))],
        out_specs=p_outs + s_outs + [blk(t_new, D_A), blk(D_A, wb), blk(D_A, wb)],
        out_shape=group_shapes(n, 1) + group_shapes(b * cs, b)
                  + [jax.ShapeDtypeStruct((b, t_new, D_A), BF16),
                     jax.ShapeDtypeStruct((b, D_A, wb), F32), jax.ShapeDtypeStruct((b, D_A, wb), F32)],
        scratch_shapes=state + state + [pltpu.VMEM((nrow, w), F32) for w, _ in PATTERNS]
                       + [pltpu.VMEM((len(PATTERNS), nrow, LANES), F32)],
        compiler_params=_params(("arbitrary",)),
        name="mixers",
    )(*prompt_m, *sample_m, qkv, cache_kt, cache_vt, *bias_u, bias_w)
    return res[0:4], res[4:8], res[8], res[9], res[10]


def _split3(x):
    a = x.astype(BF16)
    r = x - a.astype(F32)
    b = r.astype(BF16)
    c = (r - b.astype(F32)).astype(BF16)
    return a, b, c


def _mlstm_load_state(c0_ref, n0_ref, m0_ref, cx_s, m_s):
    for h in range(H_M):
        cx_s[h, 0:DV_M, :] = c0_ref[0, h].T
        cx_s[h, DV_M:, :] = jnp.broadcast_to(n0_ref[0, h], (DV_M, DK_M))
    m_s[...] = m0_ref[0]


def _mlstm_store_state(c_out, n_out, m_out, cx_s, m_s):
    for h in range(H_M):
        c_out[0, h] = cx_s[h, 0:DV_M, :].T
        n_out[0, h] = cx_s[h, DV_M:DV_M + 8, :]
    m_out[0] = m_s[...]


def _mlstm_chunk(qt_ref, k_ref, vt_ref, ot_ref, g_ref, gtt_ref, gw_ref, h_ref, cx_s, m_s, *, c, off):
    tok = slice(off, off + c)
    gates = g_ref[tok, :]
    gates_t = gtt_ref[:, tok]
    lane = lax.broadcasted_iota(jnp.int32, gates.shape, 1)
    lf = jnp.where(jnp.logical_and(lane >= H_M, lane < 2 * H_M), gates, 0.0)
    grow = lax.broadcasted_iota(jnp.int32, gates_t.shape, 0)
    lf_t = jnp.where(jnp.logical_and(grow >= H_M, grow < 2 * H_M), gates_t, 0.0)
    ri = lax.broadcasted_iota(jnp.int32, (c, c), 0)
    cj = lax.broadcasted_iota(jnp.int32, (c, c), 1)
    upper = ri <= cj
    tril = (ri >= cj).astype(BF16)
    triu = upper.astype(BF16)
    bsum = brow = None
    for part, part_t in zip(_split3(lf), _split3(lf_t)):
        t1 = jnp.dot(tril, part, preferred_element_type=F32)
        t2 = jnp.dot(part_t, triu, preferred_element_type=F32)
        bsum = t1 if bsum is None else bsum + t1
        brow = t2 if brow is None else brow + t2
    ones = jnp.ones((DV_M, c), BF16)

    for h in range(H_M):
        sl = slice(h * DK_M, (h + 1) * DK_M)
        qt, kh, vt = qt_ref[sl, tok], k_ref[tok, sl], vt_ref[sl, tok]
        b_row = brow[H_M + h:H_M + h + 1, :]
        i_row = gates_t[h:h + 1, :]
        a_col = gates[:, h:h + 1] - bsum[:, H_M + h:H_M + h + 1]
        m_prev = m_s[h, 0:1, 0:1]
        dmat = jnp.where(upper, b_row + a_col, NEG_INF)
        inter = b_row + m_prev
        mt = jnp.maximum(inter, jnp.max(dmat, axis=0, keepdims=True))
        st = jnp.dot(kh, qt, preferred_element_type=F32)
        smt = (st * jnp.exp(dmat - mt)).astype(BF16)
        iw = jnp.exp(inter - mt)
        vext = jnp.concatenate([vt, ones], axis=0)
        cxh = cx_s[h]
        ne = (jnp.dot(vext, smt, preferred_element_type=F32)
              + iw * jnp.dot(cxh.astype(BF16), qt, preferred_element_type=F32))
        num, den = ne[:DV_M], ne[DV_M:]
        hq = num / jnp.maximum(jnp.abs(den), jnp.exp(-mt))
        hn = hq * lax.rsqrt(jnp.mean(hq * hq, axis=0, keepdims=True) + RMS_EPS) * gw_ref[sl, :]
        h_ref[sl, tok] = (jax.nn.sigmoid(ot_ref[sl, tok]) * hn).astype(h_ref.dtype)

        b_last = b_row[:, c - 1:c]
        g_row = b_last - b_row + i_row
        m_new = jnp.maximum(b_last + m_prev, jnp.max(g_row, axis=1, keepdims=True))
        ws = jnp.exp(g_row - m_new)
        wc = jnp.exp(b_last + m_prev - m_new)
        vw = (vext.astype(F32) * ws).astype(BF16)
        cx_s[h] = wc * cxh + jnp.dot(vw, kh, preferred_element_type=F32)
        m_s[h] = jnp.broadcast_to(m_new, m_s.shape[1:])


def _outffn_kernel(att_ref, hmt_ref, x_ref, prev_ref, wo_ref, gpost_ref, gpre_ref, wup_ref, cw_ref, cb_ref,
                   wdn_ref, gffn_ref, y_ref, st_ref, u_ref, a_ref, *, tile, shift, hdr):
    i = pl.program_id(0)

    @pl.when(i == 0)
    def _():
        u_ref[hdr - 2 * shift:hdr, :] = prev_ref[...]

    @pl.when(i > 0)
    def _():
        u_ref[0:hdr, :] = u_ref[tile:tile + hdr, :]

    mixed = (jnp.dot(att_ref[...], wo_ref[0:D_A, :], preferred_element_type=F32)
             + lax.dot_general(hmt_ref[...], wo_ref[D_A:, :], (((0,), (0,)), ((), ())),
                               preferred_element_type=F32))
    x1 = x_ref[...] + _rms(mixed, gpost_ref[...])
    h2 = _rms(x1, gpre_ref[...]).astype(BF16)
    u_ref[hdr:hdr + tile, :] = jnp.dot(h2, wup_ref[...], preferred_element_type=F32)

    def conv(c):
        cols = slice(c * LANES, (c + 1) * LANES)
        y = cb_ref[:, cols] + u_ref[hdr - 2 * shift:hdr - 2 * shift + tile, cols] * cw_ref[0:1, cols]
        y = y + u_ref[hdr - shift:hdr - shift + tile, cols] * cw_ref[1:2, cols]
        return y + u_ref[hdr:hdr + tile, cols] * cw_ref[2:3, cols]

    nff = D_FF // LANES
    for c in range(nff):
        gate, val = conv(c), conv(nff + c)
        e = jnp.exp(gate * (GELU_C1 + GELU_C3 * (gate * gate)))
        a_ref[:, c * LANES:(c + 1) * LANES] = (gate * val / (1.0 + e)).astype(BF16)

    y2 = jnp.dot(a_ref[...], wdn_ref[...], preferred_element_type=F32)
    y_ref[...] = x1 + _rms(y2, gffn_ref[...])
    st_ref[...] = u_ref[tile:tile + hdr, :]


def _outffn(att, hmt, x, prev, wo, gpost, gpre, wup, cw, cb, wdn, gffn, *, tile, shift):
    n = x.shape[0]
    assert n % tile == 0 and (shift == 1 or n == tile)
    hdr = max(8, 2 * shift)
    assert hdr % 8 == 0 and (tile % LANES == 0 or tile == n)
    row = lambda w: pl.BlockSpec((tile, w), lambda i: (i, 0))
    return pl.pallas_call(
        functools.partial(_outffn_kernel, tile=tile, shift=shift, hdr=hdr),
        grid=(n // tile,),
        in_specs=[row(D_A), pl.BlockSpec((D_M, tile), lambda i: (0, i)), row(D_MODEL),
                  _const_spec((2 * shift, 2 * D_FF)),
                  _const_spec((D_A + D_M, D_MODEL)), _const_spec((1, D_MODEL)), _const_spec((1, D_MODEL)),
                  _const_spec((D_MODEL, 2 * D_FF)), _const_spec((CONV_W, 2 * D_FF)), _const_spec((1, 2 * D_FF)),
                  _const_spec((D_FF, D_MODEL)), _const_spec((1, D_MODEL))],
        out_specs=[row(D_MODEL), pl.BlockSpec((hdr, 2 * D_FF), lambda i: (0, 0))],
        out_shape=[jax.ShapeDtypeStruct((n, D_MODEL), F32), jax.ShapeDtypeStruct((hdr, 2 * D_FF), F32)],
        scratch_shapes=[pltpu.VMEM((hdr + tile, 2 * D_FF), F32), pltpu.VMEM((tile, D_FF), BF16)],
        compiler_params=_params(("arbitrary",)),
        name="outffn",
    )(att, hmt, x, prev, wo, gpost, gpre, wup, cw, cb, wdn, gffn)


def _t5_bucket(dist):
    max_exact = N_BUCKETS // 2
    d = jnp.maximum(dist, max_exact).astype(F32)
    large = max_exact + (jnp.log(d / max_exact) / math.log(BUCKET_MAX_DIST / max_exact)
                         * (N_BUCKETS - max_exact)).astype(jnp.int32)
    large = jnp.minimum(large, N_BUCKETS - 1)
    return jnp.where(dist < max_exact, dist, large)


def _pattern_bias(rel_bias, dil):
    j = jnp.arange(SPAN + 1)
    return rel_bias[_t5_bucket(j * dil)].T.astype(F32)


def _prompt_bias(rel_bias):
    out = []
    for _, dil in PATTERNS:
        bh = _pattern_bias(rel_bias, dil)
        v = jnp.concatenate([bh[:, ::-1], jnp.full((H_A, SPAN - 1), NEG_INF, F32)], axis=1)
        out.append(v.reshape(H_A // 2, 2, 2 * SPAN))
    return jnp.stack(out)


def _sample_bias(rel_bias, t_new):
    us, ws = [], []
    fill = lambda n: jnp.full((H_A, n), NEG_INF, F32)
    for win, dil in PATTERNS:
        bh = _pattern_bias(rel_bias, dil)
        dist = jnp.arange((SPAN + 1) * dil)
        g = jnp.where(dist % dil == 0, jnp.repeat(bh, dil, axis=1), NEG_INF)[:, :win + 1]
        us.append(jnp.concatenate([g[:, ::-1], fill(LANES - 1)], axis=1))
        ws.append(jnp.concatenate([g[:, :LANES - t_new + 1][:, ::-1], fill(t_new - 1)], axis=1))
    return us, jnp.stack(ws)


def _layer_weights(g_mix_pre, w_in, b_igate, b_fgate, g_mlstm_out, w_out, g_mix_post, g_ffn_pre, w_up,
                   conv_w, conv_b, w_down, g_ffn_post):
    wg = jnp.zeros((D_MODEL, LANES), F32).at[:, :2 * H_M].set(w_in[:, N_QKV + N_MIX:])
    bg = jnp.zeros((1, LANES), F32).at[0, :H_M].set(b_igate.astype(F32)).at[0, H_M:2 * H_M].set(b_fgate.astype(F32))
    row = lambda v: v.astype(F32).reshape(1, -1)
    mix = lambda j: w_in[:, N_QKV + j * D_M:N_QKV + (j + 1) * D_M]
    wmt = jnp.concatenate([mix(0), mix(2), mix(3)], axis=1).T.astype(BF16)
    gw = jnp.broadcast_to(g_mlstm_out.astype(F32)[:, None], (D_M, MLSTM_STEP))
    return dict(
        g_pre=row(g_mix_pre), wa=w_in[:, :N_QKV].astype(BF16), wkvt=w_in[:, D_A:N_QKV].T.astype(BF16),
        wmt=wmt, wk=mix(1).astype(BF16),
        wg=wg.astype(BF16), bg=bg, gw=gw, wo=w_out.astype(BF16), g_post=row(g_mix_post),
        g_ffn_pre=row(g_ffn_pre), wup=w_up.astype(BF16), cw=conv_w.astype(F32), cb=row(conv_b),
        wdn=w_down.astype(BF16), g_ffn_post=row(g_ffn_post))


def _state_in(c0, n0, m0):
    nb = c0.shape[0]
    return (c0.astype(F32), n0.astype(F32).reshape(nb, H_M, 1, DK_M),
            jnp.broadcast_to(m0.astype(F32)[..., None, None], (nb, H_M, 8, LANES)))


def _state_out(c, n, m):
    return c, n[:, :, 0, :], m[:, :, 0, 0]


def _layer(xp, xs, w, bias_p, bias_u, bias_w, cache_k, cache_v, c0, n0, m0, conv_prev, *, tile_in, tile_ffn):
    s_len = xp.shape[0]
    b, t_new, _ = xs.shape
    n = b * t_new
    wb = cache_k.shape[1]
    proj = (w["g_pre"], w["wa"], w["wmt"], w["wk"], w["wg"], w["bg"])

    wbp = min(max(wd for wd, _ in PATTERNS), s_len)
    dils = tuple(d for _, d in PATTERNS if d > 1)
    qkv1, qkv4, qkv16, tail_p, qmt_p, km_p, vmt_p, omt_p, gates_p, gates_tp = _inproj(
        xp, *proj, w["wkvt"], tile=tile_in, dils=dils, tail_rows=wbp)
    att_p = _attn_prompt([qkv1.reshape(1, s_len, N_QKV), qkv4, qkv16], bias_p)

    _, tail, qmt, km, vmt, omt, gates, gates_t = _inproj(xs.reshape(n, D_MODEL), *proj, tile=n, dils=(), tail_rows=n)
    c = LANES
    assert t_new <= c
    padr = lambda a: jnp.pad(a.reshape(b, t_new, -1), ((0, 0), (0, c - t_new), (0, 0))).reshape(b * c, -1)
    padc = lambda a: jnp.pad(a.reshape(-1, b, t_new), ((0, 0), (0, 0), (0, c - t_new))).reshape(-1, b * c)
    gpad = jnp.where(jnp.arange(LANES) < H_M, NEG_INF, 0.0).astype(F32)
    gates_pad = jnp.concatenate([gates.reshape(b, t_new, LANES),
                                 jnp.broadcast_to(gpad, (b, c - t_new, LANES))], axis=1).reshape(b * c, LANES)
    gates_tpad = jnp.concatenate([gates_t.reshape(GATE_ROWS, b, t_new),
                                  jnp.broadcast_to(gpad[:GATE_ROWS, None, None], (GATE_ROWS, b, c - t_new))],
                                 axis=2).reshape(GATE_ROWS, b * c)

    to_t = lambda a: jnp.transpose(a.astype(F32), (0, 2, 3, 1)).reshape(b, D_A, wb)
    from_t = lambda a: jnp.transpose(a.reshape(b, H_A, HD_A, wb), (0, 3, 1, 2))
    state0 = _state_in(jnp.zeros((1, H_M, DK_M, DV_M), F32), jnp.zeros((1, H_M, DK_M), F32),
                       jnp.zeros((1, H_M), F32))
    (hmt_p, c_p, n_p, m_p), (hmt_s, c_o, n_o, m_o), att, k_win, v_win = _mixers(
        (qmt_p, km_p, vmt_p, omt_p, gates_p, gates_tp, w["gw"]) + state0,
        (padc(qmt), padr(km), padc(vmt), padc(omt), gates_pad, gates_tpad, w["gw"][:, :c]) + _state_in(c0, n0, m0),
        tail.reshape(b, t_new, N_QKV), to_t(cache_k), to_t(cache_v), bias_u, bias_w, c=MLSTM_STEP, cs=c)

    prev0 = jnp.zeros((CONV_W - 1, 2 * D_FF), F32)
    yp, st_p = _outffn(att_p, hmt_p, xp, prev0, w["wo"], w["g_post"], w["g_ffn_pre"], w["wup"], w["cw"], w["cb"],
                       w["wdn"], w["g_ffn_post"], tile=tile_ffn, shift=1)
    window = lambda a: jnp.transpose(a.reshape(H_A, HD_A, wbp), (2, 0, 1))[None]
    k_win_p, v_win_p = window(tail_p[0:D_A]), window(tail_p[D_A:2 * D_A])
    state_p = (k_win_p, v_win_p) + _state_out(c_p, n_p, m_p) + (st_p[-(CONV_W - 1):][None],)

    tm = lambda a: a.reshape(b, t_new, -1).transpose(1, 0, 2).reshape(n, -1)
    hmt = hmt_s.reshape(D_M, b, c)[:, :, :t_new].transpose(0, 2, 1).reshape(D_M, n)
    prev = conv_prev.astype(F32).transpose(1, 0, 2).reshape((CONV_W - 1) * b, 2 * D_FF)
    ys, st = _outffn(tm(att), hmt, tm(xs), prev, w["wo"], w["g_post"], w["g_ffn_pre"], w["wup"], w["cw"],
                     w["cb"], w["wdn"], w["g_ffn_post"], tile=n, shift=b)
    ys = ys.reshape(t_new, b, D_MODEL).transpose(1, 0, 2)
    conv_state = st.reshape(CONV_W - 1, b, 2 * D_FF).transpose(1, 0, 2)
    state_s = (from_t(k_win), from_t(v_win)) + _state_out(c_o, n_o, m_o) + (conv_state,)
    return yp, ys, state_p, state_s


def kernel(x_prompt, x_sample, cache_attn_k, cache_attn_v, state_mlstm_C, state_mlstm_n, state_mlstm_m,
           state_ffn_conv, rel_bias, g_mix_pre, w_in, b_igate, b_fgate, g_mlstm_out, w_out, g_mix_post,
           g_ffn_pre, w_up, conv_w, conv_b, w_down, g_ffn_post):
    depth = w_in.shape[0]
    batch, s_len, _ = x_prompt.shape
    assert batch == 1
    t_new = x_sample.shape[1]
    bias_p = _prompt_bias(rel_bias)
    bias_u, bias_w = _sample_bias(rel_bias, t_new)
    yp = x_prompt[0]
    ys = x_sample
    new_p, new_s = [], []
    for l in range(depth):
        w = _layer_weights(g_mix_pre[l], w_in[l], b_igate[l], b_fgate[l], g_mlstm_out[l], w_out[l],
                           g_mix_post[l], g_ffn_pre[l], w_up[l], conv_w[l], conv_b[l], w_down[l], g_ffn_post[l])
        yp, ys, sp, ss = _layer(yp, ys, w, bias_p, bias_u, bias_w, cache_attn_k[l], cache_attn_v[l],
                                state_mlstm_C[l], state_mlstm_n[l], state_mlstm_m[l], state_ffn_conv[l],
                                tile_in=256, tile_ffn=512)
        new_p.append(sp)
        new_s.append(ss)
    stack = lambda states, i: jnp.stack([s[i] for s in states])
    return ((yp[None], ys) + tuple(stack(new_p, i) for i in range(6))
            + tuple(stack(new_s, i) for i in range(6)))
```

```python
import functools
import math

import jax
import jax.numpy as jnp
import numpy as np
from jax import lax
from jax.experimental import pallas as pl
from jax.experimental.pallas import tpu as pltpu

F32 = jnp.float32
BF16 = jnp.bfloat16

D_MODEL = 1024
HD_A = 64
H_A = 8
D_A = H_A * HD_A
DK_M = 128
DV_M = 128
H_M = 4
D_M = H_M * DV_M
PATTERNS = ((128, 1), (512, 4), (2048, 16))
SPAN = 128
N_BUCKETS = 32
BUCKET_MAX_DIST = 2048
MLSTM_STEP = 256
D_FF = 2816
CONV_W = 3
RMS_EPS = 1e-6
N_QKV = 3 * D_A
N_MIX = 4 * D_M
LANES = 128
GATE_ROWS = 16
SUPER = 2048
VMEM_LIMIT = 58 * 1024 * 1024
NEG_INF = float("-inf")
LOG2E = math.log2(math.e)
Q_SCALE = HD_A ** -0.5 * LOG2E
GELU_C1 = -2.0 * math.sqrt(2.0 / math.pi)
GELU_C3 = GELU_C1 * 0.044715

for _w, _d in PATTERNS:
    assert _w // _d == SPAN


def _params(sem, vmem=VMEM_LIMIT):
    return pltpu.CompilerParams(dimension_semantics=sem, vmem_limit_bytes=vmem)


def _const_spec(shape):
    nd = len(shape)
    return pl.BlockSpec(shape, lambda *_: (0,) * nd, pipeline_mode=pl.Buffered(1))


def _rms(x, g):
    return x * lax.rsqrt(jnp.mean(x * x, axis=-1, keepdims=True) + RMS_EPS) * g


def _inproj_kernel(x_ref, g_ref, wa_ref, wmt_ref, wk_ref, wg_ref, bg_ref, *refs, tile, dils, first_tail):
    nd = len(dils)
    perm_ref, wkvt_ref = (refs[0], refs[1]) if nd else (None, None)
    refs = refs[2:] if nd else refs
    qkv1_ref = refs[0]
    dil_refs = refs[1:1 + nd]
    tail_ref, qmt_ref, km_ref, vmt_ref, omt_ref, gt_ref, gtt_ref = refs[1 + nd:]

    hb = _rms(x_ref[...], g_ref[...]).astype(BF16)
    za = jnp.dot(hb, wa_ref[...], preferred_element_type=F32)
    za = jnp.concatenate([za[:, :D_A] * Q_SCALE, za[:, D_A:]], axis=1)
    zb = za.astype(BF16)
    qkv1_ref[...] = zb
    if not nd:
        tail_ref[...] = za

    if nd:
        zp = jnp.dot(perm_ref[...], zb, preferred_element_type=F32).astype(BF16)
        base = 0
        for dil, out_ref in zip(dils, dil_refs):
            for r in range(dil):
                out_ref[r] = zp[base + r * (tile // dil):base + (r + 1) * (tile // dil)]
            base += tile

    zt = lax.dot_general(wmt_ref[...], hb, (((1,), (1,)), ((), ())), preferred_element_type=F32)
    qmt_ref[...] = zt[0:D_M].astype(BF16)
    vmt_ref[...] = zt[D_M:2 * D_M].astype(BF16)
    omt_ref[...] = zt[2 * D_M:3 * D_M]
    zk = jnp.dot(hb, wk_ref[...], preferred_element_type=F32)
    km_ref[...] = (zk * (DK_M ** -0.5)).astype(BF16)

    zg = jnp.dot(hb, wg_ref[...], preferred_element_type=F32) + bg_ref[...]
    logsig = jnp.minimum(zg, 0.0) - jnp.log1p(jnp.exp(-jnp.abs(zg)))
    glane = lax.broadcasted_iota(jnp.int32, zg.shape, 1)
    gates = jnp.where(glane < H_M, zg, logsig)
    gt_ref[...] = gates
    gtt_ref[...] = gates.T[0:GATE_ROWS]

    if nd:
        @pl.when(pl.program_id(0) >= first_tail)
        def _():
            tail_ref[...] = lax.dot_general(wkvt_ref[...], hb, (((1,), (1,)), ((), ())),
                                            preferred_element_type=F32)


def _row_permutation(tile, dils):
    p = np.zeros((len(dils) * tile, tile), np.float32)
    for j, dil in enumerate(dils):
        for r in range(dil):
            for m in range(tile // dil):
                p[j * tile + r * (tile // dil) + m, m * dil + r] = 1.0
    return jnp.asarray(p, BF16)


def _inproj(x, g, wa, wmt, wk, wg, bg, wkvt=None, *, tile, dils, tail_rows):
    n = x.shape[0]
    assert n % tile == 0 and tail_rows % tile == 0 and (tile % LANES == 0 or tile == n)
    for d in dils:
        assert tile % (16 * d) == 0
    steps = n // tile
    first_tail = (n - tail_rows) // tile
    row = lambda w: pl.BlockSpec((tile, w), lambda i: (i, 0))
    col = lambda h: pl.BlockSpec((h, tile), lambda i: (0, i))
    out_shape = [jax.ShapeDtypeStruct((n, N_QKV), BF16)]
    out_specs = [row(N_QKV)]
    for d in dils:
        out_shape.append(jax.ShapeDtypeStruct((d, n // d, N_QKV), BF16))
        out_specs.append(pl.BlockSpec((d, tile // d, N_QKV), lambda i: (0, i, 0)))
    if dils:
        out_shape.append(jax.ShapeDtypeStruct((2 * D_A, tail_rows), F32))
        out_specs.append(pl.BlockSpec((2 * D_A, tile), lambda i: (0, jnp.maximum(i - first_tail, 0))))
    else:
        out_shape.append(jax.ShapeDtypeStruct((tail_rows, N_QKV), F32))
        out_specs.append(pl.BlockSpec((tile, N_QKV), lambda i: (jnp.maximum(i - first_tail, 0), 0)))
    out_shape += [jax.ShapeDtypeStruct((D_M, n), BF16), jax.ShapeDtypeStruct((n, D_M), BF16),
                  jax.ShapeDtypeStruct((D_M, n), BF16), jax.ShapeDtypeStruct((D_M, n), F32),
                  jax.ShapeDtypeStruct((n, LANES), F32), jax.ShapeDtypeStruct((GATE_ROWS, n), F32)]
    out_specs += [col(D_M), row(D_M), col(D_M), col(D_M), row(LANES), col(GATE_ROWS)]
    in_specs = [row(D_MODEL), _const_spec((1, D_MODEL)), _const_spec((D_MODEL, N_QKV)),
                _const_spec((3 * D_M, D_MODEL)), _const_spec((D_MODEL, D_M)),
                _const_spec((D_MODEL, LANES)), _const_spec((1, LANES))]
    args = [x, g, wa, wmt, wk, wg, bg]
    if dils:
        in_specs += [_const_spec((len(dils) * tile, tile)), _const_spec((2 * D_A, D_MODEL))]
        args += [_row_permutation(tile, dils), wkvt]
    return pl.pallas_call(
        functools.partial(_inproj_kernel, tile=tile, dils=dils, first_tail=first_tail),
        grid=(steps,),
        in_specs=in_specs,
        out_specs=out_specs,
        out_shape=out_shape,
        compiler_params=_params(("arbitrary",)),
        name="inproj",
    )(*args)


def _attn_prompt_kernel(*refs):
    np_ = len(PATTERNS)
    in_refs = refs[:5 * np_]
    bvec_ref, o_ref = refs[5 * np_], refs[5 * np_ + 1]
    num_s, l_s, m_s, bias_s = refs[5 * np_ + 2:]
    n = pl.program_id(1)

    lane = lax.broadcasted_iota(jnp.int32, (SPAN, LANES), 1)
    even = lane < HD_A
    ones = jnp.ones((2 * SPAN, LANES), BF16)

    @pl.when(n == 0)
    def _():
        key_col = lax.broadcasted_iota(jnp.int32, (SPAN, 2 * SPAN), 1)
        for p in range(np_):
            for e in range(2):
                vb = jnp.broadcast_to(bvec_ref[p, 0, e:e + 1, :], (SPAN, 2 * SPAN))
                table = pltpu.roll(vb, 0, 1, stride=1, stride_axis=0)
                bias_s[p, 0, e * SPAN:(e + 1) * SPAN, :] = table
                bias_s[p, 1, e * SPAN:(e + 1) * SPAN, :] = jnp.where(key_col >= SPAN, table, NEG_INF)

    first_n = (n == 0).astype(jnp.int32)

    def with_prev(p, part, r, mb):
        cur_ref, prev_ref = in_refs[5 * p + part], in_refs[5 * p + part + 2]
        if mb == 0:
            return jnp.concatenate([prev_ref[r], cur_ref[r, 0:SPAN, :]], axis=0)
        return cur_ref[r, (mb - 1) * SPAN:(mb + 1) * SPAN, :]

    def unit(p, j):
        dil = PATTERNS[p][1]
        r, mb = divmod(j, SUPER // dil // SPAN)
        row0 = mb * SPAN
        qb = in_refs[5 * p][r, row0:row0 + SPAN, :]
        kb, vb = with_prev(p, 1, r, mb), with_prev(p, 2, r, mb)
        bias = bias_s[p, first_n] if mb == 0 else bias_s[p, 0]
        zero = jnp.zeros_like(qb)
        qs = jnp.concatenate([jnp.where(even, qb, zero), jnp.where(even, zero, qb)], axis=0)
        s = lax.dot_general(qs, kb, (((1,), (1,)), ((), ())), preferred_element_type=F32) + bias
        m = jnp.max(s, axis=-1, keepdims=True)
        pr = jnp.exp2(s - m).astype(BF16)
        res = jnp.dot(pr, jnp.concatenate([vb, ones], axis=1), preferred_element_type=F32)
        mb_ = jnp.broadcast_to(m, (2 * SPAN, LANES))
        rows = pl.ds(row0, SPAN) if dil == 1 else pl.ds(row0 * dil + r, SPAN, stride=dil)
        num_s[p, rows, :] = jnp.where(even, res[:SPAN, :LANES], res[SPAN:, :LANES])
        l_s[p, rows, :] = jnp.where(even, res[:SPAN, LANES:], res[SPAN:, LANES:])
        m_s[p, rows, :] = jnp.where(even, mb_[:SPAN], mb_[SPAN:])

    def combine(jn):
        rows = slice(jn * SPAN, (jn + 1) * SPAN)
        ms = [m_s[p, rows, :] for p in range(np_)]
        m = functools.reduce(jnp.maximum, ms)
        num = jnp.zeros((SPAN, LANES), F32)
        den = jnp.zeros((SPAN, LANES), F32)
        for p in range(np_):
            a = jnp.exp2(ms[p] - m)
            num = num + a * num_s[p, rows, :]
            den = den + a * l_s[p, rows, :]
        o_ref[rows, :] = (num / den).astype(o_ref.dtype)

    done = set()
    for jn in range(SUPER // SPAN):
        for p in sorted(range(np_), key=lambda p: -PATTERNS[p][1]):
            dil = PATTERNS[p][1]
            nblk = SUPER // dil // SPAN
            for r in range(dil):
                j = r * nblk + jn // dil
                if (p, j) not in done:
                    done.add((p, j))
                    unit(p, j)
        combine(jn)


def _attn_prompt(qkvs, bvec):
    s_len = qkvs[0].shape[1]
    assert s_len % SUPER == 0
    nsb = s_len // SUPER
    npair = D_A // LANES
    in_specs, args, scratch = [], [], []
    for (_, dil), a in zip(PATTERNS, qkvs):
        rows = SUPER // dil
        nblk = rows // SPAN
        for part in range(3):
            in_specs.append(pl.BlockSpec((dil, rows, LANES),
                                         lambda hp, n, part=part: (0, n, part * npair + hp)))
            args.append(a)
        for part in (1, 2):
            in_specs.append(pl.BlockSpec((dil, SPAN, LANES),
                                         lambda hp, n, part=part, nblk=nblk:
                                         (0, jnp.maximum(n * nblk - 1, 0), part * npair + hp)))
            args.append(a)
    in_specs.append(pl.BlockSpec((len(PATTERNS), 1, 2, 2 * SPAN), lambda hp, n: (0, hp, 0, 0)))
    args.append(bvec)
    scratch += [pltpu.VMEM((len(PATTERNS), SUPER, LANES), F32)] * 3
    scratch.append(pltpu.VMEM((len(PATTERNS), 2, 2 * SPAN, 2 * SPAN), F32))
    return pl.pallas_call(
        _attn_prompt_kernel,
        grid=(npair, nsb),
        in_specs=in_specs,
        out_specs=pl.BlockSpec((SUPER, LANES), lambda hp, n: (n, hp)),
        out_shape=jax.ShapeDtypeStruct((s_len, D_A), BF16),
        scratch_shapes=scratch,
        compiler_params=_params(("arbitrary", "arbitrary")),
        name="attn_prompt",
    )(*args)


def _sample_bias_tables(u_refs, w_ref, bc_refs, bn_ref, t_new):
    new_lane = lax.broadcasted_iota(jnp.int32, (t_new, LANES), 1) >= LANES - t_new
    for p, (win, _) in enumerate(PATTERNS):
        for h in range(H_A):
            rows = slice(h * t_new, (h + 1) * t_new)
            u = jnp.broadcast_to(u_refs[p][h:h + 1, :], (t_new, win + LANES))
            bc_refs[p][rows, :] = pltpu.roll(u, 0, 1, stride=1, stride_axis=0)[:, :win]
            w = jnp.broadcast_to(w_ref[p, h:h + 1, :], (t_new, LANES))
            bn_ref[p, rows, :] = jnp.where(new_lane, pltpu.roll(w, 0, 1, stride=1, stride_axis=0), NEG_INF)


def _attn_sample_step(qkv_ref, kc_ref, vc_ref, bc_refs, bn_ref, att_ref, ko_ref, vo_ref, *, t_new, wb):
    np_ = len(PATTERNS)
    qkv = qkv_ref[0]
    qf, kn, vn = qkv[:, 0:D_A], qkv[:, D_A:2 * D_A], qkv[:, 2 * D_A:3 * D_A]
    nrow = H_A * t_new
    row_h = lax.broadcasted_iota(jnp.int32, (nrow, D_A), 0) // t_new
    lane_h = lax.broadcasted_iota(jnp.int32, (nrow, D_A), 1) // HD_A
    own = row_h == lane_h
    qs = jnp.where(own, jnp.concatenate([qf] * H_A, axis=0), 0.0).astype(BF16)
    pad = jnp.zeros((LANES - t_new, D_A), F32)
    knt = jnp.concatenate([pad, kn], axis=0).T
    vnt = jnp.concatenate([pad, vn], axis=0).T
    kct = kc_ref[0]
    vct = vc_ref[0]
    s_c = jnp.dot(qs, kct.astype(BF16), preferred_element_type=F32)
    s_n = jnp.dot(qs, knt.astype(BF16), preferred_element_type=F32)

    wins = [w for w, _ in PATTERNS]
    es_c, es_n = [], []
    m = None
    for p, win in enumerate(wins):
        e_c = s_c[:, wb - win:] + bc_refs[p][...]
        e_n = s_n + bn_ref[p]
        es_c.append(e_c)
        es_n.append(e_n)
        mp = jnp.maximum(jnp.max(e_c, axis=-1, keepdims=True), jnp.max(e_n, axis=-1, keepdims=True))
        m = mp if m is None else jnp.maximum(m, mp)
    pn = None
    for p in range(np_):
        e = jnp.exp2(es_n[p] - m)
        pn = e if pn is None else pn + e
    order = sorted(range(np_), key=lambda p: -wins[p])
    assert wins[order[0]] == wb
    pc = jnp.exp2(es_c[order[0]] - m)
    for p in order[1:]:
        e = jnp.exp2(es_c[p] - m)
        pc = jnp.concatenate([pc[:, :wb - wins[p]], pc[:, wb - wins[p]:] + e], axis=1)
    den = jnp.sum(pc, axis=-1, keepdims=True) + jnp.sum(pn, axis=-1, keepdims=True)
    nt = (((1,), (1,)), ((), ()))
    num = (lax.dot_general(pc.astype(BF16), vct.astype(BF16), nt, preferred_element_type=F32)
           + lax.dot_general(pn.astype(BF16), vnt.astype(BF16), nt, preferred_element_type=F32))
    o = jnp.where(own, num / den, 0.0)
    att = o[0:t_new]
    for h in range(1, H_A):
        att = att + o[h * t_new:(h + 1) * t_new]
    att_ref[0] = att.astype(att_ref.dtype)

    is_new = lax.broadcasted_iota(jnp.int32, (D_A, LANES), 1) >= LANES - t_new
    for src, new, dst in ((kct, knt, ko_ref), (vct, vnt, vo_ref)):
        rolled = pltpu.roll(src, wb - t_new, 1)
        dst[0, :, 0:wb - LANES] = rolled[:, 0:wb - LANES]
        dst[0, :, wb - LANES:wb] = jnp.where(is_new, new, rolled[:, wb - LANES:wb])


def _mixers_kernel(*refs, t_new, wb, c, chunks, cs):
    np_ = len(PATTERNS)
    nm = 10
    p_in, s_in = refs[0:nm], refs[nm:2 * nm]
    qkv_ref, kc_ref, vc_ref = refs[2 * nm:2 * nm + 3]
    u_refs = refs[2 * nm + 3:2 * nm + 3 + np_]
    w_ref = refs[2 * nm + 3 + np_]
    outs = refs[2 * nm + 4 + np_:]
    (ph_ref, pc_out, pn_out, pm_out), (sh_ref, sc_out, sn_out, sm_out) = outs[0:4], outs[4:8]
    att_ref, ko_ref, vo_ref = outs[8:11]
    pcx_s, pm_s, scx_s, sm_s = outs[11:15]
    bc_refs, bn_ref = outs[15:15 + np_], outs[15 + np_]
    i = pl.program_id(0)

    @pl.when(i == 0)
    def _():
        _mlstm_load_state(*p_in[7:10], pcx_s, pm_s)
        _sample_bias_tables(u_refs, w_ref, bc_refs, bn_ref, t_new)

    _attn_sample_step(qkv_ref, kc_ref, vc_ref, bc_refs, bn_ref, att_ref, ko_ref, vo_ref, t_new=t_new, wb=wb)
    _mlstm_load_state(*s_in[7:10], scx_s, sm_s)
    _mlstm_chunk(*s_in[0:7], sh_ref, scx_s, sm_s, c=cs, off=0)
    _mlstm_store_state(sc_out, sn_out, sm_out, scx_s, sm_s)
    for j in range(chunks):
        _mlstm_chunk(*p_in[0:7], ph_ref, pcx_s, pm_s, c=c, off=j * c)

    @pl.when(i == pl.num_programs(0) - 1)
    def _():
        _mlstm_store_state(pc_out, pn_out, pm_out, pcx_s, pm_s)


def _mixers(prompt_m, sample_m, qkv, cache_kt, cache_vt, bias_u, bias_w, *, c, cs):
    n = prompt_m[1].shape[0]
    b, t_new, _ = qkv.shape
    wb = cache_kt.shape[2]
    assert wb == max(w for w, _ in PATTERNS) and t_new % 8 == 0 and t_new <= LANES
    assert n % (b * c) == 0 and sample_m[1].shape[0] == b * cs
    chunks = n // (b * c)
    nrow = H_A * t_new
    blk = lambda r, w: pl.BlockSpec((1, r, w), lambda i: (i, 0, 0))
    fixed = lambda shape: pl.BlockSpec((1,) + shape, lambda i: (0,) * (1 + len(shape)))
    per_b = lambda shape: pl.BlockSpec((1,) + shape, lambda i: (i,) + (0,) * len(shape))

    def group(tokens, chunk, st):
        row = lambda w: pl.BlockSpec((tokens, w), lambda i: (i, 0))
        col = lambda h: pl.BlockSpec((h, tokens), lambda i: (0, i))
        ins = [col(D_M), row(D_M), col(D_M), col(D_M), row(LANES), col(GATE_ROWS), _const_spec((D_M, chunk)),
               st((H_M, DK_M, DV_M)), st((H_M, 1, DK_M)), st((H_M, 8, LANES))]
        outs = [col(D_M), st((H_M, DK_M, DV_M)), st((H_M, 8, DK_M)), st((H_M, 8, LANES))]
        return ins, outs

    def group_shapes(tokens_total, nb):
        return [jax.ShapeDtypeStruct((D_M, tokens_total), BF16), jax.ShapeDtypeStruct((nb, H_M, DK_M, DV_M), F32),
                jax.ShapeDtypeStruct((nb, H_M, 8, DK_M), F32), jax.ShapeDtypeStruct((nb, H_M, 8, LANES), F32)]

    p_ins, p_outs = group(chunks * c, c, fixed)
    s_ins, s_outs = group(cs, cs, per_b)
    state = [pltpu.VMEM((H_M, 2 * DV_M, DK_M), F32), pltpu.VMEM((H_M, 8, LANES), F32)]
    res = pl.pallas_call(
        functools.partial(_mixers_kernel, t_new=t_new, wb=wb, c=c, chunks=chunks, cs=cs),
        grid=(b,),
        in_specs=p_ins + s_ins + [blk(t_new, N_QKV), blk(D_A, wb), blk(D_A, wb)]
                 + [_const_spec((H_A, w + LANES)) for w, _ in PATTERNS]
                 + [_const_spec((len(PATTERNS), H_A, LANES))],
        out_specs=p_outs + s_outs + [blk(t_new, D_A), blk(D_A, wb), blk(D_A, wb)],
        out_shape=group_shapes(n, 1) + group_shapes(b * cs, b)
                  + [jax.ShapeDtypeStruct((b, t_new, D_A), BF16),
                     jax.ShapeDtypeStruct((b, D_A, wb), F32), jax.ShapeDtypeStruct((b, D_A, wb), F32)],
        scratch_shapes=state + state + [pltpu.VMEM((nrow, w), F32) for w, _ in PATTERNS]
                       + [pltpu.VMEM((len(PATTERNS), nrow, LANES), F32)],
        compiler_params=_params(("arbitrary",)),
        name="mixers",
    )(*prompt_m, *sample_m, qkv, cache_kt, cache_vt, *bias_u, bias_w)
    return res[0:4], res[4:8], res[8], res[9], res[10]


def _split3(x):
    a = x.astype(BF16)
    r = x - a.astype(F32)
    b = r.astype(BF16)
    c = (r - b.astype(F32)).astype(BF16)
    return a, b, c


def _mlstm_load_state(c0_ref, n0_ref, m0_ref, cx_s, m_s):
    for h in range(H_M):
        cx_s[h, 0:DV_M, :] = c0_ref[0, h].T
        cx_s[h, DV_M:, :] = jnp.broadcast_to(n0_ref[0, h], (DV_M, DK_M))
    m_s[...] = m0_ref[0]


def _mlstm_store_state(c_out, n_out, m_out, cx_s, m_s):
    for h in range(H_M):
        c_out[0, h] = cx_s[h, 0:DV_M, :].T
        n_out[0, h] = cx_s[h, DV_M:DV_M + 8, :]
    m_out[0] = m_s[...]


def _mlstm_chunk(qt_ref, k_ref, vt_ref, ot_ref, g_ref, gtt_ref, gw_ref, h_ref, cx_s, m_s, *, c, off):
    tok = slice(off, off + c)
    gates = g_ref[tok, :]
    gates_t = gtt_ref[:, tok]
    lane = lax.broadcasted_iota(jnp.int32, gates.shape, 1)
    lf = jnp.where(jnp.logical_and(lane >= H_M, lane < 2 * H_M), gates, 0.0)
    grow = lax.broadcasted_iota(jnp.int32, gates_t.shape, 0)
    lf_t = jnp.where(jnp.logical_and(grow >= H_M, grow < 2 * H_M), gates_t, 0.0)
    ri = lax.broadcasted_iota(jnp.int32, (c, c), 0)
    cj = lax.broadcasted_iota(jnp.int32, (c, c), 1)
    upper = ri <= cj
    tril = (ri >= cj).astype(BF16)
    triu = upper.astype(BF16)
    bsum = brow = None
    for part, part_t in zip(_split3(lf), _split3(lf_t)):
        t1 = jnp.dot(tril, part, preferred_element_type=F32)
        t2 = jnp.dot(part_t, triu, preferred_element_type=F32)
        bsum = t1 if bsum is None else bsum + t1
        brow = t2 if brow is None else brow + t2
    ones = jnp.ones((DV_M, c), BF16)

    for h in range(H_M):
        sl = slice(h * DK_M, (h + 1) * DK_M)
        qt, kh, vt = qt_ref[sl, tok], k_ref[tok, sl], vt_ref[sl, tok]
        b_row = brow[H_M + h:H_M + h + 1, :]
        i_row = gates_t[h:h + 1, :]
        a_col = gates[:, h:h + 1] - bsum[:, H_M + h:H_M + h + 1]
        m_prev = m_s[h, 0:1, 0:1]
        dmat = jnp.where(upper, b_row + a_col, NEG_INF)
        inter = b_row + m_prev
        mt = jnp.maximum(inter, jnp.max(dmat, axis=0, keepdims=True))
        st = jnp.dot(kh, qt, preferred_element_type=F32)
        smt = (st * jnp.exp(dmat - mt)).astype(BF16)
        iw = jnp.exp(inter - mt)
        vext = jnp.concatenate([vt, ones], axis=0)
        cxh = cx_s[h]
        ne = (jnp.dot(vext, smt, preferred_element_type=F32)
              + iw * jnp.dot(cxh.astype(BF16), qt, preferred_element_type=F32))
        num, den = ne[:DV_M], ne[DV_M:]
        hq = num / jnp.maximum(jnp.abs(den), jnp.exp(-mt))
        hn = hq * lax.rsqrt(jnp.mean(hq * hq, axis=0, keepdims=True) + RMS_EPS) * gw_ref[sl, :]
        h_ref[sl, tok] = (jax.nn.sigmoid(ot_ref[sl, tok]) * hn).astype(h_ref.dtype)

        b_last = b_row[:, c - 1:c]
        g_row = b_last - b_row + i_row
        m_new = jnp.maximum(b_last + m_prev, jnp.max(g_row, axis=1, keepdims=True))
        ws = jnp.exp(g_row - m_new)
        wc = jnp.exp(b_last + m_prev - m_new)
        vw = (vext.astype(F32) * ws).astype(BF16)
        cx_s[h] = wc * cxh + jnp.dot(vw, kh, preferred_element_type=F32)
        m_s[h] = jnp.broadcast_to(m_new, m_s.shape[1:])


def _outffn_kernel(att_ref, hmt_ref, x_ref, prev_ref, wo_ref, gpost_ref, gpre_ref, wup_ref, cw_ref, cb_ref,
                   wdn_ref, gffn_ref, y_ref, st_ref, u_ref, a_ref, *, tile, shift, hdr):
    i = pl.program_id(0)

    @pl.when(i == 0)
    def _():
        u_ref[hdr - 2 * shift:hdr, :] = prev_ref[...]

    @pl.when(i > 0)
    def _():
        u_ref[0:hdr, :] = u_ref[tile:tile + hdr, :]

    mixed = (jnp.dot(att_ref[...], wo_ref[0:D_A, :], preferred_element_type=F32)
             + lax.dot_general(hmt_ref[...], wo_ref[D_A:, :], (((0,), (0,)), ((), ())),
                               preferred_element_type=F32))
    x1 = x_ref[...] + _rms(mixed, gpost_ref[...])
    h2 = _rms(x1, gpre_ref[...]).astype(BF16)
    u_ref[hdr:hdr + tile, :] = jnp.dot(h2, wup_ref[...], preferred_element_type=F32)

    def conv(c):
        cols = slice(c * LANES, (c + 1) * LANES)
        y = cb_ref[:, cols] + u_ref[hdr - 2 * shift:hdr - 2 * shift + tile, cols] * cw_ref[0:1, cols]
        y = y + u_ref[hdr - shift:hdr - shift + tile, cols] * cw_ref[1:2, cols]
        return y + u_ref[hdr:hdr + tile, cols] * cw_ref[2:3, cols]

    nff = D_FF // LANES
    for c in range(nff):
        gate, val = conv(c), conv(nff + c)
        e = jnp.exp(gate * (GELU_C1 + GELU_C3 * (gate * gate)))
        a_ref[:, c * LANES:(c + 1) * LANES] = (gate * val / (1.0 + e)).astype(BF16)

    y2 = jnp.dot(a_ref[...], wdn_ref[...], preferred_element_type=F32)
    y_ref[...] = x1 + _rms(y2, gffn_ref[...])
    st_ref[...] = u_ref[tile:tile + hdr, :]


def _outffn(att, hmt, x, prev, wo, gpost, gpre, wup, cw, cb, wdn, gffn, *, tile, shift):
    n = x.shape[0]
    assert n % tile == 0 and (shift == 1 or n == tile)
    hdr = max(8, 2 * shift)
    assert hdr % 8 == 0 and (tile % LANES == 0 or tile == n)
    row = lambda w: pl.BlockSpec((tile, w), lambda i: (i, 0))
    return pl.pallas_call(
        functools.partial(_outffn_kernel, tile=tile, shift=shift, hdr=hdr),
        grid=(n // tile,),
        in_specs=[row(D_A), pl.BlockSpec((D_M, tile), lambda i: (0, i)), row(D_MODEL),
                  _const_spec((2 * shift, 2 * D_FF)),
                  _const_spec((D_A + D_M, D_MODEL)), _const_spec((1, D_MODEL)), _const_spec((1, D_MODEL)),
                  _const_spec((D_MODEL, 2 * D_FF)), _const_spec((CONV_W, 2 * D_FF)), _const_spec((1, 2 * D_FF)),
                  _const_spec((D_FF, D_MODEL)), _const_spec((1, D_MODEL))],
        out_specs=[row(D_MODEL), pl.BlockSpec((hdr, 2 * D_FF), lambda i: (0, 0))],
        out_shape=[jax.ShapeDtypeStruct((n, D_MODEL), F32), jax.ShapeDtypeStruct((hdr, 2 * D_FF), F32)],
        scratch_shapes=[pltpu.VMEM((hdr + tile, 2 * D_FF), F32), pltpu.VMEM((tile, D_FF), BF16)],
        compiler_params=_params(("arbitrary",)),
        name="outffn",
    )(att, hmt, x, prev, wo, gpost, gpre, wup, cw, cb, wdn, gffn)


def _t5_bucket(dist):
    max_exact = N_BUCKETS // 2
    d = jnp.maximum(dist, max_exact).astype(F32)
    large = max_exact + (jnp.log(d / max_exact) / math.log(BUCKET_MAX_DIST / max_exact)
                         * (N_BUCKETS - max_exact)).astype(jnp.int32)
    large = jnp.minimum(large, N_BUCKETS - 1)
    return jnp.where(dist < max_exact, dist, large)


def _pattern_bias(rel_bias, dil):
    j = jnp.arange(SPAN + 1)
    return rel_bias[_t5_bucket(j * dil)].T.astype(F32) * LOG2E


def _prompt_bias(rel_bias):
    out = []
    for _, dil in PATTERNS:
        bh = _pattern_bias(rel_bias, dil)
        v = jnp.concatenate([bh[:, ::-1], jnp.full((H_A, SPAN - 1), NEG_INF, F32)], axis=1)
        out.append(v.reshape(H_A // 2, 2, 2 * SPAN))
    return jnp.stack(out)


def _sample_bias(rel_bias, t_new):
    us, ws = [], []
    fill = lambda n: jnp.full((H_A, n), NEG_INF, F32)
    for win, dil in PATTERNS:
        bh = _pattern_bias(rel_bias, dil)
        dist = jnp.arange((SPAN + 1) * dil)
        g = jnp.where(dist % dil == 0, jnp.repeat(bh, dil, axis=1), NEG_INF)[:, :win + 1]
        us.append(jnp.concatenate([g[:, ::-1], fill(LANES - 1)], axis=1))
        ws.append(jnp.concatenate([g[:, :LANES - t_new + 1][:, ::-1], fill(t_new - 1)], axis=1))
    return us, jnp.stack(ws)


def _layer_weights(g_mix_pre, w_in, b_igate, b_fgate, g_mlstm_out, w_out, g_mix_post, g_ffn_pre, w_up,
                   conv_w, conv_b, w_down, g_ffn_post):
    wg = jnp.zeros((D_MODEL, LANES), F32).at[:, :2 * H_M].set(w_in[:, N_QKV + N_MIX:])
    bg = jnp.zeros((1, LANES), F32).at[0, :H_M].set(b_igate.astype(F32)).at[0, H_M:2 * H_M].set(b_fgate.astype(F32))
    row = lambda v: v.astype(F32).reshape(1, -1)
    mix = lambda j: w_in[:, N_QKV + j * D_M:N_QKV + (j + 1) * D_M]
    wmt = jnp.concatenate([mix(0), mix(2), mix(3)], axis=1).T.astype(BF16)
    gw = jnp.broadcast_to(g_mlstm_out.astype(F32)[:, None], (D_M, MLSTM_STEP))
    return dict(
        g_pre=row(g_mix_pre), wa=w_in[:, :N_QKV].astype(BF16), wkvt=w_in[:, D_A:N_QKV].T.astype(BF16),
        wmt=wmt, wk=mix(1).astype(BF16),
        wg=wg.astype(BF16), bg=bg, gw=gw, wo=w_out.astype(BF16), g_post=row(g_mix_post),
        g_ffn_pre=row(g_ffn_pre), wup=w_up.astype(BF16), cw=conv_w.astype(F32), cb=row(conv_b),
        wdn=w_down.astype(BF16), g_ffn_post=row(g_ffn_post))


def _state_in(c0, n0, m0):
    nb = c0.shape[0]
    return (c0.astype(F32), n0.astype(F32).reshape(nb, H_M, 1, DK_M),
            jnp.broadcast_to(m0.astype(F32)[..., None, None], (nb, H_M, 8, LANES)))


def _state_out(c, n, m):
    return c, n[:, :, 0, :], m[:, :, 0, 0]


def _layer(xp, xs, w, bias_p, bias_u, bias_w, cache_k, cache_v, c0, n0, m0, conv_prev, *, tile_in, tile_ffn):
    s_len = xp.shape[0]
    b, t_new, _ = xs.shape
    n = b * t_new
    wb = cache_k.shape[1]
    proj = (w["g_pre"], w["wa"], w["wmt"], w["wk"], w["wg"], w["bg"])

    wbp = min(max(wd for wd, _ in PATTERNS), s_len)
    dils = tuple(d for _, d in PATTERNS if d > 1)
    qkv1, qkv4, qkv16, tail_p, qmt_p, km_p, vmt_p, omt_p, gates_p, gates_tp = _inproj(
        xp, *proj, w["wkvt"], tile=tile_in, dils=dils, tail_rows=wbp)
    att_p = _attn_prompt([qkv1.reshape(1, s_len, N_QKV), qkv4, qkv16], bias_p)

    _, tail, qmt, km, vmt, omt, gates, gates_t = _inproj(xs.reshape(n, D_MODEL), *proj, tile=n, dils=(), tail_rows=n)
    c = LANES
    assert t_new <= c
    padr = lambda a: jnp.pad(a.reshape(b, t_new, -1), ((0, 0), (0, c - t_new), (0, 0))).reshape(b * c, -1)
    padc = lambda a: jnp.pad(a.reshape(-1, b, t_new), ((0, 0), (0, 0), (0, c - t_new))).reshape(-1, b * c)
    gpad = jnp.where(jnp.arange(LANES) < H_M, NEG_INF, 0.0).astype(F32)
    gates_pad = jnp.concatenate([gates.reshape(b, t_new, LANES),
                                 jnp.broadcast_to(gpad, (b, c - t_new, LANES))], axis=1).reshape(b * c, LANES)
    gates_tpad = jnp.concatenate([gates_t.reshape(GATE_ROWS, b, t_new),
                                  jnp.broadcast_to(gpad[:GATE_ROWS, None, None], (GATE_ROWS, b, c - t_new))],
                                 axis=2).reshape(GATE_ROWS, b * c)

    to_t = lambda a: jnp.transpose(a.astype(F32), (0, 2, 3, 1)).reshape(b, D_A, wb)
    from_t = lambda a: jnp.transpose(a.reshape(b, H_A, HD_A, wb), (0, 3, 1, 2))
    state0 = _state_in(jnp.zeros((1, H_M, DK_M, DV_M), F32), jnp.zeros((1, H_M, DK_M), F32),
                       jnp.zeros((1, H_M), F32))
    (hmt_p, c_p, n_p, m_p), (hmt_s, c_o, n_o, m_o), att, k_win, v_win = _mixers(
        (qmt_p, km_p, vmt_p, omt_p, gates_p, gates_tp, w["gw"]) + state0,
        (padc(qmt), padr(km), padc(vmt), padc(omt), gates_pad, gates_tpad, w["gw"][:, :c]) + _state_in(c0, n0, m0),
        tail.reshape(b, t_new, N_QKV), to_t(cache_k), to_t(cache_v), bias_u, bias_w, c=MLSTM_STEP, cs=c)

    prev0 = jnp.zeros((CONV_W - 1, 2 * D_FF), F32)
    yp, st_p = _outffn(att_p, hmt_p, xp, prev0, w["wo"], w["g_post"], w["g_ffn_pre"], w["wup"], w["cw"], w["cb"],
                       w["wdn"], w["g_ffn_post"], tile=tile_ffn, shift=1)
    window = lambda a: jnp.transpose(a.reshape(H_A, HD_A, wbp), (2, 0, 1))[None]
    k_win_p, v_win_p = window(tail_p[0:D_A]), window(tail_p[D_A:2 * D_A])
    state_p = (k_win_p, v_win_p) + _state_out(c_p, n_p, m_p) + (st_p[-(CONV_W - 1):][None],)

    tm = lambda a: a.reshape(b, t_new, -1).transpose(1, 0, 2).reshape(n, -1)
    hmt = hmt_s.reshape(D_M, b, c)[:, :, :t_new].transpose(0, 2, 1).reshape(D_M, n)
    prev = conv_prev.astype(F32).transpose(1, 0, 2).reshape((CONV_W - 1) * b, 2 * D_FF)
    ys, st = _outffn(tm(att), hmt, tm(xs), prev, w["wo"], w["g_post"], w["g_ffn_pre"], w["wup"], w["cw"],
                     w["cb"], w["wdn"], w["g_ffn_post"], tile=n, shift=b)
    ys = ys.reshape(t_new, b, D_MODEL).transpose(1, 0, 2)
    conv_state = st.reshape(CONV_W - 1, b, 2 * D_FF).transpose(1, 0, 2)
    state_s = (from_t(k_win), from_t(v_win)) + _state_out(c_o, n_o, m_o) + (conv_state,)
    return yp, ys, state_p, state_s


def kernel(x_prompt, x_sample, cache_attn_k, cache_attn_v, state_mlstm_C, state_mlstm_n, state_mlstm_m,
           state_ffn_conv, rel_bias, g_mix_pre, w_in, b_igate, b_fgate, g_mlstm_out, w_out, g_mix_post,
           g_ffn_pre, w_up, conv_w, conv_b, w_down, g_ffn_post):
    depth = w_in.shape[0]
    batch, s_len, _ = x_prompt.shape
    assert batch == 1
    t_new = x_sample.shape[1]
    bias_p = _prompt_bias(rel_bias)
    bias_u, bias_w = _sample_bias(rel_bias, t_new)
    yp = x_prompt[0]
    ys = x_sample
    new_p, new_s = [], []
    for l in range(depth):
        w = _layer_weights(g_mix_pre[l], w_in[l], b_igate[l], b_fgate[l], g_mlstm_out[l], w_out[l],
                           g_mix_post[l], g_ffn_pre[l], w_up[l], conv_w[l], conv_b[l], w_down[l], g_ffn_post[l])
        yp, ys, sp, ss = _layer(yp, ys, w, bias_p, bias_u, bias_w, cache_attn_k[l], cache_attn_v[l],
                                state_mlstm_C[l], state_mlstm_n[l], state_mlstm_m[l], state_ffn_conv[l],
                                tile_in=256, tile_ffn=512)
        new_p.append(sp)
        new_s.append(ss)
    stack = lambda states, i: jnp.stack([s[i] for s in states])
    return ((yp[None], ys) + tuple(stack(new_p, i) for i in range(6))
            + tuple(stack(new_s, i) for i in range(6)))
```

```python
import functools
import math

import jax
import jax.numpy as jnp
import numpy as np
from jax import lax
from jax.experimental import pallas as pl
from jax.experimental.pallas import tpu as pltpu

F32 = jnp.float32
BF16 = jnp.bfloat16

D_MODEL = 1024
HD_A = 64
H_A = 8
D_A = H_A * HD_A
DK_M = 128
DV_M = 128
H_M = 4
D_M = H_M * DV_M
PATTERNS = ((128, 1), (512, 4), (2048, 16))
SPAN = 128
N_BUCKETS = 32
BUCKET_MAX_DIST = 2048
MLSTM_STEP = 256
D_FF = 2816
CONV_W = 3
RMS_EPS = 1e-6
N_QKV = 3 * D_A
N_MIX = 4 * D_M
LANES = 128
GATE_ROWS = 16
SUPER = 2048
VMEM_LIMIT = 58 * 1024 * 1024
NEG_INF = float("-inf")
LOG2E = math.log2(math.e)
Q_SCALE = HD_A ** -0.5 * LOG2E
GELU_C1 = -2.0 * math.sqrt(2.0 / math.pi)
GELU_C3 = GELU_C1 * 0.044715

for _w, _d in PATTERNS:
    assert _w // _d == SPAN


def _params(sem, vmem=VMEM_LIMIT):
    return pltpu.CompilerParams(dimension_semantics=sem, vmem_limit_bytes=vmem)


def _const_spec(shape):
    nd = len(shape)
    return pl.BlockSpec(shape, lambda *_: (0,) * nd, pipeline_mode=pl.Buffered(1))


def _rms(x, g):
    return x * lax.rsqrt(jnp.mean(x * x, axis=-1, keepdims=True) + RMS_EPS) * g


def _inproj_kernel(x_ref, g_ref, wa_ref, wmt_ref, wk_ref, wg_ref, bg_ref, *refs, tile, dils, first_tail, spread):
    nd = len(dils)
    extra_a, extra_b = refs[0], refs[1]
    qkv1_ref = refs[2]
    dil_refs = refs[3:3 + nd]
    tail_ref, qmt_ref, km_ref, vmt_ref, omt_ref, gt_ref, gtt_ref = refs[3 + nd:]

    hb = _rms(x_ref[...], g_ref[...]).astype(BF16)
    za = jnp.dot(hb, wa_ref[...], preferred_element_type=F32)
    za = jnp.concatenate([za[:, :D_A] * Q_SCALE, za[:, D_A:]], axis=1)
    zb = za.astype(BF16)
    qkv1_ref[...] = zb
    if not nd:
        tail_ref[...] = za

    if nd:
        zp = jnp.dot(extra_a[...], zb, preferred_element_type=F32).astype(BF16)
        base = 0
        for dil, out_ref in zip(dils, dil_refs):
            for r in range(dil):
                out_ref[r] = zp[base + r * (tile // dil):base + (r + 1) * (tile // dil)]
            base += tile

    zt = lax.dot_general(wmt_ref[...], hb, (((1,), (1,)), ((), ())), preferred_element_type=F32)
    qmt, vmt, omt = zt[0:D_M].astype(BF16), zt[D_M:2 * D_M].astype(BF16), zt[2 * D_M:3 * D_M]
    km = (jnp.dot(hb, wk_ref[...], preferred_element_type=F32) * (DK_M ** -0.5)).astype(BF16)

    zg = jnp.dot(hb, wg_ref[...], preferred_element_type=F32) + bg_ref[...]
    logsig = jnp.minimum(zg, 0.0) - jnp.log1p(jnp.exp(-jnp.abs(zg)))
    glane = lax.broadcasted_iota(jnp.int32, zg.shape, 1)
    gates = jnp.where(glane < H_M, zg, logsig)
    gates_t = gates.T[0:GATE_ROWS]

    if spread is None:
        qmt_ref[...], vmt_ref[...], omt_ref[...], km_ref[...] = qmt, vmt, omt, km
        gt_ref[...], gtt_ref[...] = gates, gates_t
    else:
        chunk, valid = spread
        to_cols = lambda a: jnp.dot(a, extra_a[...], preferred_element_type=F32)
        to_rows = lambda a: jnp.dot(extra_b[...], a, preferred_element_type=F32)
        exact = lambda move, a: functools.reduce(jnp.add, [move(part) for part in _split3(a)])
        qmt_ref[...] = to_cols(qmt).astype(BF16)
        vmt_ref[...] = to_cols(vmt).astype(BF16)
        omt_ref[...] = exact(to_cols, omt)
        km_ref[...] = to_rows(km).astype(BF16)
        tok = lax.broadcasted_iota(jnp.int32, gt_ref.shape, 0) % chunk
        ig_lane = lax.broadcasted_iota(jnp.int32, gt_ref.shape, 1) < H_M
        gt_ref[...] = jnp.where(jnp.logical_and(tok >= valid, ig_lane), NEG_INF, exact(to_rows, gates))
        tok_t = lax.broadcasted_iota(jnp.int32, gtt_ref.shape, 1) % chunk
        ig_row = lax.broadcasted_iota(jnp.int32, gtt_ref.shape, 0) < H_M
        gtt_ref[...] = jnp.where(jnp.logical_and(tok_t >= valid, ig_row), NEG_INF, exact(to_cols, gates_t))

    if nd:
        @pl.when(pl.program_id(0) >= first_tail)
        def _():
            tail_ref[...] = lax.dot_general(extra_b[...], hb, (((1,), (1,)), ((), ())),
                                            preferred_element_type=F32)


def _spread_matrix(n, chunk, valid):
    p = np.zeros((n, n // valid * chunk), np.float32)
    tok = np.arange(n)
    p[tok, tok // valid * chunk + tok % valid] = 1.0
    return jnp.asarray(p, BF16)


def _row_permutation(tile, dils):
    p = np.zeros((len(dils) * tile, tile), np.float32)
    for j, dil in enumerate(dils):
        for r in range(dil):
            for m in range(tile // dil):
                p[j * tile + r * (tile // dil) + m, m * dil + r] = 1.0
    return jnp.asarray(p, BF16)


def _inproj(x, g, wa, wmt, wk, wg, bg, wkvt=None, *, tile, dils=(), tail_rows, spread=None):
    n = x.shape[0]
    assert bool(dils) != (spread is not None)
    assert n % tile == 0 and tail_rows % tile == 0 and (tile % LANES == 0 or tile == n)
    for d in dils:
        assert tile % (16 * d) == 0
    steps = n // tile
    first_tail = (n - tail_rows) // tile
    m, mt = n, tile
    if spread:
        chunk, valid = spread
        assert steps == 1 and n % valid == 0 and valid <= chunk
        m = mt = n // valid * chunk
    row = lambda w: pl.BlockSpec((tile, w), lambda i: (i, 0))
    col = lambda h: pl.BlockSpec((h, tile), lambda i: (0, i))
    out_shape = [jax.ShapeDtypeStruct((n, N_QKV), BF16)]
    out_specs = [row(N_QKV)]
    for d in dils:
        out_shape.append(jax.ShapeDtypeStruct((d, n // d, N_QKV), BF16))
        out_specs.append(pl.BlockSpec((d, tile // d, N_QKV), lambda i: (0, i, 0)))
    if dils:
        out_shape.append(jax.ShapeDtypeStruct((2 * D_A, tail_rows), F32))
        out_specs.append(pl.BlockSpec((2 * D_A, tile), lambda i: (0, jnp.maximum(i - first_tail, 0))))
    else:
        out_shape.append(jax.ShapeDtypeStruct((tail_rows, N_QKV), F32))
        out_specs.append(pl.BlockSpec((tile, N_QKV), lambda i: (jnp.maximum(i - first_tail, 0), 0)))
    mrow = lambda w: pl.BlockSpec((mt, w), lambda i: (i, 0))
    mcol = lambda h: pl.BlockSpec((h, mt), lambda i: (0, i))
    out_shape += [jax.ShapeDtypeStruct((D_M, m), BF16), jax.ShapeDtypeStruct((m, D_M), BF16),
                  jax.ShapeDtypeStruct((D_M, m), BF16), jax.ShapeDtypeStruct((D_M, m), F32),
                  jax.ShapeDtypeStruct((m, LANES), F32), jax.ShapeDtypeStruct((GATE_ROWS, m), F32)]
    out_specs += [mcol(D_M), mrow(D_M), mcol(D_M), mcol(D_M), mrow(LANES), mcol(GATE_ROWS)]
    in_specs = [row(D_MODEL), _const_spec((1, D_MODEL)), _const_spec((D_MODEL, N_QKV)),
                _const_spec((3 * D_M, D_MODEL)), _const_spec((D_MODEL, D_M)),
                _const_spec((D_MODEL, LANES)), _const_spec((1, LANES))]
    args = [x, g, wa, wmt, wk, wg, bg]
    if dils:
        in_specs += [_const_spec((len(dils) * tile, tile)), _const_spec((2 * D_A, D_MODEL))]
        args += [_row_permutation(tile, dils), wkvt]
    else:
        sp = _spread_matrix(n, chunk, valid)
        in_specs += [_const_spec((n, m)), _const_spec((m, n))]
        args += [sp, sp.T]
    return pl.pallas_call(
        functools.partial(_inproj_kernel, tile=tile, dils=dils, first_tail=first_tail, spread=spread),
        grid=(steps,),
        in_specs=in_specs,
        out_specs=out_specs,
        out_shape=out_shape,
        compiler_params=_params(("arbitrary",)),
        name="inproj",
    )(*args)


def _attn_prompt_kernel(*refs):
    np_ = len(PATTERNS)
    in_refs = refs[:5 * np_]
    bvec_ref, o_ref = refs[5 * np_], refs[5 * np_ + 1]
    num_s, l_s, m_s, bias_s = refs[5 * np_ + 2:]
    n = pl.program_id(1)

    lane = lax.broadcasted_iota(jnp.int32, (SPAN, LANES), 1)
    even = lane < HD_A
    ones = jnp.ones((2 * SPAN, LANES), BF16)

    @pl.when(n == 0)
    def _():
        key_col = lax.broadcasted_iota(jnp.int32, (SPAN, 2 * SPAN), 1)
        for p in range(np_):
            for e in range(2):
                vb = jnp.broadcast_to(bvec_ref[p, 0, e:e + 1, :], (SPAN, 2 * SPAN))
                table = pltpu.roll(vb, 0, 1, stride=1, stride_axis=0)
                bias_s[p, 0, e * SPAN:(e + 1) * SPAN, :] = table
                bias_s[p, 1, e * SPAN:(e + 1) * SPAN, :] = jnp.where(key_col >= SPAN, table, NEG_INF)

    first_n = (n == 0).astype(jnp.int32)

    def with_prev(p, part, r, mb):
        cur_ref, prev_ref = in_refs[5 * p + part], in_refs[5 * p + part + 2]
        if mb == 0:
            return jnp.concatenate([prev_ref[r], cur_ref[r, 0:SPAN, :]], axis=0)
        return cur_ref[r, (mb - 1) * SPAN:(mb + 1) * SPAN, :]

    def unit(p, j):
        dil = PATTERNS[p][1]
        r, mb = divmod(j, SUPER // dil // SPAN)
        row0 = mb * SPAN
        qb = in_refs[5 * p][r, row0:row0 + SPAN, :]
        kb, vb = with_prev(p, 1, r, mb), with_prev(p, 2, r, mb)
        bias = bias_s[p, first_n] if mb == 0 else bias_s[p, 0]
        zero = jnp.zeros_like(qb)
        qs = jnp.concatenate([jnp.where(even, qb, zero), jnp.where(even, zero, qb)], axis=0)
        s = lax.dot_general(qs, kb, (((1,), (1,)), ((), ())), preferred_element_type=F32) + bias
        m = jnp.max(s, axis=-1, keepdims=True)
        pr = jnp.exp2(s - m).astype(BF16)
        res = jnp.dot(pr, jnp.concatenate([vb, ones], axis=1), preferred_element_type=F32)
        mb_ = jnp.broadcast_to(m, (2 * SPAN, LANES))
        rows = pl.ds(row0, SPAN) if dil == 1 else pl.ds(row0 * dil + r, SPAN, stride=dil)
        num_s[p, rows, :] = jnp.where(even, res[:SPAN, :LANES], res[SPAN:, :LANES])
        l_s[p, rows, :] = jnp.where(even, res[:SPAN, LANES:], res[SPAN:, LANES:])
        m_s[p, rows, :] = jnp.where(even, mb_[:SPAN], mb_[SPAN:])

    def combine(jn):
        rows = slice(jn * SPAN, (jn + 1) * SPAN)
        ms = [m_s[p, rows, :] for p in range(np_)]
        m = functools.reduce(jnp.maximum, ms)
        num = jnp.zeros((SPAN, LANES), F32)
        den = jnp.zeros((SPAN, LANES), F32)
        for p in range(np_):
            a = jnp.exp2(ms[p] - m)
            num = num + a * num_s[p, rows, :]
            den = den + a * l_s[p, rows, :]
        o_ref[rows, :] = (num / den).astype(o_ref.dtype)

    done = set()
    for jn in range(SUPER // SPAN):
        for p in sorted(range(np_), key=lambda p: -PATTERNS[p][1]):
            dil = PATTERNS[p][1]
            nblk = SUPER // dil // SPAN
            for r in range(dil):
                j = r * nblk + jn // dil
                if (p, j) not in done:
                    done.add((p, j))
                    unit(p, j)
        combine(jn)


def _attn_prompt(qkvs, bvec):
    s_len = qkvs[0].shape[1]
    assert s_len % SUPER == 0
    nsb = s_len // SUPER
    npair = D_A // LANES
    in_specs, args, scratch = [], [], []
    for (_, dil), a in zip(PATTERNS, qkvs):
        rows = SUPER // dil
        nblk = rows // SPAN
        for part in range(3):
            in_specs.append(pl.BlockSpec((dil, rows, LANES),
                                         lambda hp, n, part=part: (0, n, part * npair + hp)))
            args.append(a)
        for part in (1, 2):
            in_specs.append(pl.BlockSpec((dil, SPAN, LANES),
                                         lambda hp, n, part=part, nblk=nblk:
                                         (0, jnp.maximum(n * nblk - 1, 0), part * npair + hp)))
            args.append(a)
    in_specs.append(pl.BlockSpec((len(PATTERNS), 1, 2, 2 * SPAN), lambda hp, n: (0, hp, 0, 0)))
    args.append(bvec)
    scratch += [pltpu.VMEM((len(PATTERNS), SUPER, LANES), F32)] * 3
    scratch.append(pltpu.VMEM((len(PATTERNS), 2, 2 * SPAN, 2 * SPAN), F32))
    return pl.pallas_call(
        _attn_prompt_kernel,
        grid=(npair, nsb),
        in_specs=in_specs,
        out_specs=pl.BlockSpec((SUPER, LANES), lambda hp, n: (n, hp)),
        out_shape=jax.ShapeDtypeStruct((s_len, D_A), BF16),
        scratch_shapes=scratch,
        compiler_params=_params(("arbitrary", "arbitrary")),
        name="attn_prompt",
    )(*args)


def _sample_bias_tables(u_refs, w_ref, bc_refs, bn_ref, t_new):
    new_lane = lax.broadcasted_iota(jnp.int32, (t_new, LANES), 1) >= LANES - t_new
    for p, (win, _) in enumerate(PATTERNS):
        for h in range(H_A):
            rows = slice(h * t_new, (h + 1) * t_new)
            u = jnp.broadcast_to(u_refs[p][h:h + 1, :], (t_new, win + LANES))
            bc_refs[p][rows, :] = pltpu.roll(u, 0, 1, stride=1, stride_axis=0)[:, :win]
            w = jnp.broadcast_to(w_ref[p, h:h + 1, :], (t_new, LANES))
            bn_ref[p, rows, :] = jnp.where(new_lane, pltpu.roll(w, 0, 1, stride=1, stride_axis=0), NEG_INF)


def _attn_sample_step(qkv_ref, kc_ref, vc_ref, bc_refs, bn_ref, att_ref, ko_ref, vo_ref, *, t_new, wb):
    np_ = len(PATTERNS)
    qkv = qkv_ref[0]
    qf, kn, vn = qkv[:, 0:D_A], qkv[:, D_A:2 * D_A], qkv[:, 2 * D_A:3 * D_A]
    nrow = H_A * t_new
    row_h = lax.broadcasted_iota(jnp.int32, (nrow, D_A), 0) // t_new
    lane_h = lax.broadcasted_iota(jnp.int32, (nrow, D_A), 1) // HD_A
    own = row_h == lane_h
    qs = jnp.where(own, jnp.concatenate([qf] * H_A, axis=0), 0.0).astype(BF16)
    pad = jnp.zeros((LANES - t_new, D_A), F32)
    knt = jnp.concatenate([pad, kn], axis=0).T
    vnt = jnp.concatenate([pad, vn], axis=0).T
    kct = kc_ref[0]
    vct = vc_ref[0]
    s_c = jnp.dot(qs, kct.astype(BF16), preferred_element_type=F32)
    s_n = jnp.dot(qs, knt.astype(BF16), preferred_element_type=F32)

    wins = [w for w, _ in PATTERNS]
    es_c, es_n = [], []
    m = None
    for p, win in enumerate(wins):
        e_c = s_c[:, wb - win:] + bc_refs[p][...]
        e_n = s_n + bn_ref[p]
        es_c.append(e_c)
        es_n.append(e_n)
        mp = jnp.maximum(jnp.max(e_c, axis=-1, keepdims=True), jnp.max(e_n, axis=-1, keepdims=True))
        m = mp if m is None else jnp.maximum(m, mp)
    pn = None
    for p in range(np_):
        e = jnp.exp2(es_n[p] - m)
        pn = e if pn is None else pn + e
    order = sorted(range(np_), key=lambda p: -wins[p])
    assert wins[order[0]] == wb
    pc = jnp.exp2(es_c[order[0]] - m)
    for p in order[1:]:
        e = jnp.exp2(es_c[p] - m)
        pc = jnp.concatenate([pc[:, :wb - wins[p]], pc[:, wb - wins[p]:] + e], axis=1)
    den = jnp.sum(pc, axis=-1, keepdims=True) + jnp.sum(pn, axis=-1, keepdims=True)
    nt = (((1,), (1,)), ((), ()))
    num = (lax.dot_general(pc.astype(BF16), vct.astype(BF16), nt, preferred_element_type=F32)
           + lax.dot_general(pn.astype(BF16), vnt.astype(BF16), nt, preferred_element_type=F32))
    o = jnp.where(own, num / den, 0.0)
    att = o[0:t_new]
    for h in range(1, H_A):
        att = att + o[h * t_new:(h + 1) * t_new]
    att_ref[0] = att.astype(att_ref.dtype)

    is_new = lax.broadcasted_iota(jnp.int32, (D_A, LANES), 1) >= LANES - t_new
    for src, new, dst in ((kct, knt, ko_ref), (vct, vnt, vo_ref)):
        rolled = pltpu.roll(src, wb - t_new, 1)
        dst[0, :, 0:wb - LANES] = rolled[:, 0:wb - LANES]
        dst[0, :, wb - LANES:wb] = jnp.where(is_new, new, rolled[:, wb - LANES:wb])


def _mixers_kernel(*refs, t_new, wb, c, chunks, cs):
    np_ = len(PATTERNS)
    nm = 10
    p_in, s_in = refs[0:nm], refs[nm:2 * nm]
    qkv_ref, kc_ref, vc_ref = refs[2 * nm:2 * nm + 3]
    u_refs = refs[2 * nm + 3:2 * nm + 3 + np_]
    w_ref = refs[2 * nm + 3 + np_]
    outs = refs[2 * nm + 4 + np_:]
    (ph_ref, pc_out, pn_out, pm_out), (sh_ref, sc_out, sn_out, sm_out) = outs[0:4], outs[4:8]
    att_ref, ko_ref, vo_ref = outs[8:11]
    pcx_s, pm_s, scx_s, sm_s = outs[11:15]
    bc_refs, bn_ref = outs[15:15 + np_], outs[15 + np_]
    i = pl.program_id(0)

    @pl.when(i == 0)
    def _():
        _mlstm_load_state(*p_in[7:10], pcx_s, pm_s)
        _sample_bias_tables(u_refs, w_ref, bc_refs, bn_ref, t_new)

    _attn_sample_step(qkv_ref, kc_ref, vc_ref, bc_refs, bn_ref, att_ref, ko_ref, vo_ref, t_new=t_new, wb=wb)
    _mlstm_load_state(*s_in[7:10], scx_s, sm_s)
    _mlstm_chunk(*s_in[0:7], sh_ref, scx_s, sm_s, c=cs, off=0)
    _mlstm_store_state(sc_out, sn_out, sm_out, scx_s, sm_s)
    for j in range(chunks):
        _mlstm_chunk(*p_in[0:7], ph_ref, pcx_s, pm_s, c=c, off=j * c)

    @pl.when(i == pl.num_programs(0) - 1)
    def _():
        _mlstm_store_state(pc_out, pn_out, pm_out, pcx_s, pm_s)


def _mixers(prompt_m, sample_m, qkv, cache_kt, cache_vt, bias_u, bias_w, *, c, cs):
    n = prompt_m[1].shape[0]
    b, t_new, _ = qkv.shape
    wb = cache_kt.shape[2]
    assert wb == max(w for w, _ in PATTERNS) and t_new % 8 == 0 and t_new <= LANES
    assert n % (b * c) == 0 and sample_m[1].shape[0] == b * cs
    chunks = n // (b * c)
    nrow = H_A * t_new
    blk = lambda r, w: pl.BlockSpec((1, r, w), lambda i: (i, 0, 0))
    fixed = lambda shape: pl.BlockSpec((1,) + shape, lambda i: (0,) * (1 + len(shape)))
    per_b = lambda shape: pl.BlockSpec((1,) + shape, lambda i: (i,) + (0,) * len(shape))

    def group(tokens, chunk, st):
        row = lambda w: pl.BlockSpec((tokens, w), lambda i: (i, 0))
        col = lambda h: pl.BlockSpec((h, tokens), lambda i: (0, i))
        ins = [col(D_M), row(D_M), col(D_M), col(D_M), row(LANES), col(GATE_ROWS), _const_spec((D_M, chunk)),
               st((H_M, DK_M, DV_M)), st((H_M, 1, DK_M)), st((H_M, 8, LANES))]
        outs = [col(D_M), st((H_M, DK_M, DV_M)), st((H_M, 8, DK_M)), st((H_M, 8, LANES))]
        return ins, outs

    def group_shapes(tokens_total, nb):
        return [jax.ShapeDtypeStruct((D_M, tokens_total), BF16), jax.ShapeDtypeStruct((nb, H_M, DK_M, DV_M), F32),
                jax.ShapeDtypeStruct((nb, H_M, 8, DK_M), F32), jax.ShapeDtypeStruct((nb, H_M, 8, LANES), F32)]

    p_ins, p_outs = group(chunks * c, c, fixed)
    s_ins, s_outs = group(cs, cs, per_b)
    state = [pltpu.VMEM((H_M, 2 * DV_M, DK_M), F32), pltpu.VMEM((H_M, 8, LANES), F32)]
    res = pl.pallas_call(
        functools.partial(_mixers_kernel, t_new=t_new, wb=wb, c=c, chunks=chunks, cs=cs),
        grid=(b,),
        in_specs=p_ins + s_ins + [blk(t_new, N_QKV), blk(D_A, wb), blk(D_A, wb)]
                 + [_const_spec((H_A, w + LANES)) for w, _ in PATTERNS]
                 + [_const_spec((len(PATTERNS), H_A, LANES))],
        out_specs=p_outs + s_outs + [blk(t_new, D_A), blk(D_A, wb), blk(D_A, wb)],
        out_shape=group_shapes(n, 1) + group_shapes(b * cs, b)
                  + [jax.ShapeDtypeStruct((b, t_new, D_A), BF16),
                     jax.ShapeDtypeStruct((b, D_A, wb), F32), jax.ShapeDtypeStruct((b, D_A, wb), F32)],
        scratch_shapes=state + state + [pltpu.VMEM((nrow, w), F32) for w, _ in PATTERNS]
                       + [pltpu.VMEM((len(PATTERNS), nrow, LANES), F32)],
        compiler_params=_params(("arbitrary",)),
        name="mixers",
    )(*prompt_m, *sample_m, qkv, cache_kt, cache_vt, *bias_u, bias_w)
    return res[0:4], res[4:8], res[8], res[9], res[10]


def _split3(x):
    a = x.astype(BF16)
    r = x - a.astype(F32)
    b = r.astype(BF16)
    c = (r - b.astype(F32)).astype(BF16)
    return a, b, c


def _mlstm_load_state(c0_ref, n0_ref, m0_ref, cx_s, m_s):
    for h in range(H_M):
        cx_s[h, 0:DV_M, :] = c0_ref[0, h].T
        cx_s[h, DV_M:, :] = jnp.broadcast_to(n0_ref[0, h], (DV_M, DK_M))
    m_s[...] = m0_ref[0]


def _mlstm_store_state(c_out, n_out, m_out, cx_s, m_s):
    for h in range(H_M):
        c_out[0, h] = cx_s[h, 0:DV_M, :].T
        n_out[0, h] = cx_s[h, DV_M:DV_M + 8, :]
    m_out[0] = m_s[...]


def _mlstm_chunk(qt_ref, k_ref, vt_ref, ot_ref, g_ref, gtt_ref, gw_ref, h_ref, cx_s, m_s, *, c, off):
    tok = slice(off, off + c)
    gates = g_ref[tok, :]
    gates_t = gtt_ref[:, tok]
    lane = lax.broadcasted_iota(jnp.int32, gates.shape, 1)
    lf = jnp.where(jnp.logical_and(lane >= H_M, lane < 2 * H_M), gates, 0.0)
    grow = lax.broadcasted_iota(jnp.int32, gates_t.shape, 0)
    lf_t = jnp.where(jnp.logical_and(grow >= H_M, grow < 2 * H_M), gates_t, 0.0)
    ri = lax.broadcasted_iota(jnp.int32, (c, c), 0)
    cj = lax.broadcasted_iota(jnp.int32, (c, c), 1)
    upper = ri <= cj
    tril = (ri >= cj).astype(BF16)
    triu = upper.astype(BF16)
    bsum = brow = None
    for part, part_t in zip(_split3(lf), _split3(lf_t)):
        t1 = jnp.dot(tril, part, preferred_element_type=F32)
        t2 = jnp.dot(part_t, triu, preferred_element_type=F32)
        bsum = t1 if bsum is None else bsum + t1
        brow = t2 if brow is None else brow + t2
    ones = jnp.ones((DV_M, c), BF16)

    for h in range(H_M):
        sl = slice(h * DK_M, (h + 1) * DK_M)
        qt, kh, vt = qt_ref[sl, tok], k_ref[tok, sl], vt_ref[sl, tok]
        b_row = brow[H_M + h:H_M + h + 1, :]
        i_row = gates_t[h:h + 1, :]
        a_col = gates[:, h:h + 1] - bsum[:, H_M + h:H_M + h + 1]
        m_prev = m_s[h, 0:1, 0:1]
        dmat = jnp.where(upper, b_row + a_col, NEG_INF)
        inter = b_row + m_prev
        mt = jnp.maximum(inter, jnp.max(dmat, axis=0, keepdims=True))
        st = jnp.dot(kh, qt, preferred_element_type=F32)
        smt = (st * jnp.exp(dmat - mt)).astype(BF16)
        iw = jnp.exp(inter - mt)
        vext = jnp.concatenate([vt, ones], axis=0)
        cxh = cx_s[h]
        ne = (jnp.dot(vext, smt, preferred_element_type=F32)
              + iw * jnp.dot(cxh.astype(BF16), qt, preferred_element_type=F32))
        num, den = ne[:DV_M], ne[DV_M:]
        hq = num / jnp.maximum(jnp.abs(den), jnp.exp(-mt))
        hn = hq * lax.rsqrt(jnp.mean(hq * hq, axis=0, keepdims=True) + RMS_EPS) * gw_ref[sl, :]
        h_ref[sl, tok] = (jax.nn.sigmoid(ot_ref[sl, tok]) * hn).astype(h_ref.dtype)

        b_last = b_row[:, c - 1:c]
        g_row = b_last - b_row + i_row
        m_new = jnp.maximum(b_last + m_prev, jnp.max(g_row, axis=1, keepdims=True))
        ws = jnp.exp(g_row - m_new)
        wc = jnp.exp(b_last + m_prev - m_new)
        vw = (vext.astype(F32) * ws).astype(BF16)
        cx_s[h] = wc * cxh + jnp.dot(vw, kh, preferred_element_type=F32)
        m_s[h] = jnp.broadcast_to(m_new, m_s.shape[1:])


def _outffn_kernel(att_ref, hmt_ref, x_ref, prev_ref, wo_ref, gpost_ref, gpre_ref, wup_ref, cw_ref, cb_ref,
                   wdn_ref, gffn_ref, y_ref, st_ref, u_ref, a_ref, *, tile, shift, hdr):
    i = pl.program_id(0)

    @pl.when(i == 0)
    def _():
        u_ref[hdr - 2 * shift:hdr, :] = prev_ref[...]

    @pl.when(i > 0)
    def _():
        u_ref[0:hdr, :] = u_ref[tile:tile + hdr, :]

    mixed = (jnp.dot(att_ref[...], wo_ref[0:D_A, :], preferred_element_type=F32)
             + lax.dot_general(hmt_ref[...], wo_ref[D_A:, :], (((0,), (0,)), ((), ())),
                               preferred_element_type=F32))
    x1 = x_ref[...] + _rms(mixed, gpost_ref[...])
    h2 = _rms(x1, gpre_ref[...]).astype(BF16)
    u_ref[hdr:hdr + tile, :] = jnp.dot(h2, wup_ref[...], preferred_element_type=F32)

    def conv(c):
        cols = slice(c * LANES, (c + 1) * LANES)
        y = cb_ref[:, cols] + u_ref[hdr - 2 * shift:hdr - 2 * shift + tile, cols] * cw_ref[0:1, cols]
        y = y + u_ref[hdr - shift:hdr - shift + tile, cols] * cw_ref[1:2, cols]
        return y + u_ref[hdr:hdr + tile, cols] * cw_ref[2:3, cols]

    nff = D_FF // LANES
    for c in range(nff):
        gate, val = conv(c), conv(nff + c)
        e = jnp.exp(gate * (GELU_C1 + GELU_C3 * (gate * gate)))
        a_ref[:, c * LANES:(c + 1) * LANES] = (gate * val / (1.0 + e)).astype(BF16)

    y2 = jnp.dot(a_ref[...], wdn_ref[...], preferred_element_type=F32)
    y_ref[...] = x1 + _rms(y2, gffn_ref[...])
    st_ref[...] = u_ref[tile:tile + hdr, :]


def _outffn(att, hmt, x, prev, wo, gpost, gpre, wup, cw, cb, wdn, gffn, *, tile, shift):
    n = x.shape[0]
    assert n % tile == 0 and (shift == 1 or n == tile)
    hdr = max(8, 2 * shift)
    assert hdr % 8 == 0 and (tile % LANES == 0 or tile == n)
    row = lambda w: pl.BlockSpec((tile, w), lambda i: (i, 0))
    return pl.pallas_call(
        functools.partial(_outffn_kernel, tile=tile, shift=shift, hdr=hdr),
        grid=(n // tile,),
        in_specs=[row(D_A), pl.BlockSpec((D_M, tile), lambda i: (0, i)), row(D_MODEL),
                  _const_spec((2 * shift, 2 * D_FF)),
                  _const_spec((D_A + D_M, D_MODEL)), _const_spec((1, D_MODEL)), _const_spec((1, D_MODEL)),
                  _const_spec((D_MODEL, 2 * D_FF)), _const_spec((CONV_W, 2 * D_FF)), _const_spec((1, 2 * D_FF)),
                  _const_spec((D_FF, D_MODEL)), _const_spec((1, D_MODEL))],
        out_specs=[row(D_MODEL), pl.BlockSpec((hdr, 2 * D_FF), lambda i: (0, 0))],
        out_shape=[jax.ShapeDtypeStruct((n, D_MODEL), F32), jax.ShapeDtypeStruct((hdr, 2 * D_FF), F32)],
        scratch_shapes=[pltpu.VMEM((hdr + tile, 2 * D_FF), F32), pltpu.VMEM((tile, D_FF), BF16)],
        compiler_params=_params(("arbitrary",)),
        name="outffn",
    )(att, hmt, x, prev, wo, gpost, gpre, wup, cw, cb, wdn, gffn)


def _t5_bucket(dist):
    max_exact = N_BUCKETS // 2
    d = jnp.maximum(dist, max_exact).astype(F32)
    large = max_exact + (jnp.log(d / max_exact) / math.log(BUCKET_MAX_DIST / max_exact)
                         * (N_BUCKETS - max_exact)).astype(jnp.int32)
    large = jnp.minimum(large, N_BUCKETS - 1)
    return jnp.where(dist < max_exact, dist, large)


def _pattern_bias(rel_bias, dil):
    j = jnp.arange(SPAN + 1)
    return rel_bias[_t5_bucket(j * dil)].T.astype(F32) * LOG2E


def _prompt_bias(rel_bias):
    out = []
    for _, dil in PATTERNS:
        bh = _pattern_bias(rel_bias, dil)
        v = jnp.concatenate([bh[:, ::-1], jnp.full((H_A, SPAN - 1), NEG_INF, F32)], axis=1)
        out.append(v.reshape(H_A // 2, 2, 2 * SPAN))
    return jnp.stack(out)


def _sample_bias(rel_bias, t_new):
    us, ws = [], []
    fill = lambda n: jnp.full((H_A, n), NEG_INF, F32)
    for win, dil in PATTERNS:
        bh = _pattern_bias(rel_bias, dil)
        dist = jnp.arange((SPAN + 1) * dil)
        g = jnp.where(dist % dil == 0, jnp.repeat(bh, dil, axis=1), NEG_INF)[:, :win + 1]
        us.append(jnp.concatenate([g[:, ::-1], fill(LANES - 1)], axis=1))
        ws.append(jnp.concatenate([g[:, :LANES - t_new + 1][:, ::-1], fill(t_new - 1)], axis=1))
    return us, jnp.stack(ws)


def _layer_weights(g_mix_pre, w_in, b_igate, b_fgate, g_mlstm_out, w_out, g_mix_post, g_ffn_pre, w_up,
                   conv_w, conv_b, w_down, g_ffn_post):
    wg = jnp.zeros((D_MODEL, LANES), F32).at[:, :2 * H_M].set(w_in[:, N_QKV + N_MIX:])
    bg = jnp.zeros((1, LANES), F32).at[0, :H_M].set(b_igate.astype(F32)).at[0, H_M:2 * H_M].set(b_fgate.astype(F32))
    row = lambda v: v.astype(F32).reshape(1, -1)
    mix = lambda j: w_in[:, N_QKV + j * D_M:N_QKV + (j + 1) * D_M]
    wmt = jnp.concatenate([mix(0), mix(2), mix(3)], axis=1).T.astype(BF16)
    gw = jnp.broadcast_to(g_mlstm_out.astype(F32)[:, None], (D_M, MLSTM_STEP))
    return dict(
        g_pre=row(g_mix_pre), wa=w_in[:, :N_QKV].astype(BF16), wkvt=w_in[:, D_A:N_QKV].T.astype(BF16),
        wmt=wmt, wk=mix(1).astype(BF16),
        wg=wg.astype(BF16), bg=bg, gw=gw, wo=w_out.astype(BF16), g_post=row(g_mix_post),
        g_ffn_pre=row(g_ffn_pre), wup=w_up.astype(BF16), cw=conv_w.astype(F32), cb=row(conv_b),
        wdn=w_down.astype(BF16), g_ffn_post=row(g_ffn_post))


def _state_in(c0, n0, m0):
    nb = c0.shape[0]
    return (c0.astype(F32), n0.astype(F32).reshape(nb, H_M, 1, DK_M),
            jnp.broadcast_to(m0.astype(F32)[..., None, None], (nb, H_M, 8, LANES)))


def _state_out(c, n, m):
    return c, n[:, :, 0, :], m[:, :, 0, 0]


def _layer(xp, xs, w, bias_p, bias_u, bias_w, cache_k, cache_v, c0, n0, m0, conv_prev, *, tile_in, tile_ffn):
    s_len = xp.shape[0]
    b, t_new, _ = xs.shape
    n = b * t_new
    wb = cache_k.shape[1]
    proj = (w["g_pre"], w["wa"], w["wmt"], w["wk"], w["wg"], w["bg"])

    wbp = min(max(wd for wd, _ in PATTERNS), s_len)
    dils = tuple(d for _, d in PATTERNS if d > 1)
    qkv1, qkv4, qkv16, tail_p, qmt_p, km_p, vmt_p, omt_p, gates_p, gates_tp = _inproj(
        xp, *proj, w["wkvt"], tile=tile_in, dils=dils, tail_rows=wbp)
    att_p = _attn_prompt([qkv1.reshape(1, s_len, N_QKV), qkv4, qkv16], bias_p)

    c = LANES
    _, tail, *sample_m = _inproj(xs.reshape(n, D_MODEL), *proj, tile=n, tail_rows=n, spread=(c, t_new))

    to_t = lambda a: jnp.transpose(a.astype(F32), (0, 2, 3, 1)).reshape(b, D_A, wb)
    from_t = lambda a: jnp.transpose(a.reshape(b, H_A, HD_A, wb), (0, 3, 1, 2))
    state0 = _state_in(jnp.zeros((1, H_M, DK_M, DV_M), F32), jnp.zeros((1, H_M, DK_M), F32),
                       jnp.zeros((1, H_M), F32))
    (hmt_p, c_p, n_p, m_p), (hmt_s, c_o, n_o, m_o), att, k_win, v_win = _mixers(
        (qmt_p, km_p, vmt_p, omt_p, gates_p, gates_tp, w["gw"]) + state0,
        tuple(sample_m) + (w["gw"][:, :c],) + _state_in(c0, n0, m0),
        tail.reshape(b, t_new, N_QKV), to_t(cache_k), to_t(cache_v), bias_u, bias_w, c=MLSTM_STEP, cs=c)

    prev0 = jnp.zeros((CONV_W - 1, 2 * D_FF), F32)
    yp, st_p = _outffn(att_p, hmt_p, xp, prev0, w["wo"], w["g_post"], w["g_ffn_pre"], w["wup"], w["cw"], w["cb"],
                       w["wdn"], w["g_ffn_post"], tile=tile_ffn, shift=1)
    window = lambda a: jnp.transpose(a.reshape(H_A, HD_A, wbp), (2, 0, 1))[None]
    k_win_p, v_win_p = window(tail_p[0:D_A]), window(tail_p[D_A:2 * D_A])
    state_p = (k_win_p, v_win_p) + _state_out(c_p, n_p, m_p) + (st_p[-(CONV_W - 1):][None],)

    tm = lambda a: a.reshape(b, t_new, -1).transpose(1, 0, 2).reshape(n, -1)
    hmt = hmt_s.reshape(D_M, b, c)[:, :, :t_new].transpose(0, 2, 1).reshape(D_M, n)
    prev = conv_prev.astype(F32).transpose(1, 0, 2).reshape((CONV_W - 1) * b, 2 * D_FF)
    ys, st = _outffn(tm(att), hmt, tm(xs), prev, w["wo"], w["g_post"], w["g_ffn_pre"], w["wup"], w["cw"],
                     w["cb"], w["wdn"], w["g_ffn_post"], tile=n, shift=b)
    ys = ys.reshape(t_new, b, D_MODEL).transpose(1, 0, 2)
    conv_state = st.reshape(CONV_W - 1, b, 2 * D_FF).transpose(1, 0, 2)
    state_s = (from_t(k_win), from_t(v_win)) + _state_out(c_o, n_o, m_o) + (conv_state,)
    return yp, ys, state_p, state_s


def kernel(x_prompt, x_sample, cache_attn_k, cache_attn_v, state_mlstm_C, state_mlstm_n, state_mlstm_m,
           state_ffn_conv, rel_bias, g_mix_pre, w_in, b_igate, b_fgate, g_mlstm_out, w_out, g_mix_post,
           g_ffn_pre, w_up, conv_w, conv_b, w_down, g_ffn_post):
    depth = w_in.shape[0]
    batch, s_len, _ = x_prompt.shape
    assert batch == 1
    t_new = x_sample.shape[1]
    bias_p = _prompt_bias(rel_bias)
    bias_u, bias_w = _sample_bias(rel_bias, t_new)
    yp = x_prompt[0]
    ys = x_sample
    new_p, new_s = [], []
    for l in range(depth):
        w = _layer_weights(g_mix_pre[l], w_in[l], b_igate[l], b_fgate[l], g_mlstm_out[l], w_out[l],
                           g_mix_post[l], g_ffn_pre[l], w_up[l], conv_w[l], conv_b[l], w_down[l], g_ffn_post[l])
        yp, ys, sp, ss = _layer(yp, ys, w, bias_p, bias_u, bias_w, cache_attn_k[l], cache_attn_v[l],
                                state_mlstm_C[l], state_mlstm_n[l], state_mlstm_m[l], state_ffn_conv[l],
                                tile_in=256, tile_ffn=512)
        new_p.append(sp)
        new_s.append(ss)
    stack = lambda states, i: jnp.stack([s[i] for s in states])
    return ((yp[None], ys) + tuple(stack(new_p, i) for i in range(6))
            + tuple(stack(new_s, i) for i in range(6)))
```

```python
import functools
import math

import jax
import jax.numpy as jnp
import numpy as np
from jax import lax
from jax.experimental import pallas as pl
from jax.experimental.pallas import tpu as pltpu

F32 = jnp.float32
BF16 = jnp.bfloat16

D_MODEL = 1024
HD_A = 64
H_A = 8
D_A = H_A * HD_A
DK_M = 128
DV_M = 128
H_M = 4
D_M = H_M * DV_M
PATTERNS = ((128, 1), (512, 4), (2048, 16))
SPAN = 128
N_BUCKETS = 32
BUCKET_MAX_DIST = 2048
MLSTM_STEP = 256
D_FF = 2816
CONV_W = 3
RMS_EPS = 1e-6
N_QKV = 3 * D_A
N_MIX = 4 * D_M
LANES = 128
GATE_ROWS = 16
SUPER = 2048
VMEM_LIMIT = 58 * 1024 * 1024
NEG_INF = float("-inf")
LOG2E = math.log2(math.e)
Q_SCALE = HD_A ** -0.5 * LOG2E
GELU_C1 = -2.0 * math.sqrt(2.0 / math.pi)
GELU_C3 = GELU_C1 * 0.044715

for _w, _d in PATTERNS:
    assert _w // _d == SPAN


def _params(sem, vmem=VMEM_LIMIT):
    return pltpu.CompilerParams(dimension_semantics=sem, vmem_limit_bytes=vmem)


def _const_spec(shape):
    nd = len(shape)
    return pl.BlockSpec(shape, lambda *_: (0,) * nd, pipeline_mode=pl.Buffered(1))


def _rms(x, g):
    return x * lax.rsqrt(jnp.mean(x * x, axis=-1, keepdims=True) + RMS_EPS) * g


def _inproj_kernel(x_ref, g_ref, wa_ref, wmt_ref, wk_ref, wg_ref, bg_ref, *refs, tile, dils, first_tail, spread):
    nd = len(dils)
    extra_a, extra_b = refs[0], refs[1]
    qkv1_ref = refs[2]
    dil_refs = refs[3:3 + nd]
    ntail = 2 if nd else 1
    tail_refs = refs[3 + nd:3 + nd + ntail]
    qmt_ref, km_ref, vmt_ref, omt_ref, gt_ref, gtt_ref = refs[3 + nd + ntail:]

    hb = _rms(x_ref[...], g_ref[...]).astype(BF16)
    za = jnp.dot(hb, wa_ref[...], preferred_element_type=F32)
    za = jnp.concatenate([za[:, :D_A] * Q_SCALE, za[:, D_A:]], axis=1)
    zb = za.astype(BF16)
    qkv1_ref[...] = zb
    if not nd:
        tail_refs[0][...] = za

    if nd:
        zp = jnp.dot(extra_a[...], zb, preferred_element_type=F32).astype(BF16)
        base = 0
        for dil, out_ref in zip(dils, dil_refs):
            for r in range(dil):
                out_ref[r] = zp[base + r * (tile // dil):base + (r + 1) * (tile // dil)]
            base += tile

    zt = lax.dot_general(wmt_ref[...], hb, (((1,), (1,)), ((), ())), preferred_element_type=F32)
    qmt, vmt, omt = zt[0:D_M].astype(BF16), zt[D_M:2 * D_M].astype(BF16), zt[2 * D_M:3 * D_M]
    km = (jnp.dot(hb, wk_ref[...], preferred_element_type=F32) * (DK_M ** -0.5)).astype(BF16)

    zg = jnp.dot(hb, wg_ref[...], preferred_element_type=F32) + bg_ref[...]
    logsig = jnp.minimum(zg, 0.0) - jnp.log1p(jnp.exp(-jnp.abs(zg)))
    glane = lax.broadcasted_iota(jnp.int32, zg.shape, 1)
    gates = jnp.where(glane < H_M, zg, logsig)
    gates_t = gates.T[0:GATE_ROWS]

    if spread is None:
        qmt_ref[...], vmt_ref[...], omt_ref[...], km_ref[...] = qmt, vmt, omt, km
        gt_ref[...], gtt_ref[...] = gates, gates_t
    else:
        chunk, valid = spread
        to_cols = lambda a: jnp.dot(a, extra_a[...], preferred_element_type=F32)
        to_rows = lambda a: jnp.dot(extra_b[...], a, preferred_element_type=F32)
        exact = lambda move, a: functools.reduce(jnp.add, [move(part) for part in _split3(a)])
        qmt_ref[...] = to_cols(qmt).astype(BF16)
        vmt_ref[...] = to_cols(vmt).astype(BF16)
        omt_ref[...] = exact(to_cols, omt)
        km_ref[...] = to_rows(km).astype(BF16)
        tok = lax.broadcasted_iota(jnp.int32, gt_ref.shape, 0) % chunk
        ig_lane = lax.broadcasted_iota(jnp.int32, gt_ref.shape, 1) < H_M
        gt_ref[...] = jnp.where(jnp.logical_and(tok >= valid, ig_lane), NEG_INF, exact(to_rows, gates))
        tok_t = lax.broadcasted_iota(jnp.int32, gtt_ref.shape, 1) % chunk
        ig_row = lax.broadcasted_iota(jnp.int32, gtt_ref.shape, 0) < H_M
        gtt_ref[...] = jnp.where(jnp.logical_and(tok_t >= valid, ig_row), NEG_INF, exact(to_cols, gates_t))

    if nd:
        @pl.when(pl.program_id(0) >= first_tail)
        def _():
            kvt = lax.dot_general(extra_b[...], hb, (((1,), (1,)), ((), ())), preferred_element_type=F32)
            tail_refs[0][...] = kvt[0:D_A]
            tail_refs[1][...] = kvt[D_A:2 * D_A]


def _spread_matrix(n, chunk, valid):
    p = np.zeros((n, n // valid * chunk), np.float32)
    tok = np.arange(n)
    p[tok, tok // valid * chunk + tok % valid] = 1.0
    return jnp.asarray(p, BF16)


def _row_permutation(tile, dils):
    p = np.zeros((len(dils) * tile, tile), np.float32)
    for j, dil in enumerate(dils):
        for r in range(dil):
            for m in range(tile // dil):
                p[j * tile + r * (tile // dil) + m, m * dil + r] = 1.0
    return jnp.asarray(p, BF16)


def _inproj(x, g, wa, wmt, wk, wg, bg, wkvt=None, *, tile, dils=(), tail_rows, spread=None):
    n = x.shape[0]
    assert bool(dils) != (spread is not None)
    assert n % tile == 0 and tail_rows % tile == 0 and (tile % LANES == 0 or tile == n)
    for d in dils:
        assert tile % (16 * d) == 0
    steps = n // tile
    first_tail = (n - tail_rows) // tile
    m, mt = n, tile
    if spread:
        chunk, valid = spread
        assert steps == 1 and n % valid == 0 and valid <= chunk
        m = mt = n // valid * chunk
    row = lambda w: pl.BlockSpec((tile, w), lambda i: (i, 0))
    col = lambda h: pl.BlockSpec((h, tile), lambda i: (0, i))
    out_shape = [jax.ShapeDtypeStruct((n, N_QKV), BF16)]
    out_specs = [row(N_QKV)]
    for d in dils:
        out_shape.append(jax.ShapeDtypeStruct((d, n // d, N_QKV), BF16))
        out_specs.append(pl.BlockSpec((d, tile // d, N_QKV), lambda i: (0, i, 0)))
    if dils:
        out_shape += [jax.ShapeDtypeStruct((D_A, tail_rows), F32)] * 2
        out_specs += [pl.BlockSpec((D_A, tile), lambda i: (0, jnp.maximum(i - first_tail, 0)))] * 2
    else:
        out_shape.append(jax.ShapeDtypeStruct((tail_rows, N_QKV), F32))
        out_specs.append(pl.BlockSpec((tile, N_QKV), lambda i: (jnp.maximum(i - first_tail, 0), 0)))
    mrow = lambda w: pl.BlockSpec((mt, w), lambda i: (i, 0))
    mcol = lambda h: pl.BlockSpec((h, mt), lambda i: (0, i))
    out_shape += [jax.ShapeDtypeStruct((D_M, m), BF16), jax.ShapeDtypeStruct((m, D_M), BF16),
                  jax.ShapeDtypeStruct((D_M, m), BF16), jax.ShapeDtypeStruct((D_M, m), F32),
                  jax.ShapeDtypeStruct((m, LANES), F32), jax.ShapeDtypeStruct((GATE_ROWS, m), F32)]
    out_specs += [mcol(D_M), mrow(D_M), mcol(D_M), mcol(D_M), mrow(LANES), mcol(GATE_ROWS)]
    in_specs = [row(D_MODEL), _const_spec((1, D_MODEL)), _const_spec((D_MODEL, N_QKV)),
                _const_spec((3 * D_M, D_MODEL)), _const_spec((D_MODEL, D_M)),
                _const_spec((D_MODEL, LANES)), _const_spec((1, LANES))]
    args = [x, g, wa, wmt, wk, wg, bg]
    if dils:
        in_specs += [_const_spec((len(dils) * tile, tile)), _const_spec((2 * D_A, D_MODEL))]
        args += [_row_permutation(tile, dils), wkvt]
    else:
        sp = _spread_matrix(n, chunk, valid)
        in_specs += [_const_spec((n, m)), _const_spec((m, n))]
        args += [sp, sp.T]
    return pl.pallas_call(
        functools.partial(_inproj_kernel, tile=tile, dils=dils, first_tail=first_tail, spread=spread),
        grid=(steps,),
        in_specs=in_specs,
        out_specs=out_specs,
        out_shape=out_shape,
        compiler_params=_params(("arbitrary",)),
        name="inproj",
    )(*args)


def _attn_prompt_kernel(*refs):
    np_ = len(PATTERNS)
    in_refs = refs[:5 * np_]
    bvec_ref, o_ref = refs[5 * np_], refs[5 * np_ + 1]
    num_s, l_s, m_s, bias_s = refs[5 * np_ + 2:]
    n = pl.program_id(1)

    lane = lax.broadcasted_iota(jnp.int32, (SPAN, LANES), 1)
    even = lane < HD_A
    ones = jnp.ones((2 * SPAN, LANES), BF16)

    @pl.when(n == 0)
    def _():
        key_col = lax.broadcasted_iota(jnp.int32, (SPAN, 2 * SPAN), 1)
        for p in range(np_):
            for e in range(2):
                vb = jnp.broadcast_to(bvec_ref[p, 0, e:e + 1, :], (SPAN, 2 * SPAN))
                table = pltpu.roll(vb, 0, 1, stride=1, stride_axis=0)
                bias_s[p, 0, e * SPAN:(e + 1) * SPAN, :] = table
                bias_s[p, 1, e * SPAN:(e + 1) * SPAN, :] = jnp.where(key_col >= SPAN, table, NEG_INF)

    first_n = (n == 0).astype(jnp.int32)

    def with_prev(p, part, r, mb):
        cur_ref, prev_ref = in_refs[5 * p + part], in_refs[5 * p + part + 2]
        if mb == 0:
            return jnp.concatenate([prev_ref[r], cur_ref[r, 0:SPAN, :]], axis=0)
        return cur_ref[r, (mb - 1) * SPAN:(mb + 1) * SPAN, :]

    def unit(p, j):
        dil = PATTERNS[p][1]
        r, mb = divmod(j, SUPER // dil // SPAN)
        row0 = mb * SPAN
        qb = in_refs[5 * p][r, row0:row0 + SPAN, :]
        kb, vb = with_prev(p, 1, r, mb), with_prev(p, 2, r, mb)
        bias = bias_s[p, first_n] if mb == 0 else bias_s[p, 0]
        zero = jnp.zeros_like(qb)
        qs = jnp.concatenate([jnp.where(even, qb, zero), jnp.where(even, zero, qb)], axis=0)
        s = lax.dot_general(qs, kb, (((1,), (1,)), ((), ())), preferred_element_type=F32) + bias
        m = jnp.max(s, axis=-1, keepdims=True)
        pr = jnp.exp2(s - m).astype(BF16)
        res = jnp.dot(pr, jnp.concatenate([vb, ones], axis=1), preferred_element_type=F32)
        mb_ = jnp.broadcast_to(m, (2 * SPAN, LANES))
        rows = pl.ds(row0, SPAN) if dil == 1 else pl.ds(row0 * dil + r, SPAN, stride=dil)
        num_s[p, rows, :] = jnp.where(even, res[:SPAN, :LANES], res[SPAN:, :LANES])
        l_s[p, rows, :] = jnp.where(even, res[:SPAN, LANES:], res[SPAN:, LANES:])
        m_s[p, rows, :] = jnp.where(even, mb_[:SPAN], mb_[SPAN:])

    def combine(jn):
        rows = slice(jn * SPAN, (jn + 1) * SPAN)
        ms = [m_s[p, rows, :] for p in range(np_)]
        m = functools.reduce(jnp.maximum, ms)
        num = jnp.zeros((SPAN, LANES), F32)
        den = jnp.zeros((SPAN, LANES), F32)
        for p in range(np_):
            a = jnp.exp2(ms[p] - m)
            num = num + a * num_s[p, rows, :]
            den = den + a * l_s[p, rows, :]
        o_ref[rows, :] = (num / den).astype(o_ref.dtype)

    done = set()
    for jn in range(SUPER // SPAN):
        for p in sorted(range(np_), key=lambda p: -PATTERNS[p][1]):
            dil = PATTERNS[p][1]
            nblk = SUPER // dil // SPAN
            for r in range(dil):
                j = r * nblk + jn // dil
                if (p, j) not in done:
                    done.add((p, j))
                    unit(p, j)
        combine(jn)


def _attn_prompt(qkvs, bvec):
    s_len = qkvs[0].shape[1]
    assert s_len % SUPER == 0
    nsb = s_len // SUPER
    npair = D_A // LANES
    in_specs, args, scratch = [], [], []
    for (_, dil), a in zip(PATTERNS, qkvs):
        rows = SUPER // dil
        nblk = rows // SPAN
        for part in range(3):
            in_specs.append(pl.BlockSpec((dil, rows, LANES),
                                         lambda hp, n, part=part: (0, n, part * npair + hp)))
            args.append(a)
        for part in (1, 2):
            in_specs.append(pl.BlockSpec((dil, SPAN, LANES),
                                         lambda hp, n, part=part, nblk=nblk:
                                         (0, jnp.maximum(n * nblk - 1, 0), part * npair + hp)))
            args.append(a)
    in_specs.append(pl.BlockSpec((len(PATTERNS), 1, 2, 2 * SPAN), lambda hp, n: (0, hp, 0, 0)))
    args.append(bvec)
    scratch += [pltpu.VMEM((len(PATTERNS), SUPER, LANES), F32)] * 3
    scratch.append(pltpu.VMEM((len(PATTERNS), 2, 2 * SPAN, 2 * SPAN), F32))
    return pl.pallas_call(
        _attn_prompt_kernel,
        grid=(npair, nsb),
        in_specs=in_specs,
        out_specs=pl.BlockSpec((SUPER, LANES), lambda hp, n: (n, hp)),
        out_shape=jax.ShapeDtypeStruct((s_len, D_A), BF16),
        scratch_shapes=scratch,
        compiler_params=_params(("arbitrary", "arbitrary")),
        name="attn_prompt",
    )(*args)


def _sample_bias_tables(u_refs, w_ref, bc_refs, bn_ref, t_new):
    new_lane = lax.broadcasted_iota(jnp.int32, (t_new, LANES), 1) >= LANES - t_new
    for p, (win, _) in enumerate(PATTERNS):
        for h in range(H_A):
            rows = slice(h * t_new, (h + 1) * t_new)
            u = jnp.broadcast_to(u_refs[p][h:h + 1, :], (t_new, win + LANES))
            bc_refs[p][rows, :] = pltpu.roll(u, 0, 1, stride=1, stride_axis=0)[:, :win]
            w = jnp.broadcast_to(w_ref[p, h:h + 1, :], (t_new, LANES))
            bn_ref[p, rows, :] = jnp.where(new_lane, pltpu.roll(w, 0, 1, stride=1, stride_axis=0), NEG_INF)


def _attn_sample_step(qkv_ref, kc_ref, vc_ref, bc_refs, bn_ref, att_ref, ko_ref, vo_ref, *, t_new, wb):
    np_ = len(PATTERNS)
    qkv = qkv_ref[0]
    qf, kn, vn = qkv[:, 0:D_A], qkv[:, D_A:2 * D_A], qkv[:, 2 * D_A:3 * D_A]
    nrow = H_A * t_new
    row_h = lax.broadcasted_iota(jnp.int32, (nrow, D_A), 0) // t_new
    lane_h = lax.broadcasted_iota(jnp.int32, (nrow, D_A), 1) // HD_A
    own = row_h == lane_h
    qs = jnp.where(own, jnp.concatenate([qf] * H_A, axis=0), 0.0).astype(BF16)
    pad = jnp.zeros((LANES - t_new, D_A), F32)
    knt = jnp.concatenate([pad, kn], axis=0).T
    vnt = jnp.concatenate([pad, vn], axis=0).T
    kct = kc_ref[0]
    vct = vc_ref[0]
    s_c = jnp.dot(qs, kct.astype(BF16), preferred_element_type=F32)
    s_n = jnp.dot(qs, knt.astype(BF16), preferred_element_type=F32)

    wins = [w for w, _ in PATTERNS]
    es_c, es_n = [], []
    m = None
    for p, win in enumerate(wins):
        e_c = s_c[:, wb - win:] + bc_refs[p][...]
        e_n = s_n + bn_ref[p]
        es_c.append(e_c)
        es_n.append(e_n)
        mp = jnp.maximum(jnp.max(e_c, axis=-1, keepdims=True), jnp.max(e_n, axis=-1, keepdims=True))
        m = mp if m is None else jnp.maximum(m, mp)
    pn = None
    for p in range(np_):
        e = jnp.exp2(es_n[p] - m)
        pn = e if pn is None else pn + e
    order = sorted(range(np_), key=lambda p: -wins[p])
    assert wins[order[0]] == wb
    pc = jnp.exp2(es_c[order[0]] - m)
    for p in order[1:]:
        e = jnp.exp2(es_c[p] - m)
        pc = jnp.concatenate([pc[:, :wb - wins[p]], pc[:, wb - wins[p]:] + e], axis=1)
    den = jnp.sum(pc, axis=-1, keepdims=True) + jnp.sum(pn, axis=-1, keepdims=True)
    nt = (((1,), (1,)), ((), ()))
    num = (lax.dot_general(pc.astype(BF16), vct.astype(BF16), nt, preferred_element_type=F32)
           + lax.dot_general(pn.astype(BF16), vnt.astype(BF16), nt, preferred_element_type=F32))
    o = jnp.where(own, num / den, 0.0)
    att = o[0:t_new]
    for h in range(1, H_A):
        att = att + o[h * t_new:(h + 1) * t_new]
    att_ref[0] = att.astype(att_ref.dtype)

    is_new = lax.broadcasted_iota(jnp.int32, (D_A, LANES), 1) >= LANES - t_new
    for src, new, dst in ((kct, knt, ko_ref), (vct, vnt, vo_ref)):
        rolled = pltpu.roll(src, wb - t_new, 1)
        dst[0, :, 0:wb - LANES] = rolled[:, 0:wb - LANES]
        dst[0, :, wb - LANES:wb] = jnp.where(is_new, new, rolled[:, wb - LANES:wb])


def _mixers_kernel(*refs, t_new, wb, c, chunks, cs):
    np_ = len(PATTERNS)
    nm = 10
    p_in, s_in = refs[0:nm], refs[nm:2 * nm]
    qkv_ref, kc_ref, vc_ref = refs[2 * nm:2 * nm + 3]
    u_refs = refs[2 * nm + 3:2 * nm + 3 + np_]
    w_ref = refs[2 * nm + 3 + np_]
    outs = refs[2 * nm + 4 + np_:]
    (ph_ref, pc_out, pn_out, pm_out), (sh_ref, sc_out, sn_out, sm_out) = outs[0:4], outs[4:8]
    att_ref, ko_ref, vo_ref = outs[8:11]
    pcx_s, pm_s, scx_s, sm_s = outs[11:15]
    bc_refs, bn_ref = outs[15:15 + np_], outs[15 + np_]
    i = pl.program_id(0)

    @pl.when(i == 0)
    def _():
        _mlstm_load_state(*p_in[7:10], pcx_s, pm_s)
        _sample_bias_tables(u_refs, w_ref, bc_refs, bn_ref, t_new)

    _attn_sample_step(qkv_ref, kc_ref, vc_ref, bc_refs, bn_ref, att_ref, ko_ref, vo_ref, t_new=t_new, wb=wb)
    _mlstm_load_state(*s_in[7:10], scx_s, sm_s)
    _mlstm_chunk(*s_in[0:7], sh_ref, scx_s, sm_s, c=cs, off=0)
    _mlstm_store_state(sc_out, sn_out, sm_out, scx_s, sm_s)
    for j in range(chunks):
        _mlstm_chunk(*p_in[0:7], ph_ref, pcx_s, pm_s, c=c, off=j * c)

    @pl.when(i == pl.num_programs(0) - 1)
    def _():
        _mlstm_store_state(pc_out, pn_out, pm_out, pcx_s, pm_s)


def _mixers(prompt_m, sample_m, qkv, cache_kt, cache_vt, bias_u, bias_w, *, c, cs):
    n = prompt_m[1].shape[0]
    b, t_new, _ = qkv.shape
    wb = cache_kt.shape[2]
    assert wb == max(w for w, _ in PATTERNS) and t_new % 8 == 0 and t_new <= LANES
    assert n % (b * c) == 0 and sample_m[1].shape[0] == b * cs
    chunks = n // (b * c)
    nrow = H_A * t_new
    blk = lambda r, w: pl.BlockSpec((1, r, w), lambda i: (i, 0, 0))
    fixed = lambda shape: pl.BlockSpec((1,) + shape, lambda i: (0,) * (1 + len(shape)))
    per_b = lambda shape: pl.BlockSpec((1,) + shape, lambda i: (i,) + (0,) * len(shape))

    def group(tokens, chunk, st):
        row = lambda w: pl.BlockSpec((tokens, w), lambda i: (i, 0))
        col = lambda h: pl.BlockSpec((h, tokens), lambda i: (0, i))
        ins = [col(D_M), row(D_M), col(D_M), col(D_M), row(LANES), col(GATE_ROWS), _const_spec((D_M, chunk)),
               st((H_M, DK_M, DV_M)), st((H_M, 1, DK_M)), st((H_M, 8, LANES))]
        outs = [col(D_M), st((H_M, DK_M, DV_M)), st((H_M, 8, DK_M)), st((H_M, 8, LANES))]
        return ins, outs

    def group_shapes(tokens_total, nb):
        return [jax.ShapeDtypeStruct((D_M, tokens_total), BF16), jax.ShapeDtypeStruct((nb, H_M, DK_M, DV_M), F32),
                jax.ShapeDtypeStruct((nb, H_M, 8, DK_M), F32), jax.ShapeDtypeStruct((nb, H_M, 8, LANES), F32)]

    p_ins, p_outs = group(chunks * c, c, fixed)
    s_ins, s_outs = group(cs, cs, per_b)
    state = [pltpu.VMEM((H_M, 2 * DV_M, DK_M), F32), pltpu.VMEM((H_M, 8, LANES), F32)]
    res = pl.pallas_call(
        functools.partial(_mixers_kernel, t_new=t_new, wb=wb, c=c, chunks=chunks, cs=cs),
        grid=(b,),
        in_specs=p_ins + s_ins + [blk(t_new, N_QKV), blk(D_A, wb), blk(D_A, wb)]
                 + [_const_spec((H_A, w + LANES)) for w, _ in PATTERNS]
                 + [_const_spec((len(PATTERNS), H_A, LANES))],
        out_specs=p_outs + s_outs + [blk(t_new, D_A), blk(D_A, wb), blk(D_A, wb)],
        out_shape=group_shapes(n, 1) + group_shapes(b * cs, b)
                  + [jax.ShapeDtypeStruct((b, t_new, D_A), BF16),
                     jax.ShapeDtypeStruct((b, D_A, wb), F32), jax.ShapeDtypeStruct((b, D_A, wb), F32)],
        scratch_shapes=state + state + [pltpu.VMEM((nrow, w), F32) for w, _ in PATTERNS]
                       + [pltpu.VMEM((len(PATTERNS), nrow, LANES), F32)],
        compiler_params=_params(("arbitrary",)),
        name="mixers",
    )(*prompt_m, *sample_m, qkv, cache_kt, cache_vt, *bias_u, bias_w)
    return res[0:4], res[4:8], res[8], res[9], res[10]


def _split3(x):
    a = x.astype(BF16)
    r = x - a.astype(F32)
    b = r.astype(BF16)
    c = (r - b.astype(F32)).astype(BF16)
    return a, b, c


def _mlstm_load_state(c0_ref, n0_ref, m0_ref, cx_s, m_s):
    for h in range(H_M):
        cx_s[h, 0:DV_M, :] = c0_ref[0, h].T
        cx_s[h, DV_M:, :] = jnp.broadcast_to(n0_ref[0, h], (DV_M, DK_M))
    m_s[...] = m0_ref[0]


def _mlstm_store_state(c_out, n_out, m_out, cx_s, m_s):
    for h in range(H_M):
        c_out[0, h] = cx_s[h, 0:DV_M, :].T
        n_out[0, h] = cx_s[h, DV_M:DV_M + 8, :]
    m_out[0] = m_s[...]


def _mlstm_chunk(qt_ref, k_ref, vt_ref, ot_ref, g_ref, gtt_ref, gw_ref, h_ref, cx_s, m_s, *, c, off):
    tok = slice(off, off + c)
    gates = g_ref[tok, :]
    gates_t = gtt_ref[:, tok]
    lane = lax.broadcasted_iota(jnp.int32, gates.shape, 1)
    lf = jnp.where(jnp.logical_and(lane >= H_M, lane < 2 * H_M), gates, 0.0)
    grow = lax.broadcasted_iota(jnp.int32, gates_t.shape, 0)
    lf_t = jnp.where(jnp.logical_and(grow >= H_M, grow < 2 * H_M), gates_t, 0.0)
    ri = lax.broadcasted_iota(jnp.int32, (c, c), 0)
    cj = lax.broadcasted_iota(jnp.int32, (c, c), 1)
    upper = ri <= cj
    tril = (ri >= cj).astype(BF16)
    triu = upper.astype(BF16)
    bsum = brow = None
    for part, part_t in zip(_split3(lf), _split3(lf_t)):
        t1 = jnp.dot(tril, part, preferred_element_type=F32)
        t2 = jnp.dot(part_t, triu, preferred_element_type=F32)
        bsum = t1 if bsum is None else bsum + t1
        brow = t2 if brow is None else brow + t2
    ones = jnp.ones((DV_M, c), BF16)

    for h in range(H_M):
        sl = slice(h * DK_M, (h + 1) * DK_M)
        qt, kh, vt = qt_ref[sl, tok], k_ref[tok, sl], vt_ref[sl, tok]
        b_row = brow[H_M + h:H_M + h + 1, :]
        i_row = gates_t[h:h + 1, :]
        a_col = gates[:, h:h + 1] - bsum[:, H_M + h:H_M + h + 1]
        m_prev = m_s[h, 0:1, 0:1]
        dmat = jnp.where(upper, b_row + a_col, NEG_INF)
        inter = b_row + m_prev
        mt = jnp.maximum(inter, jnp.max(dmat, axis=0, keepdims=True))
        st = jnp.dot(kh, qt, preferred_element_type=F32)
        smt = (st * jnp.exp(dmat - mt)).astype(BF16)
        iw = jnp.exp(inter - mt)
        vext = jnp.concatenate([vt, ones], axis=0)
        cxh = cx_s[h]
        ne = (jnp.dot(vext, smt, preferred_element_type=F32)
              + iw * jnp.dot(cxh.astype(BF16), qt, preferred_element_type=F32))
        num, den = ne[:DV_M], ne[DV_M:]
        hq = num / jnp.maximum(jnp.abs(den), jnp.exp(-mt))
        hn = hq * lax.rsqrt(jnp.mean(hq * hq, axis=0, keepdims=True) + RMS_EPS) * gw_ref[sl, :]
        h_ref[sl, tok] = (jax.nn.sigmoid(ot_ref[sl, tok]) * hn).astype(h_ref.dtype)

        b_last = b_row[:, c - 1:c]
        g_row = b_last - b_row + i_row
        m_new = jnp.maximum(b_last + m_prev, jnp.max(g_row, axis=1, keepdims=True))
        ws = jnp.exp(g_row - m_new)
        wc = jnp.exp(b_last + m_prev - m_new)
        vw = (vext.astype(F32) * ws).astype(BF16)
        cx_s[h] = wc * cxh + jnp.dot(vw, kh, preferred_element_type=F32)
        m_s[h] = jnp.broadcast_to(m_new, m_s.shape[1:])


def _outffn_kernel(att_ref, hmt_ref, x_ref, prev_ref, wo_ref, gpost_ref, gpre_ref, wup_ref, cw_ref, cb_ref,
                   wdn_ref, gffn_ref, y_ref, st_ref, u_ref, a_ref, *, tile, shift, hdr):
    i = pl.program_id(0)

    @pl.when(i == 0)
    def _():
        u_ref[hdr - 2 * shift:hdr, :] = prev_ref[...]

    @pl.when(i > 0)
    def _():
        u_ref[0:hdr, :] = u_ref[tile:tile + hdr, :]

    mixed = (jnp.dot(att_ref[...], wo_ref[0:D_A, :], preferred_element_type=F32)
             + lax.dot_general(hmt_ref[...], wo_ref[D_A:, :], (((0,), (0,)), ((), ())),
                               preferred_element_type=F32))
    x1 = x_ref[...] + _rms(mixed, gpost_ref[...])
    h2 = _rms(x1, gpre_ref[...]).astype(BF16)
    u_ref[hdr:hdr + tile, :] = jnp.dot(h2, wup_ref[...], preferred_element_type=F32)

    def conv(c):
        cols = slice(c * LANES, (c + 1) * LANES)
        y = cb_ref[:, cols] + u_ref[hdr - 2 * shift:hdr - 2 * shift + tile, cols] * cw_ref[0:1, cols]
        y = y + u_ref[hdr - shift:hdr - shift + tile, cols] * cw_ref[1:2, cols]
        return y + u_ref[hdr:hdr + tile, cols] * cw_ref[2:3, cols]

    nff = D_FF // LANES
    for c in range(nff):
        gate, val = conv(c), conv(nff + c)
        e = jnp.exp(gate * (GELU_C1 + GELU_C3 * (gate * gate)))
        a_ref[:, c * LANES:(c + 1) * LANES] = (gate * val / (1.0 + e)).astype(BF16)

    y2 = jnp.dot(a_ref[...], wdn_ref[...], preferred_element_type=F32)
    y_ref[...] = x1 + _rms(y2, gffn_ref[...])
    st_ref[...] = u_ref[tile:tile + hdr, :]


def _outffn(att, hmt, x, prev, wo, gpost, gpre, wup, cw, cb, wdn, gffn, *, tile, shift):
    n = x.shape[0]
    assert n % tile == 0 and (shift == 1 or n == tile)
    hdr = max(8, 2 * shift)
    assert hdr % 8 == 0 and (tile % LANES == 0 or tile == n)
    row = lambda w: pl.BlockSpec((tile, w), lambda i: (i, 0))
    return pl.pallas_call(
        functools.partial(_outffn_kernel, tile=tile, shift=shift, hdr=hdr),
        grid=(n // tile,),
        in_specs=[row(D_A), pl.BlockSpec((D_M, tile), lambda i: (0, i)), row(D_MODEL),
                  _const_spec((2 * shift, 2 * D_FF)),
                  _const_spec((D_A + D_M, D_MODEL)), _const_spec((1, D_MODEL)), _const_spec((1, D_MODEL)),
                  _const_spec((D_MODEL, 2 * D_FF)), _const_spec((CONV_W, 2 * D_FF)), _const_spec((1, 2 * D_FF)),
                  _const_spec((D_FF, D_MODEL)), _const_spec((1, D_MODEL))],
        out_specs=[row(D_MODEL), pl.BlockSpec((hdr, 2 * D_FF), lambda i: (0, 0))],
        out_shape=[jax.ShapeDtypeStruct((n, D_MODEL), F32), jax.ShapeDtypeStruct((hdr, 2 * D_FF), F32)],
        scratch_shapes=[pltpu.VMEM((hdr + tile, 2 * D_FF), F32), pltpu.VMEM((tile, D_FF), BF16)],
        compiler_params=_params(("arbitrary",)),
        name="outffn",
    )(att, hmt, x, prev, wo, gpost, gpre, wup, cw, cb, wdn, gffn)


def _t5_bucket(dist):
    max_exact = N_BUCKETS // 2
    d = jnp.maximum(dist, max_exact).astype(F32)
    large = max_exact + (jnp.log(d / max_exact) / math.log(BUCKET_MAX_DIST / max_exact)
                         * (N_BUCKETS - max_exact)).astype(jnp.int32)
    large = jnp.minimum(large, N_BUCKETS - 1)
    return jnp.where(dist < max_exact, dist, large)


def _pattern_bias(rel_bias, dil):
    j = jnp.arange(SPAN + 1)
    return rel_bias[_t5_bucket(j * dil)].T.astype(F32) * LOG2E


def _prompt_bias(rel_bias):
    out = []
    for _, dil in PATTERNS:
        bh = _pattern_bias(rel_bias, dil)
        v = jnp.concatenate([bh[:, ::-1], jnp.full((H_A, SPAN - 1), NEG_INF, F32)], axis=1)
        out.append(v.reshape(H_A // 2, 2, 2 * SPAN))
    return jnp.stack(out)


def _sample_bias(rel_bias, t_new):
    us, ws = [], []
    fill = lambda n: jnp.full((H_A, n), NEG_INF, F32)
    for win, dil in PATTERNS:
        bh = _pattern_bias(rel_bias, dil)
        dist = jnp.arange((SPAN + 1) * dil)
        g = jnp.where(dist % dil == 0, jnp.repeat(bh, dil, axis=1), NEG_INF)[:, :win + 1]
        us.append(jnp.concatenate([g[:, ::-1], fill(LANES - 1)], axis=1))
        ws.append(jnp.concatenate([g[:, :LANES - t_new + 1][:, ::-1], fill(t_new - 1)], axis=1))
    return us, jnp.stack(ws)


def _layer_weights(g_mix_pre, w_in, b_igate, b_fgate, g_mlstm_out, w_out, g_mix_post, g_ffn_pre, w_up,
                   conv_w, conv_b, w_down, g_ffn_post):
    wg = jnp.zeros((D_MODEL, LANES), F32).at[:, :2 * H_M].set(w_in[:, N_QKV + N_MIX:])
    bg = jnp.zeros((1, LANES), F32).at[0, :H_M].set(b_igate.astype(F32)).at[0, H_M:2 * H_M].set(b_fgate.astype(F32))
    row = lambda v: v.astype(F32).reshape(1, -1)
    mix = lambda j: w_in[:, N_QKV + j * D_M:N_QKV + (j + 1) * D_M]
    wmt = jnp.concatenate([mix(0), mix(2), mix(3)], axis=1).T.astype(BF16)
    gw = jnp.broadcast_to(g_mlstm_out.astype(F32)[:, None], (D_M, MLSTM_STEP))
    return dict(
        g_pre=row(g_mix_pre), wa=w_in[:, :N_QKV].astype(BF16), wkvt=w_in[:, D_A:N_QKV].T.astype(BF16),
        wmt=wmt, wk=mix(1).astype(BF16),
        wg=wg.astype(BF16), bg=bg, gw=gw, wo=w_out.astype(BF16), g_post=row(g_mix_post),
        g_ffn_pre=row(g_ffn_pre), wup=w_up.astype(BF16), cw=conv_w.astype(F32), cb=row(conv_b),
        wdn=w_down.astype(BF16), g_ffn_post=row(g_ffn_post))


def _state_in(c0, n0, m0):
    nb = c0.shape[0]
    return (c0.astype(F32), n0.astype(F32).reshape(nb, H_M, 1, DK_M),
            jnp.broadcast_to(m0.astype(F32)[..., None, None], (nb, H_M, 8, LANES)))


def _state_out(c, n, m):
    return c, n[:, :, 0, :], m[:, :, 0, 0]


def _layer(xp, xs, w, bias_p, bias_u, bias_w, cache_k, cache_v, c0, n0, m0, conv_prev, *, tile_in, tile_ffn):
    s_len = xp.shape[0]
    b, t_new, _ = xs.shape
    n = b * t_new
    wb = cache_k.shape[1]
    proj = (w["g_pre"], w["wa"], w["wmt"], w["wk"], w["wg"], w["bg"])

    wbp = min(max(wd for wd, _ in PATTERNS), s_len)
    dils = tuple(d for _, d in PATTERNS if d > 1)
    qkv1, qkv4, qkv16, kt_p, vt_p, qmt_p, km_p, vmt_p, omt_p, gates_p, gates_tp = _inproj(
        xp, *proj, w["wkvt"], tile=tile_in, dils=dils, tail_rows=wbp)
    att_p = _attn_prompt([qkv1.reshape(1, s_len, N_QKV), qkv4, qkv16], bias_p)

    c = LANES
    _, tail, *sample_m = _inproj(xs.reshape(n, D_MODEL), *proj, tile=n, tail_rows=n, spread=(c, t_new))

    to_t = lambda a: jnp.transpose(a.astype(F32), (0, 2, 3, 1)).reshape(b, D_A, wb)
    from_t = lambda a: jnp.transpose(a.reshape(b, H_A, HD_A, wb), (0, 3, 1, 2))
    state0 = _state_in(jnp.zeros((1, H_M, DK_M, DV_M), F32), jnp.zeros((1, H_M, DK_M), F32),
                       jnp.zeros((1, H_M), F32))
    (hmt_p, c_p, n_p, m_p), (hmt_s, c_o, n_o, m_o), att, k_win, v_win = _mixers(
        (qmt_p, km_p, vmt_p, omt_p, gates_p, gates_tp, w["gw"]) + state0,
        tuple(sample_m) + (w["gw"][:, :c],) + _state_in(c0, n0, m0),
        tail.reshape(b, t_new, N_QKV), to_t(cache_k), to_t(cache_v), bias_u, bias_w, c=MLSTM_STEP, cs=c)

    prev0 = jnp.zeros((CONV_W - 1, 2 * D_FF), F32)
    yp, st_p = _outffn(att_p, hmt_p, xp, prev0, w["wo"], w["g_post"], w["g_ffn_pre"], w["wup"], w["cw"], w["cb"],
                       w["wdn"], w["g_ffn_post"], tile=tile_ffn, shift=1)
    window = lambda a: jnp.transpose(a.reshape(H_A, HD_A, wbp), (2, 0, 1))[None]
    k_win_p, v_win_p = window(kt_p), window(vt_p)
    state_p = (k_win_p, v_win_p) + _state_out(c_p, n_p, m_p) + (st_p[-(CONV_W - 1):][None],)

    tm = lambda a: a.reshape(b, t_new, -1).transpose(1, 0, 2).reshape(n, -1)
    hmt = hmt_s.reshape(D_M, b, c)[:, :, :t_new].transpose(0, 2, 1).reshape(D_M, n)
    prev = conv_prev.astype(F32).transpose(1, 0, 2).reshape((CONV_W - 1) * b, 2 * D_FF)
    ys, st = _outffn(tm(att), hmt, tm(xs), prev, w["wo"], w["g_post"], w["g_ffn_pre"], w["wup"], w["cw"],
                     w["cb"], w["wdn"], w["g_ffn_post"], tile=n, shift=b)
    ys = ys.reshape(t_new, b, D_MODEL).transpose(1, 0, 2)
    conv_state = st.reshape(CONV_W - 1, b, 2 * D_FF).transpose(1, 0, 2)
    state_s = (from_t(k_win), from_t(v_win)) + _state_out(c_o, n_o, m_o) + (conv_state,)
    return yp, ys, state_p, state_s


def kernel(x_prompt, x_sample, cache_attn_k, cache_attn_v, state_mlstm_C, state_mlstm_n, state_mlstm_m,
           state_ffn_conv, rel_bias, g_mix_pre, w_in, b_igate, b_fgate, g_mlstm_out, w_out, g_mix_post,
           g_ffn_pre, w_up, conv_w, conv_b, w_down, g_ffn_post):
    depth = w_in.shape[0]
    batch, s_len, _ = x_prompt.shape
    assert batch == 1
    t_new = x_sample.shape[1]
    bias_p = _prompt_bias(rel_bias)
    bias_u, bias_w = _sample_bias(rel_bias, t_new)
    yp = x_prompt[0]
    ys = x_sample
    new_p, new_s = [], []
    for l in range(depth):
        w = _layer_weights(g_mix_pre[l], w_in[l], b_igate[l], b_fgate[l], g_mlstm_out[l], w_out[l],
                           g_mix_post[l], g_ffn_pre[l], w_up[l], conv_w[l], conv_b[l], w_down[l], g_ffn_post[l])
        yp, ys, sp, ss = _layer(yp, ys, w, bias_p, bias_u, bias_w, cache_attn_k[l], cache_attn_v[l],
                                state_mlstm_C[l], state_mlstm_n[l], state_mlstm_m[l], state_ffn_conv[l],
                                tile_in=256, tile_ffn=512)
        new_p.append(sp)
        new_s.append(ss)
    stack = lambda states, i: jnp.stack([s[i] for s in states])
    return ((yp[None], ys) + tuple(stack(new_p, i) for i in range(6))
            + tuple(stack(new_s, i) for i in range(6)))
```

```python
import functools
import math

import jax
import jax.numpy as jnp
import numpy as np
from jax import lax
from jax.experimental import pallas as pl
from jax.experimental.pallas import tpu as pltpu

F32 = jnp.float32
BF16 = jnp.bfloat16

D_MODEL = 1024
HD_A = 64
H_A = 8
D_A = H_A * HD_A
DK_M = 128
DV_M = 128
H_M = 4
D_M = H_M * DV_M
PATTERNS = ((128, 1), (512, 4), (2048, 16))
SPAN = 128
N_BUCKETS = 32
BUCKET_MAX_DIST = 2048
MLSTM_STEP = 256
D_FF = 2816
CONV_W = 3
RMS_EPS = 1e-6
N_QKV = 3 * D_A
N_MIX = 4 * D_M
LANES = 128
GATE_ROWS = 16
SUPER = 2048
VMEM_LIMIT = 58 * 1024 * 1024
NEG_INF = float("-inf")
LOG2E = math.log2(math.e)
Q_SCALE = HD_A ** -0.5 * LOG2E
GELU_C1 = -2.0 * math.sqrt(2.0 / math.pi)
GELU_C3 = GELU_C1 * 0.044715

for _w, _d in PATTERNS:
    assert _w // _d == SPAN


def _params(sem, vmem=VMEM_LIMIT):
    return pltpu.CompilerParams(dimension_semantics=sem, vmem_limit_bytes=vmem)


def _const_spec(shape):
    nd = len(shape)
    return pl.BlockSpec(shape, lambda *_: (0,) * nd, pipeline_mode=pl.Buffered(1))


def _rms(x, g):
    return x * lax.rsqrt(jnp.mean(x * x, axis=-1, keepdims=True) + RMS_EPS) * g


def _inproj_kernel(x_ref, g_ref, wa_ref, wmt_ref, wk_ref, wg_ref, bg_ref, *refs, tile, dils, first_tail, spread):
    nd = len(dils)
    extra_a, extra_b = refs[0], refs[1]
    qkv1_ref = refs[2]
    dil_refs = refs[3:3 + nd]
    ntail = 2 if nd else 1
    tail_refs = refs[3 + nd:3 + nd + ntail]
    qmt_ref, km_ref, vmt_ref, omt_ref, gt_ref, gtt_ref = refs[3 + nd + ntail:]

    hb = _rms(x_ref[...], g_ref[...]).astype(BF16)
    nt = (((1,), (1,)), ((), ()))
    za = lax.dot_general(hb, wa_ref[...], nt, preferred_element_type=F32)
    za = jnp.concatenate([za[:, :D_A] * Q_SCALE, za[:, D_A:]], axis=1)
    zb = za.astype(BF16)
    qkv1_ref[...] = zb
    if not nd:
        tail_refs[0][...] = za

    if nd:
        zp = jnp.dot(extra_a[...], zb, preferred_element_type=F32).astype(BF16)
        base = 0
        for dil, out_ref in zip(dils, dil_refs):
            for r in range(dil):
                out_ref[r] = zp[base + r * (tile // dil):base + (r + 1) * (tile // dil)]
            base += tile

    zt = lax.dot_general(wmt_ref[...], hb, (((1,), (1,)), ((), ())), preferred_element_type=F32)
    qmt, vmt, omt = zt[0:D_M].astype(BF16), zt[D_M:2 * D_M].astype(BF16), zt[2 * D_M:3 * D_M]
    km = (lax.dot_general(hb, wk_ref[...], nt, preferred_element_type=F32) * (DK_M ** -0.5)).astype(BF16)

    zg = lax.dot_general(hb, wg_ref[...], nt, preferred_element_type=F32) + bg_ref[...]
    logsig = jnp.minimum(zg, 0.0) - jnp.log1p(jnp.exp(-jnp.abs(zg)))
    glane = lax.broadcasted_iota(jnp.int32, zg.shape, 1)
    gates = jnp.where(glane < H_M, zg, logsig)
    gates_t = gates.T[0:GATE_ROWS]

    if spread is None:
        qmt_ref[...], vmt_ref[...], omt_ref[...], km_ref[...] = qmt, vmt, omt, km
        gt_ref[...], gtt_ref[...] = gates, gates_t
    else:
        chunk, valid = spread
        to_cols = lambda a: jnp.dot(a, extra_a[...], preferred_element_type=F32)
        to_rows = lambda a: jnp.dot(extra_b[...], a, preferred_element_type=F32)
        exact = lambda move, a: functools.reduce(jnp.add, [move(part) for part in _split3(a)])
        qmt_ref[...] = to_cols(qmt).astype(BF16)
        vmt_ref[...] = to_cols(vmt).astype(BF16)
        omt_ref[...] = exact(to_cols, omt)
        km_ref[...] = to_rows(km).astype(BF16)
        tok = lax.broadcasted_iota(jnp.int32, gt_ref.shape, 0) % chunk
        ig_lane = lax.broadcasted_iota(jnp.int32, gt_ref.shape, 1) < H_M
        gt_ref[...] = jnp.where(jnp.logical_and(tok >= valid, ig_lane), NEG_INF, exact(to_rows, gates))
        tok_t = lax.broadcasted_iota(jnp.int32, gtt_ref.shape, 1) % chunk
        ig_row = lax.broadcasted_iota(jnp.int32, gtt_ref.shape, 0) < H_M
        gtt_ref[...] = jnp.where(jnp.logical_and(tok_t >= valid, ig_row), NEG_INF, exact(to_cols, gates_t))

    if nd:
        @pl.when(pl.program_id(0) >= first_tail)
        def _():
            kvt = lax.dot_general(extra_b[...], hb, (((1,), (1,)), ((), ())), preferred_element_type=F32)
            tail_refs[0][...] = kvt[0:D_A]
            tail_refs[1][...] = kvt[D_A:2 * D_A]


def _spread_matrix(n, chunk, valid):
    p = np.zeros((n, n // valid * chunk), np.float32)
    tok = np.arange(n)
    p[tok, tok // valid * chunk + tok % valid] = 1.0
    return jnp.asarray(p, BF16)


def _row_permutation(tile, dils):
    p = np.zeros((len(dils) * tile, tile), np.float32)
    for j, dil in enumerate(dils):
        for r in range(dil):
            for m in range(tile // dil):
                p[j * tile + r * (tile // dil) + m, m * dil + r] = 1.0
    return jnp.asarray(p, BF16)


def _inproj(x, g, wa, wmt, wk, wg, bg, wkvt=None, *, tile, dils=(), tail_rows, spread=None):
    n = x.shape[0]
    assert bool(dils) != (spread is not None)
    assert n % tile == 0 and tail_rows % tile == 0 and (tile % LANES == 0 or tile == n)
    for d in dils:
        assert tile % (16 * d) == 0
    steps = n // tile
    first_tail = (n - tail_rows) // tile
    m, mt = n, tile
    if spread:
        chunk, valid = spread
        assert steps == 1 and n % valid == 0 and valid <= chunk
        m = mt = n // valid * chunk
    row = lambda w: pl.BlockSpec((tile, w), lambda i: (i, 0))
    col = lambda h: pl.BlockSpec((h, tile), lambda i: (0, i))
    out_shape = [jax.ShapeDtypeStruct((n, N_QKV), BF16)]
    out_specs = [row(N_QKV)]
    for d in dils:
        out_shape.append(jax.ShapeDtypeStruct((d, n // d, N_QKV), BF16))
        out_specs.append(pl.BlockSpec((d, tile // d, N_QKV), lambda i: (0, i, 0)))
    if dils:
        out_shape += [jax.ShapeDtypeStruct((D_A, tail_rows), F32)] * 2
        out_specs += [pl.BlockSpec((D_A, tile), lambda i: (0, jnp.maximum(i - first_tail, 0)))] * 2
    else:
        out_shape.append(jax.ShapeDtypeStruct((tail_rows, N_QKV), F32))
        out_specs.append(pl.BlockSpec((tile, N_QKV), lambda i: (jnp.maximum(i - first_tail, 0), 0)))
    mrow = lambda w: pl.BlockSpec((mt, w), lambda i: (i, 0))
    mcol = lambda h: pl.BlockSpec((h, mt), lambda i: (0, i))
    out_shape += [jax.ShapeDtypeStruct((D_M, m), BF16), jax.ShapeDtypeStruct((m, D_M), BF16),
                  jax.ShapeDtypeStruct((D_M, m), BF16), jax.ShapeDtypeStruct((D_M, m), F32),
                  jax.ShapeDtypeStruct((m, LANES), F32), jax.ShapeDtypeStruct((GATE_ROWS, m), F32)]
    out_specs += [mcol(D_M), mrow(D_M), mcol(D_M), mcol(D_M), mrow(LANES), mcol(GATE_ROWS)]
    in_specs = [row(D_MODEL), _const_spec((1, D_MODEL)), _const_spec((N_QKV, D_MODEL)),
                _const_spec((3 * D_M, D_MODEL)), _const_spec((D_M, D_MODEL)),
                _const_spec((LANES, D_MODEL)), _const_spec((1, LANES))]
    args = [x, g, wa, wmt, wk, wg, bg]
    if dils:
        in_specs += [_const_spec((len(dils) * tile, tile)), _const_spec((2 * D_A, D_MODEL))]
        args += [_row_permutation(tile, dils), wkvt]
    else:
        sp = _spread_matrix(n, chunk, valid)
        in_specs += [_const_spec((n, m)), _const_spec((m, n))]
        args += [sp, sp.T]
    return pl.pallas_call(
        functools.partial(_inproj_kernel, tile=tile, dils=dils, first_tail=first_tail, spread=spread),
        grid=(steps,),
        in_specs=in_specs,
        out_specs=out_specs,
        out_shape=out_shape,
        compiler_params=_params(("arbitrary",)),
        name="inproj",
    )(*args)


def _attn_prompt_kernel(*refs):
    np_ = len(PATTERNS)
    in_refs = refs[:5 * np_]
    bvec_ref, o_ref = refs[5 * np_], refs[5 * np_ + 1]
    num_s, l_s, m_s, bias_s = refs[5 * np_ + 2:]
    n = pl.program_id(1)

    lane = lax.broadcasted_iota(jnp.int32, (SPAN, LANES), 1)
    even = lane < HD_A
    ones = jnp.ones((2 * SPAN, LANES), BF16)

    @pl.when(n == 0)
    def _():
        key_col = lax.broadcasted_iota(jnp.int32, (SPAN, 2 * SPAN), 1)
        for p in range(np_):
            for e in range(2):
                vb = jnp.broadcast_to(bvec_ref[p, 0, e:e + 1, :], (SPAN, 2 * SPAN))
                table = pltpu.roll(vb, 0, 1, stride=1, stride_axis=0)
                bias_s[p, 0, e * SPAN:(e + 1) * SPAN, :] = table
                bias_s[p, 1, e * SPAN:(e + 1) * SPAN, :] = jnp.where(key_col >= SPAN, table, NEG_INF)

    first_n = (n == 0).astype(jnp.int32)

    def with_prev(p, part, r, mb):
        cur_ref, prev_ref = in_refs[5 * p + part], in_refs[5 * p + part + 2]
        if mb == 0:
            return jnp.concatenate([prev_ref[r], cur_ref[r, 0:SPAN, :]], axis=0)
        return cur_ref[r, (mb - 1) * SPAN:(mb + 1) * SPAN, :]

    def unit(p, j):
        dil = PATTERNS[p][1]
        r, mb = divmod(j, SUPER // dil // SPAN)
        row0 = mb * SPAN
        qb = in_refs[5 * p][r, row0:row0 + SPAN, :]
        kb, vb = with_prev(p, 1, r, mb), with_prev(p, 2, r, mb)
        bias = bias_s[p, first_n] if mb == 0 else bias_s[p, 0]
        zero = jnp.zeros_like(qb)
        qs = jnp.concatenate([jnp.where(even, qb, zero), jnp.where(even, zero, qb)], axis=0)
        s = lax.dot_general(qs, kb, (((1,), (1,)), ((), ())), preferred_element_type=F32) + bias
        m = jnp.max(s, axis=-1, keepdims=True)
        pr = jnp.exp2(s - m).astype(BF16)
        res = jnp.dot(pr, jnp.concatenate([vb, ones], axis=1), preferred_element_type=F32)
        mb_ = jnp.broadcast_to(m, (2 * SPAN, LANES))
        rows = pl.ds(row0, SPAN) if dil == 1 else pl.ds(row0 * dil + r, SPAN, stride=dil)
        num_s[p, rows, :] = jnp.where(even, res[:SPAN, :LANES], res[SPAN:, :LANES])
        l_s[p, rows, :] = jnp.where(even, res[:SPAN, LANES:], res[SPAN:, LANES:])
        m_s[p, rows, :] = jnp.where(even, mb_[:SPAN], mb_[SPAN:])

    def combine(jn):
        rows = slice(jn * SPAN, (jn + 1) * SPAN)
        ms = [m_s[p, rows, :] for p in range(np_)]
        m = functools.reduce(jnp.maximum, ms)
        num = jnp.zeros((SPAN, LANES), F32)
        den = jnp.zeros((SPAN, LANES), F32)
        for p in range(np_):
            a = jnp.exp2(ms[p] - m)
            num = num + a * num_s[p, rows, :]
            den = den + a * l_s[p, rows, :]
        o_ref[rows, :] = (num / den).astype(o_ref.dtype)

    done = set()
    for jn in range(SUPER // SPAN):
        for p in sorted(range(np_), key=lambda p: -PATTERNS[p][1]):
            dil = PATTERNS[p][1]
            nblk = SUPER // dil // SPAN
            for r in range(dil):
                j = r * nblk + jn // dil
                if (p, j) not in done:
                    done.add((p, j))
                    unit(p, j)
        combine(jn)


def _attn_prompt(qkvs, bvec):
    s_len = qkvs[0].shape[1]
    assert s_len % SUPER == 0
    nsb = s_len // SUPER
    npair = D_A // LANES
    in_specs, args, scratch = [], [], []
    for (_, dil), a in zip(PATTERNS, qkvs):
        rows = SUPER // dil
        nblk = rows // SPAN
        for part in range(3):
            in_specs.append(pl.BlockSpec((dil, rows, LANES),
                                         lambda hp, n, part=part: (0, n, part * npair + hp)))
            args.append(a)
        for part in (1, 2):
            in_specs.append(pl.BlockSpec((dil, SPAN, LANES),
                                         lambda hp, n, part=part, nblk=nblk:
                                         (0, jnp.maximum(n * nblk - 1, 0), part * npair + hp)))
            args.append(a)
    in_specs.append(pl.BlockSpec((len(PATTERNS), 1, 2, 2 * SPAN), lambda hp, n: (0, hp, 0, 0)))
    args.append(bvec)
    scratch += [pltpu.VMEM((len(PATTERNS), SUPER, LANES), F32)] * 3
    scratch.append(pltpu.VMEM((len(PATTERNS), 2, 2 * SPAN, 2 * SPAN), F32))
    return pl.pallas_call(
        _attn_prompt_kernel,
        grid=(npair, nsb),
        in_specs=in_specs,
        out_specs=pl.BlockSpec((SUPER, LANES), lambda hp, n: (n, hp)),
        out_shape=jax.ShapeDtypeStruct((s_len, D_A), BF16),
        scratch_shapes=scratch,
        compiler_params=_params(("arbitrary", "arbitrary")),
        name="attn_prompt",
    )(*args)


def _sample_bias_tables(u_refs, w_ref, bc_refs, bn_ref, t_new):
    new_lane = lax.broadcasted_iota(jnp.int32, (t_new, LANES), 1) >= LANES - t_new
    for p, (win, _) in enumerate(PATTERNS):
        for h in range(H_A):
            rows = slice(h * t_new, (h + 1) * t_new)
            u = jnp.broadcast_to(u_refs[p][h:h + 1, :], (t_new, win + LANES))
            bc_refs[p][rows, :] = pltpu.roll(u, 0, 1, stride=1, stride_axis=0)[:, :win]
            w = jnp.broadcast_to(w_ref[p, h:h + 1, :], (t_new, LANES))
            bn_ref[p, rows, :] = jnp.where(new_lane, pltpu.roll(w, 0, 1, stride=1, stride_axis=0), NEG_INF)


def _attn_sample_step(qkv_ref, kc_ref, vc_ref, bc_refs, bn_ref, att_ref, ko_ref, vo_ref, *, t_new, wb):
    np_ = len(PATTERNS)
    qkv = qkv_ref[0]
    qf, kn, vn = qkv[:, 0:D_A], qkv[:, D_A:2 * D_A], qkv[:, 2 * D_A:3 * D_A]
    nrow = H_A * t_new
    row_h = lax.broadcasted_iota(jnp.int32, (nrow, D_A), 0) // t_new
    lane_h = lax.broadcasted_iota(jnp.int32, (nrow, D_A), 1) // HD_A
    own = row_h == lane_h
    qs = jnp.where(own, jnp.concatenate([qf] * H_A, axis=0), 0.0).astype(BF16)
    pad = jnp.zeros((LANES - t_new, D_A), F32)
    knt = jnp.concatenate([pad, kn], axis=0).T
    vnt = jnp.concatenate([pad, vn], axis=0).T
    kct = kc_ref[0]
    vct = vc_ref[0]
    s_c = jnp.dot(qs, kct.astype(BF16), preferred_element_type=F32)
    s_n = jnp.dot(qs, knt.astype(BF16), preferred_element_type=F32)

    wins = [w for w, _ in PATTERNS]
    es_c, es_n = [], []
    m = None
    for p, win in enumerate(wins):
        e_c = s_c[:, wb - win:] + bc_refs[p][...]
        e_n = s_n + bn_ref[p]
        es_c.append(e_c)
        es_n.append(e_n)
        mp = jnp.maximum(jnp.max(e_c, axis=-1, keepdims=True), jnp.max(e_n, axis=-1, keepdims=True))
        m = mp if m is None else jnp.maximum(m, mp)
    pn = None
    for p in range(np_):
        e = jnp.exp2(es_n[p] - m)
        pn = e if pn is None else pn + e
    order = sorted(range(np_), key=lambda p: -wins[p])
    assert wins[order[0]] == wb
    pc = jnp.exp2(es_c[order[0]] - m)
    for p in order[1:]:
        e = jnp.exp2(es_c[p] - m)
        pc = jnp.concatenate([pc[:, :wb - wins[p]], pc[:, wb - wins[p]:] + e], axis=1)
    den = jnp.sum(pc, axis=-1, keepdims=True) + jnp.sum(pn, axis=-1, keepdims=True)
    nt = (((1,), (1,)), ((), ()))
    num = (lax.dot_general(pc.astype(BF16), vct.astype(BF16), nt, preferred_element_type=F32)
           + lax.dot_general(pn.astype(BF16), vnt.astype(BF16), nt, preferred_element_type=F32))
    o = jnp.where(own, num / den, 0.0)
    att = o[0:t_new]
    for h in range(1, H_A):
        att = att + o[h * t_new:(h + 1) * t_new]
    att_ref[0] = att.astype(att_ref.dtype)

    is_new = lax.broadcasted_iota(jnp.int32, (D_A, LANES), 1) >= LANES - t_new
    for src, new, dst in ((kct, knt, ko_ref), (vct, vnt, vo_ref)):
        rolled = pltpu.roll(src, wb - t_new, 1)
        dst[0, :, 0:wb - LANES] = rolled[:, 0:wb - LANES]
        dst[0, :, wb - LANES:wb] = jnp.where(is_new, new, rolled[:, wb - LANES:wb])


def _mixers_kernel(*refs, t_new, wb, c, chunks, cs):
    np_ = len(PATTERNS)
    nm = 10
    p_in, s_in = refs[0:nm], refs[nm:2 * nm]
    qkv_ref, kc_ref, vc_ref = refs[2 * nm:2 * nm + 3]
    u_refs = refs[2 * nm + 3:2 * nm + 3 + np_]
    w_ref = refs[2 * nm + 3 + np_]
    outs = refs[2 * nm + 4 + np_:]
    (ph_ref, pc_out, pn_out, pm_out), (sh_ref, sc_out, sn_out, sm_out) = outs[0:4], outs[4:8]
    att_ref, ko_ref, vo_ref = outs[8:11]
    pcx_s, pm_s, scx_s, sm_s = outs[11:15]
    bc_refs, bn_ref = outs[15:15 + np_], outs[15 + np_]
    i = pl.program_id(0)

    @pl.when(i == 0)
    def _():
        _mlstm_load_state(*p_in[7:10], pcx_s, pm_s)
        _sample_bias_tables(u_refs, w_ref, bc_refs, bn_ref, t_new)

    _attn_sample_step(qkv_ref, kc_ref, vc_ref, bc_refs, bn_ref, att_ref, ko_ref, vo_ref, t_new=t_new, wb=wb)
    _mlstm_load_state(*s_in[7:10], scx_s, sm_s)
    _mlstm_chunk(*s_in[0:7], sh_ref, scx_s, sm_s, c=cs, off=0)
    _mlstm_store_state(sc_out, sn_out, sm_out, scx_s, sm_s)
    for j in range(chunks):
        _mlstm_chunk(*p_in[0:7], ph_ref, pcx_s, pm_s, c=c, off=j * c)

    @pl.when(i == pl.num_programs(0) - 1)
    def _():
        _mlstm_store_state(pc_out, pn_out, pm_out, pcx_s, pm_s)


def _mixers(prompt_m, sample_m, qkv, cache_kt, cache_vt, bias_u, bias_w, *, c, cs):
    n = prompt_m[1].shape[0]
    b, t_new, _ = qkv.shape
    wb = cache_kt.shape[2]
    assert wb == max(w for w, _ in PATTERNS) and t_new % 8 == 0 and t_new <= LANES
    assert n % (b * c) == 0 and sample_m[1].shape[0] == b * cs
    chunks = n // (b * c)
    nrow = H_A * t_new
    blk = lambda r, w: pl.BlockSpec((1, r, w), lambda i: (i, 0, 0))
    fixed = lambda shape: pl.BlockSpec((1,) + shape, lambda i: (0,) * (1 + len(shape)))
    per_b = lambda shape: pl.BlockSpec((1,) + shape, lambda i: (i,) + (0,) * len(shape))

    def group(tokens, chunk, st):
        row = lambda w: pl.BlockSpec((tokens, w), lambda i: (i, 0))
        col = lambda h: pl.BlockSpec((h, tokens), lambda i: (0, i))
        ins = [col(D_M), row(D_M), col(D_M), col(D_M), row(LANES), col(GATE_ROWS), _const_spec((D_M, chunk)),
               st((H_M, DK_M, DV_M)), st((H_M, 1, DK_M)), st((H_M, 8, LANES))]
        outs = [col(D_M), st((H_M, DK_M, DV_M)), st((H_M, 8, DK_M)), st((H_M, 8, LANES))]
        return ins, outs

    def group_shapes(tokens_total, nb):
        return [jax.ShapeDtypeStruct((D_M, tokens_total), BF16), jax.ShapeDtypeStruct((nb, H_M, DK_M, DV_M), F32),
                jax.ShapeDtypeStruct((nb, H_M, 8, DK_M), F32), jax.ShapeDtypeStruct((nb, H_M, 8, LANES), F32)]

    p_ins, p_outs = group(chunks * c, c, fixed)
    s_ins, s_outs = group(cs, cs, per_b)
    state = [pltpu.VMEM((H_M, 2 * DV_M, DK_M), F32), pltpu.VMEM((H_M, 8, LANES), F32)]
    res = pl.pallas_call(
        functools.partial(_mixers_kernel, t_new=t_new, wb=wb, c=c, chunks=chunks, cs=cs),
        grid=(b,),
        in_specs=p_ins + s_ins + [blk(t_new, N_QKV), blk(D_A, wb), blk(D_A, wb)]
                 + [_const_spec((H_A, w + LANES)) for w, _ in PATTERNS]
                 + [_const_spec((len(PATTERNS), H_A, LANES))],
        out_specs=p_outs + s_outs + [blk(t_new, D_A), blk(D_A, wb), blk(D_A, wb)],
        out_shape=group_shapes(n, 1) + group_shapes(b * cs, b)
                  + [jax.ShapeDtypeStruct((b, t_new, D_A), BF16),
                     jax.ShapeDtypeStruct((b, D_A, wb), F32), jax.ShapeDtypeStruct((b, D_A, wb), F32)],
        scratch_shapes=state + state + [pltpu.VMEM((nrow, w), F32) for w, _ in PATTERNS]
                       + [pltpu.VMEM((len(PATTERNS), nrow, LANES), F32)],
        compiler_params=_params(("arbitrary",)),
        name="mixers",
    )(*prompt_m, *sample_m, qkv, cache_kt, cache_vt, *bias_u, bias_w)
    return res[0:4], res[4:8], res[8], res[9], res[10]


def _split3(x):
    a = x.astype(BF16)
    r = x - a.astype(F32)
    b = r.astype(BF16)
    c = (r - b.astype(F32)).astype(BF16)
    return a, b, c


def _mlstm_load_state(c0_ref, n0_ref, m0_ref, cx_s, m_s):
    for h in range(H_M):
        cx_s[h, 0:DV_M, :] = c0_ref[0, h].T
        cx_s[h, DV_M:, :] = jnp.broadcast_to(n0_ref[0, h], (DV_M, DK_M))
    m_s[...] = m0_ref[0]


def _mlstm_store_state(c_out, n_out, m_out, cx_s, m_s):
    for h in range(H_M):
        c_out[0, h] = cx_s[h, 0:DV_M, :].T
        n_out[0, h] = cx_s[h, DV_M:DV_M + 8, :]
    m_out[0] = m_s[...]


def _mlstm_chunk(qt_ref, k_ref, vt_ref, ot_ref, g_ref, gtt_ref, gw_ref, h_ref, cx_s, m_s, *, c, off):
    tok = slice(off, off + c)
    gates = g_ref[tok, :]
    gates_t = gtt_ref[:, tok]
    lane = lax.broadcasted_iota(jnp.int32, gates.shape, 1)
    lf = jnp.where(jnp.logical_and(lane >= H_M, lane < 2 * H_M), gates, 0.0)
    grow = lax.broadcasted_iota(jnp.int32, gates_t.shape, 0)
    lf_t = jnp.where(jnp.logical_and(grow >= H_M, grow < 2 * H_M), gates_t, 0.0)
    ri = lax.broadcasted_iota(jnp.int32, (c, c), 0)
    cj = lax.broadcasted_iota(jnp.int32, (c, c), 1)
    upper = ri <= cj
    tril = (ri >= cj).astype(BF16)
    triu = upper.astype(BF16)
    bsum = brow = None
    for part, part_t in zip(_split3(lf), _split3(lf_t)):
        t1 = jnp.dot(tril, part, preferred_element_type=F32)
        t2 = jnp.dot(part_t, triu, preferred_element_type=F32)
        bsum = t1 if bsum is None else bsum + t1
        brow = t2 if brow is None else brow + t2
    ones = jnp.ones((DV_M, c), BF16)

    for h in range(H_M):
        sl = slice(h * DK_M, (h + 1) * DK_M)
        qt, kh, vt = qt_ref[sl, tok], k_ref[tok, sl], vt_ref[sl, tok]
        b_row = brow[H_M + h:H_M + h + 1, :]
        i_row = gates_t[h:h + 1, :]
        a_col = gates[:, h:h + 1] - bsum[:, H_M + h:H_M + h + 1]
        m_prev = m_s[h, 0:1, 0:1]
        dmat = jnp.where(upper, b_row + a_col, NEG_INF)
        inter = b_row + m_prev
        mt = jnp.maximum(inter, jnp.max(dmat, axis=0, keepdims=True))
        st = jnp.dot(kh, qt, preferred_element_type=F32)
        smt = (st * jnp.exp(dmat - mt)).astype(BF16)
        iw = jnp.exp(inter - mt)
        vext = jnp.concatenate([vt, ones], axis=0)
        cxh = cx_s[h]
        ne = (jnp.dot(vext, smt, preferred_element_type=F32)
              + iw * jnp.dot(cxh.astype(BF16), qt, preferred_element_type=F32))
        num, den = ne[:DV_M], ne[DV_M:]
        hq = num / jnp.maximum(jnp.abs(den), jnp.exp(-mt))
        hn = hq * lax.rsqrt(jnp.mean(hq * hq, axis=0, keepdims=True) + RMS_EPS) * gw_ref[sl, :]
        h_ref[sl, tok] = (jax.nn.sigmoid(ot_ref[sl, tok]) * hn).astype(h_ref.dtype)

        b_last = b_row[:, c - 1:c]
        g_row = b_last - b_row + i_row
        m_new = jnp.maximum(b_last + m_prev, jnp.max(g_row, axis=1, keepdims=True))
        ws = jnp.exp(g_row - m_new)
        wc = jnp.exp(b_last + m_prev - m_new)
        vw = (vext.astype(F32) * ws).astype(BF16)
        cx_s[h] = wc * cxh + jnp.dot(vw, kh, preferred_element_type=F32)
        m_s[h] = jnp.broadcast_to(m_new, m_s.shape[1:])


def _outffn_kernel(att_ref, hmt_ref, x_ref, prev_ref, wo_ref, gpost_ref, gpre_ref, wup_ref, cw_ref, cb_ref,
                   wdn_ref, gffn_ref, y_ref, st_ref, u_ref, a_ref, *, tile, shift, hdr):
    i = pl.program_id(0)

    @pl.when(i == 0)
    def _():
        u_ref[hdr - 2 * shift:hdr, :] = prev_ref[...]

    @pl.when(i > 0)
    def _():
        u_ref[0:hdr, :] = u_ref[tile:tile + hdr, :]

    mixed = (jnp.dot(att_ref[...], wo_ref[0:D_A, :], preferred_element_type=F32)
             + lax.dot_general(hmt_ref[...], wo_ref[D_A:, :], (((0,), (0,)), ((), ())),
                               preferred_element_type=F32))
    x1 = x_ref[...] + _rms(mixed, gpost_ref[...])
    h2 = _rms(x1, gpre_ref[...]).astype(BF16)
    u_ref[hdr:hdr + tile, :] = jnp.dot(h2, wup_ref[...], preferred_element_type=F32)

    def conv(c):
        cols = slice(c * LANES, (c + 1) * LANES)
        y = cb_ref[:, cols] + u_ref[hdr - 2 * shift:hdr - 2 * shift + tile, cols] * cw_ref[0:1, cols]
        y = y + u_ref[hdr - shift:hdr - shift + tile, cols] * cw_ref[1:2, cols]
        return y + u_ref[hdr:hdr + tile, cols] * cw_ref[2:3, cols]

    nff = D_FF // LANES
    for c in range(nff):
        gate, val = conv(c), conv(nff + c)
        e = jnp.exp(gate * (GELU_C1 + GELU_C3 * (gate * gate)))
        a_ref[:, c * LANES:(c + 1) * LANES] = (gate * val / (1.0 + e)).astype(BF16)

    y2 = jnp.dot(a_ref[...], wdn_ref[...], preferred_element_type=F32)
    y_ref[...] = x1 + _rms(y2, gffn_ref[...])
    st_ref[...] = u_ref[tile:tile + hdr, :]


def _outffn(att, hmt, x, prev, wo, gpost, gpre, wup, cw, cb, wdn, gffn, *, tile, shift):
    n = x.shape[0]
    assert n % tile == 0 and (shift == 1 or n == tile)
    hdr = max(8, 2 * shift)
    assert hdr % 8 == 0 and (tile % LANES == 0 or tile == n)
    row = lambda w: pl.BlockSpec((tile, w), lambda i: (i, 0))
    return pl.pallas_call(
        functools.partial(_outffn_kernel, tile=tile, shift=shift, hdr=hdr),
        grid=(n // tile,),
        in_specs=[row(D_A), pl.BlockSpec((D_M, tile), lambda i: (0, i)), row(D_MODEL),
                  _const_spec((2 * shift, 2 * D_FF)),
                  _const_spec((D_A + D_M, D_MODEL)), _const_spec((1, D_MODEL)), _const_spec((1, D_MODEL)),
                  _const_spec((D_MODEL, 2 * D_FF)), _const_spec((CONV_W, 2 * D_FF)), _const_spec((1, 2 * D_FF)),
                  _const_spec((D_FF, D_MODEL)), _const_spec((1, D_MODEL))],
        out_specs=[row(D_MODEL), pl.BlockSpec((hdr, 2 * D_FF), lambda i: (0, 0))],
        out_shape=[jax.ShapeDtypeStruct((n, D_MODEL), F32), jax.ShapeDtypeStruct((hdr, 2 * D_FF), F32)],
        scratch_shapes=[pltpu.VMEM((hdr + tile, 2 * D_FF), F32), pltpu.VMEM((tile, D_FF), BF16)],
        compiler_params=_params(("arbitrary",)),
        name="outffn",
    )(att, hmt, x, prev, wo, gpost, gpre, wup, cw, cb, wdn, gffn)


def _t5_bucket(dist):
    max_exact = N_BUCKETS // 2
    d = jnp.maximum(dist, max_exact).astype(F32)
    large = max_exact + (jnp.log(d / max_exact) / math.log(BUCKET_MAX_DIST / max_exact)
                         * (N_BUCKETS - max_exact)).astype(jnp.int32)
    large = jnp.minimum(large, N_BUCKETS - 1)
    return jnp.where(dist < max_exact, dist, large)


def _pattern_bias(rel_bias, dil):
    j = jnp.arange(SPAN + 1)
    return rel_bias[_t5_bucket(j * dil)].T.astype(F32) * LOG2E


def _prompt_bias(rel_bias):
    out = []
    for _, dil in PATTERNS:
        bh = _pattern_bias(rel_bias, dil)
        v = jnp.concatenate([bh[:, ::-1], jnp.full((H_A, SPAN - 1), NEG_INF, F32)], axis=1)
        out.append(v.reshape(H_A // 2, 2, 2 * SPAN))
    return jnp.stack(out)


def _sample_bias(rel_bias, t_new):
    us, ws = [], []
    fill = lambda n: jnp.full((H_A, n), NEG_INF, F32)
    for win, dil in PATTERNS:
        bh = _pattern_bias(rel_bias, dil)
        dist = jnp.arange((SPAN + 1) * dil)
        g = jnp.where(dist % dil == 0, jnp.repeat(bh, dil, axis=1), NEG_INF)[:, :win + 1]
        us.append(jnp.concatenate([g[:, ::-1], fill(LANES - 1)], axis=1))
        ws.append(jnp.concatenate([g[:, :LANES - t_new + 1][:, ::-1], fill(t_new - 1)], axis=1))
    return us, jnp.stack(ws)


def _layer_weights(g_mix_pre, w_in, b_igate, b_fgate, g_mlstm_out, w_out, g_mix_post, g_ffn_pre, w_up,
                   conv_w, conv_b, w_down, g_ffn_post):
    wt = jnp.transpose(w_in).astype(BF16)
    mix = lambda j: wt[N_QKV + j * D_M:N_QKV + (j + 1) * D_M]
    wmt = jnp.concatenate([mix(0), mix(2), mix(3)], axis=0)
    wg = jnp.pad(wt[N_QKV + N_MIX:], ((0, LANES - 2 * H_M), (0, 0)))
    bg = jnp.zeros((1, LANES), F32).at[0, :H_M].set(b_igate.astype(F32)).at[0, H_M:2 * H_M].set(b_fgate.astype(F32))
    row = lambda v: v.astype(F32).reshape(1, -1)
    gw = jnp.broadcast_to(g_mlstm_out.astype(F32)[:, None], (D_M, MLSTM_STEP))
    return dict(
        g_pre=row(g_mix_pre), wa=wt[:N_QKV], wkvt=wt[D_A:N_QKV], wmt=wmt, wk=mix(1),
        wg=wg, bg=bg, gw=gw, wo=w_out.astype(BF16), g_post=row(g_mix_post),
        g_ffn_pre=row(g_ffn_pre), wup=w_up.astype(BF16), cw=conv_w.astype(F32), cb=row(conv_b),
        wdn=w_down.astype(BF16), g_ffn_post=row(g_ffn_post))


def _state_in(c0, n0, m0):
    nb = c0.shape[0]
    return (c0.astype(F32), n0.astype(F32).reshape(nb, H_M, 1, DK_M),
            jnp.broadcast_to(m0.astype(F32)[..., None, None], (nb, H_M, 8, LANES)))


def _state_out(c, n, m):
    return c, n[:, :, 0, :], m[:, :, 0, 0]


def _layer(xp, xs, w, bias_p, bias_u, bias_w, cache_k, cache_v, c0, n0, m0, conv_prev, *, tile_in, tile_ffn):
    s_len = xp.shape[0]
    b, t_new, _ = xs.shape
    n = b * t_new
    wb = cache_k.shape[1]
    proj = (w["g_pre"], w["wa"], w["wmt"], w["wk"], w["wg"], w["bg"])

    wbp = min(max(wd for wd, _ in PATTERNS), s_len)
    dils = tuple(d for _, d in PATTERNS if d > 1)
    qkv1, qkv4, qkv16, kt_p, vt_p, qmt_p, km_p, vmt_p, omt_p, gates_p, gates_tp = _inproj(
        xp, *proj, w["wkvt"], tile=tile_in, dils=dils, tail_rows=wbp)
    att_p = _attn_prompt([qkv1.reshape(1, s_len, N_QKV), qkv4, qkv16], bias_p)

    c = LANES
    _, tail, *sample_m = _inproj(xs.reshape(n, D_MODEL), *proj, tile=n, tail_rows=n, spread=(c, t_new))

    to_t = lambda a: jnp.transpose(a.astype(F32), (0, 2, 3, 1)).reshape(b, D_A, wb)
    from_t = lambda a: jnp.transpose(a.reshape(b, H_A, HD_A, wb), (0, 3, 1, 2))
    state0 = _state_in(jnp.zeros((1, H_M, DK_M, DV_M), F32), jnp.zeros((1, H_M, DK_M), F32),
                       jnp.zeros((1, H_M), F32))
    (hmt_p, c_p, n_p, m_p), (hmt_s, c_o, n_o, m_o), att, k_win, v_win = _mixers(
        (qmt_p, km_p, vmt_p, omt_p, gates_p, gates_tp, w["gw"]) + state0,
        tuple(sample_m) + (w["gw"][:, :c],) + _state_in(c0, n0, m0),
        tail.reshape(b, t_new, N_QKV), to_t(cache_k), to_t(cache_v), bias_u, bias_w, c=MLSTM_STEP, cs=c)

    prev0 = jnp.zeros((CONV_W - 1, 2 * D_FF), F32)
    yp, st_p = _outffn(att_p, hmt_p, xp, prev0, w["wo"], w["g_post"], w["g_ffn_pre"], w["wup"], w["cw"], w["cb"],
                       w["wdn"], w["g_ffn_post"], tile=tile_ffn, shift=1)
    window = lambda a: jnp.transpose(a.reshape(H_A, HD_A, wbp), (2, 0, 1))[None]
    k_win_p, v_win_p = window(kt_p), window(vt_p)
    state_p = (k_win_p, v_win_p) + _state_out(c_p, n_p, m_p) + (st_p[-(CONV_W - 1):][None],)

    tm = lambda a: a.reshape(b, t_new, -1).transpose(1, 0, 2).reshape(n, -1)
    hmt = hmt_s.reshape(D_M, b, c)[:, :, :t_new].transpose(0, 2, 1).reshape(D_M, n)
    prev = conv_prev.astype(F32).transpose(1, 0, 2).reshape((CONV_W - 1) * b, 2 * D_FF)
    ys, st = _outffn(tm(att), hmt, tm(xs), prev, w["wo"], w["g_post"], w["g_ffn_pre"], w["wup"], w["cw"],
                     w["cb"], w["wdn"], w["g_ffn_post"], tile=n, shift=b)
    ys = ys.reshape(t_new, b, D_MODEL).transpose(1, 0, 2)
    conv_state = st.reshape(CONV_W - 1, b, 2 * D_FF).transpose(1, 0, 2)
    state_s = (from_t(k_win), from_t(v_win)) + _state_out(c_o, n_o, m_o) + (conv_state,)
    return yp, ys, state_p, state_s


def kernel(x_prompt, x_sample, cache_attn_k, cache_attn_v, state_mlstm_C, state_mlstm_n, state_mlstm_m,
           state_ffn_conv, rel_bias, g_mix_pre, w_in, b_igate, b_fgate, g_mlstm_out, w_out, g_mix_post,
           g_ffn_pre, w_up, conv_w, conv_b, w_down, g_ffn_post):
    depth = w_in.shape[0]
    batch, s_len, _ = x_prompt.shape
    assert batch == 1
    t_new = x_sample.shape[1]
    bias_p = _prompt_bias(rel_bias)
    bias_u, bias_w = _sample_bias(rel_bias, t_new)
    yp = x_prompt[0]
    ys = x_sample
    new_p, new_s = [], []
    for l in range(depth):
        w = _layer_weights(g_mix_pre[l], w_in[l], b_igate[l], b_fgate[l], g_mlstm_out[l], w_out[l],
                           g_mix_post[l], g_ffn_pre[l], w_up[l], conv_w[l], conv_b[l], w_down[l], g_ffn_post[l])
        yp, ys, sp, ss = _layer(yp, ys, w, bias_p, bias_u, bias_w, cache_attn_k[l], cache_attn_v[l],
                                state_mlstm_C[l], state_mlstm_n[l], state_mlstm_m[l], state_ffn_conv[l],
                                tile_in=256, tile_ffn=512)
        new_p.append(sp)
        new_s.append(ss)
    stack = lambda states, i: jnp.stack([s[i] for s in states])
    return ((yp[None], ys) + tuple(stack(new_p, i) for i in range(6))
            + tuple(stack(new_s, i) for i in range(6)))
```

```python
import functools
import math

import jax
import jax.numpy as jnp
import numpy as np
from jax import lax
from jax.experimental import pallas as pl
from jax.experimental.pallas import tpu as pltpu

F32 = jnp.float32
BF16 = jnp.bfloat16

D_MODEL = 1024
HD_A = 64
H_A = 8
D_A = H_A * HD_A
DK_M = 128
DV_M = 128
H_M = 4
D_M = H_M * DV_M
PATTERNS = ((128, 1), (512, 4), (2048, 16))
SPAN = 128
N_BUCKETS = 32
BUCKET_MAX_DIST = 2048
MLSTM_STEP = 256
D_FF = 2816
CONV_W = 3
RMS_EPS = 1e-6
N_QKV = 3 * D_A
N_MIX = 4 * D_M
LANES = 128
GATE_ROWS = 16
SUPER = 2048
VMEM_LIMIT = 58 * 1024 * 1024
NEG_INF = float("-inf")
LOG2E = math.log2(math.e)
Q_SCALE = HD_A ** -0.5 * LOG2E
GELU_C1 = -2.0 * math.sqrt(2.0 / math.pi)
GELU_C3 = GELU_C1 * 0.044715

for _w, _d in PATTERNS:
    assert _w // _d == SPAN


def _params(sem, vmem=VMEM_LIMIT):
    return pltpu.CompilerParams(dimension_semantics=sem, vmem_limit_bytes=vmem)


def _const_spec(shape):
    nd = len(shape)
    return pl.BlockSpec(shape, lambda *_: (0,) * nd, pipeline_mode=pl.Buffered(1))


def _rms(x, g):
    return x * lax.rsqrt(jnp.mean(x * x, axis=-1, keepdims=True) + RMS_EPS) * g


def _inproj_kernel(x_ref, g_ref, wa_ref, wmt_ref, wk_ref, wg_ref, bg_ref, *refs, tile, dils, first_tail, spread):
    nd = len(dils)
    extra_a, extra_b = refs[0], refs[1]
    qkv1_ref = refs[2]
    dil_refs = refs[3:3 + nd]
    ntail = 2 if nd else 1
    tail_refs = refs[3 + nd:3 + nd + ntail]
    qmt_ref, km_ref, vmt_ref, omt_ref, gt_ref, gtt_ref = refs[3 + nd + ntail:]

    hb = _rms(x_ref[...], g_ref[...]).astype(BF16)
    nt = (((1,), (1,)), ((), ()))
    za = lax.dot_general(hb, wa_ref[...], nt, preferred_element_type=F32)
    za = jnp.concatenate([za[:, :D_A] * Q_SCALE, za[:, D_A:]], axis=1)
    zb = za.astype(BF16)
    qkv1_ref[...] = zb
    if not nd:
        tail_refs[0][...] = za

    if nd:
        zp = jnp.dot(extra_a[...], zb, preferred_element_type=F32).astype(BF16)
        base = 0
        for dil, out_ref in zip(dils, dil_refs):
            for r in range(dil):
                out_ref[r] = zp[base + r * (tile // dil):base + (r + 1) * (tile // dil)]
            base += tile

    zt = lax.dot_general(wmt_ref[...], hb, (((1,), (1,)), ((), ())), preferred_element_type=F32)
    qmt, vmt, omt = zt[0:D_M].astype(BF16), zt[D_M:2 * D_M].astype(BF16), zt[2 * D_M:3 * D_M]
    km = (lax.dot_general(hb, wk_ref[...], nt, preferred_element_type=F32) * (DK_M ** -0.5)).astype(BF16)

    zg = lax.dot_general(hb, wg_ref[...], nt, preferred_element_type=F32) + bg_ref[...]
    logsig = jnp.minimum(zg, 0.0) - jnp.log1p(jnp.exp(-jnp.abs(zg)))
    glane = lax.broadcasted_iota(jnp.int32, zg.shape, 1)
    gates = jnp.where(glane < H_M, zg, logsig)
    gates_t = gates.T[0:GATE_ROWS]

    if spread is None:
        qmt_ref[...], vmt_ref[...], omt_ref[...], km_ref[...] = qmt, vmt, omt, km
        gt_ref[...], gtt_ref[...] = gates, gates_t
    else:
        chunk, valid = spread
        to_cols = lambda a: jnp.dot(a, extra_a[...], preferred_element_type=F32)
        to_rows = lambda a: jnp.dot(extra_b[...], a, preferred_element_type=F32)
        exact = lambda move, a: functools.reduce(jnp.add, [move(part) for part in _split3(a)])
        qmt_ref[...] = to_cols(qmt).astype(BF16)
        vmt_ref[...] = to_cols(vmt).astype(BF16)
        omt_ref[...] = exact(to_cols, omt)
        km_ref[...] = to_rows(km).astype(BF16)
        tok = lax.broadcasted_iota(jnp.int32, gt_ref.shape, 0) % chunk
        ig_lane = lax.broadcasted_iota(jnp.int32, gt_ref.shape, 1) < H_M
        gt_ref[...] = jnp.where(jnp.logical_and(tok >= valid, ig_lane), NEG_INF, exact(to_rows, gates))
        tok_t = lax.broadcasted_iota(jnp.int32, gtt_ref.shape, 1) % chunk
        ig_row = lax.broadcasted_iota(jnp.int32, gtt_ref.shape, 0) < H_M
        gtt_ref[...] = jnp.where(jnp.logical_and(tok_t >= valid, ig_row), NEG_INF, exact(to_cols, gates_t))

    if nd:
        @pl.when(pl.program_id(0) >= first_tail)
        def _():
            kvt = lax.dot_general(extra_b[...], hb, (((1,), (1,)), ((), ())), preferred_element_type=F32)
            tail_refs[0][...] = kvt[0:D_A]
            tail_refs[1][...] = kvt[D_A:2 * D_A]


def _spread_matrix(n, chunk, valid):
    p = np.zeros((n, n // valid * chunk), np.float32)
    tok = np.arange(n)
    p[tok, tok // valid * chunk + tok % valid] = 1.0
    return jnp.asarray(p, BF16)


def _row_permutation(tile, dils):
    p = np.zeros((len(dils) * tile, tile), np.float32)
    for j, dil in enumerate(dils):
        for r in range(dil):
            for m in range(tile // dil):
                p[j * tile + r * (tile // dil) + m, m * dil + r] = 1.0
    return jnp.asarray(p, BF16)


def _inproj(x, g, wa, wmt, wk, wg, bg, wkvt=None, *, tile, dils=(), tail_rows, spread=None):
    n = x.shape[0]
    assert bool(dils) != (spread is not None)
    assert n % tile == 0 and tail_rows % tile == 0 and (tile % LANES == 0 or tile == n)
    for d in dils:
        assert tile % (16 * d) == 0
    steps = n // tile
    first_tail = (n - tail_rows) // tile
    m, mt = n, tile
    if spread:
        chunk, valid = spread
        assert steps == 1 and n % valid == 0 and valid <= chunk
        m = mt = n // valid * chunk
    row = lambda w: pl.BlockSpec((tile, w), lambda i: (i, 0))
    col = lambda h: pl.BlockSpec((h, tile), lambda i: (0, i))
    out_shape = [jax.ShapeDtypeStruct((n, N_QKV), BF16)]
    out_specs = [row(N_QKV)]
    for d in dils:
        out_shape.append(jax.ShapeDtypeStruct((d, n // d, N_QKV), BF16))
        out_specs.append(pl.BlockSpec((d, tile // d, N_QKV), lambda i: (0, i, 0)))
    if dils:
        out_shape += [jax.ShapeDtypeStruct((D_A, tail_rows), F32)] * 2
        out_specs += [pl.BlockSpec((D_A, tile), lambda i: (0, jnp.maximum(i - first_tail, 0)))] * 2
    else:
        out_shape.append(jax.ShapeDtypeStruct((tail_rows, N_QKV), F32))
        out_specs.append(pl.BlockSpec((tile, N_QKV), lambda i: (jnp.maximum(i - first_tail, 0), 0)))
    mrow = lambda w: pl.BlockSpec((mt, w), lambda i: (i, 0))
    mcol = lambda h: pl.BlockSpec((h, mt), lambda i: (0, i))
    out_shape += [jax.ShapeDtypeStruct((D_M, m), BF16), jax.ShapeDtypeStruct((m, D_M), BF16),
                  jax.ShapeDtypeStruct((D_M, m), BF16), jax.ShapeDtypeStruct((D_M, m), F32),
                  jax.ShapeDtypeStruct((m, LANES), F32), jax.ShapeDtypeStruct((GATE_ROWS, m), F32)]
    out_specs += [mcol(D_M), mrow(D_M), mcol(D_M), mcol(D_M), mrow(LANES), mcol(GATE_ROWS)]
    in_specs = [row(D_MODEL), _const_spec((1, D_MODEL)), _const_spec((N_QKV, D_MODEL)),
                _const_spec((3 * D_M, D_MODEL)), _const_spec((D_M, D_MODEL)),
                _const_spec((LANES, D_MODEL)), _const_spec((1, LANES))]
    args = [x, g, wa, wmt, wk, wg, bg]
    if dils:
        in_specs += [_const_spec((len(dils) * tile, tile)), _const_spec((2 * D_A, D_MODEL))]
        args += [_row_permutation(tile, dils), wkvt]
    else:
        sp = _spread_matrix(n, chunk, valid)
        in_specs += [_const_spec((n, m)), _const_spec((m, n))]
        args += [sp, sp.T]
    return pl.pallas_call(
        functools.partial(_inproj_kernel, tile=tile, dils=dils, first_tail=first_tail, spread=spread),
        grid=(steps,),
        in_specs=in_specs,
        out_specs=out_specs,
        out_shape=out_shape,
        compiler_params=_params(("arbitrary",)),
        name="inproj",
    )(*args)


def _attn_prompt_kernel(*refs):
    np_ = len(PATTERNS)
    in_refs = refs[:5 * np_]
    bvec_ref, o_ref = refs[5 * np_], refs[5 * np_ + 1]
    num_s, l_s, m_s, bias_s = refs[5 * np_ + 2:]
    n = pl.program_id(1)

    lane = lax.broadcasted_iota(jnp.int32, (SPAN, LANES), 1)
    even = lane < HD_A
    ones = jnp.ones((2 * SPAN, LANES), BF16)

    @pl.when(n == 0)
    def _():
        key_col = lax.broadcasted_iota(jnp.int32, (SPAN, 2 * SPAN), 1)
        for p in range(np_):
            for e in range(2):
                vb = jnp.broadcast_to(bvec_ref[p, 0, e:e + 1, :], (SPAN, 2 * SPAN))
                table = pltpu.roll(vb, 0, 1, stride=1, stride_axis=0)
                bias_s[p, 0, e * SPAN:(e + 1) * SPAN, :] = table
                bias_s[p, 1, e * SPAN:(e + 1) * SPAN, :] = jnp.where(key_col >= SPAN, table, NEG_INF)

    first_n = (n == 0).astype(jnp.int32)

    def with_prev(p, part, r, mb):
        cur_ref, prev_ref = in_refs[5 * p + part], in_refs[5 * p + part + 2]
        if mb == 0:
            return jnp.concatenate([prev_ref[r], cur_ref[r, 0:SPAN, :]], axis=0)
        return cur_ref[r, (mb - 1) * SPAN:(mb + 1) * SPAN, :]

    def unit(p, j):
        dil = PATTERNS[p][1]
        r, mb = divmod(j, SUPER // dil // SPAN)
        row0 = mb * SPAN
        qb = in_refs[5 * p][r, row0:row0 + SPAN, :]
        kb, vb = with_prev(p, 1, r, mb), with_prev(p, 2, r, mb)
        bias = bias_s[p, first_n] if mb == 0 else bias_s[p, 0]
        zero = jnp.zeros_like(qb)
        qs = jnp.concatenate([jnp.where(even, qb, zero), jnp.where(even, zero, qb)], axis=0)
        s = lax.dot_general(qs, kb, (((1,), (1,)), ((), ())), preferred_element_type=F32) + bias
        m = jnp.max(s, axis=-1, keepdims=True)
        pr = jnp.exp2(s - m).astype(BF16)
        res = jnp.dot(pr, jnp.concatenate([vb, ones], axis=1), preferred_element_type=F32)
        mb_ = jnp.broadcast_to(m, (2 * SPAN, LANES))
        rows = pl.ds(row0, SPAN) if dil == 1 else pl.ds(row0 * dil + r, SPAN, stride=dil)
        num_s[p, rows, :] = jnp.where(even, res[:SPAN, :LANES], res[SPAN:, :LANES])
        l_s[p, rows, :] = jnp.where(even, res[:SPAN, LANES:], res[SPAN:, LANES:])
        m_s[p, rows, :] = jnp.where(even, mb_[:SPAN], mb_[SPAN:])

    def combine(jn):
        rows = slice(jn * SPAN, (jn + 1) * SPAN)
        ms = [m_s[p, rows, :] for p in range(np_)]
        m = functools.reduce(jnp.maximum, ms)
        num = jnp.zeros((SPAN, LANES), F32)
        den = jnp.zeros((SPAN, LANES), F32)
        for p in range(np_):
            a = jnp.exp2(ms[p] - m)
            num = num + a * num_s[p, rows, :]
            den = den + a * l_s[p, rows, :]
        o_ref[rows, :] = (num / den).astype(o_ref.dtype)

    done = set()
    for jn in range(SUPER // SPAN):
        for p in sorted(range(np_), key=lambda p: -PATTERNS[p][1]):
            dil = PATTERNS[p][1]
            nblk = SUPER // dil // SPAN
            for r in range(dil):
                j = r * nblk + jn // dil
                if (p, j) not in done:
                    done.add((p, j))
                    unit(p, j)
        combine(jn)


def _attn_prompt(qkvs, bvec):
    s_len = qkvs[0].shape[1]
    assert s_len % SUPER == 0
    nsb = s_len // SUPER
    npair = D_A // LANES
    in_specs, args, scratch = [], [], []
    for (_, dil), a in zip(PATTERNS, qkvs):
        rows = SUPER // dil
        nblk = rows // SPAN
        for part in range(3):
            in_specs.append(pl.BlockSpec((dil, rows, LANES),
                                         lambda hp, n, part=part: (0, n, part * npair + hp)))
            args.append(a)
        for part in (1, 2):
            in_specs.append(pl.BlockSpec((dil, SPAN, LANES),
                                         lambda hp, n, part=part, nblk=nblk:
                                         (0, jnp.maximum(n * nblk - 1, 0), part * npair + hp)))
            args.append(a)
    in_specs.append(pl.BlockSpec((len(PATTERNS), 1, 2, 2 * SPAN), lambda hp, n: (0, hp, 0, 0)))
    args.append(bvec)
    scratch += [pltpu.VMEM((len(PATTERNS), SUPER, LANES), F32)] * 3
    scratch.append(pltpu.VMEM((len(PATTERNS), 2, 2 * SPAN, 2 * SPAN), F32))
    return pl.pallas_call(
        _attn_prompt_kernel,
        grid=(npair, nsb),
        in_specs=in_specs,
        out_specs=pl.BlockSpec((SUPER, LANES), lambda hp, n: (n, hp)),
        out_shape=jax.ShapeDtypeStruct((s_len, D_A), BF16),
        scratch_shapes=scratch,
        compiler_params=_params(("arbitrary", "arbitrary")),
        name="attn_prompt",
    )(*args)


def _sample_bias_tables(u_refs, w_ref, bc_refs, bn_ref, t_new):
    new_lane = lax.broadcasted_iota(jnp.int32, (t_new, LANES), 1) >= LANES - t_new
    for p, (win, _) in enumerate(PATTERNS):
        for h in range(H_A):
            rows = slice(h * t_new, (h + 1) * t_new)
            u = jnp.broadcast_to(u_refs[p][h:h + 1, :], (t_new, win + LANES))
            bc_refs[p][rows, :] = pltpu.roll(u, 0, 1, stride=1, stride_axis=0)[:, :win]
            w = jnp.broadcast_to(w_ref[p, h:h + 1, :], (t_new, LANES))
            bn_ref[p, rows, :] = jnp.where(new_lane, pltpu.roll(w, 0, 1, stride=1, stride_axis=0), NEG_INF)


def _attn_sample_step(qkv_ref, kc_ref, vc_ref, bc_refs, bn_ref, att_ref, ko_ref, vo_ref, *, t_new, wb):
    np_ = len(PATTERNS)
    qkv = qkv_ref[0]
    qf, kn, vn = qkv[:, 0:D_A], qkv[:, D_A:2 * D_A], qkv[:, 2 * D_A:3 * D_A]
    nrow = H_A * t_new
    row_h = lax.broadcasted_iota(jnp.int32, (nrow, D_A), 0) // t_new
    lane_h = lax.broadcasted_iota(jnp.int32, (nrow, D_A), 1) // HD_A
    own = row_h == lane_h
    qs = jnp.where(own, jnp.concatenate([qf] * H_A, axis=0), 0.0).astype(BF16)
    pad = jnp.zeros((LANES - t_new, D_A), F32)
    knt = jnp.concatenate([pad, kn], axis=0).T
    vnt = jnp.concatenate([pad, vn], axis=0).T
    kct = kc_ref[0]
    vct = vc_ref[0]
    s_c = jnp.dot(qs, kct.astype(BF16), preferred_element_type=F32)
    s_n = jnp.dot(qs, knt.astype(BF16), preferred_element_type=F32)

    wins = [w for w, _ in PATTERNS]
    es_c, es_n = [], []
    m = None
    for p, win in enumerate(wins):
        e_c = s_c[:, wb - win:] + bc_refs[p][...]
        e_n = s_n + bn_ref[p]
        es_c.append(e_c)
        es_n.append(e_n)
        mp = jnp.maximum(jnp.max(e_c, axis=-1, keepdims=True), jnp.max(e_n, axis=-1, keepdims=True))
        m = mp if m is None else jnp.maximum(m, mp)
    pn = None
    for p in range(np_):
        e = jnp.exp2(es_n[p] - m)
        pn = e if pn is None else pn + e
    order = sorted(range(np_), key=lambda p: -wins[p])
    assert wins[order[0]] == wb
    pc = jnp.exp2(es_c[order[0]] - m)
    for p in order[1:]:
        e = jnp.exp2(es_c[p] - m)
        pc = jnp.concatenate([pc[:, :wb - wins[p]], pc[:, wb - wins[p]:] + e], axis=1)
    den = jnp.sum(pc, axis=-1, keepdims=True) + jnp.sum(pn, axis=-1, keepdims=True)
    nt = (((1,), (1,)), ((), ()))
    num = (lax.dot_general(pc.astype(BF16), vct.astype(BF16), nt, preferred_element_type=F32)
           + lax.dot_general(pn.astype(BF16), vnt.astype(BF16), nt, preferred_element_type=F32))
    o = jnp.where(own, num / den, 0.0)
    att = o[0:t_new]
    for h in range(1, H_A):
        att = att + o[h * t_new:(h + 1) * t_new]
    att_ref[0] = att.astype(att_ref.dtype)

    is_new = lax.broadcasted_iota(jnp.int32, (D_A, LANES), 1) >= LANES - t_new
    for src, new, dst in ((kct, knt, ko_ref), (vct, vnt, vo_ref)):
        rolled = pltpu.roll(src, wb - t_new, 1)
        dst[0, :, 0:wb - LANES] = rolled[:, 0:wb - LANES]
        dst[0, :, wb - LANES:wb] = jnp.where(is_new, new, rolled[:, wb - LANES:wb])


def _mixers_kernel(*refs, t_new, wb, c, chunks, cs):
    np_ = len(PATTERNS)
    nm = 10
    p_in, s_in = refs[0:nm], refs[nm:2 * nm]
    qkv_ref, kc_ref, vc_ref = refs[2 * nm:2 * nm + 3]
    u_refs = refs[2 * nm + 3:2 * nm + 3 + np_]
    w_ref = refs[2 * nm + 3 + np_]
    outs = refs[2 * nm + 4 + np_:]
    (ph_ref, pc_out, pn_out, pm_out), (sh_ref, sc_out, sn_out, sm_out) = outs[0:4], outs[4:8]
    att_ref, ko_ref, vo_ref = outs[8:11]
    pcx_s, pm_s, scx_s, sm_s = outs[11:15]
    bc_refs, bn_ref = outs[15:15 + np_], outs[15 + np_]
    i = pl.program_id(0)

    @pl.when(i == 0)
    def _():
        _mlstm_load_state(*p_in[7:10], pcx_s, pm_s)
        _sample_bias_tables(u_refs, w_ref, bc_refs, bn_ref, t_new)

    _attn_sample_step(qkv_ref, kc_ref, vc_ref, bc_refs, bn_ref, att_ref, ko_ref, vo_ref, t_new=t_new, wb=wb)
    _mlstm_load_state(*s_in[7:10], scx_s, sm_s)
    _mlstm_chunk(*s_in[0:7], sh_ref, scx_s, sm_s, c=cs, off=0)
    _mlstm_store_state(sc_out, sn_out, sm_out, scx_s, sm_s)
    for j in range(chunks):
        _mlstm_chunk(*p_in[0:7], ph_ref, pcx_s, pm_s, c=c, off=j * c)

    @pl.when(i == pl.num_programs(0) - 1)
    def _():
        _mlstm_store_state(pc_out, pn_out, pm_out, pcx_s, pm_s)


def _mixers(prompt_m, sample_m, qkv, cache_kt, cache_vt, bias_u, bias_w, *, c, cs):
    n = prompt_m[1].shape[0]
    b, t_new, _ = qkv.shape
    wb = cache_kt.shape[2]
    assert wb == max(w for w, _ in PATTERNS) and t_new % 8 == 0 and t_new <= LANES
    assert n % (b * c) == 0 and sample_m[1].shape[0] == b * cs
    chunks = n // (b * c)
    nrow = H_A * t_new
    blk = lambda r, w: pl.BlockSpec((1, r, w), lambda i: (i, 0, 0))
    fixed = lambda shape: pl.BlockSpec((1,) + shape, lambda i: (0,) * (1 + len(shape)))
    per_b = lambda shape: pl.BlockSpec((1,) + shape, lambda i: (i,) + (0,) * len(shape))

    def group(tokens, chunk, st):
        row = lambda w: pl.BlockSpec((tokens, w), lambda i: (i, 0))
        col = lambda h: pl.BlockSpec((h, tokens), lambda i: (0, i))
        ins = [col(D_M), row(D_M), col(D_M), col(D_M), row(LANES), col(GATE_ROWS), _const_spec((D_M, chunk)),
               st((H_M, DK_M, DV_M)), st((H_M, 1, DK_M)), st((H_M, 8, LANES))]
        outs = [col(D_M), st((H_M, DK_M, DV_M)), st((H_M, 8, DK_M)), st((H_M, 8, LANES))]
        return ins, outs

    def group_shapes(tokens_total, nb):
        return [jax.ShapeDtypeStruct((D_M, tokens_total), BF16), jax.ShapeDtypeStruct((nb, H_M, DK_M, DV_M), F32),
                jax.ShapeDtypeStruct((nb, H_M, 8, DK_M), F32), jax.ShapeDtypeStruct((nb, H_M, 8, LANES), F32)]

    p_ins, p_outs = group(chunks * c, c, fixed)
    s_ins, s_outs = group(cs, cs, per_b)
    state = [pltpu.VMEM((H_M, 2 * DV_M, DK_M), F32), pltpu.VMEM((H_M, 8, LANES), F32)]
    res = pl.pallas_call(
        functools.partial(_mixers_kernel, t_new=t_new, wb=wb, c=c, chunks=chunks, cs=cs),
        grid=(b,),
        in_specs=p_ins + s_ins + [blk(t_new, N_QKV), blk(D_A, wb), blk(D_A, wb)]
                 + [_const_spec((H_A, w + LANES)) for w, _ in PATTERNS]
                 + [_const_spec((len(PATTERNS), H_A, LANES))],
        out_specs=p_outs + s_outs + [blk(t_new, D_A), blk(D_A, wb), blk(D_A, wb)],
        out_shape=group_shapes(n, 1) + group_shapes(b * cs, b)
                  + [jax.ShapeDtypeStruct((b, t_new, D_A), BF16),
                     jax.ShapeDtypeStruct((b, D_A, wb), F32), jax.ShapeDtypeStruct((b, D_A, wb), F32)],
        scratch_shapes=state + state + [pltpu.VMEM((nrow, w), F32) for w, _ in PATTERNS]
                       + [pltpu.VMEM((len(PATTERNS), nrow, LANES), F32)],
        compiler_params=_params(("arbitrary",)),
        name="mixers",
    )(*prompt_m, *sample_m, qkv, cache_kt, cache_vt, *bias_u, bias_w)
    return res[0:4], res[4:8], res[8], res[9], res[10]


def _split3(x):
    a = x.astype(BF16)
    r = x - a.astype(F32)
    b = r.astype(BF16)
    c = (r - b.astype(F32)).astype(BF16)
    return a, b, c


def _mlstm_load_state(c0_ref, n0_ref, m0_ref, cx_s, m_s):
    for h in range(H_M):
        cx_s[h, 0:DV_M, :] = c0_ref[0, h].T
        cx_s[h, DV_M:, :] = jnp.broadcast_to(n0_ref[0, h], (DV_M, DK_M))
    m_s[...] = m0_ref[0]


def _mlstm_store_state(c_out, n_out, m_out, cx_s, m_s):
    for h in range(H_M):
        c_out[0, h] = cx_s[h, 0:DV_M, :].T
        n_out[0, h] = cx_s[h, DV_M:DV_M + 8, :]
    m_out[0] = m_s[...]


def _mlstm_chunk(qt_ref, k_ref, vt_ref, ot_ref, g_ref, gtt_ref, gw_ref, h_ref, cx_s, m_s, *, c, off):
    tok = slice(off, off + c)
    gates = g_ref[tok, :]
    gates_t = gtt_ref[:, tok]
    lane = lax.broadcasted_iota(jnp.int32, gates.shape, 1)
    lf = jnp.where(jnp.logical_and(lane >= H_M, lane < 2 * H_M), gates, 0.0)
    grow = lax.broadcasted_iota(jnp.int32, gates_t.shape, 0)
    lf_t = jnp.where(jnp.logical_and(grow >= H_M, grow < 2 * H_M), gates_t, 0.0)
    ri = lax.broadcasted_iota(jnp.int32, (c, c), 0)
    cj = lax.broadcasted_iota(jnp.int32, (c, c), 1)
    upper = ri <= cj
    tril = (ri >= cj).astype(BF16)
    triu = upper.astype(BF16)
    bsum = brow = None
    for part, part_t in zip(_split3(lf), _split3(lf_t)):
        t1 = jnp.dot(tril, part, preferred_element_type=F32)
        t2 = jnp.dot(part_t, triu, preferred_element_type=F32)
        bsum = t1 if bsum is None else bsum + t1
        brow = t2 if brow is None else brow + t2
    ones = jnp.ones((DV_M, c), BF16)

    for h in range(H_M):
        sl = slice(h * DK_M, (h + 1) * DK_M)
        qt, kh, vt = qt_ref[sl, tok], k_ref[tok, sl], vt_ref[sl, tok]
        b_row = brow[H_M + h:H_M + h + 1, :]
        i_row = gates_t[h:h + 1, :]
        a_col = gates[:, h:h + 1] - bsum[:, H_M + h:H_M + h + 1]
        m_prev = m_s[h, 0:1, 0:1]
        dmat = jnp.where(upper, b_row + a_col, NEG_INF)
        inter = b_row + m_prev
        mt = jnp.maximum(inter, jnp.max(dmat, axis=0, keepdims=True))
        st = jnp.dot(kh, qt, preferred_element_type=F32)
        smt = (st * jnp.exp(dmat - mt)).astype(BF16)
        iw = jnp.exp(inter - mt)
        vext = jnp.concatenate([vt, ones], axis=0)
        cxh = cx_s[h]
        ne = (jnp.dot(vext, smt, preferred_element_type=F32)
              + iw * jnp.dot(cxh.astype(BF16), qt, preferred_element_type=F32))
        num, den = ne[:DV_M], ne[DV_M:]
        hq = num / jnp.maximum(jnp.abs(den), jnp.exp(-mt))
        hn = hq * lax.rsqrt(jnp.mean(hq * hq, axis=0, keepdims=True) + RMS_EPS) * gw_ref[sl, :]
        h_ref[sl, tok] = (jax.nn.sigmoid(ot_ref[sl, tok]) * hn).astype(h_ref.dtype)

        b_last = b_row[:, c - 1:c]
        g_row = b_last - b_row + i_row
        m_new = jnp.maximum(b_last + m_prev, jnp.max(g_row, axis=1, keepdims=True))
        ws = jnp.exp(g_row - m_new)
        wc = jnp.exp(b_last + m_prev - m_new)
        vw = (vext.astype(F32) * ws).astype(BF16)
        cx_s[h] = wc * cxh + jnp.dot(vw, kh, preferred_element_type=F32)
        m_s[h] = jnp.broadcast_to(m_new, m_s.shape[1:])


def _outffn_kernel(att_ref, hmt_ref, x_ref, prev_ref, wo_ref, gpost_ref, gpre_ref, wup_ref, cw_ref, cb_ref,
                   wdn_ref, gffn_ref, y_ref, st_ref, u_ref, a_ref, *, tile, seq, hdr):
    i = pl.program_id(0)

    @pl.when(i == 0)
    def _():
        if seq is None:
            u_ref[hdr - (CONV_W - 1):hdr, :] = prev_ref[...]
        else:
            u_ref[0:hdr, :] = jnp.zeros((hdr, u_ref.shape[1]), F32)

    @pl.when(i > 0)
    def _():
        u_ref[0:hdr, :] = u_ref[tile:tile + hdr, :]

    mixed = (jnp.dot(att_ref[...], wo_ref[0:D_A, :], preferred_element_type=F32)
             + lax.dot_general(hmt_ref[...], wo_ref[D_A:, :], (((0,), (0,)), ((), ())),
                               preferred_element_type=F32))
    x1 = x_ref[...] + _rms(mixed, gpost_ref[...])
    h2 = _rms(x1, gpre_ref[...]).astype(BF16)
    u_ref[hdr:hdr + tile, :] = jnp.dot(h2, wup_ref[...], preferred_element_type=F32)

    pos = None if seq is None else lax.broadcasted_iota(jnp.int32, (tile, LANES), 0) % seq

    def conv(c):
        cols = slice(c * LANES, (c + 1) * LANES)
        y = cb_ref[:, cols]
        for k in range(CONV_W):
            back = CONV_W - 1 - k
            rows = u_ref[hdr - back:hdr - back + tile, cols]
            if seq is not None and back:
                rows = jnp.where(pos >= back, rows, prev_ref[k, :, cols])
            y = y + rows * cw_ref[k:k + 1, cols]
        return y

    nff = D_FF // LANES
    for c in range(nff):
        gate, val = conv(c), conv(nff + c)
        e = jnp.exp(gate * (GELU_C1 + GELU_C3 * (gate * gate)))
        a_ref[:, c * LANES:(c + 1) * LANES] = (gate * val / (1.0 + e)).astype(BF16)

    y2 = jnp.dot(a_ref[...], wdn_ref[...], preferred_element_type=F32)
    y_ref[...] = x1 + _rms(y2, gffn_ref[...])
    st_ref[...] = u_ref[tile:tile + hdr, :] if seq is None else u_ref[hdr:hdr + tile, :]


def _outffn(att, hmt, x, prev, wo, gpost, gpre, wup, cw, cb, wdn, gffn, *, tile, seq=None):
    n = x.shape[0]
    assert n % tile == 0 and (seq is None or (n == tile and n % seq == 0 and seq >= CONV_W - 1))
    hdr = 8
    assert tile % LANES == 0 or tile == n
    row = lambda w: pl.BlockSpec((tile, w), lambda i: (i, 0))
    st_rows = hdr if seq is None else n
    return pl.pallas_call(
        functools.partial(_outffn_kernel, tile=tile, seq=seq, hdr=hdr),
        grid=(n // tile,),
        in_specs=[row(D_A), pl.BlockSpec((D_M, tile), lambda i: (0, i)), row(D_MODEL),
                  _const_spec(prev.shape),
                  _const_spec((D_A + D_M, D_MODEL)), _const_spec((1, D_MODEL)), _const_spec((1, D_MODEL)),
                  _const_spec((D_MODEL, 2 * D_FF)), _const_spec((CONV_W, 2 * D_FF)), _const_spec((1, 2 * D_FF)),
                  _const_spec((D_FF, D_MODEL)), _const_spec((1, D_MODEL))],
        out_specs=[row(D_MODEL), pl.BlockSpec((st_rows, 2 * D_FF), lambda i: (0, 0))],
        out_shape=[jax.ShapeDtypeStruct((n, D_MODEL), F32), jax.ShapeDtypeStruct((st_rows, 2 * D_FF), F32)],
        scratch_shapes=[pltpu.VMEM((hdr + tile, 2 * D_FF), F32), pltpu.VMEM((tile, D_FF), BF16)],
        compiler_params=_params(("arbitrary",)),
        name="outffn",
    )(att, hmt, x, prev, wo, gpost, gpre, wup, cw, cb, wdn, gffn)


def _t5_bucket(dist):
    max_exact = N_BUCKETS // 2
    d = jnp.maximum(dist, max_exact).astype(F32)
    large = max_exact + (jnp.log(d / max_exact) / math.log(BUCKET_MAX_DIST / max_exact)
                         * (N_BUCKETS - max_exact)).astype(jnp.int32)
    large = jnp.minimum(large, N_BUCKETS - 1)
    return jnp.where(dist < max_exact, dist, large)


def _pattern_bias(rel_bias, dil):
    j = jnp.arange(SPAN + 1)
    return rel_bias[_t5_bucket(j * dil)].T.astype(F32) * LOG2E


def _prompt_bias(rel_bias):
    out = []
    for _, dil in PATTERNS:
        bh = _pattern_bias(rel_bias, dil)
        v = jnp.concatenate([bh[:, ::-1], jnp.full((H_A, SPAN - 1), NEG_INF, F32)], axis=1)
        out.append(v.reshape(H_A // 2, 2, 2 * SPAN))
    return jnp.stack(out)


def _sample_bias(rel_bias, t_new):
    us, ws = [], []
    fill = lambda n: jnp.full((H_A, n), NEG_INF, F32)
    for win, dil in PATTERNS:
        bh = _pattern_bias(rel_bias, dil)
        dist = jnp.arange((SPAN + 1) * dil)
        g = jnp.where(dist % dil == 0, jnp.repeat(bh, dil, axis=1), NEG_INF)[:, :win + 1]
        us.append(jnp.concatenate([g[:, ::-1], fill(LANES - 1)], axis=1))
        ws.append(jnp.concatenate([g[:, :LANES - t_new + 1][:, ::-1], fill(t_new - 1)], axis=1))
    return us, jnp.stack(ws)


def _layer_weights(g_mix_pre, w_in, b_igate, b_fgate, g_mlstm_out, w_out, g_mix_post, g_ffn_pre, w_up,
                   conv_w, conv_b, w_down, g_ffn_post):
    wt = jnp.transpose(w_in).astype(BF16)
    mix = lambda j: wt[N_QKV + j * D_M:N_QKV + (j + 1) * D_M]
    wmt = jnp.concatenate([mix(0), mix(2), mix(3)], axis=0)
    wg = jnp.pad(wt[N_QKV + N_MIX:], ((0, LANES - 2 * H_M), (0, 0)))
    bg = jnp.zeros((1, LANES), F32).at[0, :H_M].set(b_igate.astype(F32)).at[0, H_M:2 * H_M].set(b_fgate.astype(F32))
    row = lambda v: v.astype(F32).reshape(1, -1)
    gw = jnp.broadcast_to(g_mlstm_out.astype(F32)[:, None], (D_M, MLSTM_STEP))
    return dict(
        g_pre=row(g_mix_pre), wa=wt[:N_QKV], wkvt=wt[D_A:N_QKV], wmt=wmt, wk=mix(1),
        wg=wg, bg=bg, gw=gw, wo=w_out.astype(BF16), g_post=row(g_mix_post),
        g_ffn_pre=row(g_ffn_pre), wup=w_up.astype(BF16), cw=conv_w.astype(F32), cb=row(conv_b),
        wdn=w_down.astype(BF16), g_ffn_post=row(g_ffn_post))


def _state_in(c0, n0, m0):
    nb = c0.shape[0]
    return (c0.astype(F32), n0.astype(F32).reshape(nb, H_M, 1, DK_M),
            jnp.broadcast_to(m0.astype(F32)[..., None, None], (nb, H_M, 8, LANES)))


def _state_out(c, n, m):
    return c, n[:, :, 0, :], m[:, :, 0, 0]


def _layer(xp, xs, w, bias_p, bias_u, bias_w, cache_k, cache_v, c0, n0, m0, conv_prev, *, tile_in, tile_ffn):
    s_len = xp.shape[0]
    b, t_new, _ = xs.shape
    n = b * t_new
    wb = cache_k.shape[1]
    proj = (w["g_pre"], w["wa"], w["wmt"], w["wk"], w["wg"], w["bg"])

    wbp = min(max(wd for wd, _ in PATTERNS), s_len)
    dils = tuple(d for _, d in PATTERNS if d > 1)
    qkv1, qkv4, qkv16, kt_p, vt_p, qmt_p, km_p, vmt_p, omt_p, gates_p, gates_tp = _inproj(
        xp, *proj, w["wkvt"], tile=tile_in, dils=dils, tail_rows=wbp)
    att_p = _attn_prompt([qkv1.reshape(1, s_len, N_QKV), qkv4, qkv16], bias_p)

    c = LANES
    _, tail, *sample_m = _inproj(xs.reshape(n, D_MODEL), *proj, tile=n, tail_rows=n, spread=(c, t_new))

    to_t = lambda a: jnp.transpose(a.astype(F32), (0, 2, 3, 1)).reshape(b, D_A, wb)
    from_t = lambda a: jnp.transpose(a.reshape(b, H_A, HD_A, wb), (0, 3, 1, 2))
    state0 = _state_in(jnp.zeros((1, H_M, DK_M, DV_M), F32), jnp.zeros((1, H_M, DK_M), F32),
                       jnp.zeros((1, H_M), F32))
    (hmt_p, c_p, n_p, m_p), (hmt_s, c_o, n_o, m_o), att, k_win, v_win = _mixers(
        (qmt_p, km_p, vmt_p, omt_p, gates_p, gates_tp, w["gw"]) + state0,
        tuple(sample_m) + (w["gw"][:, :c],) + _state_in(c0, n0, m0),
        tail.reshape(b, t_new, N_QKV), to_t(cache_k), to_t(cache_v), bias_u, bias_w, c=MLSTM_STEP, cs=c)

    prev0 = jnp.zeros((CONV_W - 1, 2 * D_FF), F32)
    yp, st_p = _outffn(att_p, hmt_p, xp, prev0, w["wo"], w["g_post"], w["g_ffn_pre"], w["wup"], w["cw"], w["cb"],
                       w["wdn"], w["g_ffn_post"], tile=tile_ffn)
    window = lambda a: jnp.transpose(a.reshape(H_A, HD_A, wbp), (2, 0, 1))[None]
    k_win_p, v_win_p = window(kt_p), window(vt_p)
    state_p = (k_win_p, v_win_p) + _state_out(c_p, n_p, m_p) + (st_p[-(CONV_W - 1):][None],)

    hmt = hmt_s.reshape(D_M, b, c)[:, :, :t_new].reshape(D_M, n)
    cp = conv_prev.astype(F32)
    prev = jnp.stack([jnp.pad(cp[:, k:], ((0, 0), (0, t_new - (CONV_W - 1 - k)), (0, 0))).reshape(n, 2 * D_FF)
                      for k in range(CONV_W - 1)])
    ys, st = _outffn(att.reshape(n, D_A), hmt, xs.reshape(n, D_MODEL), prev, w["wo"], w["g_post"], w["g_ffn_pre"],
                     w["wup"], w["cw"], w["cb"], w["wdn"], w["g_ffn_post"], tile=n, seq=t_new)
    ys = ys.reshape(b, t_new, D_MODEL)
    conv_state = st.reshape(b, t_new, 2 * D_FF)[:, t_new - (CONV_W - 1):]
    state_s = (from_t(k_win), from_t(v_win)) + _state_out(c_o, n_o, m_o) + (conv_state,)
    return yp, ys, state_p, state_s


def kernel(x_prompt, x_sample, cache_attn_k, cache_attn_v, state_mlstm_C, state_mlstm_n, state_mlstm_m,
           state_ffn_conv, rel_bias, g_mix_pre, w_in, b_igate, b_fgate, g_mlstm_out, w_out, g_mix_post,
           g_ffn_pre, w_up, conv_w, conv_b, w_down, g_ffn_post):
    depth = w_in.shape[0]
    batch, s_len, _ = x_prompt.shape
    assert batch == 1
    t_new = x_sample.shape[1]
    bias_p = _prompt_bias(rel_bias)
    bias_u, bias_w = _sample_bias(rel_bias, t_new)
    yp = x_prompt[0]
    ys = x_sample
    new_p, new_s = [], []
    for l in range(depth):
        w = _layer_weights(g_mix_pre[l], w_in[l], b_igate[l], b_fgate[l], g_mlstm_out[l], w_out[l],
                           g_mix_post[l], g_ffn_pre[l], w_up[l], conv_w[l], conv_b[l], w_down[l], g_ffn_post[l])
        yp, ys, sp, ss = _layer(yp, ys, w, bias_p, bias_u, bias_w, cache_attn_k[l], cache_attn_v[l],
                                state_mlstm_C[l], state_mlstm_n[l], state_mlstm_m[l], state_ffn_conv[l],
                                tile_in=256, tile_ffn=512)
        new_p.append(sp)
        new_s.append(ss)
    stack = lambda states, i: jnp.stack([s[i] for s in states])
    return ((yp[None], ys) + tuple(stack(new_p, i) for i in range(6))
            + tuple(stack(new_s, i) for i in range(6)))
```

```python
import functools
import math

import jax
import jax.numpy as jnp
import numpy as np
from jax import lax
from jax.experimental import pallas as pl
from jax.experimental.pallas import tpu as pltpu

F32 = jnp.float32
BF16 = jnp.bfloat16

D_MODEL = 1024
HD_A = 64
H_A = 8
D_A = H_A * HD_A
DK_M = 128
DV_M = 128
H_M = 4
D_M = H_M * DV_M
PATTERNS = ((128, 1), (512, 4), (2048, 16))
SPAN = 128
N_BUCKETS = 32
BUCKET_MAX_DIST = 2048
MLSTM_STEP = 256
D_FF = 2816
CONV_W = 3
RMS_EPS = 1e-6
N_QKV = 3 * D_A
N_MIX = 4 * D_M
LANES = 128
GATE_ROWS = 16
SUPER = 2048
CACHE_RING = 3
VMEM_LIMIT = 58 * 1024 * 1024
NEG_INF = float("-inf")
LOG2E = math.log2(math.e)
Q_SCALE = HD_A ** -0.5 * LOG2E
GELU_C1 = -2.0 * math.sqrt(2.0 / math.pi)
GELU_C3 = GELU_C1 * 0.044715

for _w, _d in PATTERNS:
    assert _w // _d == SPAN


def _params(sem, vmem=VMEM_LIMIT):
    return pltpu.CompilerParams(dimension_semantics=sem, vmem_limit_bytes=vmem)


def _const_spec(shape):
    nd = len(shape)
    return pl.BlockSpec(shape, lambda *_: (0,) * nd, pipeline_mode=pl.Buffered(1))


def _rms(x, g):
    return x * lax.rsqrt(jnp.mean(x * x, axis=-1, keepdims=True) + RMS_EPS) * g


def _inproj_kernel(x_ref, g_ref, wa_ref, wmt_ref, wk_ref, wg_ref, bg_ref, *refs, tile, dils, first_tail, spread):
    nd = len(dils)
    extra_a, extra_b = refs[0], refs[1]
    qkv1_ref = refs[2]
    dil_refs = refs[3:3 + nd]
    ntail = 2 if nd else 1
    tail_refs = refs[3 + nd:3 + nd + ntail]
    qmt_ref, km_ref, vmt_ref, omt_ref, gt_ref, gtt_ref = refs[3 + nd + ntail:]

    hb = _rms(x_ref[...], g_ref[...]).astype(BF16)
    nt = (((1,), (1,)), ((), ()))
    za = lax.dot_general(hb, wa_ref[...], nt, preferred_element_type=F32)
    za = jnp.concatenate([za[:, :D_A] * Q_SCALE, za[:, D_A:]], axis=1)
    zb = za.astype(BF16)
    qkv1_ref[...] = zb
    if not nd:
        tail_refs[0][...] = za

    if nd:
        zp = jnp.dot(extra_a[...], zb, preferred_element_type=F32).astype(BF16)
        base = 0
        for dil, out_ref in zip(dils, dil_refs):
            for r in range(dil):
                out_ref[r] = zp[base + r * (tile // dil):base + (r + 1) * (tile // dil)]
            base += tile

    zt = lax.dot_general(wmt_ref[...], hb, (((1,), (1,)), ((), ())), preferred_element_type=F32)
    qmt, vmt, omt = zt[0:D_M].astype(BF16), zt[D_M:2 * D_M].astype(BF16), zt[2 * D_M:3 * D_M]
    km = (lax.dot_general(hb, wk_ref[...], nt, preferred_element_type=F32) * (DK_M ** -0.5)).astype(BF16)

    zg = lax.dot_general(hb, wg_ref[...], nt, preferred_element_type=F32) + bg_ref[...]
    logsig = jnp.minimum(zg, 0.0) - jnp.log1p(jnp.exp(-jnp.abs(zg)))
    glane = lax.broadcasted_iota(jnp.int32, zg.shape, 1)
    gates = jnp.where(glane < H_M, zg, logsig)
    gates_t = gates.T[0:GATE_ROWS]

    if spread is None:
        qmt_ref[...], vmt_ref[...], omt_ref[...], km_ref[...] = qmt, vmt, omt, km
        gt_ref[...], gtt_ref[...] = gates, gates_t
    else:
        chunk, valid = spread
        to_cols = lambda a: jnp.dot(a, extra_a[...], preferred_element_type=F32)
        to_rows = lambda a: jnp.dot(extra_b[...], a, preferred_element_type=F32)
        exact = lambda move, a: functools.reduce(jnp.add, [move(part) for part in _split3(a)])
        qmt_ref[...] = to_cols(qmt).astype(BF16)
        vmt_ref[...] = to_cols(vmt).astype(BF16)
        omt_ref[...] = exact(to_cols, omt)
        km_ref[...] = to_rows(km).astype(BF16)
        tok = lax.broadcasted_iota(jnp.int32, gt_ref.shape, 0) % chunk
        ig_lane = lax.broadcasted_iota(jnp.int32, gt_ref.shape, 1) < H_M
        gt_ref[...] = jnp.where(jnp.logical_and(tok >= valid, ig_lane), NEG_INF, exact(to_rows, gates))
        tok_t = lax.broadcasted_iota(jnp.int32, gtt_ref.shape, 1) % chunk
        ig_row = lax.broadcasted_iota(jnp.int32, gtt_ref.shape, 0) < H_M
        gtt_ref[...] = jnp.where(jnp.logical_and(tok_t >= valid, ig_row), NEG_INF, exact(to_cols, gates_t))

    if nd:
        @pl.when(pl.program_id(0) >= first_tail)
        def _():
            kvt = lax.dot_general(extra_b[...], hb, (((1,), (1,)), ((), ())), preferred_element_type=F32)
            tail_refs[0][...] = kvt[0:D_A]
            tail_refs[1][...] = kvt[D_A:2 * D_A]


def _spread_matrix(n, chunk, valid):
    p = np.zeros((n, n // valid * chunk), np.float32)
    tok = np.arange(n)
    p[tok, tok // valid * chunk + tok % valid] = 1.0
    return jnp.asarray(p, BF16)


def _row_permutation(tile, dils):
    p = np.zeros((len(dils) * tile, tile), np.float32)
    for j, dil in enumerate(dils):
        for r in range(dil):
            for m in range(tile // dil):
                p[j * tile + r * (tile // dil) + m, m * dil + r] = 1.0
    return jnp.asarray(p, BF16)


def _inproj(x, g, wa, wmt, wk, wg, bg, wkvt=None, *, tile, dils=(), tail_rows, spread=None):
    n = x.shape[0]
    assert bool(dils) != (spread is not None)
    assert n % tile == 0 and tail_rows % tile == 0 and (tile % LANES == 0 or tile == n)
    for d in dils:
        assert tile % (16 * d) == 0
    steps = n // tile
    first_tail = (n - tail_rows) // tile
    m, mt = n, tile
    if spread:
        chunk, valid = spread
        assert steps == 1 and n % valid == 0 and valid <= chunk
        m = mt = n // valid * chunk
    row = lambda w: pl.BlockSpec((tile, w), lambda i: (i, 0))
    col = lambda h: pl.BlockSpec((h, tile), lambda i: (0, i))
    out_shape = [jax.ShapeDtypeStruct((n, N_QKV), BF16)]
    out_specs = [row(N_QKV)]
    for d in dils:
        out_shape.append(jax.ShapeDtypeStruct((d, n // d, N_QKV), BF16))
        out_specs.append(pl.BlockSpec((d, tile // d, N_QKV), lambda i: (0, i, 0)))
    if dils:
        out_shape += [jax.ShapeDtypeStruct((D_A, tail_rows), F32)] * 2
        out_specs += [pl.BlockSpec((D_A, tile), lambda i: (0, jnp.maximum(i - first_tail, 0)))] * 2
    else:
        out_shape.append(jax.ShapeDtypeStruct((tail_rows, N_QKV), F32))
        out_specs.append(pl.BlockSpec((tile, N_QKV), lambda i: (jnp.maximum(i - first_tail, 0), 0)))
    mrow = lambda w: pl.BlockSpec((mt, w), lambda i: (i, 0))
    mcol = lambda h: pl.BlockSpec((h, mt), lambda i: (0, i))
    out_shape += [jax.ShapeDtypeStruct((D_M, m), BF16), jax.ShapeDtypeStruct((m, D_M), BF16),
                  jax.ShapeDtypeStruct((D_M, m), BF16), jax.ShapeDtypeStruct((D_M, m), F32),
                  jax.ShapeDtypeStruct((m, LANES), F32), jax.ShapeDtypeStruct((GATE_ROWS, m), F32)]
    out_specs += [mcol(D_M), mrow(D_M), mcol(D_M), mcol(D_M), mrow(LANES), mcol(GATE_ROWS)]
    in_specs = [row(D_MODEL), _const_spec((1, D_MODEL)), _const_spec((N_QKV, D_MODEL)),
                _const_spec((3 * D_M, D_MODEL)), _const_spec((D_M, D_MODEL)),
                _const_spec((LANES, D_MODEL)), _const_spec((1, LANES))]
    args = [x, g, wa, wmt, wk, wg, bg]
    if dils:
        in_specs += [_const_spec((len(dils) * tile, tile)), _const_spec((2 * D_A, D_MODEL))]
        args += [_row_permutation(tile, dils), wkvt]
    else:
        sp = _spread_matrix(n, chunk, valid)
        in_specs += [_const_spec((n, m)), _const_spec((m, n))]
        args += [sp, sp.T]
    return pl.pallas_call(
        functools.partial(_inproj_kernel, tile=tile, dils=dils, first_tail=first_tail, spread=spread),
        grid=(steps,),
        in_specs=in_specs,
        out_specs=out_specs,
        out_shape=out_shape,
        compiler_params=_params(("arbitrary",)),
        name="inproj",
    )(*args)


def _attn_prompt_kernel(*refs):
    np_ = len(PATTERNS)
    in_refs = refs[:5 * np_]
    bvec_ref, o_ref = refs[5 * np_], refs[5 * np_ + 1]
    num_s, l_s, m_s, bias_s = refs[5 * np_ + 2:]
    n = pl.program_id(1)

    lane = lax.broadcasted_iota(jnp.int32, (SPAN, LANES), 1)
    even = lane < HD_A
    ones = jnp.ones((2 * SPAN, LANES), BF16)

    @pl.when(n == 0)
    def _():
        key_col = lax.broadcasted_iota(jnp.int32, (SPAN, 2 * SPAN), 1)
        for p in range(np_):
            for e in range(2):
                vb = jnp.broadcast_to(bvec_ref[p, 0, e:e + 1, :], (SPAN, 2 * SPAN))
                table = pltpu.roll(vb, 0, 1, stride=1, stride_axis=0)
                bias_s[p, 0, e * SPAN:(e + 1) * SPAN, :] = table
                bias_s[p, 1, e * SPAN:(e + 1) * SPAN, :] = jnp.where(key_col >= SPAN, table, NEG_INF)

    first_n = (n == 0).astype(jnp.int32)

    def with_prev(p, part, r, mb):
        cur_ref, prev_ref = in_refs[5 * p + part], in_refs[5 * p + part + 2]
        if mb == 0:
            return jnp.concatenate([prev_ref[r], cur_ref[r, 0:SPAN, :]], axis=0)
        return cur_ref[r, (mb - 1) * SPAN:(mb + 1) * SPAN, :]

    def unit(p, j):
        dil = PATTERNS[p][1]
        r, mb = divmod(j, SUPER // dil // SPAN)
        row0 = mb * SPAN
        qb = in_refs[5 * p][r, row0:row0 + SPAN, :]
        kb, vb = with_prev(p, 1, r, mb), with_prev(p, 2, r, mb)
        bias = bias_s[p, first_n] if mb == 0 else bias_s[p, 0]
        zero = jnp.zeros_like(qb)
        qs = jnp.concatenate([jnp.where(even, qb, zero), jnp.where(even, zero, qb)], axis=0)
        s = lax.dot_general(qs, kb, (((1,), (1,)), ((), ())), preferred_element_type=F32) + bias
        m = jnp.max(s, axis=-1, keepdims=True)
        pr = jnp.exp2(s - m).astype(BF16)
        res = jnp.dot(pr, jnp.concatenate([vb, ones], axis=1), preferred_element_type=F32)
        mb_ = jnp.broadcast_to(m, (2 * SPAN, LANES))
        rows = pl.ds(row0, SPAN) if dil == 1 else pl.ds(row0 * dil + r, SPAN, stride=dil)
        num_s[p, rows, :] = jnp.where(even, res[:SPAN, :LANES], res[SPAN:, :LANES])
        l_s[p, rows, :] = jnp.where(even, res[:SPAN, LANES:], res[SPAN:, LANES:])
        m_s[p, rows, :] = jnp.where(even, mb_[:SPAN], mb_[SPAN:])

    def combine(jn):
        rows = slice(jn * SPAN, (jn + 1) * SPAN)
        ms = [m_s[p, rows, :] for p in range(np_)]
        m = functools.reduce(jnp.maximum, ms)
        num = jnp.zeros((SPAN, LANES), F32)
        den = jnp.zeros((SPAN, LANES), F32)
        for p in range(np_):
            a = jnp.exp2(ms[p] - m)
            num = num + a * num_s[p, rows, :]
            den = den + a * l_s[p, rows, :]
        o_ref[rows, :] = (num / den).astype(o_ref.dtype)

    done = set()
    for jn in range(SUPER // SPAN):
        for p in sorted(range(np_), key=lambda p: -PATTERNS[p][1]):
            dil = PATTERNS[p][1]
            nblk = SUPER // dil // SPAN
            for r in range(dil):
                j = r * nblk + jn // dil
                if (p, j) not in done:
                    done.add((p, j))
                    unit(p, j)
        combine(jn)


def _attn_prompt(qkvs, bvec):
    s_len = qkvs[0].shape[1]
    assert s_len % SUPER == 0
    nsb = s_len // SUPER
    npair = D_A // LANES
    in_specs, args, scratch = [], [], []
    for (_, dil), a in zip(PATTERNS, qkvs):
        rows = SUPER // dil
        nblk = rows // SPAN
        for part in range(3):
            in_specs.append(pl.BlockSpec((dil, rows, LANES),
                                         lambda hp, n, part=part: (0, n, part * npair + hp)))
            args.append(a)
        for part in (1, 2):
            in_specs.append(pl.BlockSpec((dil, SPAN, LANES),
                                         lambda hp, n, part=part, nblk=nblk:
                                         (0, jnp.maximum(n * nblk - 1, 0), part * npair + hp)))
            args.append(a)
    in_specs.append(pl.BlockSpec((len(PATTERNS), 1, 2, 2 * SPAN), lambda hp, n: (0, hp, 0, 0)))
    args.append(bvec)
    scratch += [pltpu.VMEM((len(PATTERNS), SUPER, LANES), F32)] * 3
    scratch.append(pltpu.VMEM((len(PATTERNS), 2, 2 * SPAN, 2 * SPAN), F32))
    return pl.pallas_call(
        _attn_prompt_kernel,
        grid=(npair, nsb),
        in_specs=in_specs,
        out_specs=pl.BlockSpec((SUPER, LANES), lambda hp, n: (n, hp)),
        out_shape=jax.ShapeDtypeStruct((s_len, D_A), BF16),
        scratch_shapes=scratch,
        compiler_params=_params(("arbitrary", "arbitrary")),
        name="attn_prompt",
    )(*args)


def _sample_bias_tables(u_refs, w_ref, bc_refs, bn_ref, t_new):
    new_lane = lax.broadcasted_iota(jnp.int32, (t_new, LANES), 1) >= LANES - t_new
    for p, (win, _) in enumerate(PATTERNS):
        for h in range(H_A):
            rows = slice(h * t_new, (h + 1) * t_new)
            u = jnp.broadcast_to(u_refs[p][h:h + 1, :], (t_new, win + LANES))
            bc_refs[p][rows, :] = pltpu.roll(u, 0, 1, stride=1, stride_axis=0)[:, :win]
            w = jnp.broadcast_to(w_ref[p, h:h + 1, :], (t_new, LANES))
            bn_ref[p, rows, :] = jnp.where(new_lane, pltpu.roll(w, 0, 1, stride=1, stride_axis=0), NEG_INF)


def _attn_sample_step(qkv_ref, kc_ref, vc_ref, bc_refs, bn_ref, att_ref, ko_ref, vo_ref, *, t_new, wb):
    np_ = len(PATTERNS)
    qkv = qkv_ref[0]
    qf, kn, vn = qkv[:, 0:D_A], qkv[:, D_A:2 * D_A], qkv[:, 2 * D_A:3 * D_A]
    nrow = H_A * t_new
    row_h = lax.broadcasted_iota(jnp.int32, (nrow, D_A), 0) // t_new
    lane_h = lax.broadcasted_iota(jnp.int32, (nrow, D_A), 1) // HD_A
    own = row_h == lane_h
    qs = jnp.where(own, jnp.concatenate([qf] * H_A, axis=0), 0.0).astype(BF16)
    pad = jnp.zeros((LANES - t_new, D_A), F32)
    knt = jnp.concatenate([pad, kn], axis=0).T
    vnt = jnp.concatenate([pad, vn], axis=0).T
    kct = kc_ref[0]
    vct = vc_ref[0]
    s_c = jnp.dot(qs, kct.astype(BF16), preferred_element_type=F32)
    s_n = jnp.dot(qs, knt.astype(BF16), preferred_element_type=F32)

    wins = [w for w, _ in PATTERNS]
    es_c, es_n = [], []
    m = None
    for p, win in enumerate(wins):
        e_c = s_c[:, wb - win:] + bc_refs[p][...]
        e_n = s_n + bn_ref[p]
        es_c.append(e_c)
        es_n.append(e_n)
        mp = jnp.maximum(jnp.max(e_c, axis=-1, keepdims=True), jnp.max(e_n, axis=-1, keepdims=True))
        m = mp if m is None else jnp.maximum(m, mp)
    pn = None
    for p in range(np_):
        e = jnp.exp2(es_n[p] - m)
        pn = e if pn is None else pn + e
    order = sorted(range(np_), key=lambda p: -wins[p])
    assert wins[order[0]] == wb
    pc = jnp.exp2(es_c[order[0]] - m)
    for p in order[1:]:
        e = jnp.exp2(es_c[p] - m)
        pc = jnp.concatenate([pc[:, :wb - wins[p]], pc[:, wb - wins[p]:] + e], axis=1)
    den = jnp.sum(pc, axis=-1, keepdims=True) + jnp.sum(pn, axis=-1, keepdims=True)
    nt = (((1,), (1,)), ((), ()))
    num = (lax.dot_general(pc.astype(BF16), vct.astype(BF16), nt, preferred_element_type=F32)
           + lax.dot_general(pn.astype(BF16), vnt.astype(BF16), nt, preferred_element_type=F32))
    o = jnp.where(own, num / den, 0.0)
    att = o[0:t_new]
    for h in range(1, H_A):
        att = att + o[h * t_new:(h + 1) * t_new]
    att_ref[0] = att.astype(att_ref.dtype)

    is_new = lax.broadcasted_iota(jnp.int32, (D_A, LANES), 1) >= LANES - t_new
    for src, new, dst in ((kct, knt, ko_ref), (vct, vnt, vo_ref)):
        rolled = pltpu.roll(src, wb - t_new, 1)
        dst[0, :, 0:wb - LANES] = rolled[:, 0:wb - LANES]
        dst[0, :, wb - LANES:wb] = jnp.where(is_new, new, rolled[:, wb - LANES:wb])


def _mixers_kernel(*refs, t_new, wb, c, chunks, cs):
    np_ = len(PATTERNS)
    nm = 10
    p_in, s_in = refs[0:nm], refs[nm:2 * nm]
    qkv_ref, kc_hbm, vc_hbm = refs[2 * nm:2 * nm + 3]
    u_refs = refs[2 * nm + 3:2 * nm + 3 + np_]
    w_ref = refs[2 * nm + 3 + np_]
    outs = refs[2 * nm + 4 + np_:]
    (ph_ref, pc_out, pn_out, pm_out), (sh_ref, sc_out, sn_out, sm_out) = outs[0:4], outs[4:8]
    att_ref, ko_ref, vo_ref = outs[8:11]
    pcx_s, pm_s, scx_s, sm_s = outs[11:15]
    bc_refs, bn_ref = outs[15:15 + np_], outs[15 + np_]
    kbuf, vbuf, sems = outs[16 + np_:19 + np_]
    i = pl.program_id(0)
    nb = pl.num_programs(0)

    def cache_copies(row, slot):
        return [pltpu.make_async_copy(hbm.at[row], buf.at[slot], sems.at[which, slot])
                for which, (hbm, buf) in enumerate(((kc_hbm, kbuf), (vc_hbm, vbuf)))]

    @pl.when(i == 0)
    def _():
        for row in range(CACHE_RING):
            for cp in cache_copies(row, row):
                cp.start()
        _mlstm_load_state(*p_in[7:10], pcx_s, pm_s)
        _sample_bias_tables(u_refs, w_ref, bc_refs, bn_ref, t_new)

    slot = i % CACHE_RING
    for cp in cache_copies(i, slot):
        cp.wait()
    _attn_sample_step(qkv_ref, kbuf.at[pl.ds(slot, 1)], vbuf.at[pl.ds(slot, 1)], bc_refs, bn_ref, att_ref,
                      ko_ref, vo_ref, t_new=t_new, wb=wb)
    _mlstm_load_state(*s_in[7:10], scx_s, sm_s)
    _mlstm_chunk(*s_in[0:7], sh_ref, scx_s, sm_s, c=cs, off=0)
    _mlstm_store_state(sc_out, sn_out, sm_out, scx_s, sm_s)
    for j in range(chunks):
        _mlstm_chunk(*p_in[0:7], ph_ref, pcx_s, pm_s, c=c, off=j * c)

    @pl.when(i + CACHE_RING < nb)
    def _():
        for cp in cache_copies(i + CACHE_RING, slot):
            cp.start()

    @pl.when(i == nb - 1)
    def _():
        _mlstm_store_state(pc_out, pn_out, pm_out, pcx_s, pm_s)


def _mixers(prompt_m, sample_m, qkv, cache_kt, cache_vt, bias_u, bias_w, *, c, cs):
    n = prompt_m[1].shape[0]
    b, t_new, _ = qkv.shape
    wb = cache_kt.shape[2]
    assert wb == max(w for w, _ in PATTERNS) and t_new % 8 == 0 and t_new <= LANES
    assert n % (b * c) == 0 and sample_m[1].shape[0] == b * cs and b >= CACHE_RING
    chunks = n // (b * c)
    nrow = H_A * t_new
    blk = lambda r, w: pl.BlockSpec((1, r, w), lambda i: (i, 0, 0))
    fixed = lambda shape: pl.BlockSpec((1,) + shape, lambda i: (0,) * (1 + len(shape)))
    per_b = lambda shape: pl.BlockSpec((1,) + shape, lambda i: (i,) + (0,) * len(shape))

    def group(tokens, chunk, st):
        row = lambda w: pl.BlockSpec((tokens, w), lambda i: (i, 0))
        col = lambda h: pl.BlockSpec((h, tokens), lambda i: (0, i))
        ins = [col(D_M), row(D_M), col(D_M), col(D_M), row(LANES), col(GATE_ROWS), _const_spec((D_M, chunk)),
               st((H_M, DK_M, DV_M)), st((H_M, 1, DK_M)), st((H_M, 8, LANES))]
        outs = [col(D_M), st((H_M, DK_M, DV_M)), st((H_M, 8, DK_M)), st((H_M, 8, LANES))]
        return ins, outs

    def group_shapes(tokens_total, nb):
        return [jax.ShapeDtypeStruct((D_M, tokens_total), BF16), jax.ShapeDtypeStruct((nb, H_M, DK_M, DV_M), F32),
                jax.ShapeDtypeStruct((nb, H_M, 8, DK_M), F32), jax.ShapeDtypeStruct((nb, H_M, 8, LANES), F32)]

    p_ins, p_outs = group(chunks * c, c, fixed)
    s_ins, s_outs = group(cs, cs, per_b)
    state = [pltpu.VMEM((H_M, 2 * DV_M, DK_M), F32), pltpu.VMEM((H_M, 8, LANES), F32)]
    res = pl.pallas_call(
        functools.partial(_mixers_kernel, t_new=t_new, wb=wb, c=c, chunks=chunks, cs=cs),
        grid=(b,),
        in_specs=p_ins + s_ins + [blk(t_new, N_QKV), pl.BlockSpec(memory_space=pl.ANY),
                                  pl.BlockSpec(memory_space=pl.ANY)]
                 + [_const_spec((H_A, w + LANES)) for w, _ in PATTERNS]
                 + [_const_spec((len(PATTERNS), H_A, LANES))],
        out_specs=p_outs + s_outs + [blk(t_new, D_A), blk(D_A, wb), blk(D_A, wb)],
        out_shape=group_shapes(n, 1) + group_shapes(b * cs, b)
                  + [jax.ShapeDtypeStruct((b, t_new, D_A), BF16),
                     jax.ShapeDtypeStruct((b, D_A, wb), F32), jax.ShapeDtypeStruct((b, D_A, wb), F32)],
        scratch_shapes=state + state + [pltpu.VMEM((nrow, w), F32) for w, _ in PATTERNS]
                       + [pltpu.VMEM((len(PATTERNS), nrow, LANES), F32)]
                       + [pltpu.VMEM((CACHE_RING, D_A, wb), F32)] * 2 + [pltpu.SemaphoreType.DMA((2, CACHE_RING))],
        compiler_params=_params(("arbitrary",)),
        name="mixers",
    )(*prompt_m, *sample_m, qkv, cache_kt, cache_vt, *bias_u, bias_w)
    return res[0:4], res[4:8], res[8], res[9], res[10]


def _split3(x):
    a = x.astype(BF16)
    r = x - a.astype(F32)
    b = r.astype(BF16)
    c = (r - b.astype(F32)).astype(BF16)
    return a, b, c


def _mlstm_load_state(c0_ref, n0_ref, m0_ref, cx_s, m_s):
    for h in range(H_M):
        cx_s[h, 0:DV_M, :] = c0_ref[0, h].T
        cx_s[h, DV_M:, :] = jnp.broadcast_to(n0_ref[0, h], (DV_M, DK_M))
    m_s[...] = m0_ref[0]


def _mlstm_store_state(c_out, n_out, m_out, cx_s, m_s):
    for h in range(H_M):
        c_out[0, h] = cx_s[h, 0:DV_M, :].T
        n_out[0, h] = cx_s[h, DV_M:DV_M + 8, :]
    m_out[0] = m_s[...]


def _mlstm_chunk(qt_ref, k_ref, vt_ref, ot_ref, g_ref, gtt_ref, gw_ref, h_ref, cx_s, m_s, *, c, off):
    tok = slice(off, off + c)
    gates = g_ref[tok, :]
    gates_t = gtt_ref[:, tok]
    lane = lax.broadcasted_iota(jnp.int32, gates.shape, 1)
    lf = jnp.where(jnp.logical_and(lane >= H_M, lane < 2 * H_M), gates, 0.0)
    grow = lax.broadcasted_iota(jnp.int32, gates_t.shape, 0)
    lf_t = jnp.where(jnp.logical_and(grow >= H_M, grow < 2 * H_M), gates_t, 0.0)
    ri = lax.broadcasted_iota(jnp.int32, (c, c), 0)
    cj = lax.broadcasted_iota(jnp.int32, (c, c), 1)
    upper = ri <= cj
    tril = (ri >= cj).astype(BF16)
    triu = upper.astype(BF16)
    bsum = brow = None
    for part, part_t in zip(_split3(lf), _split3(lf_t)):
        t1 = jnp.dot(tril, part, preferred_element_type=F32)
        t2 = jnp.dot(part_t, triu, preferred_element_type=F32)
        bsum = t1 if bsum is None else bsum + t1
        brow = t2 if brow is None else brow + t2
    ones = jnp.ones((DV_M, c), BF16)

    for h in range(H_M):
        sl = slice(h * DK_M, (h + 1) * DK_M)
        qt, kh, vt = qt_ref[sl, tok], k_ref[tok, sl], vt_ref[sl, tok]
        b_row = brow[H_M + h:H_M + h + 1, :]
        i_row = gates_t[h:h + 1, :]
        a_col = gates[:, h:h + 1] - bsum[:, H_M + h:H_M + h + 1]
        m_prev = m_s[h, 0:1, 0:1]
        dmat = jnp.where(upper, b_row + a_col, NEG_INF)
        inter = b_row + m_prev
        mt = jnp.maximum(inter, jnp.max(dmat, axis=0, keepdims=True))
        st = jnp.dot(kh, qt, preferred_element_type=F32)
        smt = (st * jnp.exp(dmat - mt)).astype(BF16)
        iw = jnp.exp(inter - mt)
        vext = jnp.concatenate([vt, ones], axis=0)
        cxh = cx_s[h]
        ne = (jnp.dot(vext, smt, preferred_element_type=F32)
              + iw * jnp.dot(cxh.astype(BF16), qt, preferred_element_type=F32))
        num, den = ne[:DV_M], ne[DV_M:]
        hq = num / jnp.maximum(jnp.abs(den), jnp.exp(-mt))
        hn = hq * lax.rsqrt(jnp.mean(hq * hq, axis=0, keepdims=True) + RMS_EPS) * gw_ref[sl, :]
        h_ref[sl, tok] = (jax.nn.sigmoid(ot_ref[sl, tok]) * hn).astype(h_ref.dtype)

        b_last = b_row[:, c - 1:c]
        g_row = b_last - b_row + i_row
        m_new = jnp.maximum(b_last + m_prev, jnp.max(g_row, axis=1, keepdims=True))
        ws = jnp.exp(g_row - m_new)
        wc = jnp.exp(b_last + m_prev - m_new)
        vw = (vext.astype(F32) * ws).astype(BF16)
        cx_s[h] = wc * cxh + jnp.dot(vw, kh, preferred_element_type=F32)
        m_s[h] = jnp.broadcast_to(m_new, m_s.shape[1:])


def _outffn_kernel(att_ref, hmt_ref, x_ref, prev_ref, wo_ref, gpost_ref, gpre_ref, wup_ref, cw_ref, cb_ref,
                   wdn_ref, gffn_ref, y_ref, st_ref, u_ref, a_ref, *, tile, shift, hdr):
    i = pl.program_id(0)

    @pl.when(i == 0)
    def _():
        u_ref[hdr - 2 * shift:hdr, :] = prev_ref[...]

    @pl.when(i > 0)
    def _():
        u_ref[0:hdr, :] = u_ref[tile:tile + hdr, :]

    mixed = (jnp.dot(att_ref[...], wo_ref[0:D_A, :], preferred_element_type=F32)
             + lax.dot_general(hmt_ref[...], wo_ref[D_A:, :], (((0,), (0,)), ((), ())),
                               preferred_element_type=F32))
    x1 = x_ref[...] + _rms(mixed, gpost_ref[...])
    h2 = _rms(x1, gpre_ref[...]).astype(BF16)
    u_ref[hdr:hdr + tile, :] = jnp.dot(h2, wup_ref[...], preferred_element_type=F32)

    def conv(c):
        cols = slice(c * LANES, (c + 1) * LANES)
        y = cb_ref[:, cols] + u_ref[hdr - 2 * shift:hdr - 2 * shift + tile, cols] * cw_ref[0:1, cols]
        y = y + u_ref[hdr - shift:hdr - shift + tile, cols] * cw_ref[1:2, cols]
        return y + u_ref[hdr:hdr + tile, cols] * cw_ref[2:3, cols]

    nff = D_FF // LANES
    for c in range(nff):
        gate, val = conv(c), conv(nff + c)
        e = jnp.exp(gate * (GELU_C1 + GELU_C3 * (gate * gate)))
        a_ref[:, c * LANES:(c + 1) * LANES] = (gate * val / (1.0 + e)).astype(BF16)

    y2 = jnp.dot(a_ref[...], wdn_ref[...], preferred_element_type=F32)
    y_ref[...] = x1 + _rms(y2, gffn_ref[...])
    st_ref[...] = u_ref[tile:tile + hdr, :]


def _outffn(att, hmt, x, prev, wo, gpost, gpre, wup, cw, cb, wdn, gffn, *, tile, shift):
    n = x.shape[0]
    assert n % tile == 0 and (shift == 1 or n == tile)
    hdr = max(8, 2 * shift)
    assert hdr % 8 == 0 and (tile % LANES == 0 or tile == n)
    row = lambda w: pl.BlockSpec((tile, w), lambda i: (i, 0))
    return pl.pallas_call(
        functools.partial(_outffn_kernel, tile=tile, shift=shift, hdr=hdr),
        grid=(n // tile,),
        in_specs=[row(D_A), pl.BlockSpec((D_M, tile), lambda i: (0, i)), row(D_MODEL),
                  _const_spec((2 * shift, 2 * D_FF)),
                  _const_spec((D_A + D_M, D_MODEL)), _const_spec((1, D_MODEL)), _const_spec((1, D_MODEL)),
                  _const_spec((D_MODEL, 2 * D_FF)), _const_spec((CONV_W, 2 * D_FF)), _const_spec((1, 2 * D_FF)),
                  _const_spec((D_FF, D_MODEL)), _const_spec((1, D_MODEL))],
        out_specs=[row(D_MODEL), pl.BlockSpec((hdr, 2 * D_FF), lambda i: (0, 0))],
        out_shape=[jax.ShapeDtypeStruct((n, D_MODEL), F32), jax.ShapeDtypeStruct((hdr, 2 * D_FF), F32)],
        scratch_shapes=[pltpu.VMEM((hdr + tile, 2 * D_FF), F32), pltpu.VMEM((tile, D_FF), BF16)],
        compiler_params=_params(("arbitrary",)),
        name="outffn",
    )(att, hmt, x, prev, wo, gpost, gpre, wup, cw, cb, wdn, gffn)


def _t5_bucket(dist):
    max_exact = N_BUCKETS // 2
    d = jnp.maximum(dist, max_exact).astype(F32)
    large = max_exact + (jnp.log(d / max_exact) / math.log(BUCKET_MAX_DIST / max_exact)
                         * (N_BUCKETS - max_exact)).astype(jnp.int32)
    large = jnp.minimum(large, N_BUCKETS - 1)
    return jnp.where(dist < max_exact, dist, large)


def _pattern_bias(rel_bias, dil):
    j = jnp.arange(SPAN + 1)
    return rel_bias[_t5_bucket(j * dil)].T.astype(F32) * LOG2E


def _prompt_bias(rel_bias):
    out = []
    for _, dil in PATTERNS:
        bh = _pattern_bias(rel_bias, dil)
        v = jnp.concatenate([bh[:, ::-1], jnp.full((H_A, SPAN - 1), NEG_INF, F32)], axis=1)
        out.append(v.reshape(H_A // 2, 2, 2 * SPAN))
    return jnp.stack(out)


def _sample_bias(rel_bias, t_new):
    us, ws = [], []
    fill = lambda n: jnp.full((H_A, n), NEG_INF, F32)
    for win, dil in PATTERNS:
        bh = _pattern_bias(rel_bias, dil)
        dist = jnp.arange((SPAN + 1) * dil)
        g = jnp.where(dist % dil == 0, jnp.repeat(bh, dil, axis=1), NEG_INF)[:, :win + 1]
        us.append(jnp.concatenate([g[:, ::-1], fill(LANES - 1)], axis=1))
        ws.append(jnp.concatenate([g[:, :LANES - t_new + 1][:, ::-1], fill(t_new - 1)], axis=1))
    return us, jnp.stack(ws)


def _layer_weights(g_mix_pre, w_in, b_igate, b_fgate, g_mlstm_out, w_out, g_mix_post, g_ffn_pre, w_up,
                   conv_w, conv_b, w_down, g_ffn_post):
    wt = jnp.transpose(w_in).astype(BF16)
    mix = lambda j: wt[N_QKV + j * D_M:N_QKV + (j + 1) * D_M]
    wmt = jnp.concatenate([mix(0), mix(2), mix(3)], axis=0)
    wg = jnp.pad(wt[N_QKV + N_MIX:], ((0, LANES - 2 * H_M), (0, 0)))
    bg = jnp.zeros((1, LANES), F32).at[0, :H_M].set(b_igate.astype(F32)).at[0, H_M:2 * H_M].set(b_fgate.astype(F32))
    row = lambda v: v.astype(F32).reshape(1, -1)
    gw = jnp.broadcast_to(g_mlstm_out.astype(F32)[:, None], (D_M, MLSTM_STEP))
    return dict(
        g_pre=row(g_mix_pre), wa=wt[:N_QKV], wkvt=wt[D_A:N_QKV], wmt=wmt, wk=mix(1),
        wg=wg, bg=bg, gw=gw, wo=w_out.astype(BF16), g_post=row(g_mix_post),
        g_ffn_pre=row(g_ffn_pre), wup=w_up.astype(BF16), cw=conv_w.astype(F32), cb=row(conv_b),
        wdn=w_down.astype(BF16), g_ffn_post=row(g_ffn_post))


def _state_in(c0, n0, m0):
    nb = c0.shape[0]
    return (c0.astype(F32), n0.astype(F32).reshape(nb, H_M, 1, DK_M),
            jnp.broadcast_to(m0.astype(F32)[..., None, None], (nb, H_M, 8, LANES)))


def _state_out(c, n, m):
    return c, n[:, :, 0, :], m[:, :, 0, 0]


def _layer(xp, xs, w, bias_p, bias_u, bias_w, cache_k, cache_v, c0, n0, m0, conv_prev, *, tile_in, tile_ffn):
    s_len = xp.shape[0]
    b, t_new, _ = xs.shape
    n = b * t_new
    wb = cache_k.shape[1]
    proj = (w["g_pre"], w["wa"], w["wmt"], w["wk"], w["wg"], w["bg"])

    wbp = min(max(wd for wd, _ in PATTERNS), s_len)
    dils = tuple(d for _, d in PATTERNS if d > 1)
    qkv1, qkv4, qkv16, kt_p, vt_p, qmt_p, km_p, vmt_p, omt_p, gates_p, gates_tp = _inproj(
        xp, *proj, w["wkvt"], tile=tile_in, dils=dils, tail_rows=wbp)
    att_p = _attn_prompt([qkv1.reshape(1, s_len, N_QKV), qkv4, qkv16], bias_p)

    c = LANES
    _, tail, *sample_m = _inproj(xs.reshape(n, D_MODEL), *proj, tile=n, tail_rows=n, spread=(c, t_new))

    to_t = lambda a: jnp.transpose(a.astype(F32), (0, 2, 3, 1)).reshape(b, D_A, wb)
    from_t = lambda a: jnp.transpose(a.reshape(b, H_A, HD_A, wb), (0, 3, 1, 2))
    state0 = _state_in(jnp.zeros((1, H_M, DK_M, DV_M), F32), jnp.zeros((1, H_M, DK_M), F32),
                       jnp.zeros((1, H_M), F32))
    (hmt_p, c_p, n_p, m_p), (hmt_s, c_o, n_o, m_o), att, k_win, v_win = _mixers(
        (qmt_p, km_p, vmt_p, omt_p, gates_p, gates_tp, w["gw"]) + state0,
        tuple(sample_m) + (w["gw"][:, :c],) + _state_in(c0, n0, m0),
        tail.reshape(b, t_new, N_QKV), to_t(cache_k), to_t(cache_v), bias_u, bias_w, c=MLSTM_STEP, cs=c)

    prev0 = jnp.zeros((CONV_W - 1, 2 * D_FF), F32)
    yp, st_p = _outffn(att_p, hmt_p, xp, prev0, w["wo"], w["g_post"], w["g_ffn_pre"], w["wup"], w["cw"], w["cb"],
                       w["wdn"], w["g_ffn_post"], tile=tile_ffn, shift=1)
    window = lambda a: jnp.transpose(a.reshape(H_A, HD_A, wbp), (2, 0, 1))[None]
    k_win_p, v_win_p = window(kt_p), window(vt_p)
    state_p = (k_win_p, v_win_p) + _state_out(c_p, n_p, m_p) + (st_p[-(CONV_W - 1):][None],)

    tm = lambda a: a.reshape(b, t_new, -1).transpose(1, 0, 2).reshape(n, -1)
    hmt = hmt_s.reshape(D_M, b, c)[:, :, :t_new].transpose(0, 2, 1).reshape(D_M, n)
    prev = conv_prev.astype(F32).transpose(1, 0, 2).reshape((CONV_W - 1) * b, 2 * D_FF)
    ys, st = _outffn(tm(att), hmt, tm(xs), prev, w["wo"], w["g_post"], w["g_ffn_pre"], w["wup"], w["cw"],
                     w["cb"], w["wdn"], w["g_ffn_post"], tile=n, shift=b)
    ys = ys.reshape(t_new, b, D_MODEL).transpose(1, 0, 2)
    conv_state = st.reshape(CONV_W - 1, b, 2 * D_FF).transpose(1, 0, 2)
    state_s = (from_t(k_win), from_t(v_win)) + _state_out(c_o, n_o, m_o) + (conv_state,)
    return yp, ys, state_p, state_s


def kernel(x_prompt, x_sample, cache_attn_k, cache_attn_v, state_mlstm_C, state_mlstm_n, state_mlstm_m,
           state_ffn_conv, rel_bias, g_mix_pre, w_in, b_igate, b_fgate, g_mlstm_out, w_out, g_mix_post,
           g_ffn_pre, w_up, conv_w, conv_b, w_down, g_ffn_post):
    depth = w_in.shape[0]
    batch, s_len, _ = x_prompt.shape
    assert batch == 1
    t_new = x_sample.shape[1]
    bias_p = _prompt_bias(rel_bias)
    bias_u, bias_w = _sample_bias(rel_bias, t_new)
    yp = x_prompt[0]
    ys = x_sample
    new_p, new_s = [], []
    for l in range(depth):
        w = _layer_weights(g_mix_pre[l], w_in[l], b_igate[l], b_fgate[l], g_mlstm_out[l], w_out[l],
                           g_mix_post[l], g_ffn_pre[l], w_up[l], conv_w[l], conv_b[l], w_down[l], g_ffn_post[l])
        yp, ys, sp, ss = _layer(yp, ys, w, bias_p, bias_u, bias_w, cache_attn_k[l], cache_attn_v[l],
                                state_mlstm_C[l], state_mlstm_n[l], state_mlstm_m[l], state_ffn_conv[l],
                                tile_in=256, tile_ffn=512)
        new_p.append(sp)
        new_s.append(ss)
    stack = lambda states, i: jnp.stack([s[i] for s in states])
    return ((yp[None], ys) + tuple(stack(new_p, i) for i in range(6))
            + tuple(stack(new_s, i) for i in range(6)))
```

```python
import functools
import math

import jax
import jax.numpy as jnp
import numpy as np
from jax import lax
from jax.experimental import pallas as pl
from jax.experimental.pallas import tpu as pltpu

F32 = jnp.float32
BF16 = jnp.bfloat16

D_MODEL = 1024
HD_A = 64
H_A = 8
D_A = H_A * HD_A
DK_M = 128
DV_M = 128
H_M = 4
D_M = H_M * DV_M
PATTERNS = ((128, 1), (512, 4), (2048, 16))
SPAN = 128
N_BUCKETS = 32
BUCKET_MAX_DIST = 2048
MLSTM_STEP = 256
D_FF = 2816
CONV_W = 3
RMS_EPS = 1e-6
N_QKV = 3 * D_A
N_MIX = 4 * D_M
LANES = 128
GATE_ROWS = 16
SUPER = 2048
CACHE_RING = 3
VMEM_LIMIT = 58 * 1024 * 1024
NEG_INF = float("-inf")
LOG2E = math.log2(math.e)
Q_SCALE = HD_A ** -0.5 * LOG2E
GELU_C1 = -2.0 * math.sqrt(2.0 / math.pi)
GELU_C3 = GELU_C1 * 0.044715

for _w, _d in PATTERNS:
    assert _w // _d == SPAN


def _params(sem, vmem=VMEM_LIMIT):
    return pltpu.CompilerParams(dimension_semantics=sem, vmem_limit_bytes=vmem)


def _const_spec(shape):
    nd = len(shape)
    return pl.BlockSpec(shape, lambda *_: (0,) * nd, pipeline_mode=pl.Buffered(1))


def _rms(x, g):
    return x * lax.rsqrt(jnp.mean(x * x, axis=-1, keepdims=True) + RMS_EPS) * g


def _inproj_kernel(x_ref, g_ref, wa_ref, wmt_ref, wk_ref, wg_ref, bg_ref, *refs, tile, dils, first_tail, spread):
    nd = len(dils)
    extra_a, extra_b = refs[0], refs[1]
    qkv1_ref = refs[2]
    dil_refs = refs[3:3 + nd]
    ntail = 2 if nd else 1
    tail_refs = refs[3 + nd:3 + nd + ntail]
    qmt_ref, km_ref, vmt_ref, omt_ref, gt_ref, gtt_ref = refs[3 + nd + ntail:]

    hb = _rms(x_ref[...], g_ref[...]).astype(BF16)
    nt = (((1,), (1,)), ((), ()))
    za = lax.dot_general(hb, wa_ref[...], nt, preferred_element_type=F32)
    za = jnp.concatenate([za[:, :D_A] * Q_SCALE, za[:, D_A:]], axis=1)
    zb = za.astype(BF16)
    qkv1_ref[...] = zb
    if not nd:
        tail_refs[0][...] = za

    if nd:
        zp = jnp.dot(extra_a[...], zb, preferred_element_type=F32).astype(BF16)
        base = 0
        for dil, out_ref in zip(dils, dil_refs):
            for r in range(dil):
                out_ref[r] = zp[base + r * (tile // dil):base + (r + 1) * (tile // dil)]
            base += tile

    zt = lax.dot_general(wmt_ref[...], hb, (((1,), (1,)), ((), ())), preferred_element_type=F32)
    qmt, vmt, omt = zt[0:D_M].astype(BF16), zt[D_M:2 * D_M].astype(BF16), zt[2 * D_M:3 * D_M]
    km = (lax.dot_general(hb, wk_ref[...], nt, preferred_element_type=F32) * (DK_M ** -0.5)).astype(BF16)

    zg = lax.dot_general(hb, wg_ref[...], nt, preferred_element_type=F32) + bg_ref[...]
    logsig = jnp.minimum(zg, 0.0) - jnp.log1p(jnp.exp(-jnp.abs(zg)))
    glane = lax.broadcasted_iota(jnp.int32, zg.shape, 1)
    gates = jnp.where(glane < H_M, zg, logsig)
    gates_t = gates.T[0:GATE_ROWS]

    if spread is None:
        qmt_ref[...], vmt_ref[...], omt_ref[...], km_ref[...] = qmt, vmt, omt, km
        gt_ref[...], gtt_ref[...] = gates, gates_t
    else:
        chunk, valid = spread
        to_cols = lambda a: jnp.dot(a, extra_a[...], preferred_element_type=F32)
        to_rows = lambda a: jnp.dot(extra_b[...], a, preferred_element_type=F32)
        exact = lambda move, a: functools.reduce(jnp.add, [move(part) for part in _split3(a)])
        qmt_ref[...] = to_cols(qmt).astype(BF16)
        vmt_ref[...] = to_cols(vmt).astype(BF16)
        omt_ref[...] = exact(to_cols, omt)
        km_ref[...] = to_rows(km).astype(BF16)
        tok = lax.broadcasted_iota(jnp.int32, gt_ref.shape, 0) % chunk
        ig_lane = lax.broadcasted_iota(jnp.int32, gt_ref.shape, 1) < H_M
        gt_ref[...] = jnp.where(jnp.logical_and(tok >= valid, ig_lane), NEG_INF, exact(to_rows, gates))
        tok_t = lax.broadcasted_iota(jnp.int32, gtt_ref.shape, 1) % chunk
        ig_row = lax.broadcasted_iota(jnp.int32, gtt_ref.shape, 0) < H_M
        gtt_ref[...] = jnp.where(jnp.logical_and(tok_t >= valid, ig_row), NEG_INF, exact(to_cols, gates_t))

    if nd:
        @pl.when(pl.program_id(0) >= first_tail)
        def _():
            kvt = lax.dot_general(extra_b[...], hb, (((1,), (1,)), ((), ())), preferred_element_type=F32)
            tail_refs[0][...] = kvt[0:D_A]
            tail_refs[1][...] = kvt[D_A:2 * D_A]


def _spread_matrix(n, chunk, valid):
    p = np.zeros((n, n // valid * chunk), np.float32)
    tok = np.arange(n)
    p[tok, tok // valid * chunk + tok % valid] = 1.0
    return jnp.asarray(p, BF16)


def _row_permutation(tile, dils):
    p = np.zeros((len(dils) * tile, tile), np.float32)
    for j, dil in enumerate(dils):
        for r in range(dil):
            for m in range(tile // dil):
                p[j * tile + r * (tile // dil) + m, m * dil + r] = 1.0
    return jnp.asarray(p, BF16)


def _inproj(x, g, wa, wmt, wk, wg, bg, wkvt=None, *, tile, dils=(), tail_rows, spread=None):
    n = x.shape[0]
    assert bool(dils) != (spread is not None)
    assert n % tile == 0 and tail_rows % tile == 0 and (tile % LANES == 0 or tile == n)
    for d in dils:
        assert tile % (16 * d) == 0
    steps = n // tile
    first_tail = (n - tail_rows) // tile
    m, mt = n, tile
    if spread:
        chunk, valid = spread
        assert steps == 1 and n % valid == 0 and valid <= chunk
        m = mt = n // valid * chunk
    row = lambda w: pl.BlockSpec((tile, w), lambda i: (i, 0))
    col = lambda h: pl.BlockSpec((h, tile), lambda i: (0, i))
    out_shape = [jax.ShapeDtypeStruct((n, N_QKV), BF16)]
    out_specs = [row(N_QKV)]
    for d in dils:
        out_shape.append(jax.ShapeDtypeStruct((d, n // d, N_QKV), BF16))
        out_specs.append(pl.BlockSpec((d, tile // d, N_QKV), lambda i: (0, i, 0)))
    if dils:
        out_shape += [jax.ShapeDtypeStruct((D_A, tail_rows), F32)] * 2
        out_specs += [pl.BlockSpec((D_A, tile), lambda i: (0, jnp.maximum(i - first_tail, 0)))] * 2
    else:
        out_shape.append(jax.ShapeDtypeStruct((tail_rows, N_QKV), F32))
        out_specs.append(pl.BlockSpec((tile, N_QKV), lambda i: (jnp.maximum(i - first_tail, 0), 0)))
    mrow = lambda w: pl.BlockSpec((mt, w), lambda i: (i, 0))
    mcol = lambda h: pl.BlockSpec((h, mt), lambda i: (0, i))
    out_shape += [jax.ShapeDtypeStruct((D_M, m), BF16), jax.ShapeDtypeStruct((m, D_M), BF16),
                  jax.ShapeDtypeStruct((D_M, m), BF16), jax.ShapeDtypeStruct((D_M, m), F32),
                  jax.ShapeDtypeStruct((m, LANES), F32), jax.ShapeDtypeStruct((GATE_ROWS, m), F32)]
    out_specs += [mcol(D_M), mrow(D_M), mcol(D_M), mcol(D_M), mrow(LANES), mcol(GATE_ROWS)]
    in_specs = [row(D_MODEL), _const_spec((1, D_MODEL)), _const_spec((N_QKV, D_MODEL)),
                _const_spec((3 * D_M, D_MODEL)), _const_spec((D_M, D_MODEL)),
                _const_spec((LANES, D_MODEL)), _const_spec((1, LANES))]
    args = [x, g, wa, wmt, wk, wg, bg]
    if dils:
        in_specs += [_const_spec((len(dils) * tile, tile)), _const_spec((2 * D_A, D_MODEL))]
        args += [_row_permutation(tile, dils), wkvt]
    else:
        sp = _spread_matrix(n, chunk, valid)
        in_specs += [_const_spec((n, m)), _const_spec((m, n))]
        args += [sp, sp.T]
    return pl.pallas_call(
        functools.partial(_inproj_kernel, tile=tile, dils=dils, first_tail=first_tail, spread=spread),
        grid=(steps,),
        in_specs=in_specs,
        out_specs=out_specs,
        out_shape=out_shape,
        compiler_params=_params(("arbitrary",)),
        name="inproj",
    )(*args)


def _attn_prompt_kernel(*refs):
    np_ = len(PATTERNS)
    in_refs = refs[:5 * np_]
    bvec_ref, o_ref = refs[5 * np_], refs[5 * np_ + 1]
    num_s, l_s, m_s, bias_s = refs[5 * np_ + 2:]
    n = pl.program_id(1)

    lane = lax.broadcasted_iota(jnp.int32, (SPAN, LANES), 1)
    even = lane < HD_A
    ones = jnp.ones((2 * SPAN, LANES), BF16)

    @pl.when(n == 0)
    def _():
        key_col = lax.broadcasted_iota(jnp.int32, (SPAN, 2 * SPAN), 1)
        for p in range(np_):
            for e in range(2):
                vb = jnp.broadcast_to(bvec_ref[p, 0, e:e + 1, :], (SPAN, 2 * SPAN))
                table = pltpu.roll(vb, 0, 1, stride=1, stride_axis=0)
                bias_s[p, 0, e * SPAN:(e + 1) * SPAN, :] = table
                bias_s[p, 1, e * SPAN:(e + 1) * SPAN, :] = jnp.where(key_col >= SPAN, table, NEG_INF)

    first_n = (n == 0).astype(jnp.int32)

    def with_prev(p, part, r, mb):
        cur_ref, prev_ref = in_refs[5 * p + part], in_refs[5 * p + part + 2]
        if mb == 0:
            return jnp.concatenate([prev_ref[r], cur_ref[r, 0:SPAN, :]], axis=0)
        return cur_ref[r, (mb - 1) * SPAN:(mb + 1) * SPAN, :]

    def unit(p, j):
        dil = PATTERNS[p][1]
        r, mb = divmod(j, SUPER // dil // SPAN)
        row0 = mb * SPAN
        qb = in_refs[5 * p][r, row0:row0 + SPAN, :]
        kb, vb = with_prev(p, 1, r, mb), with_prev(p, 2, r, mb)
        bias = bias_s[p, first_n] if mb == 0 else bias_s[p, 0]
        zero = jnp.zeros_like(qb)
        qs = jnp.concatenate([jnp.where(even, qb, zero), jnp.where(even, zero, qb)], axis=0)
        s = lax.dot_general(qs, kb, (((1,), (1,)), ((), ())), preferred_element_type=F32) + bias
        m = jnp.max(s, axis=-1, keepdims=True)
        pr = jnp.exp2(s - m).astype(BF16)
        res = jnp.dot(pr, jnp.concatenate([vb, ones], axis=1), preferred_element_type=F32)
        mb_ = jnp.broadcast_to(m, (2 * SPAN, LANES))
        rows = pl.ds(row0, SPAN) if dil == 1 else pl.ds(row0 * dil + r, SPAN, stride=dil)
        num_s[p, rows, :] = jnp.where(even, res[:SPAN, :LANES], res[SPAN:, :LANES])
        l_s[p, rows, :] = jnp.where(even, res[:SPAN, LANES:], res[SPAN:, LANES:])
        m_s[p, rows, :] = jnp.where(even, mb_[:SPAN], mb_[SPAN:])

    def combine(jn):
        rows = slice(jn * SPAN, (jn + 1) * SPAN)
        ms = [m_s[p, rows, :] for p in range(np_)]
        m = functools.reduce(jnp.maximum, ms)
        num = jnp.zeros((SPAN, LANES), F32)
        den = jnp.zeros((SPAN, LANES), F32)
        for p in range(np_):
            a = jnp.exp2(ms[p] - m)
            num = num + a * num_s[p, rows, :]
            den = den + a * l_s[p, rows, :]
        o_ref[rows, :] = (num / den).astype(o_ref.dtype)

    done = set()
    for jn in range(SUPER // SPAN):
        for p in sorted(range(np_), key=lambda p: -PATTERNS[p][1]):
            dil = PATTERNS[p][1]
            nblk = SUPER // dil // SPAN
            for r in range(dil):
                j = r * nblk + jn // dil
                if (p, j) not in done:
                    done.add((p, j))
                    unit(p, j)
        combine(jn)


def _attn_prompt(qkvs, bvec):
    s_len = qkvs[0].shape[1]
    assert s_len % SUPER == 0
    nsb = s_len // SUPER
    npair = D_A // LANES
    in_specs, args, scratch = [], [], []
    for (_, dil), a in zip(PATTERNS, qkvs):
        rows = SUPER // dil
        nblk = rows // SPAN
        for part in range(3):
            in_specs.append(pl.BlockSpec((dil, rows, LANES),
                                         lambda hp, n, part=part: (0, n, part * npair + hp)))
            args.append(a)
        for part in (1, 2):
            in_specs.append(pl.BlockSpec((dil, SPAN, LANES),
                                         lambda hp, n, part=part, nblk=nblk:
                                         (0, jnp.maximum(n * nblk - 1, 0), part * npair + hp)))
            args.append(a)
    in_specs.append(pl.BlockSpec((len(PATTERNS), 1, 2, 2 * SPAN), lambda hp, n: (0, hp, 0, 0)))
    args.append(bvec)
    scratch += [pltpu.VMEM((len(PATTERNS), SUPER, LANES), F32)] * 3
    scratch.append(pltpu.VMEM((len(PATTERNS), 2, 2 * SPAN, 2 * SPAN), F32))
    return pl.pallas_call(
        _attn_prompt_kernel,
        grid=(npair, nsb),
        in_specs=in_specs,
        out_specs=pl.BlockSpec((SUPER, LANES), lambda hp, n: (n, hp)),
        out_shape=jax.ShapeDtypeStruct((s_len, D_A), BF16),
        scratch_shapes=scratch,
        compiler_params=_params(("arbitrary", "arbitrary")),
        name="attn_prompt",
    )(*args)


def _sample_bias_tables(u_refs, w_ref, bc_refs, bn_ref, t_new):
    new_lane = lax.broadcasted_iota(jnp.int32, (t_new, LANES), 1) >= LANES - t_new
    for p, (win, _) in enumerate(PATTERNS):
        for h in range(H_A):
            rows = slice(h * t_new, (h + 1) * t_new)
            u = jnp.broadcast_to(u_refs[p][h:h + 1, :], (t_new, win + LANES))
            bc_refs[p][rows, :] = pltpu.roll(u, 0, 1, stride=1, stride_axis=0)[:, :win]
            w = jnp.broadcast_to(w_ref[p, h:h + 1, :], (t_new, LANES))
            bn_ref[p, rows, :] = jnp.where(new_lane, pltpu.roll(w, 0, 1, stride=1, stride_axis=0), NEG_INF)


def _attn_sample_step(qkv_ref, kc_ref, vc_ref, bc_refs, bn_ref, att_ref, ko_ref, vo_ref, *, t_new, wb):
    np_ = len(PATTERNS)
    qkv = qkv_ref[0]
    qf, kn, vn = qkv[:, 0:D_A], qkv[:, D_A:2 * D_A], qkv[:, 2 * D_A:3 * D_A]
    nrow = H_A * t_new
    row_h = lax.broadcasted_iota(jnp.int32, (nrow, D_A), 0) // t_new
    lane_h = lax.broadcasted_iota(jnp.int32, (nrow, D_A), 1) // HD_A
    own = row_h == lane_h
    qs = jnp.where(own, jnp.concatenate([qf] * H_A, axis=0), 0.0).astype(BF16)
    pad = jnp.zeros((LANES - t_new, D_A), F32)
    knt = jnp.concatenate([pad, kn], axis=0).T
    vnt = jnp.concatenate([pad, vn], axis=0).T
    kct = kc_ref[0]
    vct = vc_ref[0]
    s_c = jnp.dot(qs, kct.astype(BF16), preferred_element_type=F32)
    s_n = jnp.dot(qs, knt.astype(BF16), preferred_element_type=F32)

    wins = [w for w, _ in PATTERNS]
    es_c, es_n = [], []
    m = None
    for p, win in enumerate(wins):
        e_c = s_c[:, wb - win:] + bc_refs[p][...]
        e_n = s_n + bn_ref[p]
        es_c.append(e_c)
        es_n.append(e_n)
        mp = jnp.maximum(jnp.max(e_c, axis=-1, keepdims=True), jnp.max(e_n, axis=-1, keepdims=True))
        m = mp if m is None else jnp.maximum(m, mp)
    pn = None
    for p in range(np_):
        e = jnp.exp2(es_n[p] - m)
        pn = e if pn is None else pn + e
    order = sorted(range(np_), key=lambda p: -wins[p])
    assert wins[order[0]] == wb
    pc = jnp.exp2(es_c[order[0]] - m)
    for p in order[1:]:
        e = jnp.exp2(es_c[p] - m)
        pc = jnp.concatenate([pc[:, :wb - wins[p]], pc[:, wb - wins[p]:] + e], axis=1)
    den = jnp.sum(pc, axis=-1, keepdims=True) + jnp.sum(pn, axis=-1, keepdims=True)
    nt = (((1,), (1,)), ((), ()))
    num = (lax.dot_general(pc.astype(BF16), vct.astype(BF16), nt, preferred_element_type=F32)
           + lax.dot_general(pn.astype(BF16), vnt.astype(BF16), nt, preferred_element_type=F32))
    o = jnp.where(own, num / den, 0.0)
    att = o[0:t_new]
    for h in range(1, H_A):
        att = att + o[h * t_new:(h + 1) * t_new]
    att_ref[0] = att.astype(att_ref.dtype)

    is_new = lax.broadcasted_iota(jnp.int32, (D_A, LANES), 1) >= LANES - t_new
    for src, new, dst in ((kct, knt, ko_ref), (vct, vnt, vo_ref)):
        rolled = pltpu.roll(src, wb - t_new, 1)
        dst[0, :, 0:wb - LANES] = rolled[:, 0:wb - LANES]
        dst[0, :, wb - LANES:wb] = jnp.where(is_new, new, rolled[:, wb - LANES:wb])


def _mixers_kernel(*refs, t_new, wb, c, chunks, cs):
    np_ = len(PATTERNS)
    nm = 10
    p_in, s_in = refs[0:nm], refs[nm:2 * nm]
    qkv_ref, kc_hbm, vc_hbm = refs[2 * nm:2 * nm + 3]
    u_refs = refs[2 * nm + 3:2 * nm + 3 + np_]
    w_ref = refs[2 * nm + 3 + np_]
    outs = refs[2 * nm + 4 + np_:]
    (ph_ref, pc_out, pn_out, pm_out), (sh_ref, sc_out, sn_out, sm_out) = outs[0:4], outs[4:8]
    att_ref, ko_hbm, vo_hbm = outs[8:11]
    pcx_s, pm_s, scx_s, sm_s = outs[11:15]
    bc_refs, bn_ref = outs[15:15 + np_], outs[15 + np_]
    kbuf, vbuf, sems, kob, vob, osems = outs[16 + np_:22 + np_]
    i = pl.program_id(0)
    nb = pl.num_programs(0)

    def window_copies(row, oslot):
        return [pltpu.make_async_copy(buf.at[oslot], hbm.at[row], osems.at[which, oslot])
                for which, (buf, hbm) in enumerate(((kob, ko_hbm), (vob, vo_hbm)))]

    def cache_copies(row, slot):
        return [pltpu.make_async_copy(hbm.at[row], buf.at[slot], sems.at[which, slot])
                for which, (hbm, buf) in enumerate(((kc_hbm, kbuf), (vc_hbm, vbuf)))]

    @pl.when(i == 0)
    def _():
        for row in range(CACHE_RING):
            for cp in cache_copies(row, row):
                cp.start()
        _mlstm_load_state(*p_in[7:10], pcx_s, pm_s)
        _sample_bias_tables(u_refs, w_ref, bc_refs, bn_ref, t_new)

    slot = i % CACHE_RING
    oslot = i % 2

    @pl.when(i >= 2)
    def _():
        for cp in window_copies(i - 2, oslot):
            cp.wait()

    for cp in cache_copies(i, slot):
        cp.wait()
    _attn_sample_step(qkv_ref, kbuf.at[pl.ds(slot, 1)], vbuf.at[pl.ds(slot, 1)], bc_refs, bn_ref, att_ref,
                      kob.at[pl.ds(oslot, 1)], vob.at[pl.ds(oslot, 1)], t_new=t_new, wb=wb)
    _mlstm_load_state(*s_in[7:10], scx_s, sm_s)
    _mlstm_chunk(*s_in[0:7], sh_ref, scx_s, sm_s, c=cs, off=0)
    _mlstm_store_state(sc_out, sn_out, sm_out, scx_s, sm_s)
    for j in range(chunks):
        _mlstm_chunk(*p_in[0:7], ph_ref, pcx_s, pm_s, c=c, off=j * c)

    for cp in window_copies(i, oslot):
        cp.start()

    @pl.when(i + CACHE_RING < nb)
    def _():
        for cp in cache_copies(i + CACHE_RING, slot):
            cp.start()

    @pl.when(i == nb - 1)
    def _():
        for cp in window_copies(i - 1, 1 - oslot) + window_copies(i, oslot):
            cp.wait()
        _mlstm_store_state(pc_out, pn_out, pm_out, pcx_s, pm_s)


def _mixers(prompt_m, sample_m, qkv, cache_kt, cache_vt, bias_u, bias_w, *, c, cs):
    n = prompt_m[1].shape[0]
    b, t_new, _ = qkv.shape
    wb = cache_kt.shape[2]
    assert wb == max(w for w, _ in PATTERNS) and t_new % 8 == 0 and t_new <= LANES
    assert n % (b * c) == 0 and sample_m[1].shape[0] == b * cs and b >= CACHE_RING
    chunks = n // (b * c)
    nrow = H_A * t_new
    blk = lambda r, w: pl.BlockSpec((1, r, w), lambda i: (i, 0, 0))
    fixed = lambda shape: pl.BlockSpec((1,) + shape, lambda i: (0,) * (1 + len(shape)))
    per_b = lambda shape: pl.BlockSpec((1,) + shape, lambda i: (i,) + (0,) * len(shape))

    def group(tokens, chunk, st):
        row = lambda w: pl.BlockSpec((tokens, w), lambda i: (i, 0))
        col = lambda h: pl.BlockSpec((h, tokens), lambda i: (0, i))
        ins = [col(D_M), row(D_M), col(D_M), col(D_M), row(LANES), col(GATE_ROWS), _const_spec((D_M, chunk)),
               st((H_M, DK_M, DV_M)), st((H_M, 1, DK_M)), st((H_M, 8, LANES))]
        outs = [col(D_M), st((H_M, DK_M, DV_M)), st((H_M, 8, DK_M)), st((H_M, 8, LANES))]
        return ins, outs

    def group_shapes(tokens_total, nb):
        return [jax.ShapeDtypeStruct((D_M, tokens_total), BF16), jax.ShapeDtypeStruct((nb, H_M, DK_M, DV_M), F32),
                jax.ShapeDtypeStruct((nb, H_M, 8, DK_M), F32), jax.ShapeDtypeStruct((nb, H_M, 8, LANES), F32)]

    p_ins, p_outs = group(chunks * c, c, fixed)
    s_ins, s_outs = group(cs, cs, per_b)
    state = [pltpu.VMEM((H_M, 2 * DV_M, DK_M), F32), pltpu.VMEM((H_M, 8, LANES), F32)]
    res = pl.pallas_call(
        functools.partial(_mixers_kernel, t_new=t_new, wb=wb, c=c, chunks=chunks, cs=cs),
        grid=(b,),
        in_specs=p_ins + s_ins + [blk(t_new, N_QKV), pl.BlockSpec(memory_space=pl.ANY),
                                  pl.BlockSpec(memory_space=pl.ANY)]
                 + [_const_spec((H_A, w + LANES)) for w, _ in PATTERNS]
                 + [_const_spec((len(PATTERNS), H_A, LANES))],
        out_specs=p_outs + s_outs + [blk(t_new, D_A), pl.BlockSpec(memory_space=pl.ANY),
                                     pl.BlockSpec(memory_space=pl.ANY)],
        out_shape=group_shapes(n, 1) + group_shapes(b * cs, b)
                  + [jax.ShapeDtypeStruct((b, t_new, D_A), BF16),
                     jax.ShapeDtypeStruct((b, D_A, wb), F32), jax.ShapeDtypeStruct((b, D_A, wb), F32)],
        scratch_shapes=state + state + [pltpu.VMEM((nrow, w), F32) for w, _ in PATTERNS]
                       + [pltpu.VMEM((len(PATTERNS), nrow, LANES), F32)]
                       + [pltpu.VMEM((CACHE_RING, D_A, wb), F32)] * 2 + [pltpu.SemaphoreType.DMA((2, CACHE_RING))]
                       + [pltpu.VMEM((2, D_A, wb), F32)] * 2 + [pltpu.SemaphoreType.DMA((2, 2))],
        compiler_params=_params(("arbitrary",)),
        name="mixers",
    )(*prompt_m, *sample_m, qkv, cache_kt, cache_vt, *bias_u, bias_w)
    return res[0:4], res[4:8], res[8], res[9], res[10]


def _split3(x):
    a = x.astype(BF16)
    r = x - a.astype(F32)
    b = r.astype(BF16)
    c = (r - b.astype(F32)).astype(BF16)
    return a, b, c


def _mlstm_load_state(c0_ref, n0_ref, m0_ref, cx_s, m_s):
    for h in range(H_M):
        cx_s[h, 0:DV_M, :] = c0_ref[0, h].T
        cx_s[h, DV_M:, :] = jnp.broadcast_to(n0_ref[0, h], (DV_M, DK_M))
    m_s[...] = m0_ref[0]


def _mlstm_store_state(c_out, n_out, m_out, cx_s, m_s):
    for h in range(H_M):
        c_out[0, h] = cx_s[h, 0:DV_M, :].T
        n_out[0, h] = cx_s[h, DV_M:DV_M + 8, :]
    m_out[0] = m_s[...]


def _mlstm_chunk(qt_ref, k_ref, vt_ref, ot_ref, g_ref, gtt_ref, gw_ref, h_ref, cx_s, m_s, *, c, off):
    tok = slice(off, off + c)
    gates = g_ref[tok, :]
    gates_t = gtt_ref[:, tok]
    lane = lax.broadcasted_iota(jnp.int32, gates.shape, 1)
    lf = jnp.where(jnp.logical_and(lane >= H_M, lane < 2 * H_M), gates, 0.0)
    grow = lax.broadcasted_iota(jnp.int32, gates_t.shape, 0)
    lf_t = jnp.where(jnp.logical_and(grow >= H_M, grow < 2 * H_M), gates_t, 0.0)
    ri = lax.broadcasted_iota(jnp.int32, (c, c), 0)
    cj = lax.broadcasted_iota(jnp.int32, (c, c), 1)
    upper = ri <= cj
    tril = (ri >= cj).astype(BF16)
    triu = upper.astype(BF16)
    bsum = brow = None
    for part, part_t in zip(_split3(lf), _split3(lf_t)):
        t1 = jnp.dot(tril, part, preferred_element_type=F32)
        t2 = jnp.dot(part_t, triu, preferred_element_type=F32)
        bsum = t1 if bsum is None else bsum + t1
        brow = t2 if brow is None else brow + t2
    ones = jnp.ones((DV_M, c), BF16)

    for h in range(H_M):
        sl = slice(h * DK_M, (h + 1) * DK_M)
        qt, kh, vt = qt_ref[sl, tok], k_ref[tok, sl], vt_ref[sl, tok]
        b_row = brow[H_M + h:H_M + h + 1, :]
        i_row = gates_t[h:h + 1, :]
        a_col = gates[:, h:h + 1] - bsum[:, H_M + h:H_M + h + 1]
        m_prev = m_s[h, 0:1, 0:1]
        dmat = jnp.where(upper, b_row + a_col, NEG_INF)
        inter = b_row + m_prev
        mt = jnp.maximum(inter, jnp.max(dmat, axis=0, keepdims=True))
        st = jnp.dot(kh, qt, preferred_element_type=F32)
        smt = (st * jnp.exp(dmat - mt)).astype(BF16)
        iw = jnp.exp(inter - mt)
        vext = jnp.concatenate([vt, ones], axis=0)
        cxh = cx_s[h]
        ne = (jnp.dot(vext, smt, preferred_element_type=F32)
              + iw * jnp.dot(cxh.astype(BF16), qt, preferred_element_type=F32))
        num, den = ne[:DV_M], ne[DV_M:]
        hq = num / jnp.maximum(jnp.abs(den), jnp.exp(-mt))
        hn = hq * lax.rsqrt(jnp.mean(hq * hq, axis=0, keepdims=True) + RMS_EPS) * gw_ref[sl, :]
        h_ref[sl, tok] = (jax.nn.sigmoid(ot_ref[sl, tok]) * hn).astype(h_ref.dtype)

        b_last = b_row[:, c - 1:c]
        g_row = b_last - b_row + i_row
        m_new = jnp.maximum(b_last + m_prev, jnp.max(g_row, axis=1, keepdims=True))
        ws = jnp.exp(g_row - m_new)
        wc = jnp.exp(b_last + m_prev - m_new)
        vw = (vext.astype(F32) * ws).astype(BF16)
        cx_s[h] = wc * cxh + jnp.dot(vw, kh, preferred_element_type=F32)
        m_s[h] = jnp.broadcast_to(m_new, m_s.shape[1:])


def _outffn_kernel(att_ref, hmt_ref, x_ref, prev_ref, wo_ref, gpost_ref, gpre_ref, wup_ref, cw_ref, cb_ref,
                   wdn_ref, gffn_ref, y_ref, st_ref, u_ref, a_ref, *, tile, shift, hdr):
    i = pl.program_id(0)

    @pl.when(i == 0)
    def _():
        u_ref[hdr - 2 * shift:hdr, :] = prev_ref[...]

    @pl.when(i > 0)
    def _():
        u_ref[0:hdr, :] = u_ref[tile:tile + hdr, :]

    mixed = (jnp.dot(att_ref[...], wo_ref[0:D_A, :], preferred_element_type=F32)
             + lax.dot_general(hmt_ref[...], wo_ref[D_A:, :], (((0,), (0,)), ((), ())),
                               preferred_element_type=F32))
    x1 = x_ref[...] + _rms(mixed, gpost_ref[...])
    h2 = _rms(x1, gpre_ref[...]).astype(BF16)
    u_ref[hdr:hdr + tile, :] = jnp.dot(h2, wup_ref[...], preferred_element_type=F32)

    def conv(c):
        cols = slice(c * LANES, (c + 1) * LANES)
        y = cb_ref[:, cols] + u_ref[hdr - 2 * shift:hdr - 2 * shift + tile, cols] * cw_ref[0:1, cols]
        y = y + u_ref[hdr - shift:hdr - shift + tile, cols] * cw_ref[1:2, cols]
        return y + u_ref[hdr:hdr + tile, cols] * cw_ref[2:3, cols]

    nff = D_FF // LANES
    for c in range(nff):
        gate, val = conv(c), conv(nff + c)
        e = jnp.exp(gate * (GELU_C1 + GELU_C3 * (gate * gate)))
        a_ref[:, c * LANES:(c + 1) * LANES] = (gate * val / (1.0 + e)).astype(BF16)

    y2 = jnp.dot(a_ref[...], wdn_ref[...], preferred_element_type=F32)
    y_ref[...] = x1 + _rms(y2, gffn_ref[...])
    st_ref[...] = u_ref[tile:tile + hdr, :]


def _outffn(att, hmt, x, prev, wo, gpost, gpre, wup, cw, cb, wdn, gffn, *, tile, shift):
    n = x.shape[0]
    assert n % tile == 0 and (shift == 1 or n == tile)
    hdr = max(8, 2 * shift)
    assert hdr % 8 == 0 and (tile % LANES == 0 or tile == n)
    row = lambda w: pl.BlockSpec((tile, w), lambda i: (i, 0))
    return pl.pallas_call(
        functools.partial(_outffn_kernel, tile=tile, shift=shift, hdr=hdr),
        grid=(n // tile,),
        in_specs=[row(D_A), pl.BlockSpec((D_M, tile), lambda i: (0, i)), row(D_MODEL),
                  _const_spec((2 * shift, 2 * D_FF)),
                  _const_spec((D_A + D_M, D_MODEL)), _const_spec((1, D_MODEL)), _const_spec((1, D_MODEL)),
                  _const_spec((D_MODEL, 2 * D_FF)), _const_spec((CONV_W, 2 * D_FF)), _const_spec((1, 2 * D_FF)),
                  _const_spec((D_FF, D_MODEL)), _const_spec((1, D_MODEL))],
        out_specs=[row(D_MODEL), pl.BlockSpec((hdr, 2 * D_FF), lambda i: (0, 0))],
        out_shape=[jax.ShapeDtypeStruct((n, D_MODEL), F32), jax.ShapeDtypeStruct((hdr, 2 * D_FF), F32)],
        scratch_shapes=[pltpu.VMEM((hdr + tile, 2 * D_FF), F32), pltpu.VMEM((tile, D_FF), BF16)],
        compiler_params=_params(("arbitrary",)),
        name="outffn",
    )(att, hmt, x, prev, wo, gpost, gpre, wup, cw, cb, wdn, gffn)


def _t5_bucket(dist):
    max_exact = N_BUCKETS // 2
    d = jnp.maximum(dist, max_exact).astype(F32)
    large = max_exact + (jnp.log(d / max_exact) / math.log(BUCKET_MAX_DIST / max_exact)
                         * (N_BUCKETS - max_exact)).astype(jnp.int32)
    large = jnp.minimum(large, N_BUCKETS - 1)
    return jnp.where(dist < max_exact, dist, large)


def _pattern_bias(rel_bias, dil):
    j = jnp.arange(SPAN + 1)
    return rel_bias[_t5_bucket(j * dil)].T.astype(F32) * LOG2E


def _prompt_bias(rel_bias):
    out = []
    for _, dil in PATTERNS:
        bh = _pattern_bias(rel_bias, dil)
        v = jnp.concatenate([bh[:, ::-1], jnp.full((H_A, SPAN - 1), NEG_INF, F32)], axis=1)
        out.append(v.reshape(H_A // 2, 2, 2 * SPAN))
    return jnp.stack(out)


def _sample_bias(rel_bias, t_new):
    us, ws = [], []
    fill = lambda n: jnp.full((H_A, n), NEG_INF, F32)
    for win, dil in PATTERNS:
        bh = _pattern_bias(rel_bias, dil)
        dist = jnp.arange((SPAN + 1) * dil)
        g = jnp.where(dist % dil == 0, jnp.repeat(bh, dil, axis=1), NEG_INF)[:, :win + 1]
        us.append(jnp.concatenate([g[:, ::-1], fill(LANES - 1)], axis=1))
        ws.append(jnp.concatenate([g[:, :LANES - t_new + 1][:, ::-1], fill(t_new - 1)], axis=1))
    return us, jnp.stack(ws)


def _layer_weights(g_mix_pre, w_in, b_igate, b_fgate, g_mlstm_out, w_out, g_mix_post, g_ffn_pre, w_up,
                   conv_w, conv_b, w_down, g_ffn_post):
    wt = jnp.transpose(w_in).astype(BF16)
    mix = lambda j: wt[N_QKV + j * D_M:N_QKV + (j + 1) * D_M]
    wmt = jnp.concatenate([mix(0), mix(2), mix(3)], axis=0)
    wg = jnp.pad(wt[N_QKV + N_MIX:], ((0, LANES - 2 * H_M), (0, 0)))
    bg = jnp.zeros((1, LANES), F32).at[0, :H_M].set(b_igate.astype(F32)).at[0, H_M:2 * H_M].set(b_fgate.astype(F32))
    row = lambda v: v.astype(F32).reshape(1, -1)
    gw = jnp.broadcast_to(g_mlstm_out.astype(F32)[:, None], (D_M, MLSTM_STEP))
    return dict(
        g_pre=row(g_mix_pre), wa=wt[:N_QKV], wkvt=wt[D_A:N_QKV], wmt=wmt, wk=mix(1),
        wg=wg, bg=bg, gw=gw, wo=w_out.astype(BF16), g_post=row(g_mix_post),
        g_ffn_pre=row(g_ffn_pre), wup=w_up.astype(BF16), cw=conv_w.astype(F32), cb=row(conv_b),
        wdn=w_down.astype(BF16), g_ffn_post=row(g_ffn_post))


def _state_in(c0, n0, m0):
    nb = c0.shape[0]
    return (c0.astype(F32), n0.astype(F32).reshape(nb, H_M, 1, DK_M),
            jnp.broadcast_to(m0.astype(F32)[..., None, None], (nb, H_M, 8, LANES)))


def _state_out(c, n, m):
    return c, n[:, :, 0, :], m[:, :, 0, 0]


def _layer(xp, xs, w, bias_p, bias_u, bias_w, cache_k, cache_v, c0, n0, m0, conv_prev, *, tile_in, tile_ffn):
    s_len = xp.shape[0]
    b, t_new, _ = xs.shape
    n = b * t_new
    wb = cache_k.shape[1]
    proj = (w["g_pre"], w["wa"], w["wmt"], w["wk"], w["wg"], w["bg"])

    wbp = min(max(wd for wd, _ in PATTERNS), s_len)
    dils = tuple(d for _, d in PATTERNS if d > 1)
    qkv1, qkv4, qkv16, kt_p, vt_p, qmt_p, km_p, vmt_p, omt_p, gates_p, gates_tp = _inproj(
        xp, *proj, w["wkvt"], tile=tile_in, dils=dils, tail_rows=wbp)
    att_p = _attn_prompt([qkv1.reshape(1, s_len, N_QKV), qkv4, qkv16], bias_p)

    c = LANES
    _, tail, *sample_m = _inproj(xs.reshape(n, D_MODEL), *proj, tile=n, tail_rows=n, spread=(c, t_new))

    to_t = lambda a: jnp.transpose(a.astype(F32), (0, 2, 3, 1)).reshape(b, D_A, wb)
    from_t = lambda a: jnp.transpose(a.reshape(b, H_A, HD_A, wb), (0, 3, 1, 2))
    state0 = _state_in(jnp.zeros((1, H_M, DK_M, DV_M), F32), jnp.zeros((1, H_M, DK_M), F32),
                       jnp.zeros((1, H_M), F32))
    (hmt_p, c_p, n_p, m_p), (hmt_s, c_o, n_o, m_o), att, k_win, v_win = _mixers(
        (qmt_p, km_p, vmt_p, omt_p, gates_p, gates_tp, w["gw"]) + state0,
        tuple(sample_m) + (w["gw"][:, :c],) + _state_in(c0, n0, m0),
        tail.reshape(b, t_new, N_QKV), to_t(cache_k), to_t(cache_v), bias_u, bias_w, c=MLSTM_STEP, cs=c)

    prev0 = jnp.zeros((CONV_W - 1, 2 * D_FF), F32)
    yp, st_p = _outffn(att_p, hmt_p, xp, prev0, w["wo"], w["g_post"], w["g_ffn_pre"], w["wup"], w["cw"], w["cb"],
                       w["wdn"], w["g_ffn_post"], tile=tile_ffn, shift=1)
    window = lambda a: jnp.transpose(a.reshape(H_A, HD_A, wbp), (2, 0, 1))[None]
    k_win_p, v_win_p = window(kt_p), window(vt_p)
    state_p = (k_win_p, v_win_p) + _state_out(c_p, n_p, m_p) + (st_p[-(CONV_W - 1):][None],)

    tm = lambda a: a.reshape(b, t_new, -1).transpose(1, 0, 2).reshape(n, -1)
    hmt = hmt_s.reshape(D_M, b, c)[:, :, :t_new].transpose(0, 2, 1).reshape(D_M, n)
    prev = conv_prev.astype(F32).transpose(1, 0, 2).reshape((CONV_W - 1) * b, 2 * D_FF)
    ys, st = _outffn(tm(att), hmt, tm(xs), prev, w["wo"], w["g_post"], w["g_ffn_pre"], w["wup"], w["cw"],
                     w["cb"], w["wdn"], w["g_ffn_post"], tile=n, shift=b)
    ys = ys.reshape(t_new, b, D_MODEL).transpose(1, 0, 2)
    conv_state = st.reshape(CONV_W - 1, b, 2 * D_FF).transpose(1, 0, 2)
    state_s = (from_t(k_win), from_t(v_win)) + _state_out(c_o, n_o, m_o) + (conv_state,)
    return yp, ys, state_p, state_s


def kernel(x_prompt, x_sample, cache_attn_k, cache_attn_v, state_mlstm_C, state_mlstm_n, state_mlstm_m,
           state_ffn_conv, rel_bias, g_mix_pre, w_in, b_igate, b_fgate, g_mlstm_out, w_out, g_mix_post,
           g_ffn_pre, w_up, conv_w, conv_b, w_down, g_ffn_post):
    depth = w_in.shape[0]
    batch, s_len, _ = x_prompt.shape
    assert batch == 1
    t_new = x_sample.shape[1]
    bias_p = _prompt_bias(rel_bias)
    bias_u, bias_w = _sample_bias(rel_bias, t_new)
    yp = x_prompt[0]
    ys = x_sample
    new_p, new_s = [], []
    for l in range(depth):
        w = _layer_weights(g_mix_pre[l], w_in[l], b_igate[l], b_fgate[l], g_mlstm_out[l], w_out[l],
                           g_mix_post[l], g_ffn_pre[l], w_up[l], conv_w[l], conv_b[l], w_down[l], g_ffn_post[l])
        yp, ys, sp, ss = _layer(yp, ys, w, bias_p, bias_u, bias_w, cache_attn_k[l], cache_attn_v[l],
                                state_mlstm_C[l], state_mlstm_n[l], state_mlstm_m[l], state_ffn_conv[l],
                                tile_in=256, tile_ffn=512)
        new_p.append(sp)
        new_s.append(ss)
    stack = lambda states, i: jnp.stack([s[i] for s in states])
    return ((yp[None], ys) + tuple(stack(new_p, i) for i in range(6))
            + tuple(stack(new_s, i) for i in range(6)))
```
